```python
import jax, jax.numpy as jnp
from jax import lax
import numpy as np

D_MODEL = 1024
BATCH = 2
SEQ = 8192
DEPTH = 2

F32 = jnp.float32
EPS = 1e-6
NEG = -1e30
FORCE = 1e9
D_MIX = D_MODEL
D_LRU = D_MIX // 4
D_ATT = D_MIX // 2
D_CONV = D_MIX - D_LRU - D_ATT
LRU_BLOCKS = 4
LRU_BW = D_LRU // LRU_BLOCKS
LRU_CONV_W = 4
LRU_C = 8.0
HEAD_DIM = 64
N_HEADS = D_ATT // HEAD_DIM
N_KV = 2
GQA = N_HEADS // N_KV
KV_W = N_KV * HEAD_DIM
ROPE_THETA = 10000.0
CMP_LEN = 32
CMP_STRIDE = 16
CMP_HID = 256
SLC_LEN = 64
SLC_TOPN = 16
WINDOW = 512
Q_BLOCK = 128
CONV_K = 31
MOE_GROUPS = 4
MOE_EXPERTS = 4
MOE_TOPK = 2
MOE_HID = 512
IN_SIZES = [D_LRU, D_LRU, D_ATT, KV_W, KV_W, KV_W, KV_W, KV_W, KV_W, 3 * N_HEADS, 2 * D_CONV]
IN_COLS = sum(IN_SIZES)

kernel_name = 'hybrid_rglru_nsa_conformer_hmoe_adaln'


def _rmsnorm(x, g):
    xf = x.astype(F32)
    y = xf * lax.rsqrt(jnp.mean(xf * xf, axis=-1, keepdims=True) + EPS)
    return (y * g.astype(F32)).astype(x.dtype)


def _layernorm(x, g, b):
    xf = x.astype(F32)
    mu = jnp.mean(xf, axis=-1, keepdims=True)
    var = jnp.mean(jnp.square(xf - mu), axis=-1, keepdims=True)
    return ((xf - mu) * lax.rsqrt(var + EPS) * g.astype(F32) + b.astype(F32)).astype(x.dtype)


def _rope_tables(positions):
    inv = ROPE_THETA ** (-jnp.arange(0, HEAD_DIM, 2, dtype=F32) / HEAD_DIM)
    ang = positions.astype(F32)[..., None] * inv
    ang = jnp.concatenate([ang, ang], axis=-1)[:, :, None, :]
    return jnp.cos(ang), jnp.sin(ang)


def _rope(x, cos, sin):
    half = x.shape[-1] // 2
    rot = jnp.concatenate([-x[..., half:], x[..., :half]], axis=-1)
    return (x * cos + rot * sin).astype(x.dtype)


def _causal_dwconv(x, w, b):
    k = w.shape[0]
    y = lax.conv_general_dilated(x, w[:, None, :].astype(x.dtype), window_strides=(1,),
                                 padding=[(k - 1, 0)], dimension_numbers=('NWC', 'WIO', 'NWC'),
                                 feature_group_count=x.shape[-1])
    return y + b


def _rglru(xl, gl, conv_w, conv_b, wa, ba, wx, bx, lam):
    B_, S_, _ = xl.shape
    xc = _causal_dwconv(xl, conv_w, conv_b)
    xb = xc.reshape(B_, S_, LRU_BLOCKS, LRU_BW)
    r = jax.nn.sigmoid(jnp.einsum('bsnd,nde->bsne', xb, wa).reshape(B_, S_, D_LRU) + ba)
    i = jax.nn.sigmoid(jnp.einsum('bsnd,nde->bsne', xb, wx).reshape(B_, S_, D_LRU) + bx)
    log_a = LRU_C * r.astype(F32) * jax.nn.log_sigmoid(lam.astype(F32))
    a = jnp.exp(log_a)
    u = jnp.sqrt(-jnp.expm1(2.0 * log_a)) * (i * xc).astype(F32)

    def combine(left, right):
        a1, b1 = left
        a2, b2 = right
        return a1 * a2, a2 * b1 + b2

    _, h = lax.associative_scan(combine, (a, u), axis=1)
    return h.astype(xl.dtype) * jax.nn.gelu(gl)


def _nsa(q, kc, vc, ks, vs, kw, vw, gate_logits, cos, sin, cmp_k_w1, cmp_k_w2, cmp_v_w1, cmp_v_w2,
         cmp_pos_k, cmp_pos_v):
    B_, S_ = q.shape[:2]
    dt = q.dtype
    q = _rope(q.reshape(B_, S_, N_HEADS, HEAD_DIM), cos, sin)
    kv_shape = (B_, S_, N_KV, HEAD_DIM)
    kc = _rope(kc.reshape(kv_shape), cos, sin)
    ks = _rope(ks.reshape(kv_shape), cos, sin)
    kw = _rope(kw.reshape(kv_shape), cos, sin)
    vc, vs, vw = vc.reshape(kv_shape), vs.reshape(kv_shape), vw.reshape(kv_shape)

    n_cmp = (S_ - CMP_LEN) // CMP_STRIDE + 1
    idx = jnp.arange(n_cmp)[:, None] * CMP_STRIDE + jnp.arange(CMP_LEN)[None, :]

    def compress(t, pos, w1, w2):
        blk = t[:, idx] + pos[None, None, :, None, :]
        blk = jnp.transpose(blk, (0, 1, 3, 2, 4)).reshape(B_, n_cmp, N_KV, CMP_LEN * HEAD_DIM)
        return jax.nn.gelu(blk @ w1) @ w2

    kcmp = compress(kc, cmp_pos_k, cmp_k_w1, cmp_k_w2)
    vcmp = compress(vc, cmp_pos_v, cmp_v_w1, cmp_v_w2)
    cmp_end = jnp.arange(n_cmp) * CMP_STRIDE + CMP_LEN - 1

    n_slc = S_ // SLC_LEN
    n_top = min(SLC_TOPN, n_slc)
    ratio = SLC_LEN // CMP_STRIDE
    pad_cmp = n_slc * ratio - n_cmp
    ks_blk = jnp.transpose(ks.reshape(B_, n_slc, SLC_LEN, N_KV, HEAD_DIM), (0, 3, 1, 2, 4))
    vs_blk = jnp.transpose(vs.reshape(B_, n_slc, SLC_LEN, N_KV, HEAD_DIM), (0, 3, 1, 2, 4))
    kw_pad = jnp.pad(kw, ((0, 0), (WINDOW, 0), (0, 0), (0, 0)))
    vw_pad = jnp.pad(vw, ((0, 0), (WINDOW, 0), (0, 0), (0, 0)))
    gates = jax.nn.sigmoid(gate_logits.astype(F32)).reshape(B_, S_, N_HEADS, 3)
    scale = HEAD_DIM ** -0.5
    bi = jnp.arange(B_)[:, None, None, None]
    ki = jnp.arange(N_KV)[None, :, None, None]
    jsl = jnp.arange(n_slc)

    def block(qb):
        s0 = qb * Q_BLOCK
        t = s0 + jnp.arange(Q_BLOCK)
        qg = lax.dynamic_slice_in_dim(q, s0, Q_BLOCK, axis=1).reshape(B_, Q_BLOCK, N_KV, GQA, HEAD_DIM) * scale
        sc = jnp.einsum('bqkgd,bnkd->bkgqn', qg, kcmp).astype(F32)
        valid_c = cmp_end[None, :] <= t[:, None]
        pc = jax.nn.softmax(jnp.where(valid_c, sc, NEG), axis=-1) * jnp.any(valid_c, axis=-1)[:, None].astype(F32)
        o_cmp = jnp.einsum('bkgqn,bnkd->bqkgd', pc.astype(dt), vcmp)
        imp = jnp.pad(pc.sum(axis=2), ((0, 0), (0, 0), (0, 0), (0, pad_cmp)))
        imp = imp.reshape(B_, N_KV, Q_BLOCK, n_slc, ratio).sum(-1)
        forced = (jsl[None, :] == 0) | (jsl[None, :] == (t // SLC_LEN)[:, None])
        causal_s = (jsl[None, :] * SLC_LEN) <= t[:, None]
        imp = jnp.where(forced, FORCE, jnp.where(causal_s, imp, -1.0))
        _, sel = lax.top_k(imp, n_top)
        ksel = ks_blk[bi, ki, sel]
        vsel = vs_blk[bi, ki, sel]
        ss = jnp.einsum('bqkgd,bkqnld->bkgqnl', qg, ksel).astype(F32).reshape(B_, N_KV, GQA, Q_BLOCK, n_top * SLC_LEN)
        pos_s = (sel[..., None] * SLC_LEN + jnp.arange(SLC_LEN)).reshape(B_, N_KV, Q_BLOCK, n_top * SLC_LEN)
        mask_s = pos_s <= t[None, None, :, None]
        ps = jax.nn.softmax(jnp.where(mask_s[:, :, None], ss, NEG), axis=-1)
        ps = ps.reshape(B_, N_KV, GQA, Q_BLOCK, n_top, SLC_LEN)
        o_slc = jnp.einsum('bkgqnl,bkqnld->bqkgd', ps.astype(dt), vsel)
        kwin = lax.dynamic_slice_in_dim(kw_pad, s0, WINDOW + Q_BLOCK, axis=1)
        vwin = lax.dynamic_slice_in_dim(vw_pad, s0, WINDOW + Q_BLOCK, axis=1)
        pos_w = s0 - WINDOW + jnp.arange(WINDOW + Q_BLOCK)
        dist = t[:, None] - pos_w[None, :]
        mask_w = (pos_w[None, :] >= 0) & (dist >= 0) & (dist < WINDOW)
        sw = jnp.einsum('bqkgd,bwkd->bkgqw', qg, kwin).astype(F32)
        pw = jax.nn.softmax(jnp.where(mask_w, sw, NEG), axis=-1)
        o_win = jnp.einsum('bkgqw,bwkd->bqkgd', pw.astype(dt), vwin)
        g = lax.dynamic_slice_in_dim(gates, s0, Q_BLOCK, axis=1).reshape(B_, Q_BLOCK, N_KV, GQA, 3)
        o = g[..., 0:1] * o_cmp + g[..., 1:2] * o_slc + g[..., 2:3] * o_win
        return o.reshape(B_, Q_BLOCK, D_ATT).astype(dt)

    out = lax.map(block, jnp.arange(S_ // Q_BLOCK))
    return jnp.transpose(out, (1, 0, 2, 3)).reshape(B_, S_, D_ATT)


def _conformer_conv(u, dw_w, dw_b, ln_g, ln_b):
    v = u[..., :D_CONV] * jax.nn.sigmoid(u[..., D_CONV:])
    v = _causal_dwconv(v, dw_w, dw_b)
    v = _layernorm(v, ln_g, ln_b)
    return jax.nn.silu(v)


def _hier_moe(h, rg_w, rg_b, re_w, re_b, w_gate, w_up, w_down):
    B_, S_, D_ = h.shape
    t = h.reshape(B_ * S_, D_)
    lg = (t @ rg_w + rg_b).astype(F32)
    pg = jax.nn.softmax(lg, axis=-1)
    g_star = jnp.argmax(lg, axis=-1)
    pg_star = jnp.take_along_axis(pg, g_star[:, None], axis=1)
    le = (jnp.einsum('td,gde->tge', t, re_w) + re_b).astype(F32)
    le_star = jnp.take_along_axis(le, g_star[:, None, None], axis=1)[:, 0]
    top_v, top_i = lax.top_k(le_star, MOE_TOPK)
    pe = jax.nn.softmax(top_v, axis=-1)
    w_e = jnp.einsum('tk,tke->te', pe, jax.nn.one_hot(top_i, MOE_EXPERTS, dtype=F32))
    comb = (jax.nn.one_hot(g_star, MOE_GROUPS, dtype=F32)[:, :, None] * w_e[:, None, :]
            * pg_star[:, :, None]).astype(t.dtype)
    y = jnp.zeros_like(t)
    for g in range(MOE_GROUPS):
        hid = jax.nn.silu(jnp.einsum('td,edf->tef', t, w_gate[g])) * jnp.einsum('td,edf->tef', t, w_up[g])
        y = y + jnp.einsum('tef,efd->td', hid * comb[:, g, :, None], w_down[g])
    return y.reshape(B_, S_, D_)


def setup_inputs(seed: int = 0) -> dict:
    key = jax.random.key(seed)
    ks = iter(jax.random.split(key, 48))
    L = DEPTH

    def nrm(shape, s):
        return jax.random.normal(next(ks), shape, F32) * s

    def gain(shape):
        return 1.0 + nrm(shape, 0.02)

    x = nrm((BATCH, SEQ, D_MODEL), 1.0)
    c = nrm((BATCH, D_MODEL), 1.0)
    start = jax.random.randint(next(ks), (BATCH, 1), 0, 1024, dtype=jnp.int32)
    positions = start + jnp.arange(SEQ, dtype=jnp.int32)[None, :]
    a0 = jax.random.uniform(next(ks), (L, D_LRU), F32, 0.9, 0.999)
    lru_lambda = jnp.log(a0) - jnp.log1p(-a0)
    return {
        'x': x, 'c': c, 'positions': positions,
        'ada_w': nrm((L, D_MODEL, 6 * D_MODEL), 0.5 * D_MODEL ** -0.5),
        'ada_b': nrm((L, 6 * D_MODEL), 0.02),
        'norm1_g': gain((L, D_MODEL)),
        'norm2_g': gain((L, D_MODEL)),
        'w_in': nrm((L, D_MODEL, IN_COLS), D_MODEL ** -0.5),
        'lru_conv_w': nrm((L, LRU_CONV_W, D_LRU), LRU_CONV_W ** -0.5),
        'lru_conv_b': nrm((L, D_LRU), 0.02),
        'lru_wa': nrm((L, LRU_BLOCKS, LRU_BW, LRU_BW), LRU_BW ** -0.5),
        'lru_ba': nrm((L, D_LRU), 0.02),
        'lru_wx': nrm((L, LRU_BLOCKS, LRU_BW, LRU_BW), LRU_BW ** -0.5),
        'lru_bx': nrm((L, D_LRU), 0.02),
        'lru_lambda': lru_lambda,
        'cmp_k_w1': nrm((L, CMP_LEN * HEAD_DIM, CMP_HID), (CMP_LEN * HEAD_DIM) ** -0.5),
        'cmp_k_w2': nrm((L, CMP_HID, HEAD_DIM), CMP_HID ** -0.5),
        'cmp_v_w1': nrm((L, CMP_LEN * HEAD_DIM, CMP_HID), (CMP_LEN * HEAD_DIM) ** -0.5),
        'cmp_v_w2': nrm((L, CMP_HID, HEAD_DIM), CMP_HID ** -0.5),
        'cmp_pos_k': nrm((L, CMP_LEN, HEAD_DIM), 0.02),
        'cmp_pos_v': nrm((L, CMP_LEN, HEAD_DIM), 0.02),
        'cnv_dw_w': nrm((L, CONV_K, D_CONV), CONV_K ** -0.5),
        'cnv_dw_b': nrm((L, D_CONV), 0.02),
        'cnv_ln_g': gain((L, D_CONV)),
        'cnv_ln_b': nrm((L, D_CONV), 0.02),
        'out_norm_g': gain((L, D_MIX)),
        'w_out': nrm((L, D_MIX, D_MODEL), D_MIX ** -0.5),
        'moe_rg_w': nrm((L, D_MODEL, MOE_GROUPS), D_MODEL ** -0.5),
        'moe_rg_b': nrm((L, MOE_GROUPS), 0.01),
        'moe_re_w': nrm((L, MOE_GROUPS, D_MODEL, MOE_EXPERTS), D_MODEL ** -0.5),
        'moe_re_b': nrm((L, MOE_GROUPS, MOE_EXPERTS), 0.01),
        'moe_w_gate': nrm((L, MOE_GROUPS, MOE_EXPERTS, D_MODEL, MOE_HID), D_MODEL ** -0.5),
        'moe_w_up': nrm((L, MOE_GROUPS, MOE_EXPERTS, D_MODEL, MOE_HID), D_MODEL ** -0.5),
        'moe_w_down': nrm((L, MOE_GROUPS, MOE_EXPERTS, MOE_HID, D_MODEL), MOE_HID ** -0.5),
        'final_norm_g': gain((D_MODEL,)),
    }


def reference(x, c, positions, ada_w, ada_b, norm1_g, norm2_g, w_in, lru_conv_w, lru_conv_b, lru_wa, lru_ba,
              lru_wx, lru_bx, lru_lambda, cmp_k_w1, cmp_k_w2, cmp_v_w1, cmp_v_w2, cmp_pos_k, cmp_pos_v,
              cnv_dw_w, cnv_dw_b, cnv_ln_g, cnv_ln_b, out_norm_g, w_out, moe_rg_w, moe_rg_b, moe_re_w,
              moe_re_b, moe_w_gate, moe_w_up, moe_w_down, final_norm_g):
    cos, sin = _rope_tables(positions)
    split_pts = [int(v) for v in np.cumsum(IN_SIZES)[:-1]]
    sc = jax.nn.silu(c)
    for l in range(DEPTH):
        mod = sc @ ada_w[l] + ada_b[l]
        sh1, sc1, g1, sh2, sc2, g2 = [m[:, None, :] for m in jnp.split(mod, 6, axis=-1)]
        h = _rmsnorm(x, norm1_g[l]) * (1.0 + sc1) + sh1
        p = h @ w_in[l]
        (xl, gl, qa, kc, vc, ksl, vsl, kwn, vwn, gate_lg, cv) = jnp.split(p, split_pts, axis=-1)
        y_lru = _rglru(xl, gl, lru_conv_w[l], lru_conv_b[l], lru_wa[l], lru_ba[l], lru_wx[l], lru_bx[l],
                       lru_lambda[l])
        y_att = _nsa(qa, kc, vc, ksl, vsl, kwn, vwn, gate_lg, cos, sin, cmp_k_w1[l], cmp_k_w2[l],
                     cmp_v_w1[l], cmp_v_w2[l], cmp_pos_k[l], cmp_pos_v[l])
        y_cnv = _conformer_conv(cv, cnv_dw_w[l], cnv_dw_b[l], cnv_ln_g[l], cnv_ln_b[l])
        gn = out_norm_g[l]
        y = jnp.concatenate([_rmsnorm(y_lru, gn[:D_LRU]),
                             _rmsnorm(y_att, gn[D_LRU:D_LRU + D_ATT]),
                             _rmsnorm(y_cnv, gn[D_LRU + D_ATT:])], axis=-1)
        x = x + g1 * (y @ w_out[l])
        h2 = _rmsnorm(x, norm2_g[l]) * (1.0 + sc2) + sh2
        x = x + g2 * _hier_moe(h2, moe_rg_w[l], moe_rg_b[l], moe_re_w[l], moe_re_b[l], moe_w_gate[l],
                               moe_w_up[l], moe_w_down[l])
    return _rmsnorm(x, final_norm_g)
```

```python
import functools

import jax
import jax.numpy as jnp
from jax import lax
from jax.experimental import pallas as pl
from jax.experimental.pallas import tpu as pltpu

F32 = jnp.float32
BF16 = jnp.bfloat16

D_MODEL = 1024
D_LRU = 256
D_ATT = 512
D_CONV = 256
LRU_BLOCKS = 4
LRU_CONV_W = 4
LRU_C = 8.0
HEAD_DIM = 64
N_HEADS = 8
N_KV = 2
GQA = 4
ROPE_THETA = 10000.0
CMP_LEN = 32
CMP_STRIDE = 16
SLC_LEN = 64
SLC_TOPN = 16
WINDOW = 512
CONV_K = 31
MOE_GROUPS = 4
MOE_EXPERTS = 4
MOE_HID = 512
EPS = 1e-6
NEG = -1e30
FORCE = 1e9

LANES = 128
Q_BLOCK = 128
KEY_TILE = 512
N_SLC = 128
TM = 512
TC = 512
SCAN_ROWS = 64
LRU_TAIL = 8
CNV_TAIL = 32
IN_PAD = 2432
VMEM_LIMIT = 56 * 1024 * 1024


def _cparams(n_axes, vmem=VMEM_LIMIT):
    return pltpu.CompilerParams(dimension_semantics=("arbitrary",) * n_axes, vmem_limit_bytes=vmem)


def _dot(a, b):
    return jnp.dot(a, b, preferred_element_type=F32)


def _dot_nt(a, b):
    return lax.dot_general(a, b, (((1,), (1,)), ((), ())), preferred_element_type=F32)


def _rms(v, g):
    return v * lax.rsqrt(jnp.mean(v * v, axis=-1, keepdims=True) + EPS) * g


def _rope_kernel(pos_ref, inv_ref, sign_ref, cos_ref, sin_ref):
    ang = pos_ref[...].astype(F32) * inv_ref[...]
    cos_ref[...] = jnp.cos(ang)
    sin_ref[...] = jnp.sin(ang) * sign_ref[...]


def _rope_tables(positions):
    t = positions.size
    inv = ROPE_THETA ** (-jnp.arange(0, HEAD_DIM, 2, dtype=F32) / HEAD_DIM)
    inv128 = jnp.tile(inv, 4)[None, :]
    sign128 = jnp.tile(jnp.concatenate([-jnp.ones((32,), F32), jnp.ones((32,), F32)]), 2)[None, :]
    tr = 1024
    return pl.pallas_call(
        _rope_kernel,
        grid=(t // tr,),
        in_specs=[pl.BlockSpec((tr, 1), lambda i: (i, 0)),
                  pl.BlockSpec((1, LANES), lambda i: (0, 0)),
                  pl.BlockSpec((1, LANES), lambda i: (0, 0))],
        out_specs=[pl.BlockSpec((tr, LANES), lambda i: (i, 0))] * 2,
        out_shape=[jax.ShapeDtypeStruct((t, LANES), F32)] * 2,
        compiler_params=_cparams(1),
        name="rope_tables",
    )(positions.reshape(t, 1), inv128, sign128)


def _mod_kernel(c_ref, w_ref, b_ref, o_ref):
    sc = jax.nn.silu(c_ref[...])
    o_ref[...] = _dot(sc.astype(BF16), w_ref[...].astype(BF16)) + b_ref[...]


def _modulation(c, ada_w, ada_b):
    nl, d, n6 = ada_w.shape
    b = c.shape[0]
    rows = 16
    cp = jnp.zeros((rows, d), F32).at[:b].set(c)
    tn = 1536
    out = pl.pallas_call(
        _mod_kernel,
        grid=(nl, n6 // tn),
        in_specs=[pl.BlockSpec((rows, d), lambda l, j: (0, 0)),
                  pl.BlockSpec((None, d, tn), lambda l, j: (l, 0, j)),
                  pl.BlockSpec((None, 1, tn), lambda l, j: (l, 0, j))],
        out_specs=pl.BlockSpec((None, rows, tn), lambda l, j: (l, 0, j)),
        out_shape=jax.ShapeDtypeStruct((nl, rows, n6), F32),
        compiler_params=_cparams(2),
        name="adaln_mod",
    )(cp, ada_w, ada_b.reshape(nl, 1, n6))
    return out[:, :b].reshape(nl, b, 6, d)


def _inproj_kernel(x_ref, mod_ref, g_ref, w_ref, cos_ref, sin_ref,
                   xl_ref, gl_ref, q_ref, kc_ref, vc_ref, ksa_ref, vsa_ref, kw_ref, vwa_ref, gate_ref, cv_ref,
                   *, tiles_per_seq):
    tm = x_ref.shape[0]
    x = x_ref[...]
    h = _rms(x, g_ref[...]) * (1.0 + mod_ref[1:2, :]) + mod_ref[0:1, :]
    p = _dot(h.astype(BF16), w_ref[...])

    cos = cos_ref[...]
    sin = sin_ref[...]
    lane = lax.broadcasted_iota(jnp.int32, (tm, LANES), 1)
    first_half = (lane & 63) < 32

    def rope(v):
        rot = jnp.where(first_half, pltpu.roll(v, 96, 1), pltpu.roll(v, 32, 1))
        return v * cos + rot * sin

    xl_ref[...] = p[:, 0:256]
    gl_ref[...] = p[:, 256:512]
    scale = HEAD_DIM ** -0.5
    low = lane < 64
    for m in range(4):
        slab = rope(p[:, 512 + 128 * m:640 + 128 * m]) * scale
        swapped = pltpu.roll(slab, 64, 1)
        for hh in range(2):
            head = 2 * m + hh
            kv = head // GQA
            src = slab if hh == kv else swapped
            keep = low if kv == 0 else jnp.logical_not(low)
            q_ref[:, head * LANES:(head + 1) * LANES] = jnp.where(keep, src, 0.0).astype(BF16)
    kc_ref[...] = rope(p[:, 1024:1152])
    vc_ref[...] = p[:, 1152:1280]
    row = lax.broadcasted_iota(jnp.int32, (tm, LANES), 0)
    s_base = (pl.program_id(0) % tiles_per_seq) * tm
    onehot = jnp.where(lane == ((s_base + row) >> 6), 1.0, 0.0).astype(BF16)
    ones = jnp.ones((tm, LANES), BF16)
    ksa_ref[:, 0:LANES] = rope(p[:, 1280:1408]).astype(BF16)
    ksa_ref[:, LANES:2 * LANES] = onehot
    vsa_ref[:, 0:LANES] = p[:, 1408:1536].astype(BF16)
    vsa_ref[:, LANES:2 * LANES] = ones
    kw_ref[...] = rope(p[:, 1536:1664]).astype(BF16)
    vwa_ref[:, 0:LANES] = p[:, 1664:1792].astype(BF16)
    vwa_ref[:, LANES:2 * LANES] = ones
    cv_ref[...] = p[:, 1792:2304]
    gate_ref[...] = jax.nn.sigmoid(p[:, 2304:2432])


def _inproj(x2, mod_l, g, w_p, cos, sin, seq):
    t, d = x2.shape
    tps = seq // TM
    row = lambda w: pl.BlockSpec((TM, w), lambda i: (i, 0))
    outs = [(256, F32), (256, F32), (N_HEADS * LANES, BF16), (LANES, F32), (LANES, F32), (2 * LANES, BF16),
            (2 * LANES, BF16), (LANES, BF16), (2 * LANES, BF16), (LANES, F32), (2 * D_CONV, F32)]
    return pl.pallas_call(
        functools.partial(_inproj_kernel, tiles_per_seq=tps),
        grid=(t // TM,),
        in_specs=[row(d),
                  pl.BlockSpec((None, 6, d), lambda i: (i // tps, 0, 0)),
                  pl.BlockSpec((1, d), lambda i: (0, 0)),
                  pl.BlockSpec((d, IN_PAD), lambda i: (0, 0)),
                  row(LANES), row(LANES)],
        out_specs=[row(w) for w, _ in outs],
        out_shape=[jax.ShapeDtypeStruct((t, w), dt) for w, dt in outs],
        compiler_params=_cparams(1),
        name="in_proj",
    )(x2, mod_l, g, w_p, cos, sin)


def _shift_rows(v, s, fill, row):
    return jnp.where(row < s, fill, pltpu.roll(v, s, 0))


def _mixer_kernel(xl_ref, gl_ref, cv_ref, lcw_ref, lcb_ref, wa_ref, ba_ref, wx_ref, bx_ref, lam_ref,
                  cw_ref, cb_ref, lng_ref, lnb_ref, ylru_ref, ycnv_ref,
                  extl, extc, abuf, ubuf, hcar):
    tc = xl_ref.shape[0]

    @pl.when(pl.program_id(1) == 0)
    def _():
        extl[0:LRU_TAIL, :] = jnp.zeros((LRU_TAIL, D_LRU), F32)
        extc[0:CNV_TAIL, :] = jnp.zeros((CNV_TAIL, D_CONV), F32)
        hcar[...] = jnp.zeros_like(hcar)

    xl = xl_ref[...]
    extl[LRU_TAIL:LRU_TAIL + tc, :] = xl
    xc = lcb_ref[...]
    for k in range(LRU_CONV_W):
        xc = xc + lcw_ref[k:k + 1, :] * extl[pl.ds(LRU_TAIL - (LRU_CONV_W - 1) + k, tc), :]
    extl[0:LRU_TAIL, :] = xl[tc - LRU_TAIL:tc, :]
    xcb = xc.astype(BF16)
    r = jax.nn.sigmoid(_dot(xcb, wa_ref[...]) + ba_ref[...])
    gi = jax.nn.sigmoid(_dot(xcb, wx_ref[...]) + bx_ref[...])
    log_a = LRU_C * r * jax.nn.log_sigmoid(lam_ref[...])
    th = jnp.tanh(log_a)
    one_minus_a2 = -2.0 * th / (1.0 - th)
    abuf[...] = jnp.exp(log_a)
    ubuf[...] = jnp.sqrt(one_minus_a2) * (gi * xc)

    row = lax.broadcasted_iota(jnp.int32, (SCAN_ROWS, D_LRU), 0)

    def scan_chunk(c, h_prev):
        off = pl.multiple_of(c * SCAN_ROWS, SCAN_ROWS)
        a = abuf[pl.ds(off, SCAN_ROWS), :]
        b = ubuf[pl.ds(off, SCAN_ROWS), :]
        s = 1
        while s < SCAN_ROWS:
            b = b + a * _shift_rows(b, s, 0.0, row)
            a = a * _shift_rows(a, s, 1.0, row)
            s *= 2
        h = b + a * h_prev
        ylru_ref[pl.ds(off, SCAN_ROWS), :] = h * jax.nn.gelu(gl_ref[pl.ds(off, SCAN_ROWS), :])
        return h[SCAN_ROWS - 1:SCAN_ROWS, :]

    hcar[0:1, :] = lax.fori_loop(0, tc // SCAN_ROWS, scan_chunk, hcar[0:1, :])

    u = cv_ref[...]
    v = u[:, 0:D_CONV] * jax.nn.sigmoid(u[:, D_CONV:2 * D_CONV])
    extc[CNV_TAIL:CNV_TAIL + tc, :] = v
    acc = cb_ref[...]
    for k in range(CONV_K):
        acc = acc + cw_ref[k:k + 1, :] * extc[pl.ds(CNV_TAIL - (CONV_K - 1) + k, tc), :]
    extc[0:CNV_TAIL, :] = v[tc - CNV_TAIL:tc, :]
    mu = jnp.mean(acc, axis=-1, keepdims=True)
    var = jnp.mean(jnp.square(acc - mu), axis=-1, keepdims=True)
    y = (acc - mu) * lax.rsqrt(var + EPS) * lng_ref[...] + lnb_ref[...]
    ycnv_ref[...] = jax.nn.silu(y)


def _block_diag(w):
    n, bw, _ = w.shape
    out = jnp.zeros((n * bw, n * bw), w.dtype)
    for i in range(n):
        out = out.at[i * bw:(i + 1) * bw, i * bw:(i + 1) * bw].set(w[i])
    return out


def _mixer(xl, gl, cv, lcw, lcb, wa, ba, wx, bx, lam, cw, cb, lng, lnb, batch, seq):
    seqrow = lambda w: pl.BlockSpec((None, TC, w), lambda b, t: (b, t, 0))
    full = lambda a: pl.BlockSpec(a.shape, lambda b, t: (0,) * a.ndim)
    small = [lcw, lcb[None, :], _block_diag(wa).astype(BF16), ba[None, :], _block_diag(wx).astype(BF16), bx[None, :],
             lam[None, :], jnp.zeros((32, D_CONV), F32).at[:CONV_K].set(cw), cb[None, :], lng[None, :], lnb[None, :]]
    return pl.pallas_call(
        _mixer_kernel,
        grid=(batch, seq // TC),
        in_specs=[seqrow(D_LRU), seqrow(D_LRU), seqrow(2 * D_CONV)] + [full(a) for a in small],
        out_specs=[seqrow(D_LRU), seqrow(D_CONV)],
        out_shape=[jax.ShapeDtypeStruct((batch, seq, D_LRU), F32), jax.ShapeDtypeStruct((batch, seq, D_CONV), F32)],
        scratch_shapes=[pltpu.VMEM((TC + LRU_TAIL, D_LRU), F32), pltpu.VMEM((TC + CNV_TAIL, D_CONV), F32),
                        pltpu.VMEM((TC, D_LRU), F32), pltpu.VMEM((TC, D_LRU), F32), pltpu.VMEM((8, D_LRU), F32)],
        compiler_params=_cparams(2),
        name="mixer_stream",
    )(xl.reshape(batch, seq, D_LRU), gl.reshape(batch, seq, D_LRU), cv.reshape(batch, seq, 2 * D_CONV), *small)


def _cmp_kernel(g_ref, pos_ref, w1_ref, w2_ref, o_ref):
    half = CMP_STRIDE * HEAD_DIM
    quarter = g_ref.shape[1] // 4
    outs = []
    for kv in range(N_KV):
        g = g_ref[kv]
        ga = (g + pos_ref[0:1, :]).astype(BF16)
        gb = (g + pos_ref[1:2, :]).astype(BF16)
        first = _dot(ga, w1_ref[0:half, :])
        second = _dot(gb, w1_ref[half:2 * half, :])
        nxt = jnp.concatenate([second[quarter:], pltpu.roll(second[0:quarter], quarter - 1, 0)], axis=0)
        hid = jax.nn.gelu(first + nxt)
        outs.append(_dot(hid.astype(BF16), w2_ref[...]))
    o_ref[...] = jnp.concatenate(outs, axis=1).astype(o_ref.dtype)


def _compress(t, pos, w1, w2, batch, seq):
    nh = seq // CMP_STRIDE
    nj = nh // 4
    g = t.reshape(batch, nj, 4, CMP_STRIDE, N_KV, HEAD_DIM)
    g = jnp.transpose(g, (0, 4, 2, 1, 3, 5)).reshape(batch, N_KV, nh, CMP_STRIDE * HEAD_DIM)
    return pl.pallas_call(
        _cmp_kernel,
        grid=(batch,),
        in_specs=[pl.BlockSpec((None, N_KV, nh, CMP_STRIDE * HEAD_DIM), lambda b: (b, 0, 0, 0)),
                  pl.BlockSpec((2, CMP_STRIDE * HEAD_DIM), lambda b: (0, 0)),
                  pl.BlockSpec(w1.shape, lambda b: (0, 0)),
                  pl.BlockSpec(w2.shape, lambda b: (0, 0))],
        out_specs=pl.BlockSpec((None, nh, N_KV * HEAD_DIM), lambda b: (b, 0, 0)),
        out_shape=jax.ShapeDtypeStruct((batch, nh, N_KV * HEAD_DIM), BF16),
        compiler_params=_cparams(1),
        name="nsa_compress",
    )(g, pos.reshape(2, CMP_STRIDE * HEAD_DIM), w1.astype(BF16), w2.astype(BF16))


def _attn_kernel(q_ref, kcmp_ref, vcmp_ref, ksa_ref, vsa_ref, kw_ref, vwa_ref, gate_ref, o_ref, acc_ref, m_ref):
    rows = GQA * Q_BLOCK
    qb = pl.program_id(1)
    s0 = qb * Q_BLOCK
    kd = s0 // KEY_TILE

    r_i = lax.broadcasted_iota(jnp.int32, (rows, KEY_TILE), 0)
    c_i = lax.broadcasted_iota(jnp.int32, (rows, KEY_TILE), 1)
    t_i = s0 + (r_i & (Q_BLOCK - 1))
    lane = lax.broadcasted_iota(jnp.int32, (Q_BLOCK, LANES), 1)
    sub = lax.broadcasted_iota(jnp.int32, (Q_BLOCK, LANES), 0)
    gate = gate_ref[...]

    def flash_init():
        acc_ref[...] = jnp.zeros_like(acc_ref)
        m_ref[...] = jnp.full_like(m_ref, NEG)

    def flash_step(s, v_tile):
        m_old = m_ref[...]
        m_new = jnp.maximum(m_old, jnp.max(s, axis=-1, keepdims=True))
        p = jnp.exp(s - m_new)
        acc_ref[...] = acc_ref[...] * jnp.exp(m_old - m_new) + _dot(p.astype(BF16), v_tile)
        m_ref[...] = m_new

    def flash_result():
        acc = acc_ref[...]
        return acc[:, 0:LANES] / acc[:, LANES:2 * LANES]

    for kv in range(N_KV):
        q128 = jnp.concatenate([q_ref[:, (kv * GQA + g) * LANES:(kv * GQA + g + 1) * LANES] for g in range(GQA)],
                               axis=0)

        s_c = _dot_nt(q128, kcmp_ref[...])
        cmp_end = SLC_LEN * (c_i & (N_SLC - 1)) + CMP_STRIDE * (c_i >> 7) + (CMP_LEN - 1)
        valid = cmp_end <= t_i
        sm = jnp.where(valid, s_c, NEG)
        e = jnp.where(valid, jnp.exp(sm - jnp.max(sm, axis=-1, keepdims=True)), 0.0)
        l = jnp.sum(e, axis=-1, keepdims=True)
        pc = e * (1.0 / jnp.where(l > 0.0, l, 1.0))
        o_cmp = _dot(pc.astype(BF16), vcmp_ref[...])

        pg = pc[0:Q_BLOCK] + pc[Q_BLOCK:2 * Q_BLOCK] + pc[2 * Q_BLOCK:3 * Q_BLOCK] + pc[3 * Q_BLOCK:4 * Q_BLOCK]
        imp = pg[:, 0:N_SLC] + pg[:, N_SLC:2 * N_SLC] + pg[:, 2 * N_SLC:3 * N_SLC] + pg[:, 3 * N_SLC:4 * N_SLC]
        imp_t = imp.T
        t_q = s0 + lane
        forced = (sub == 0) | (sub == (t_q >> 6))
        val = jnp.where(forced, FORCE, jnp.where(sub * SLC_LEN <= t_q, imp_t, -1.0))
        sel_t = jnp.zeros((N_SLC, Q_BLOCK), F32)
        for _ in range(SLC_TOPN):
            mx = jnp.max(val, axis=0, keepdims=True)
            first = jnp.min(jnp.where(val == mx, sub, N_SLC), axis=0, keepdims=True)
            pick = sub == first
            sel_t = jnp.where(pick, 1.0, sel_t)
            val = jnp.where(pick, -jnp.inf, val)
        bias = ((sel_t.T - 1.0) * (-NEG)).astype(BF16)
        q_aug = jnp.concatenate([q128, jnp.concatenate([bias] * GQA, axis=0)], axis=1)

        flash_init()

        def slc_step(kt, carry):
            off = pl.multiple_of(kt * KEY_TILE, KEY_TILE)
            flash_step(_dot_nt(q_aug, ksa_ref[pl.ds(off, KEY_TILE), :]), vsa_ref[pl.ds(off, KEY_TILE), :])
            return carry

        lax.fori_loop(0, kd, slc_step, 0)
        off_d = pl.multiple_of(kd * KEY_TILE, KEY_TILE)
        pos_d = off_d + c_i
        s_d = _dot_nt(q_aug, ksa_ref[pl.ds(off_d, KEY_TILE), :])
        flash_step(jnp.where(pos_d <= t_i, s_d, NEG), vsa_ref[pl.ds(off_d, KEY_TILE), :])
        o_slc = flash_result()

        flash_init()

        @pl.when(kd > 0)
        def _():
            off_p = pl.multiple_of((kd - 1) * KEY_TILE, KEY_TILE)
            dist = t_i - (off_p + c_i)
            s_p = _dot_nt(q128, kw_ref[pl.ds(off_p, KEY_TILE), :])
            flash_step(jnp.where(dist < WINDOW, s_p, NEG), vwa_ref[pl.ds(off_p, KEY_TILE), :])

        s_w = _dot_nt(q128, kw_ref[pl.ds(off_d, KEY_TILE), :])
        flash_step(jnp.where(pos_d <= t_i, s_w, NEG), vwa_ref[pl.ds(off_d, KEY_TILE), :])
        o_win = flash_result()

        pieces = []
        for g in range(GQA):
            col = 3 * (kv * GQA + g)
            rs = slice(g * Q_BLOCK, (g + 1) * Q_BLOCK)
            pieces.append(gate[:, col:col + 1] * o_cmp[rs] + gate[:, col + 1:col + 2] * o_slc[rs]
                          + gate[:, col + 2:col + 3] * o_win[rs])
        for mm in range(2):
            even, odd = pieces[2 * mm], pieces[2 * mm + 1]
            if kv == 0:
                odd = pltpu.roll(odd, 64, 1)
            else:
                even = pltpu.roll(even, 64, 1)
            slab = kv * 2 + mm
            o_ref[:, slab * LANES:(slab + 1) * LANES] = jnp.where(lane < 64, even, odd)


def _attention(q, kcmp, vcmp, ksa, vsa, kw, vwa, gate, batch, seq):
    nh = kcmp.shape[1]
    res = lambda w: pl.BlockSpec((None, seq, w), lambda b, i: (b, 0, 0))
    blk = lambda w: pl.BlockSpec((None, Q_BLOCK, w), lambda b, i: (b, i, 0))
    r3 = lambda a, w: a.reshape(batch, seq, w)
    return pl.pallas_call(
        _attn_kernel,
        grid=(batch, seq // Q_BLOCK),
        in_specs=[blk(N_HEADS * LANES),
                  pl.BlockSpec((None, nh, LANES), lambda b, i: (b, 0, 0)),
                  pl.BlockSpec((None, nh, LANES), lambda b, i: (b, 0, 0)),
                  res(2 * LANES), res(2 * LANES), res(LANES), res(2 * LANES), blk(LANES)],
        out_specs=blk(D_ATT),
        out_shape=jax.ShapeDtypeStruct((batch, seq, D_ATT), F32),
        scratch_shapes=[pltpu.VMEM((GQA * Q_BLOCK, 2 * LANES), F32), pltpu.VMEM((GQA * Q_BLOCK, 1), F32)],
        compiler_params=_cparams(2),
        name="nsa_attention",
    )(r3(q, N_HEADS * LANES), kcmp, vcmp, r3(ksa, 2 * LANES), r3(vsa, 2 * LANES), r3(kw, LANES), r3(vwa, 2 * LANES),
      r3(gate, LANES))


def _outproj_kernel(x_ref, yl_ref, ya_ref, yc_ref, gn_ref, w_ref, mod_ref, n2_ref, rw_ref, rb_ref,
                    xn_ref, h2_ref, comb_ref):
    tm = x_ref.shape[0]
    a, b = D_LRU, D_LRU + D_ATT
    yl = _rms(yl_ref[...], gn_ref[:, 0:a]).astype(BF16)
    ya = _rms(ya_ref[...], gn_ref[:, a:b]).astype(BF16)
    yc = _rms(yc_ref[...], gn_ref[:, b:D_MODEL]).astype(BF16)
    y = _dot(yl, w_ref[0:a, :]) + _dot(ya, w_ref[a:b, :]) + _dot(yc, w_ref[b:D_MODEL, :])
    xn = x_ref[...] + mod_ref[2:3, :] * y
    xn_ref[...] = xn
    h2 = _rms(xn, n2_ref[...]) * (1.0 + mod_ref[4:5, :]) + mod_ref[3:4, :]
    h2_ref[...] = h2.astype(BF16)

    logit = jnp.dot(h2, rw_ref[...], preferred_element_type=F32, precision=lax.Precision.HIGHEST) + rb_ref[...]
    lane = lax.broadcasted_iota(jnp.int32, (tm, LANES), 1)
    ninf = -jnp.inf
    is_g = lane < MOE_GROUPS
    is_e = (lane >= MOE_GROUPS) & (lane < MOE_GROUPS + MOE_GROUPS * MOE_EXPERTS)
    lg_max = jnp.max(jnp.where(is_g, logit, ninf), axis=-1, keepdims=True)
    g_star = jnp.min(jnp.where(is_g & (logit == lg_max), lane, LANES), axis=-1, keepdims=True)
    pg_star = 1.0 / jnp.sum(jnp.where(is_g, jnp.exp(logit - lg_max), 0.0), axis=-1, keepdims=True)
    in_grp = is_e & (((lane - MOE_GROUPS) >> 2) == g_star)
    v1 = jnp.max(jnp.where(in_grp, logit, ninf), axis=-1, keepdims=True)
    i1 = jnp.min(jnp.where(in_grp & (logit == v1), lane, LANES), axis=-1, keepdims=True)
    rest = in_grp & (lane != i1)
    v2 = jnp.max(jnp.where(rest, logit, ninf), axis=-1, keepdims=True)
    i2 = jnp.min(jnp.where(rest & (logit == v2), lane, LANES), axis=-1, keepdims=True)
    d = jnp.exp(v2 - v1)
    pe1 = 1.0 / (1.0 + d)
    pe2 = d / (1.0 + d)
    comb_ref[...] = jnp.where(lane == i1, pe1, jnp.where(lane == i2, pe2, 0.0)) * pg_star


def _outproj(x2, yl, ya, yc, gn, w_out, mod_l, n2g, rw, rb, seq):
    t, d = x2.shape
    tps = seq // TM
    row = lambda w: pl.BlockSpec((TM, w), lambda i: (i, 0))
    full = lambda a: pl.BlockSpec(a.shape, lambda i: (0,) * a.ndim)
    return pl.pallas_call(
        _outproj_kernel,
        grid=(t // TM,),
        in_specs=[row(d), row(D_LRU), row(D_ATT), row(D_CONV), full(gn), full(w_out),
                  pl.BlockSpec((None, 6, d), lambda i: (i // tps, 0, 0)), full(n2g), full(rw), full(rb)],
        out_specs=[row(d), row(d), row(LANES)],
        out_shape=[jax.ShapeDtypeStruct((t, d), F32), jax.ShapeDtypeStruct((t, d), BF16),
                   jax.ShapeDtypeStruct((t, LANES), F32)],
        compiler_params=_cparams(1),
        name="out_proj_router",
    )(x2, yl.reshape(t, D_LRU), ya.reshape(t, D_ATT), yc.reshape(t, D_CONV), gn, w_out, mod_l, n2g, rw, rb)


def _moe_kernel(h2_ref, comb_ref, wg_ref, wu_ref, wd_ref, xn_ref, mod_ref, fg_ref, o_ref, acc_ref, *, final):
    tm = h2_ref.shape[0]
    e = pl.program_id(1)

    @pl.when(e == 0)
    def _():
        acc_ref[...] = jnp.zeros_like(acc_ref)

    h2 = h2_ref[...]
    hid = jax.nn.silu(_dot(h2, wg_ref[...])) * _dot(h2, wu_ref[...])
    lane = lax.broadcasted_iota(jnp.int32, (tm, LANES), 1)
    c = jnp.sum(jnp.where(lane == e + MOE_GROUPS, comb_ref[...], 0.0), axis=-1, keepdims=True)
    acc_ref[...] += _dot((hid * c).astype(BF16), wd_ref[...])

    @pl.when(e == pl.num_programs(1) - 1)
    def _():
        xo = xn_ref[...] + mod_ref[5:6, :] * acc_ref[...]
        if final:
            xo = _rms(xo, fg_ref[...])
        o_ref[...] = xo


def _moe(h2, comb, wg, wu, wd, xn, mod_l, fg, seq, final):
    t, d = xn.shape
    ne = wg.shape[0]
    tps = seq // TM
    return pl.pallas_call(
        functools.partial(_moe_kernel, final=final),
        grid=(t // TM, ne),
        in_specs=[pl.BlockSpec((TM, d), lambda i, e: (i, 0)),
                  pl.BlockSpec((TM, LANES), lambda i, e: (i, 0)),
                  pl.BlockSpec((None, d, MOE_HID), lambda i, e: (e, 0, 0)),
                  pl.BlockSpec((None, d, MOE_HID), lambda i, e: (e, 0, 0)),
                  pl.BlockSpec((None, MOE_HID, d), lambda i, e: (e, 0, 0)),
                  pl.BlockSpec((TM, d), lambda i, e: (i, 0)),
                  pl.BlockSpec((None, 6, d), lambda i, e: (i // tps, 0, 0)),
                  pl.BlockSpec((1, d), lambda i, e: (0, 0))],
        out_specs=pl.BlockSpec((TM, d), lambda i, e: (i, 0)),
        out_shape=jax.ShapeDtypeStruct((t, d), F32),
        scratch_shapes=[pltpu.VMEM((TM, d), F32)],
        compiler_params=_cparams(2),
        name="moe_experts",
    )(h2, comb, wg, wu, wd, xn, mod_l, fg)


def _permute_w_in(w):
    gate0 = D_LRU * 2 + D_ATT + 6 * LANES
    cv0 = gate0 + 3 * N_HEADS
    pad = jnp.zeros((w.shape[0], IN_PAD - w.shape[1]), w.dtype)
    return jnp.concatenate([w[:, :gate0], w[:, cv0:], w[:, gate0:cv0], pad], axis=1).astype(BF16)


def _router_weights(rg_w, rg_b, re_w, re_b):
    d = rg_w.shape[0]
    ne = MOE_GROUPS * MOE_EXPERTS
    w = jnp.concatenate([rg_w, jnp.transpose(re_w, (1, 0, 2)).reshape(d, ne),
                         jnp.zeros((d, LANES - MOE_GROUPS - ne), F32)], axis=1)
    b = jnp.concatenate([rg_b, re_b.reshape(ne), jnp.zeros((LANES - MOE_GROUPS - ne,), F32)])[None, :]
    return w, b


def kernel(x, c, positions, ada_w, ada_b, norm1_g, norm2_g, w_in, lru_conv_w, lru_conv_b, lru_wa, lru_ba, lru_wx, lru_bx, lru_lambda, cmp_k_w1, cmp_k_w2, cmp_v_w1, cmp_v_w2, cmp_pos_k, cmp_pos_v, cnv_dw_w, cnv_dw_b, cnv_ln_g, cnv_ln_b, out_norm_g, w_out, moe_rg_w, moe_rg_b, moe_re_w, moe_re_b, moe_w_gate, moe_w_up, moe_w_down, final_norm_g):
    batch, seq, d = x.shape
    depth = ada_w.shape[0]
    assert d == D_MODEL and seq == N_SLC * SLC_LEN and seq % KEY_TILE == 0
    t = batch * seq
    ne = MOE_GROUPS * MOE_EXPERTS

    cos, sin = _rope_tables(positions)
    mod = _modulation(c, ada_w, ada_b)
    x2 = x.reshape(t, d)
    fg = final_norm_g[None, :]
    for l in range(depth):
        (xl, gl, q, kc, vc, ksa, vsa, kw, vwa, gate, cv) = _inproj(
            x2, mod[l], norm1_g[l][None, :], _permute_w_in(w_in[l]), cos, sin, seq)
        y_lru, y_cnv = _mixer(xl, gl, cv, lru_conv_w[l], lru_conv_b[l], lru_wa[l], lru_ba[l], lru_wx[l], lru_bx[l],
                              lru_lambda[l], cnv_dw_w[l], cnv_dw_b[l], cnv_ln_g[l], cnv_ln_b[l], batch, seq)
        kcmp = _compress(kc, cmp_pos_k[l], cmp_k_w1[l], cmp_k_w2[l], batch, seq)
        vcmp = _compress(vc, cmp_pos_v[l], cmp_v_w1[l], cmp_v_w2[l], batch, seq)
        y_att = _attention(q, kcmp, vcmp, ksa, vsa, kw, vwa, gate, batch, seq)
        rw, rb = _router_weights(moe_rg_w[l], moe_rg_b[l], moe_re_w[l], moe_re_b[l])
        xn, h2, comb = _outproj(x2, y_lru, y_att, y_cnv, out_norm_g[l][None, :], w_out[l].astype(BF16), mod[l],
                                norm2_g[l][None, :], rw, rb, seq)
        x2 = _moe(h2, comb,
                  moe_w_gate[l].reshape(ne, d, MOE_HID).astype(BF16),
                  moe_w_up[l].reshape(ne, d, MOE_HID).astype(BF16),
                  moe_w_down[l].reshape(ne, MOE_HID, d).astype(BF16),
                  xn, mod[l], fg, seq, final=(l == depth - 1))
    return x2.reshape(batch, seq, d)
```

```python
import functools

import jax
import jax.numpy as jnp
from jax import lax
from jax.experimental import pallas as pl
from jax.experimental.pallas import tpu as pltpu

F32 = jnp.float32
BF16 = jnp.bfloat16

D_MODEL = 1024
D_LRU = 256
D_ATT = 512
D_CONV = 256
LRU_BLOCKS = 4
LRU_CONV_W = 4
LRU_C = 8.0
HEAD_DIM = 64
N_HEADS = 8
N_KV = 2
GQA = 4
ROPE_THETA = 10000.0
CMP_LEN = 32
CMP_STRIDE = 16
SLC_LEN = 64
SLC_TOPN = 16
WINDOW = 512
CONV_K = 31
MOE_GROUPS = 4
MOE_EXPERTS = 4
MOE_HID = 512
EPS = 1e-6
NEG = -1e30
FORCE = 1e9

LANES = 128
Q_BLOCK = 128
KEY_TILE = 512
N_SLC = 128
TM = 512
MOE_SUB = 128
MOE_TILE = 1024
TC = 512
SCAN_ROWS = 64
LRU_TAIL = 8
CNV_TAIL = 32
IN_PAD = 2432
VMEM_LIMIT = 56 * 1024 * 1024


def _cparams(n_axes, vmem=VMEM_LIMIT):
    return pltpu.CompilerParams(dimension_semantics=("arbitrary",) * n_axes, vmem_limit_bytes=vmem)


def _dot(a, b):
    return jnp.dot(a, b, preferred_element_type=F32)


def _dot_nt(a, b):
    return lax.dot_general(a, b, (((1,), (1,)), ((), ())), preferred_element_type=F32)


def _rms(v, g):
    return v * lax.rsqrt(jnp.mean(v * v, axis=-1, keepdims=True) + EPS) * g


def _rope_kernel(pos_ref, inv_ref, sign_ref, cos_ref, sin_ref):
    ang = pos_ref[...].astype(F32) * inv_ref[...]
    cos_ref[...] = jnp.cos(ang)
    sin_ref[...] = jnp.sin(ang) * sign_ref[...]


def _rope_tables(positions):
    t = positions.size
    inv = ROPE_THETA ** (-jnp.arange(0, HEAD_DIM, 2, dtype=F32) / HEAD_DIM)
    inv128 = jnp.tile(inv, 4)[None, :]
    sign128 = jnp.tile(jnp.concatenate([-jnp.ones((32,), F32), jnp.ones((32,), F32)]), 2)[None, :]
    tr = 1024
    return pl.pallas_call(
        _rope_kernel,
        grid=(t // tr,),
        in_specs=[pl.BlockSpec((tr, 1), lambda i: (i, 0)),
                  pl.BlockSpec((1, LANES), lambda i: (0, 0)),
                  pl.BlockSpec((1, LANES), lambda i: (0, 0))],
        out_specs=[pl.BlockSpec((tr, LANES), lambda i: (i, 0))] * 2,
        out_shape=[jax.ShapeDtypeStruct((t, LANES), F32)] * 2,
        compiler_params=_cparams(1),
        name="rope_tables",
    )(positions.reshape(t, 1), inv128, sign128)


def _mod_kernel(c_ref, w_ref, b_ref, o_ref):
    sc = jax.nn.silu(c_ref[...])
    o_ref[...] = _dot(sc.astype(BF16), w_ref[...].astype(BF16)) + b_ref[...]


def _modulation(c, ada_w, ada_b):
    nl, d, n6 = ada_w.shape
    b = c.shape[0]
    rows = 16
    cp = jnp.zeros((rows, d), F32).at[:b].set(c)
    tn = 1536
    out = pl.pallas_call(
        _mod_kernel,
        grid=(nl, n6 // tn),
        in_specs=[pl.BlockSpec((rows, d), lambda l, j: (0, 0)),
                  pl.BlockSpec((None, d, tn), lambda l, j: (l, 0, j)),
                  pl.BlockSpec((None, 1, tn), lambda l, j: (l, 0, j))],
        out_specs=pl.BlockSpec((None, rows, tn), lambda l, j: (l, 0, j)),
        out_shape=jax.ShapeDtypeStruct((nl, rows, n6), F32),
        compiler_params=_cparams(2),
        name="adaln_mod",
    )(cp, ada_w, ada_b.reshape(nl, 1, n6))
    return out[:, :b].reshape(nl, b, 6, d)


def _inproj_kernel(x_ref, mod_ref, g_ref, w_ref, cos_ref, sin_ref,
                   xl_ref, gl_ref, q_ref, kc_ref, vc_ref, ksa_ref, vsa_ref, kw_ref, vwa_ref, gate_ref, cv_ref,
                   *, tiles_per_seq):
    tm = x_ref.shape[0]
    x = x_ref[...]
    h = _rms(x, g_ref[...]) * (1.0 + mod_ref[1:2, :]) + mod_ref[0:1, :]
    p = _dot(h.astype(BF16), w_ref[...])

    cos = cos_ref[...]
    sin = sin_ref[...]
    lane = lax.broadcasted_iota(jnp.int32, (tm, LANES), 1)
    first_half = (lane & 63) < 32

    def rope(v):
        rot = jnp.where(first_half, pltpu.roll(v, 96, 1), pltpu.roll(v, 32, 1))
        return v * cos + rot * sin

    xl_ref[...] = p[:, 0:256]
    gl_ref[...] = p[:, 256:512]
    scale = HEAD_DIM ** -0.5
    low = lane < 64
    for m in range(4):
        slab = rope(p[:, 512 + 128 * m:640 + 128 * m]) * scale
        swapped = pltpu.roll(slab, 64, 1)
        for hh in range(2):
            head = 2 * m + hh
            kv = head // GQA
            src = slab if hh == kv else swapped
            keep = low if kv == 0 else jnp.logical_not(low)
            q_ref[:, head * LANES:(head + 1) * LANES] = jnp.where(keep, src, 0.0).astype(BF16)
    kc_ref[...] = rope(p[:, 1024:1152])
    vc_ref[...] = p[:, 1152:1280]
    row = lax.broadcasted_iota(jnp.int32, (tm, LANES), 0)
    s_base = (pl.program_id(0) % tiles_per_seq) * tm
    onehot = jnp.where(lane == ((s_base + row) >> 6), 1.0, 0.0).astype(BF16)
    ones = jnp.ones((tm, LANES), BF16)
    ksa_ref[:, 0:LANES] = rope(p[:, 1280:1408]).astype(BF16)
    ksa_ref[:, LANES:2 * LANES] = onehot
    vsa_ref[:, 0:LANES] = p[:, 1408:1536].astype(BF16)
    vsa_ref[:, LANES:2 * LANES] = ones
    kw_ref[...] = rope(p[:, 1536:1664]).astype(BF16)
    vwa_ref[:, 0:LANES] = p[:, 1664:1792].astype(BF16)
    vwa_ref[:, LANES:2 * LANES] = ones
    cv_ref[...] = p[:, 1792:2304]
    gate_ref[...] = jax.nn.sigmoid(p[:, 2304:2432])


def _inproj(x2, mod_l, g, w_p, cos, sin, seq):
    t, d = x2.shape
    tps = seq // TM
    row = lambda w: pl.BlockSpec((TM, w), lambda i: (i, 0))
    outs = [(256, F32), (256, F32), (N_HEADS * LANES, BF16), (LANES, F32), (LANES, F32), (2 * LANES, BF16),
            (2 * LANES, BF16), (LANES, BF16), (2 * LANES, BF16), (LANES, F32), (2 * D_CONV, F32)]
    return pl.pallas_call(
        functools.partial(_inproj_kernel, tiles_per_seq=tps),
        grid=(t // TM,),
        in_specs=[row(d),
                  pl.BlockSpec((None, 6, d), lambda i: (i // tps, 0, 0)),
                  pl.BlockSpec((1, d), lambda i: (0, 0)),
                  pl.BlockSpec((d, IN_PAD), lambda i: (0, 0)),
                  row(LANES), row(LANES)],
        out_specs=[row(w) for w, _ in outs],
        out_shape=[jax.ShapeDtypeStruct((t, w), dt) for w, dt in outs],
        compiler_params=_cparams(1),
        name="in_proj",
    )(x2, mod_l, g, w_p, cos, sin)


def _shift_rows(v, s, fill, row):
    return jnp.where(row < s, fill, pltpu.roll(v, s, 0))


def _mixer_kernel(xl_ref, gl_ref, cv_ref, lcw_ref, lcb_ref, wa_ref, ba_ref, wx_ref, bx_ref, lam_ref,
                  cw_ref, cb_ref, lng_ref, lnb_ref, ylru_ref, ycnv_ref,
                  extl, extc, abuf, ubuf, hcar):
    tc = xl_ref.shape[0]

    @pl.when(pl.program_id(1) == 0)
    def _():
        extl[0:LRU_TAIL, :] = jnp.zeros((LRU_TAIL, D_LRU), F32)
        extc[0:CNV_TAIL, :] = jnp.zeros((CNV_TAIL, D_CONV), F32)
        hcar[...] = jnp.zeros_like(hcar)

    xl = xl_ref[...]
    extl[LRU_TAIL:LRU_TAIL + tc, :] = xl
    xc = lcb_ref[...]
    for k in range(LRU_CONV_W):
        xc = xc + lcw_ref[k:k + 1, :] * extl[pl.ds(LRU_TAIL - (LRU_CONV_W - 1) + k, tc), :]
    extl[0:LRU_TAIL, :] = xl[tc - LRU_TAIL:tc, :]
    xcb = xc.astype(BF16)
    r = jax.nn.sigmoid(_dot(xcb, wa_ref[...]) + ba_ref[...])
    gi = jax.nn.sigmoid(_dot(xcb, wx_ref[...]) + bx_ref[...])
    log_a = LRU_C * r * jax.nn.log_sigmoid(lam_ref[...])
    th = jnp.tanh(log_a)
    one_minus_a2 = -2.0 * th / (1.0 - th)
    abuf[...] = jnp.exp(log_a)
    ubuf[...] = jnp.sqrt(one_minus_a2) * (gi * xc)

    row = lax.broadcasted_iota(jnp.int32, (SCAN_ROWS, D_LRU), 0)

    def scan_chunk(c, h_prev):
        off = pl.multiple_of(c * SCAN_ROWS, SCAN_ROWS)
        a = abuf[pl.ds(off, SCAN_ROWS), :]
        b = ubuf[pl.ds(off, SCAN_ROWS), :]
        s = 1
        while s < SCAN_ROWS:
            b = b + a * _shift_rows(b, s, 0.0, row)
            a = a * _shift_rows(a, s, 1.0, row)
            s *= 2
        h = b + a * h_prev
        ylru_ref[pl.ds(off, SCAN_ROWS), :] = h * jax.nn.gelu(gl_ref[pl.ds(off, SCAN_ROWS), :])
        return h[SCAN_ROWS - 1:SCAN_ROWS, :]

    hcar[0:1, :] = lax.fori_loop(0, tc // SCAN_ROWS, scan_chunk, hcar[0:1, :])

    u = cv_ref[...]
    v = u[:, 0:D_CONV] * jax.nn.sigmoid(u[:, D_CONV:2 * D_CONV])
    extc[CNV_TAIL:CNV_TAIL + tc, :] = v
    acc = cb_ref[...]
    for k in range(CONV_K):
        acc = acc + cw_ref[k:k + 1, :] * extc[pl.ds(CNV_TAIL - (CONV_K - 1) + k, tc), :]
    extc[0:CNV_TAIL, :] = v[tc - CNV_TAIL:tc, :]
    mu = jnp.mean(acc, axis=-1, keepdims=True)
    var = jnp.mean(jnp.square(acc - mu), axis=-1, keepdims=True)
    y = (acc - mu) * lax.rsqrt(var + EPS) * lng_ref[...] + lnb_ref[...]
    ycnv_ref[...] = jax.nn.silu(y)


def _block_diag(w):
    n, bw, _ = w.shape
    out = jnp.zeros((n * bw, n * bw), w.dtype)
    for i in range(n):
        out = out.at[i * bw:(i + 1) * bw, i * bw:(i + 1) * bw].set(w[i])
    return out


def _mixer(xl, gl, cv, lcw, lcb, wa, ba, wx, bx, lam, cw, cb, lng, lnb, batch, seq):
    seqrow = lambda w: pl.BlockSpec((None, TC, w), lambda b, t: (b, t, 0))
    full = lambda a: pl.BlockSpec(a.shape, lambda b, t: (0,) * a.ndim)
    small = [lcw, lcb[None, :], _block_diag(wa).astype(BF16), ba[None, :], _block_diag(wx).astype(BF16), bx[None, :],
             lam[None, :], jnp.zeros((32, D_CONV), F32).at[:CONV_K].set(cw), cb[None, :], lng[None, :], lnb[None, :]]
    return pl.pallas_call(
        _mixer_kernel,
        grid=(batch, seq // TC),
        in_specs=[seqrow(D_LRU), seqrow(D_LRU), seqrow(2 * D_CONV)] + [full(a) for a in small],
        out_specs=[seqrow(D_LRU), seqrow(D_CONV)],
        out_shape=[jax.ShapeDtypeStruct((batch, seq, D_LRU), F32), jax.ShapeDtypeStruct((batch, seq, D_CONV), F32)],
        scratch_shapes=[pltpu.VMEM((TC + LRU_TAIL, D_LRU), F32), pltpu.VMEM((TC + CNV_TAIL, D_CONV), F32),
                        pltpu.VMEM((TC, D_LRU), F32), pltpu.VMEM((TC, D_LRU), F32), pltpu.VMEM((8, D_LRU), F32)],
        compiler_params=_cparams(2),
        name="mixer_stream",
    )(xl.reshape(batch, seq, D_LRU), gl.reshape(batch, seq, D_LRU), cv.reshape(batch, seq, 2 * D_CONV), *small)


def _cmp_kernel(g_ref, pos_ref, w1_ref, w2_ref, o_ref):
    half = CMP_STRIDE * HEAD_DIM
    quarter = g_ref.shape[1] // 4
    outs = []
    for kv in range(N_KV):
        g = g_ref[kv]
        ga = (g + pos_ref[0:1, :]).astype(BF16)
        gb = (g + pos_ref[1:2, :]).astype(BF16)
        first = _dot(ga, w1_ref[0:half, :])
        second = _dot(gb, w1_ref[half:2 * half, :])
        nxt = jnp.concatenate([second[quarter:], pltpu.roll(second[0:quarter], quarter - 1, 0)], axis=0)
        hid = jax.nn.gelu(first + nxt)
        outs.append(_dot(hid.astype(BF16), w2_ref[...]))
    o_ref[...] = jnp.concatenate(outs, axis=1).astype(o_ref.dtype)


def _compress(t, pos, w1, w2, batch, seq):
    nh = seq // CMP_STRIDE
    nj = nh // 4
    g = t.reshape(batch, nj, 4, CMP_STRIDE, N_KV, HEAD_DIM)
    g = jnp.transpose(g, (0, 4, 2, 1, 3, 5)).reshape(batch, N_KV, nh, CMP_STRIDE * HEAD_DIM)
    return pl.pallas_call(
        _cmp_kernel,
        grid=(batch,),
        in_specs=[pl.BlockSpec((None, N_KV, nh, CMP_STRIDE * HEAD_DIM), lambda b: (b, 0, 0, 0)),
                  pl.BlockSpec((2, CMP_STRIDE * HEAD_DIM), lambda b: (0, 0)),
                  pl.BlockSpec(w1.shape, lambda b: (0, 0)),
                  pl.BlockSpec(w2.shape, lambda b: (0, 0))],
        out_specs=pl.BlockSpec((None, nh, N_KV * HEAD_DIM), lambda b: (b, 0, 0)),
        out_shape=jax.ShapeDtypeStruct((batch, nh, N_KV * HEAD_DIM), BF16),
        compiler_params=_cparams(1),
        name="nsa_compress",
    )(g, pos.reshape(2, CMP_STRIDE * HEAD_DIM), w1.astype(BF16), w2.astype(BF16))


def _attn_kernel(q_ref, kcmp_ref, vcmp_ref, ksa_ref, vsa_ref, kw_ref, vwa_ref, gate_ref, o_ref,
                 qa_ref, part_ref, acc_ref, m_ref, accw_ref, mw_ref):
    rows = GQA * Q_BLOCK
    reps = KEY_TILE // LANES
    qb = pl.program_id(1)
    s0 = qb * Q_BLOCK
    kd = s0 // KEY_TILE
    off_d = pl.multiple_of(kd * KEY_TILE, KEY_TILE)
    off_p = pl.multiple_of(jnp.maximum(kd - 1, 0) * KEY_TILE, KEY_TILE)
    d0 = s0 - off_d

    lane = lax.broadcasted_iota(jnp.int32, (Q_BLOCK, LANES), 1)
    sub = lax.broadcasted_iota(jnp.int32, (Q_BLOCK, LANES), 0)
    r_q = lax.broadcasted_iota(jnp.int32, (Q_BLOCK, KEY_TILE), 0)
    c_k = lax.broadcasted_iota(jnp.int32, (Q_BLOCK, KEY_TILE), 1)
    rel = c_k - r_q
    gate = gate_ref[...]

    def tile_rows(plane):
        return pltpu.repeat(plane, GQA, axis=0)

    def flash_step(acc, m, kv, s, v_tile):
        m_old = m[kv]
        m_new = jnp.maximum(m_old, jnp.broadcast_to(jnp.max(s, axis=-1, keepdims=True), m_old.shape))
        p = jnp.exp(s - pltpu.repeat(m_new, reps, axis=1))
        acc[kv] = acc[kv] * pltpu.repeat(jnp.exp(m_old - m_new), 2, axis=1) + _dot(p.astype(BF16), v_tile)
        m[kv] = m_new

    def flash_result(acc, kv):
        a = acc[kv]
        return a[:, 0:LANES] / a[:, LANES:2 * LANES]

    q128 = [jnp.concatenate([q_ref[:, (kv * GQA + g) * LANES:(kv * GQA + g + 1) * LANES] for g in range(GQA)], axis=0)
            for kv in range(N_KV)]

    kcmp = kcmp_ref[...]
    vcmp = vcmp_ref[...]
    cmp_end = SLC_LEN * (c_k & (N_SLC - 1)) + CMP_STRIDE * (c_k >> 7) + (CMP_LEN - 1)
    cmp_bias = tile_rows(jnp.where(cmp_end - r_q <= s0, 0.0, NEG))
    has_cmp = tile_rows(jnp.where(s0 + sub >= CMP_LEN - 1, 1.0, 0.0))
    o_cmp, imp_t = [], []
    for kv in range(N_KV):
        s_c = _dot_nt(q128[kv], kcmp) + cmp_bias
        mb = jnp.broadcast_to(jnp.max(s_c, axis=-1, keepdims=True), (rows, LANES))
        e = jnp.exp(s_c - pltpu.repeat(mb, reps, axis=1))
        lb = jnp.broadcast_to(jnp.sum(e, axis=-1, keepdims=True), (rows, LANES))
        pc = e * pltpu.repeat(has_cmp / lb, reps, axis=1)
        o_cmp.append(_dot(pc.astype(BF16), vcmp))
        pg = pc[0:Q_BLOCK] + pc[Q_BLOCK:2 * Q_BLOCK] + pc[2 * Q_BLOCK:3 * Q_BLOCK] + pc[3 * Q_BLOCK:4 * Q_BLOCK]
        imp = pg[:, 0:N_SLC] + pg[:, N_SLC:2 * N_SLC] + pg[:, 2 * N_SLC:3 * N_SLC] + pg[:, 3 * N_SLC:4 * N_SLC]
        imp_t.append(imp.T)

    accw_ref[...] = jnp.zeros_like(accw_ref)
    mw_ref[...] = jnp.full_like(mw_ref, NEG)
    prev_bias = tile_rows(jnp.where(rel > jnp.where(kd > 0, d0, KEY_TILE), 0.0, NEG))
    diag_bias = tile_rows(jnp.where(rel <= d0, 0.0, NEG))
    kw_p, vw_p = kw_ref[pl.ds(off_p, KEY_TILE), :], vwa_ref[pl.ds(off_p, KEY_TILE), :]
    kw_d, vw_d = kw_ref[pl.ds(off_d, KEY_TILE), :], vwa_ref[pl.ds(off_d, KEY_TILE), :]
    for kv in range(N_KV):
        flash_step(accw_ref, mw_ref, kv, _dot_nt(q128[kv], kw_p) + prev_bias, vw_p)
    for kv in range(N_KV):
        flash_step(accw_ref, mw_ref, kv, _dot_nt(q128[kv], kw_d) + diag_bias, vw_d)
    for kv in range(N_KV):
        o_win = flash_result(accw_ref, kv)
        for g in range(GQA):
            col = 3 * (kv * GQA + g)
            rs = slice(g * Q_BLOCK, (g + 1) * Q_BLOCK)
            part_ref[kv, rs, :] = gate[:, col:col + 1] * o_cmp[kv][rs] + gate[:, col + 2:col + 3] * o_win[rs]

    t_q = s0 + lane
    forced = (sub == 0) | (sub == (t_q >> 6))
    val = [jnp.where(forced, FORCE, jnp.where(sub * SLC_LEN <= t_q, imp_t[kv], -1.0)) for kv in range(N_KV)]
    sel_t = [jnp.zeros((N_SLC, Q_BLOCK), F32) for _ in range(N_KV)]
    for _ in range(SLC_TOPN):
        for kv in range(N_KV):
            mx = jnp.max(val[kv], axis=0, keepdims=True)
            first = jnp.min(jnp.where(val[kv] == mx, sub, N_SLC), axis=0, keepdims=True)
            pick = sub == first
            sel_t[kv] = jnp.where(pick, 1.0, sel_t[kv])
            val[kv] = jnp.where(pick, -jnp.inf, val[kv])
    for kv in range(N_KV):
        bias = ((sel_t[kv].T - 1.0) * (-NEG)).astype(BF16)
        qa_ref[kv] = jnp.concatenate([q128[kv], jnp.concatenate([bias] * GQA, axis=0)], axis=1)

    acc_ref[...] = jnp.zeros_like(acc_ref)
    m_ref[...] = jnp.full_like(m_ref, NEG)

    def slc_step(kt, carry):
        off = pl.multiple_of(kt * KEY_TILE, KEY_TILE)
        k_t, v_t = ksa_ref[pl.ds(off, KEY_TILE), :], vsa_ref[pl.ds(off, KEY_TILE), :]
        for kv in range(N_KV):
            flash_step(acc_ref, m_ref, kv, _dot_nt(qa_ref[kv], k_t), v_t)
        return carry

    lax.fori_loop(0, kd, slc_step, 0)
    k_d, v_d = ksa_ref[pl.ds(off_d, KEY_TILE), :], vsa_ref[pl.ds(off_d, KEY_TILE), :]
    causal_bias = tile_rows(jnp.where(rel <= d0, 0.0, NEG))
    for kv in range(N_KV):
        flash_step(acc_ref, m_ref, kv, _dot_nt(qa_ref[kv], k_d) + causal_bias, v_d)

    for kv in range(N_KV):
        o_slc = flash_result(acc_ref, kv)
        pieces = []
        for g in range(GQA):
            col = 3 * (kv * GQA + g)
            rs = slice(g * Q_BLOCK, (g + 1) * Q_BLOCK)
            pieces.append(part_ref[kv, rs, :] + gate[:, col + 1:col + 2] * o_slc[rs])
        for mm in range(2):
            even, odd = pieces[2 * mm], pieces[2 * mm + 1]
            if kv == 0:
                odd = pltpu.roll(odd, 64, 1)
            else:
                even = pltpu.roll(even, 64, 1)
            slab = kv * 2 + mm
            o_ref[:, slab * LANES:(slab + 1) * LANES] = jnp.where(lane < 64, even, odd)


def _attention(q, kcmp, vcmp, ksa, vsa, kw, vwa, gate, batch, seq):
    nh = kcmp.shape[1]
    assert nh == KEY_TILE
    rows = GQA * Q_BLOCK
    res = lambda w: pl.BlockSpec((None, seq, w), lambda b, i: (b, 0, 0))
    blk = lambda w: pl.BlockSpec((None, Q_BLOCK, w), lambda b, i: (b, i, 0))
    r3 = lambda a, w: a.reshape(batch, seq, w)
    return pl.pallas_call(
        _attn_kernel,
        grid=(batch, seq // Q_BLOCK),
        in_specs=[blk(N_HEADS * LANES),
                  pl.BlockSpec((None, nh, LANES), lambda b, i: (b, 0, 0)),
                  pl.BlockSpec((None, nh, LANES), lambda b, i: (b, 0, 0)),
                  res(2 * LANES), res(2 * LANES), res(LANES), res(2 * LANES), blk(LANES)],
        out_specs=blk(D_ATT),
        out_shape=jax.ShapeDtypeStruct((batch, seq, D_ATT), F32),
        scratch_shapes=[pltpu.VMEM((N_KV, rows, 2 * LANES), BF16), pltpu.VMEM((N_KV, rows, LANES), F32),
                        pltpu.VMEM((N_KV, rows, 2 * LANES), F32), pltpu.VMEM((N_KV, rows, LANES), F32),
                        pltpu.VMEM((N_KV, rows, 2 * LANES), F32), pltpu.VMEM((N_KV, rows, LANES), F32)],
        compiler_params=_cparams(2),
        name="nsa_attention",
    )(r3(q, N_HEADS * LANES), kcmp, vcmp, r3(ksa, 2 * LANES), r3(vsa, 2 * LANES), r3(kw, LANES), r3(vwa, 2 * LANES),
      r3(gate, LANES))


def _outproj_kernel(x_ref, yl_ref, ya_ref, yc_ref, gn_ref, w_ref, mod_ref, n2_ref, rw_ref, rb_ref,
                    xn_ref, h2_ref, comb_ref, pos_ref, seg_ref):
    tm = x_ref.shape[0]
    a, b = D_LRU, D_LRU + D_ATT
    yl = _rms(yl_ref[...], gn_ref[:, 0:a]).astype(BF16)
    ya = _rms(ya_ref[...], gn_ref[:, a:b]).astype(BF16)
    yc = _rms(yc_ref[...], gn_ref[:, b:D_MODEL]).astype(BF16)
    y = _dot(yl, w_ref[0:a, :]) + _dot(ya, w_ref[a:b, :]) + _dot(yc, w_ref[b:D_MODEL, :])
    xn = x_ref[...] + mod_ref[2:3, :] * y
    xn_ref[...] = xn
    h2 = _rms(xn, n2_ref[...]) * (1.0 + mod_ref[4:5, :]) + mod_ref[3:4, :]

    logit = jnp.dot(h2, rw_ref[...], preferred_element_type=F32, precision=lax.Precision.HIGHEST) + rb_ref[...]
    lane = lax.broadcasted_iota(jnp.int32, (tm, LANES), 1)
    ninf = -jnp.inf
    is_g = lane < MOE_GROUPS
    is_e = (lane >= MOE_GROUPS) & (lane < MOE_GROUPS + MOE_GROUPS * MOE_EXPERTS)
    lg_max = jnp.max(jnp.where(is_g, logit, ninf), axis=-1, keepdims=True)
    g_star = jnp.min(jnp.where(is_g & (logit == lg_max), lane, LANES), axis=-1, keepdims=True)
    pg_star = 1.0 / jnp.sum(jnp.where(is_g, jnp.exp(logit - lg_max), 0.0), axis=-1, keepdims=True)
    in_grp = is_e & (((lane - MOE_GROUPS) >> 2) == g_star)
    v1 = jnp.max(jnp.where(in_grp, logit, ninf), axis=-1, keepdims=True)
    i1 = jnp.min(jnp.where(in_grp & (logit == v1), lane, LANES), axis=-1, keepdims=True)
    rest = in_grp & (lane != i1)
    v2 = jnp.max(jnp.where(rest, logit, ninf), axis=-1, keepdims=True)
    i2 = jnp.min(jnp.where(rest & (logit == v2), lane, LANES), axis=-1, keepdims=True)
    d = jnp.exp(v2 - v1)
    pe1 = 1.0 / (1.0 + d)
    pe2 = d / (1.0 + d)
    comb = jnp.where(lane == i1, pe1, jnp.where(lane == i2, pe2, 0.0)) * pg_star

    onehot = jnp.where(lane == g_star, 1.0, 0.0)
    r_i = lax.broadcasted_iota(jnp.int32, (tm, tm), 0)
    c_i = lax.broadcasted_iota(jnp.int32, (tm, tm), 1)
    earlier = jnp.where(c_i < r_i, 1.0, 0.0).astype(BF16)
    rank = jnp.sum(onehot * _dot(earlier, onehot.astype(BF16)), axis=-1, keepdims=True)
    n_g = jnp.sum(onehot, axis=0, keepdims=True)
    lane1 = lax.broadcasted_iota(jnp.int32, (1, LANES), 1)
    start = jnp.zeros((1, LANES), F32)
    below = jnp.zeros((1, 1), F32)
    for g in range(1, MOE_GROUPS):
        below = below + jnp.sum(jnp.where(lane1 == g - 1, n_g, 0.0), axis=-1, keepdims=True)
        start = start + jnp.where(lane1 == g, below, 0.0)
    pos = jnp.sum(onehot * start, axis=-1, keepdims=True) + rank
    pos_rep = jnp.broadcast_to(pos, (tm, LANES))
    take_t = jnp.where(pltpu.repeat(pos_rep, tm // LANES, axis=1) == c_i.astype(F32), 1.0, 0.0)
    take = take_t.T
    h2_ref[...] = _dot(take.astype(BF16), h2.astype(BF16)).astype(BF16)
    comb_ref[...] = jnp.dot(take, comb, preferred_element_type=F32, precision=lax.Precision.HIGHEST)
    pos_ref[...] = pos_rep
    seg_ref[...] = jnp.zeros(seg_ref.shape, jnp.int32)
    seg_ref[0:1, :] = n_g.astype(jnp.int32)
    seg_ref[1:2, :] = start.astype(jnp.int32)


def _outproj(x2, yl, ya, yc, gn, w_out, mod_l, n2g, rw, rb, seq):
    t, d = x2.shape
    tps = seq // TM
    row = lambda w: pl.BlockSpec((TM, w), lambda i: (i, 0))
    full = lambda a: pl.BlockSpec(a.shape, lambda i: (0,) * a.ndim)
    return pl.pallas_call(
        _outproj_kernel,
        grid=(t // TM,),
        in_specs=[row(d), row(D_LRU), row(D_ATT), row(D_CONV), full(gn), full(w_out),
                  pl.BlockSpec((None, 6, d), lambda i: (i // tps, 0, 0)), full(n2g), full(rw), full(rb)],
        out_specs=[row(d), row(d), row(LANES), row(LANES), pl.BlockSpec((None, 8, LANES), lambda i: (i, 0, 0))],
        out_shape=[jax.ShapeDtypeStruct((t, d), F32), jax.ShapeDtypeStruct((t, d), BF16),
                   jax.ShapeDtypeStruct((t, LANES), F32), jax.ShapeDtypeStruct((t, LANES), F32),
                   jax.ShapeDtypeStruct((t // TM, 8, LANES), jnp.int32)],
        compiler_params=_cparams(1),
        name="out_proj_router",
    )(x2, yl.reshape(t, D_LRU), ya.reshape(t, D_ATT), yc.reshape(t, D_CONV), gn, w_out, mod_l, n2g, rw, rb)


def _moe_kernel(seg_ref, h2_ref, comb_ref, pos_ref, wg_ref, wu_ref, wd_ref, xn_ref, mod_ref, fg_ref, o_ref, acc_ref,
                *, final):
    chunks = h2_ref.shape[0] // TM
    i = pl.program_id(0)
    e = pl.program_id(1)
    grp = e // MOE_EXPERTS

    @pl.when(e == 0)
    def _():
        acc_ref[...] = jnp.zeros_like(acc_ref)

    lane = lax.broadcasted_iota(jnp.int32, (MOE_SUB, LANES), 1)
    for ck in range(chunks):
        base = (i * chunks + ck) * 8
        first = seg_ref[base + MOE_GROUPS + grp]
        last = first + seg_ref[base + grp]
        for j in range(TM // MOE_SUB):
            @pl.when((first < (j + 1) * MOE_SUB) & (last > j * MOE_SUB))
            def _(ck=ck, j=j):
                rs = slice(ck * TM + j * MOE_SUB, ck * TM + (j + 1) * MOE_SUB)
                h2 = h2_ref[rs, :]
                hid = jax.nn.silu(_dot(h2, wg_ref[...])) * _dot(h2, wu_ref[...])
                c = jnp.sum(jnp.where(lane == e + MOE_GROUPS, comb_ref[rs, :], 0.0), axis=-1, keepdims=True)
                acc_ref[rs, :] += _dot((hid * c).astype(BF16), wd_ref[...])

    @pl.when(e == pl.num_programs(1) - 1)
    def _():
        slot = lax.broadcasted_iota(jnp.int32, (TM, TM), 1).astype(F32)
        for ck in range(chunks):
            rs = slice(ck * TM, (ck + 1) * TM)
            y = acc_ref[rs, :]
            hi = y.astype(BF16)
            rest = y - hi.astype(F32)
            mid = rest.astype(BF16)
            lo = (rest - mid.astype(F32)).astype(BF16)
            take_t = jnp.where(pltpu.repeat(pos_ref[rs, :], TM // LANES, axis=1) == slot, 1.0, 0.0).astype(BF16)
            y_tok = (_dot(take_t, hi) + _dot(take_t, mid)) + _dot(take_t, lo)
            xo = xn_ref[rs, :] + mod_ref[5:6, :] * y_tok
            if final:
                xo = _rms(xo, fg_ref[...])
            o_ref[rs, :] = xo


def _moe(seg, h2, comb, pos, wg, wu, wd, xn, mod_l, fg, seq, final):
    t, d = xn.shape
    ne = wg.shape[0]
    tps = seq // MOE_TILE
    grid_spec = pltpu.PrefetchScalarGridSpec(
        num_scalar_prefetch=1,
        grid=(t // MOE_TILE, ne),
        in_specs=[pl.BlockSpec((MOE_TILE, d), lambda i, e, s: (i, 0)),
                  pl.BlockSpec((MOE_TILE, LANES), lambda i, e, s: (i, 0)),
                  pl.BlockSpec((MOE_TILE, LANES), lambda i, e, s: (i, 0)),
                  pl.BlockSpec((None, d, MOE_HID), lambda i, e, s: (e, 0, 0)),
                  pl.BlockSpec((None, d, MOE_HID), lambda i, e, s: (e, 0, 0)),
                  pl.BlockSpec((None, MOE_HID, d), lambda i, e, s: (e, 0, 0)),
                  pl.BlockSpec((MOE_TILE, d), lambda i, e, s: (i, 0)),
                  pl.BlockSpec((None, 6, d), lambda i, e, s: (i // tps, 0, 0)),
                  pl.BlockSpec((1, d), lambda i, e, s: (0, 0))],
        out_specs=pl.BlockSpec((MOE_TILE, d), lambda i, e, s: (i, 0)),
        scratch_shapes=[pltpu.VMEM((MOE_TILE, d), F32)],
    )
    return pl.pallas_call(
        functools.partial(_moe_kernel, final=final),
        grid_spec=grid_spec,
        out_shape=jax.ShapeDtypeStruct((t, d), F32),
        compiler_params=_cparams(2),
        name="moe_experts",
    )(seg, h2, comb, pos, wg, wu, wd, xn, mod_l, fg)


def _permute_w_in(w):
    gate0 = D_LRU * 2 + D_ATT + 6 * LANES
    cv0 = gate0 + 3 * N_HEADS
    pad = jnp.zeros((w.shape[0], IN_PAD - w.shape[1]), w.dtype)
    return jnp.concatenate([w[:, :gate0], w[:, cv0:], w[:, gate0:cv0], pad], axis=1).astype(BF16)


def _router_weights(rg_w, rg_b, re_w, re_b):
    d = rg_w.shape[0]
    ne = MOE_GROUPS * MOE_EXPERTS
    w = jnp.concatenate([rg_w, jnp.transpose(re_w, (1, 0, 2)).reshape(d, ne),
                         jnp.zeros((d, LANES - MOE_GROUPS - ne), F32)], axis=1)
    b = jnp.concatenate([rg_b, re_b.reshape(ne), jnp.zeros((LANES - MOE_GROUPS - ne,), F32)])[None, :]
    return w, b


def kernel(x, c, positions, ada_w, ada_b, norm1_g, norm2_g, w_in, lru_conv_w, lru_conv_b, lru_wa, lru_ba, lru_wx, lru_bx, lru_lambda, cmp_k_w1, cmp_k_w2, cmp_v_w1, cmp_v_w2, cmp_pos_k, cmp_pos_v, cnv_dw_w, cnv_dw_b, cnv_ln_g, cnv_ln_b, out_norm_g, w_out, moe_rg_w, moe_rg_b, moe_re_w, moe_re_b, moe_w_gate, moe_w_up, moe_w_down, final_norm_g):
    batch, seq, d = x.shape
    depth = ada_w.shape[0]
    assert d == D_MODEL and seq == N_SLC * SLC_LEN and seq % KEY_TILE == 0
    t = batch * seq
    ne = MOE_GROUPS * MOE_EXPERTS

    cos, sin = _rope_tables(positions)
    mod = _modulation(c, ada_w, ada_b)
    x2 = x.reshape(t, d)
    fg = final_norm_g[None, :]
    for l in range(depth):
        (xl, gl, q, kc, vc, ksa, vsa, kw, vwa, gate, cv) = _inproj(
            x2, mod[l], norm1_g[l][None, :], _permute_w_in(w_in[l]), cos, sin, seq)
        y_lru, y_cnv = _mixer(xl, gl, cv, lru_conv_w[l], lru_conv_b[l], lru_wa[l], lru_ba[l], lru_wx[l], lru_bx[l],
                              lru_lambda[l], cnv_dw_w[l], cnv_dw_b[l], cnv_ln_g[l], cnv_ln_b[l], batch, seq)
        kcmp = _compress(kc, cmp_pos_k[l], cmp_k_w1[l], cmp_k_w2[l], batch, seq)
        vcmp = _compress(vc, cmp_pos_v[l], cmp_v_w1[l], cmp_v_w2[l], batch, seq)
        y_att = _attention(q, kcmp, vcmp, ksa, vsa, kw, vwa, gate, batch, seq)
        rw, rb = _router_weights(moe_rg_w[l], moe_rg_b[l], moe_re_w[l], moe_re_b[l])
        xn, h2, comb, pos, seg = _outproj(x2, y_lru, y_att, y_cnv, out_norm_g[l][None, :], w_out[l].astype(BF16),
                                          mod[l], norm2_g[l][None, :], rw, rb, seq)
        seg = jnp.concatenate([seg[:, 0, :MOE_GROUPS], seg[:, 1, :MOE_GROUPS]], axis=1).reshape(-1)
        x2 = _moe(seg, h2, comb, pos,
                  moe_w_gate[l].reshape(ne, d, MOE_HID).astype(BF16),
                  moe_w_up[l].reshape(ne, d, MOE_HID).astype(BF16),
                  moe_w_down[l].reshape(ne, MOE_HID, d).astype(BF16),
                  xn, mod[l], fg, seq, final=(l == depth - 1))
    return x2.reshape(batch, seq, d)
```

```python
import functools

import jax
import jax.numpy as jnp
from jax import lax
from jax.experimental import pallas as pl
from jax.experimental.pallas import tpu as pltpu

F32 = jnp.float32
BF16 = jnp.bfloat16

D_MODEL = 1024
D_LRU = 256
D_ATT = 512
D_CONV = 256
LRU_BLOCKS = 4
LRU_CONV_W = 4
LRU_C = 8.0
HEAD_DIM = 64
N_HEADS = 8
N_KV = 2
GQA = 4
ROPE_THETA = 10000.0
CMP_LEN = 32
CMP_STRIDE = 16
SLC_LEN = 64
SLC_TOPN = 16
WINDOW = 512
CONV_K = 31
MOE_GROUPS = 4
MOE_EXPERTS = 4
MOE_HID = 512
EPS = 1e-6
NEG = -1e30
FORCE = 1e9
LOG2_E = 1.4426950408889634

LANES = 128
Q_BLOCK = 128
KEY_TILE = 512
N_SLC = 128
TM = 512
MOE_SUB = 256
MOE_TILE = 1024
TC = 512
SCAN_ROWS = 64
LRU_TAIL = 8
CNV_TAIL = 32
IN_PAD = 2432
VMEM_LIMIT = 56 * 1024 * 1024


def _cparams(n_axes, vmem=VMEM_LIMIT):
    return pltpu.CompilerParams(dimension_semantics=("arbitrary",) * n_axes, vmem_limit_bytes=vmem)


def _dot(a, b):
    return jnp.dot(a, b, preferred_element_type=F32)


def _dot_nt(a, b):
    return lax.dot_general(a, b, (((1,), (1,)), ((), ())), preferred_element_type=F32)


def _rep(v, n, axis):
    return jnp.concatenate([v] * n, axis=axis)


def _rms(v, g):
    return v * lax.rsqrt(jnp.mean(v * v, axis=-1, keepdims=True) + EPS) * g


def _rope_kernel(pos_ref, inv_ref, sign_ref, cos_ref, sin_ref):
    ang = pos_ref[...].astype(F32) * inv_ref[...]
    cos_ref[...] = jnp.cos(ang)
    sin_ref[...] = jnp.sin(ang) * sign_ref[...]


def _rope_tables(positions):
    t = positions.size
    inv = ROPE_THETA ** (-jnp.arange(0, HEAD_DIM, 2, dtype=F32) / HEAD_DIM)
    inv128 = jnp.tile(inv, 4)[None, :]
    sign128 = jnp.tile(jnp.concatenate([-jnp.ones((32,), F32), jnp.ones((32,), F32)]), 2)[None, :]
    tr = 1024
    return pl.pallas_call(
        _rope_kernel,
        grid=(t // tr,),
        in_specs=[pl.BlockSpec((tr, 1), lambda i: (i, 0)),
                  pl.BlockSpec((1, LANES), lambda i: (0, 0)),
                  pl.BlockSpec((1, LANES), lambda i: (0, 0))],
        out_specs=[pl.BlockSpec((tr, LANES), lambda i: (i, 0))] * 2,
        out_shape=[jax.ShapeDtypeStruct((t, LANES), F32)] * 2,
        compiler_params=_cparams(1),
        name="rope_tables",
    )(positions.reshape(t, 1), inv128, sign128)


def _mod_kernel(c_ref, w_ref, b_ref, o_ref):
    sc = jax.nn.silu(c_ref[...])
    o_ref[...] = _dot(sc.astype(BF16), w_ref[...].astype(BF16)) + b_ref[...]


def _modulation(c, ada_w, ada_b):
    nl, d, n6 = ada_w.shape
    b = c.shape[0]
    rows = 16
    cp = jnp.zeros((rows, d), F32).at[:b].set(c)
    tn = 1536
    out = pl.pallas_call(
        _mod_kernel,
        grid=(nl, n6 // tn),
        in_specs=[pl.BlockSpec((rows, d), lambda l, j: (0, 0)),
                  pl.BlockSpec((None, d, tn), lambda l, j: (l, 0, j)),
                  pl.BlockSpec((None, 1, tn), lambda l, j: (l, 0, j))],
        out_specs=pl.BlockSpec((None, rows, tn), lambda l, j: (l, 0, j)),
        out_shape=jax.ShapeDtypeStruct((nl, rows, n6), F32),
        compiler_params=_cparams(2),
        name="adaln_mod",
    )(cp, ada_w, ada_b.reshape(nl, 1, n6))
    return out[:, :b].reshape(nl, b, 6, d)


def _inproj_kernel(x_ref, mod_ref, g_ref, w_ref, cos_ref, sin_ref,
                   xl_ref, gl_ref, q_ref, kc_ref, vc_ref, ksa_ref, vsa_ref, kw_ref, vwa_ref, gate_ref, cv_ref,
                   *, tiles_per_seq):
    tm = x_ref.shape[0]
    x = x_ref[...]
    h = _rms(x, g_ref[...]) * (1.0 + mod_ref[1:2, :]) + mod_ref[0:1, :]
    p = _dot(h.astype(BF16), w_ref[...])

    cos = cos_ref[...]
    sin = sin_ref[...]
    lane = lax.broadcasted_iota(jnp.int32, (tm, LANES), 1)
    first_half = (lane & 63) < 32

    def rope(v):
        rot = jnp.where(first_half, pltpu.roll(v, 96, 1), pltpu.roll(v, 32, 1))
        return v * cos + rot * sin

    xl_ref[...] = p[:, 0:256]
    gl_ref[...] = p[:, 256:512]
    scale = HEAD_DIM ** -0.5 * LOG2_E
    low = lane < 64
    for m in range(4):
        slab = rope(p[:, 512 + 128 * m:640 + 128 * m]) * scale
        swapped = pltpu.roll(slab, 64, 1)
        for hh in range(2):
            head = 2 * m + hh
            kv = head // GQA
            src = slab if hh == kv else swapped
            keep = low if kv == 0 else jnp.logical_not(low)
            q_ref[:, head * LANES:(head + 1) * LANES] = jnp.where(keep, src, 0.0).astype(BF16)
    kc_ref[...] = rope(p[:, 1024:1152])
    vc_ref[...] = p[:, 1152:1280]
    row = lax.broadcasted_iota(jnp.int32, (tm, LANES), 0)
    s_base = (pl.program_id(0) % tiles_per_seq) * tm
    onehot = jnp.where(lane == ((s_base + row) >> 6), 1.0, 0.0).astype(BF16)
    ones = jnp.ones((tm, LANES), BF16)
    ksa_ref[:, 0:LANES] = rope(p[:, 1280:1408]).astype(BF16)
    ksa_ref[:, LANES:2 * LANES] = onehot
    vsa_ref[:, 0:LANES] = p[:, 1408:1536].astype(BF16)
    vsa_ref[:, LANES:2 * LANES] = ones
    kw_ref[...] = rope(p[:, 1536:1664]).astype(BF16)
    vwa_ref[:, 0:LANES] = p[:, 1664:1792].astype(BF16)
    vwa_ref[:, LANES:2 * LANES] = ones
    cv_ref[...] = p[:, 1792:2304]
    gate_ref[...] = jax.nn.sigmoid(p[:, 2304:2432])


def _inproj(x2, mod_l, g, w_p, cos, sin, seq):
    t, d = x2.shape
    tps = seq // TM
    row = lambda w: pl.BlockSpec((TM, w), lambda i: (i, 0))
    outs = [(256, F32), (256, F32), (N_HEADS * LANES, BF16), (LANES, F32), (LANES, F32), (2 * LANES, BF16),
            (2 * LANES, BF16), (LANES, BF16), (2 * LANES, BF16), (LANES, F32), (2 * D_CONV, F32)]
    return pl.pallas_call(
        functools.partial(_inproj_kernel, tiles_per_seq=tps),
        grid=(t // TM,),
        in_specs=[row(d),
                  pl.BlockSpec((None, 6, d), lambda i: (i // tps, 0, 0)),
                  pl.BlockSpec((1, d), lambda i: (0, 0)),
                  pl.BlockSpec((d, IN_PAD), lambda i: (0, 0)),
                  row(LANES), row(LANES)],
        out_specs=[row(w) for w, _ in outs],
        out_shape=[jax.ShapeDtypeStruct((t, w), dt) for w, dt in outs],
        compiler_params=_cparams(1),
        name="in_proj",
    )(x2, mod_l, g, w_p, cos, sin)


def _shift_rows(v, s, fill, row):
    return jnp.where(row < s, fill, pltpu.roll(v, s, 0))


def _causal_taps(ext, w_ref, bias, width, tail, tc):
    acc = bias
    for b in range(min(8, width)):
        shifted = pltpu.roll(ext, b, 0) if b else ext
        for k in range(width):
            back = width - 1 - k
            if back % 8 == b:
                start = tail - (back - b)
                acc = acc + w_ref[k:k + 1, :] * shifted[start:start + tc, :]
    return acc


def _mixer_kernel(xl_ref, gl_ref, cv_ref, lcw_ref, lcb_ref, wa_ref, ba_ref, wx_ref, bx_ref, lam_ref,
                  cw_ref, cb_ref, lng_ref, lnb_ref, ylru_ref, ycnv_ref,
                  extl, extc, abuf, ubuf, hcar):
    tc = xl_ref.shape[0]

    @pl.when(pl.program_id(1) == 0)
    def _():
        extl[0:LRU_TAIL, :] = jnp.zeros((LRU_TAIL, D_LRU), F32)
        extc[0:CNV_TAIL, :] = jnp.zeros((CNV_TAIL, D_CONV), F32)
        hcar[...] = jnp.zeros_like(hcar)

    xl = xl_ref[...]
    extl[LRU_TAIL:LRU_TAIL + tc, :] = xl
    xc = _causal_taps(extl[...], lcw_ref, lcb_ref[...], LRU_CONV_W, LRU_TAIL, tc)
    extl[0:LRU_TAIL, :] = xl[tc - LRU_TAIL:tc, :]
    xcb = xc.astype(BF16)
    r = jax.nn.sigmoid(_dot(xcb, wa_ref[...]) + ba_ref[...])
    gi = jax.nn.sigmoid(_dot(xcb, wx_ref[...]) + bx_ref[...])
    log_a = LRU_C * r * jax.nn.log_sigmoid(lam_ref[...])
    th = jnp.tanh(log_a)
    one_minus_a2 = -2.0 * th / (1.0 - th)
    abuf[...] = jnp.exp(log_a)
    ubuf[...] = jnp.sqrt(one_minus_a2) * (gi * xc)

    row = lax.broadcasted_iota(jnp.int32, (SCAN_ROWS, D_LRU), 0)

    def scan_chunk(c, h_prev):
        off = pl.multiple_of(c * SCAN_ROWS, SCAN_ROWS)
        a = abuf[pl.ds(off, SCAN_ROWS), :]
        b = ubuf[pl.ds(off, SCAN_ROWS), :]
        s = 1
        while s < SCAN_ROWS:
            b = b + a * _shift_rows(b, s, 0.0, row)
            a = a * _shift_rows(a, s, 1.0, row)
            s *= 2
        h = b + a * h_prev
        ylru_ref[pl.ds(off, SCAN_ROWS), :] = h * jax.nn.gelu(gl_ref[pl.ds(off, SCAN_ROWS), :])
        return h[SCAN_ROWS - 1:SCAN_ROWS, :]

    hcar[0:1, :] = lax.fori_loop(0, tc // SCAN_ROWS, scan_chunk, hcar[0:1, :])

    u = cv_ref[...]
    v = u[:, 0:D_CONV] * jax.nn.sigmoid(u[:, D_CONV:2 * D_CONV])
    extc[CNV_TAIL:CNV_TAIL + tc, :] = v
    acc = _causal_taps(extc[...], cw_ref, cb_ref[...], CONV_K, CNV_TAIL, tc)
    extc[0:CNV_TAIL, :] = v[tc - CNV_TAIL:tc, :]
    mu = jnp.mean(acc, axis=-1, keepdims=True)
    var = jnp.mean(jnp.square(acc - mu), axis=-1, keepdims=True)
    y = (acc - mu) * lax.rsqrt(var + EPS) * lng_ref[...] + lnb_ref[...]
    ycnv_ref[...] = jax.nn.silu(y)


def _block_diag(w):
    n, bw, _ = w.shape
    out = jnp.zeros((n * bw, n * bw), w.dtype)
    for i in range(n):
        out = out.at[i * bw:(i + 1) * bw, i * bw:(i + 1) * bw].set(w[i])
    return out


def _mixer(xl, gl, cv, lcw, lcb, wa, ba, wx, bx, lam, cw, cb, lng, lnb, batch, seq):
    seqrow = lambda w: pl.BlockSpec((None, TC, w), lambda b, t: (b, t, 0))
    full = lambda a: pl.BlockSpec(a.shape, lambda b, t: (0,) * a.ndim)
    small = [lcw, lcb[None, :], _block_diag(wa).astype(BF16), ba[None, :], _block_diag(wx).astype(BF16), bx[None, :],
             lam[None, :], jnp.zeros((32, D_CONV), F32).at[:CONV_K].set(cw), cb[None, :], lng[None, :], lnb[None, :]]
    return pl.pallas_call(
        _mixer_kernel,
        grid=(batch, seq // TC),
        in_specs=[seqrow(D_LRU), seqrow(D_LRU), seqrow(2 * D_CONV)] + [full(a) for a in small],
        out_specs=[seqrow(D_LRU), seqrow(D_CONV)],
        out_shape=[jax.ShapeDtypeStruct((batch, seq, D_LRU), F32), jax.ShapeDtypeStruct((batch, seq, D_CONV), F32)],
        scratch_shapes=[pltpu.VMEM((TC + LRU_TAIL, D_LRU), F32), pltpu.VMEM((TC + CNV_TAIL, D_CONV), F32),
                        pltpu.VMEM((TC, D_LRU), F32), pltpu.VMEM((TC, D_LRU), F32), pltpu.VMEM((8, D_LRU), F32)],
        compiler_params=_cparams(2),
        name="mixer_stream",
    )(xl.reshape(batch, seq, D_LRU), gl.reshape(batch, seq, D_LRU), cv.reshape(batch, seq, 2 * D_CONV), *small)


def _cmp_kernel(kc_ref, vc_ref, posk_ref, posv_ref, wk1_ref, wk2_ref, wv1_ref, wv2_ref, ko_ref, vo_ref, nat_ref):
    nh = ko_ref.shape[0]
    quarter = nh // 4

    def one(t_ref, pos_ref, w1_ref, w2_ref, o_ref):
        first = jnp.zeros((nh, w1_ref.shape[2]), F32)
        second = jnp.zeros((nh, w1_ref.shape[2]), F32)
        for l in range(CMP_STRIDE):
            x = t_ref[pl.ds(l, nh, stride=CMP_STRIDE), :]
            first = first + _dot((x + pos_ref[l:l + 1, :]).astype(BF16), w1_ref[l])
            second = second + _dot((x + pos_ref[CMP_STRIDE + l:CMP_STRIDE + l + 1, :]).astype(BF16),
                                   w1_ref[CMP_STRIDE + l])
        hid = jax.nn.gelu(first + pltpu.roll(second, nh - 1, 0))
        nat_ref[...] = _dot(hid.astype(BF16), w2_ref[...])
        for r in range(4):
            o_ref[r * quarter:(r + 1) * quarter, :] = nat_ref[pl.ds(r, quarter, stride=4), :].astype(o_ref.dtype)

    one(kc_ref, posk_ref, wk1_ref, wk2_ref, ko_ref)
    one(vc_ref, posv_ref, wv1_ref, wv2_ref, vo_ref)


def _cmp_weights(pos, w1, w2):
    hid = w1.shape[1]
    w1 = w1.reshape(CMP_LEN, HEAD_DIM, hid)
    z1 = jnp.zeros_like(w1)
    w1x = jnp.concatenate([jnp.concatenate([w1, z1], axis=2), jnp.concatenate([z1, w1], axis=2)], axis=1)
    z2 = jnp.zeros_like(w2)
    w2x = jnp.concatenate([jnp.concatenate([w2, z2], axis=1), jnp.concatenate([z2, w2], axis=1)], axis=0)
    return jnp.tile(pos, (1, N_KV)), w1x.astype(BF16), w2x.astype(BF16)


def _compress(kc, vc, pos_k, wk1, wk2, pos_v, wv1, wv2, batch, seq):
    nh = seq // CMP_STRIDE
    wide = N_KV * HEAD_DIM
    pk, wk1x, wk2x = _cmp_weights(pos_k, wk1, wk2)
    pv, wv1x, wv2x = _cmp_weights(pos_v, wv1, wv2)
    full = lambda a: pl.BlockSpec(a.shape, lambda b: (0,) * a.ndim)
    seqs = pl.BlockSpec((None, seq, wide), lambda b: (b, 0, 0))
    outs = pl.BlockSpec((None, nh, wide), lambda b: (b, 0, 0))
    return pl.pallas_call(
        _cmp_kernel,
        grid=(batch,),
        in_specs=[seqs, seqs, full(pk), full(pv), full(wk1x), full(wk2x), full(wv1x), full(wv2x)],
        out_specs=[outs, outs],
        out_shape=[jax.ShapeDtypeStruct((batch, nh, wide), BF16)] * 2,
        scratch_shapes=[pltpu.VMEM((nh, wide), F32)],
        compiler_params=_cparams(1),
        name="nsa_compress",
    )(kc.reshape(batch, seq, wide), vc.reshape(batch, seq, wide), pk, pv, wk1x, wk2x, wv1x, wv2x)


def _attn_kernel(q_ref, kcmp_ref, vcmp_ref, ksa_ref, vsa_ref, kw_ref, vwa_ref, gate_ref, o_ref,
                 qa_ref, part_ref, acc_ref, m_ref, accw_ref, mw_ref, sa_ref, sb_ref):
    rows = GQA * Q_BLOCK
    reps = KEY_TILE // LANES
    qb = pl.program_id(1)
    s0 = qb * Q_BLOCK
    kd = s0 // KEY_TILE
    off_d = pl.multiple_of(kd * KEY_TILE, KEY_TILE)
    off_p = pl.multiple_of(jnp.maximum(kd - 1, 0) * KEY_TILE, KEY_TILE)
    d0 = s0 - off_d

    lane = lax.broadcasted_iota(jnp.int32, (Q_BLOCK, LANES), 1)
    sub = lax.broadcasted_iota(jnp.int32, (Q_BLOCK, LANES), 0)
    r_q = lax.broadcasted_iota(jnp.int32, (Q_BLOCK, KEY_TILE), 0)
    c_k = lax.broadcasted_iota(jnp.int32, (Q_BLOCK, KEY_TILE), 1)
    rel = c_k - r_q
    gate = gate_ref[...]

    def tile_rows(plane):
        return _rep(plane, GQA, 0)

    def flash_step(acc, m, kv, s, v_tile):
        m_old = m[kv]
        m_new = jnp.maximum(m_old, jnp.broadcast_to(jnp.max(s, axis=-1, keepdims=True), m_old.shape))
        p = jnp.exp2(s - _rep(m_new, reps, 1))
        acc[kv] = acc[kv] * _rep(jnp.exp2(m_old - m_new), 2, 1) + _dot(p.astype(BF16), v_tile)
        m[kv] = m_new

    def flash_result(acc, kv):
        a = acc[kv]
        return a[:, 0:LANES] / a[:, LANES:2 * LANES]

    q128 = [jnp.concatenate([q_ref[:, (kv * GQA + g) * LANES:(kv * GQA + g + 1) * LANES] for g in range(GQA)], axis=0)
            for kv in range(N_KV)]

    kcmp = kcmp_ref[...]
    vcmp = vcmp_ref[...]
    cmp_end = SLC_LEN * (c_k & (N_SLC - 1)) + CMP_STRIDE * (c_k >> 7) + (CMP_LEN - 1)
    cmp_bias = tile_rows(jnp.where(cmp_end - r_q <= s0, 0.0, NEG))
    has_cmp = tile_rows(jnp.where(s0 + sub >= CMP_LEN - 1, 1.0, 0.0))
    o_cmp, imp_t = [], []
    for kv in range(N_KV):
        s_c = _dot_nt(q128[kv], kcmp) + cmp_bias
        mb = jnp.broadcast_to(jnp.max(s_c, axis=-1, keepdims=True), (rows, LANES))
        e = jnp.exp2(s_c - _rep(mb, reps, 1))
        lb = jnp.broadcast_to(jnp.sum(e, axis=-1, keepdims=True), (rows, LANES))
        inv = has_cmp / lb
        o_cmp.append(_dot(e.astype(BF16), vcmp) * inv)
        e4 = (e[:, 0:N_SLC] + e[:, N_SLC:2 * N_SLC] + e[:, 2 * N_SLC:3 * N_SLC] + e[:, 3 * N_SLC:4 * N_SLC]) * inv
        imp = e4[0:Q_BLOCK] + e4[Q_BLOCK:2 * Q_BLOCK] + e4[2 * Q_BLOCK:3 * Q_BLOCK] + e4[3 * Q_BLOCK:4 * Q_BLOCK]
        imp_t.append(imp.T)

    accw_ref[...] = jnp.zeros_like(accw_ref)
    mw_ref[...] = jnp.full_like(mw_ref, NEG)
    prev_bias = tile_rows(jnp.where(rel > jnp.where(kd > 0, d0, KEY_TILE), 0.0, NEG))
    diag_bias = tile_rows(jnp.where(rel <= d0, 0.0, NEG))
    kw_p, vw_p = kw_ref[pl.ds(off_p, KEY_TILE), :], vwa_ref[pl.ds(off_p, KEY_TILE), :]
    kw_d, vw_d = kw_ref[pl.ds(off_d, KEY_TILE), :], vwa_ref[pl.ds(off_d, KEY_TILE), :]
    for kv in range(N_KV):
        flash_step(accw_ref, mw_ref, kv, _dot_nt(q128[kv], kw_p) + prev_bias, vw_p)
    for kv in range(N_KV):
        flash_step(accw_ref, mw_ref, kv, _dot_nt(q128[kv], kw_d) + diag_bias, vw_d)
    for kv in range(N_KV):
        o_win = flash_result(accw_ref, kv)
        for g in range(GQA):
            col = 3 * (kv * GQA + g)
            rs = slice(g * Q_BLOCK, (g + 1) * Q_BLOCK)
            part_ref[kv, rs, :] = gate[:, col:col + 1] * o_cmp[kv][rs] + gate[:, col + 2:col + 3] * o_win[rs]

    t_q = s0 + lane
    forced = (sub == 0) | (sub == (t_q >> 6))
    val = [jnp.where(forced, FORCE, jnp.where(sub * SLC_LEN <= t_q, imp_t[kv], -1.0)) for kv in range(N_KV)]
    sel_t = [jnp.zeros((N_SLC, Q_BLOCK), F32) for _ in range(N_KV)]
    sub_f = sub.astype(F32)
    for _ in range(SLC_TOPN):
        for kv in range(N_KV):
            mx = jnp.max(val[kv], axis=0, keepdims=True)
            first = jnp.min(jnp.where(val[kv] == mx, sub_f, float(N_SLC)), axis=0, keepdims=True)
            pick = sub_f == first
            sel_t[kv] = jnp.where(pick, 1.0, sel_t[kv])
            val[kv] = jnp.where(pick, -jnp.inf, val[kv])
    for kv in range(N_KV):
        bias = ((sel_t[kv].T - 1.0) * (-NEG)).astype(BF16)
        qa_ref[kv] = jnp.concatenate([q128[kv], jnp.concatenate([bias] * GQA, axis=0)], axis=1)

    acc_ref[...] = jnp.zeros_like(acc_ref)
    m_ref[...] = jnp.full_like(m_ref, NEG)

    def scores_into(dst, kt):
        k_t = ksa_ref[pl.ds(pl.multiple_of(kt * KEY_TILE, KEY_TILE), KEY_TILE), :]
        for kv in range(N_KV):
            dst[kv] = _dot_nt(qa_ref[kv], k_t)

    def consume(src, kt, bias):
        v_t = vsa_ref[pl.ds(pl.multiple_of(kt * KEY_TILE, KEY_TILE), KEY_TILE), :]
        for kv in range(N_KV):
            flash_step(acc_ref, m_ref, kv, src[kv] if bias is None else src[kv] + bias, v_t)

    n_pairs = kd // 2
    scores_into(sa_ref, 0)

    def slc_pair(j, carry):
        a = 2 * j
        scores_into(sb_ref, a + 1)
        consume(sa_ref, a, None)
        scores_into(sa_ref, a + 2)
        consume(sb_ref, a + 1, None)
        return carry

    lax.fori_loop(0, n_pairs, slc_pair, 0)
    left = 2 * n_pairs
    scores_into(sb_ref, kd)
    consume(sa_ref, left, jnp.where(left < kd, 0.0, NEG))
    consume(sb_ref, kd, tile_rows(jnp.where(rel <= d0, 0.0, NEG)))

    for kv in range(N_KV):
        o_slc = flash_result(acc_ref, kv)
        pieces = []
        for g in range(GQA):
            col = 3 * (kv * GQA + g)
            rs = slice(g * Q_BLOCK, (g + 1) * Q_BLOCK)
            pieces.append(part_ref[kv, rs, :] + gate[:, col + 1:col + 2] * o_slc[rs])
        for mm in range(2):
            even, odd = pieces[2 * mm], pieces[2 * mm + 1]
            if kv == 0:
                odd = pltpu.roll(odd, 64, 1)
            else:
                even = pltpu.roll(even, 64, 1)
            slab = kv * 2 + mm
            o_ref[:, slab * LANES:(slab + 1) * LANES] = jnp.where(lane < 64, even, odd)


def _attention(q, kcmp, vcmp, ksa, vsa, kw, vwa, gate, batch, seq):
    nh = kcmp.shape[1]
    assert nh == KEY_TILE
    rows = GQA * Q_BLOCK
    res = lambda w: pl.BlockSpec((None, seq, w), lambda b, i: (b, 0, 0))
    blk = lambda w: pl.BlockSpec((None, Q_BLOCK, w), lambda b, i: (b, i, 0))
    r3 = lambda a, w: a.reshape(batch, seq, w)
    return pl.pallas_call(
        _attn_kernel,
        grid=(batch, seq // Q_BLOCK),
        in_specs=[blk(N_HEADS * LANES),
                  pl.BlockSpec((None, nh, LANES), lambda b, i: (b, 0, 0)),
                  pl.BlockSpec((None, nh, LANES), lambda b, i: (b, 0, 0)),
                  res(2 * LANES), res(2 * LANES), res(LANES), res(2 * LANES), blk(LANES)],
        out_specs=blk(D_ATT),
        out_shape=jax.ShapeDtypeStruct((batch, seq, D_ATT), F32),
        scratch_shapes=[pltpu.VMEM((N_KV, rows, 2 * LANES), BF16), pltpu.VMEM((N_KV, rows, LANES), F32),
                        pltpu.VMEM((N_KV, rows, 2 * LANES), F32), pltpu.VMEM((N_KV, rows, LANES), F32),
                        pltpu.VMEM((N_KV, rows, 2 * LANES), F32), pltpu.VMEM((N_KV, rows, LANES), F32),
                        pltpu.VMEM((N_KV, rows, KEY_TILE), F32), pltpu.VMEM((N_KV, rows, KEY_TILE), F32)],
        compiler_params=_cparams(2),
        name="nsa_attention",
    )(r3(q, N_HEADS * LANES), kcmp, vcmp, r3(ksa, 2 * LANES), r3(vsa, 2 * LANES), r3(kw, LANES), r3(vwa, 2 * LANES),
      r3(gate, LANES))


def _outproj_kernel(x_ref, yl_ref, ya_ref, yc_ref, gn_ref, w_ref, mod_ref, n2_ref, rw_ref, rb_ref,
                    xn_ref, h2_ref, comb_ref, pos_ref, seg_ref):
    tm = x_ref.shape[0]
    a, b = D_LRU, D_LRU + D_ATT
    yl = _rms(yl_ref[...], gn_ref[:, 0:a]).astype(BF16)
    ya = _rms(ya_ref[...], gn_ref[:, a:b]).astype(BF16)
    yc = _rms(yc_ref[...], gn_ref[:, b:D_MODEL]).astype(BF16)
    y = _dot(yl, w_ref[0:a, :]) + _dot(ya, w_ref[a:b, :]) + _dot(yc, w_ref[b:D_MODEL, :])
    xn = x_ref[...] + mod_ref[2:3, :] * y
    xn_ref[...] = xn
    h2 = _rms(xn, n2_ref[...]) * (1.0 + mod_ref[4:5, :]) + mod_ref[3:4, :]

    h_hi = h2.astype(BF16)
    h_mid = (h2 - h_hi.astype(F32)).astype(BF16)
    logit = _dot(h_hi, rw_ref[0]) + (_dot(h_hi, rw_ref[1]) + _dot(h_mid, rw_ref[0])) + rb_ref[...]
    lane = lax.broadcasted_iota(jnp.int32, (tm, LANES), 1)
    ninf = -jnp.inf
    is_g = lane < MOE_GROUPS
    is_e = (lane >= MOE_GROUPS) & (lane < MOE_GROUPS + MOE_GROUPS * MOE_EXPERTS)
    lg_max = jnp.max(jnp.where(is_g, logit, ninf), axis=-1, keepdims=True)
    g_star = jnp.min(jnp.where(is_g & (logit == lg_max), lane, LANES), axis=-1, keepdims=True)
    pg_star = 1.0 / jnp.sum(jnp.where(is_g, jnp.exp(logit - lg_max), 0.0), axis=-1, keepdims=True)
    in_grp = is_e & (((lane - MOE_GROUPS) >> 2) == g_star)
    v1 = jnp.max(jnp.where(in_grp, logit, ninf), axis=-1, keepdims=True)
    i1 = jnp.min(jnp.where(in_grp & (logit == v1), lane, LANES), axis=-1, keepdims=True)
    rest = in_grp & (lane != i1)
    v2 = jnp.max(jnp.where(rest, logit, ninf), axis=-1, keepdims=True)
    i2 = jnp.min(jnp.where(rest & (logit == v2), lane, LANES), axis=-1, keepdims=True)
    d = jnp.exp(v2 - v1)
    pe1 = 1.0 / (1.0 + d)
    pe2 = d / (1.0 + d)
    comb = jnp.where(lane == i1, pe1, jnp.where(lane == i2, pe2, 0.0)) * pg_star

    onehot = jnp.where(lane == g_star, 1.0, 0.0)
    r_i = lax.broadcasted_iota(jnp.int32, (tm, tm), 0)
    c_i = lax.broadcasted_iota(jnp.int32, (tm, tm), 1)
    earlier = jnp.where(c_i < r_i, 1.0, 0.0).astype(BF16)
    rank = jnp.sum(onehot * _dot(earlier, onehot.astype(BF16)), axis=-1, keepdims=True)
    n_g = jnp.sum(onehot, axis=0, keepdims=True)
    lane1 = lax.broadcasted_iota(jnp.int32, (1, LANES), 1)
    start = jnp.zeros((1, LANES), F32)
    below = jnp.zeros((1, 1), F32)
    for g in range(1, MOE_GROUPS):
        below = below + jnp.sum(jnp.where(lane1 == g - 1, n_g, 0.0), axis=-1, keepdims=True)
        start = start + jnp.where(lane1 == g, below, 0.0)
    pos = jnp.sum(onehot * start, axis=-1, keepdims=True) + rank
    pos_rep = jnp.broadcast_to(pos, (tm, LANES))
    take_t = jnp.where(_rep(pos_rep, tm // LANES, 1) == c_i.astype(F32), 1.0, 0.0)
    take = take_t.T
    take = take.astype(BF16)
    h2_ref[...] = _dot(take, h_hi).astype(BF16)
    c_hi = comb.astype(BF16)
    c_rest = comb - c_hi.astype(F32)
    c_mid = c_rest.astype(BF16)
    c_lo = (c_rest - c_mid.astype(F32)).astype(BF16)
    comb_ref[...] = (_dot(take, c_hi) + _dot(take, c_mid)) + _dot(take, c_lo)
    pos_ref[...] = pos_rep
    seg_ref[...] = jnp.zeros(seg_ref.shape, jnp.int32)
    seg_ref[0:1, :] = n_g.astype(jnp.int32)
    seg_ref[1:2, :] = start.astype(jnp.int32)


def _outproj(x2, yl, ya, yc, gn, w_out, mod_l, n2g, rw, rb, seq):
    t, d = x2.shape
    tps = seq // TM
    row = lambda w: pl.BlockSpec((TM, w), lambda i: (i, 0))
    full = lambda a: pl.BlockSpec(a.shape, lambda i: (0,) * a.ndim)
    return pl.pallas_call(
        _outproj_kernel,
        grid=(t // TM,),
        in_specs=[row(d), row(D_LRU), row(D_ATT), row(D_CONV), full(gn), full(w_out),
                  pl.BlockSpec((None, 6, d), lambda i: (i // tps, 0, 0)), full(n2g), full(rw), full(rb)],
        out_specs=[row(d), row(d), row(LANES), row(LANES), pl.BlockSpec((None, 8, LANES), lambda i: (i, 0, 0))],
        out_shape=[jax.ShapeDtypeStruct((t, d), F32), jax.ShapeDtypeStruct((t, d), BF16),
                   jax.ShapeDtypeStruct((t, LANES), F32), jax.ShapeDtypeStruct((t, LANES), F32),
                   jax.ShapeDtypeStruct((t // TM, 8, LANES), jnp.int32)],
        compiler_params=_cparams(1),
        name="out_proj_router",
    )(x2, yl.reshape(t, D_LRU), ya.reshape(t, D_ATT), yc.reshape(t, D_CONV), gn, w_out, mod_l, n2g, rw, rb)


def _moe_kernel(seg_ref, h2_ref, comb_ref, pos_ref, wg_ref, wu_ref, wd_ref, xn_ref, mod_ref, fg_ref, o_ref, acc_ref,
                *, final):
    chunks = h2_ref.shape[0] // TM
    i = pl.program_id(0)
    e = pl.program_id(1)
    grp = e // MOE_EXPERTS

    @pl.when(e == 0)
    def _():
        acc_ref[...] = jnp.zeros_like(acc_ref)

    lane = lax.broadcasted_iota(jnp.int32, (MOE_SUB, LANES), 1)
    for ck in range(chunks):
        base = (i * chunks + ck) * 8
        first = seg_ref[base + MOE_GROUPS + grp]
        last = first + seg_ref[base + grp]
        for j in range(TM // MOE_SUB):
            @pl.when((first < (j + 1) * MOE_SUB) & (last > j * MOE_SUB))
            def _(ck=ck, j=j):
                rs = slice(ck * TM + j * MOE_SUB, ck * TM + (j + 1) * MOE_SUB)
                h2 = h2_ref[rs, :]
                hid = jax.nn.silu(_dot(h2, wg_ref[...])) * _dot(h2, wu_ref[...])
                c = jnp.sum(jnp.where(lane == e + MOE_GROUPS, comb_ref[rs, :], 0.0), axis=-1, keepdims=True)
                acc_ref[rs, :] += _dot((hid * c).astype(BF16), wd_ref[...])

    @pl.when(e == pl.num_programs(1) - 1)
    def _():
        slot = lax.broadcasted_iota(jnp.int32, (TM, TM), 1).astype(F32)
        for ck in range(chunks):
            rs = slice(ck * TM, (ck + 1) * TM)
            y = acc_ref[rs, :]
            hi = y.astype(BF16)
            rest = y - hi.astype(F32)
            mid = rest.astype(BF16)
            lo = (rest - mid.astype(F32)).astype(BF16)
            take_t = jnp.where(_rep(pos_ref[rs, :], TM // LANES, 1) == slot, 1.0, 0.0).astype(BF16)
            y_tok = (_dot(take_t, hi) + _dot(take_t, mid)) + _dot(take_t, lo)
            xo = xn_ref[rs, :] + mod_ref[5:6, :] * y_tok
            if final:
                xo = _rms(xo, fg_ref[...])
            o_ref[rs, :] = xo


def _moe(seg, h2, comb, pos, wg, wu, wd, xn, mod_l, fg, seq, final):
    t, d = xn.shape
    ne = wg.shape[0]
    tps = seq // MOE_TILE
    grid_spec = pltpu.PrefetchScalarGridSpec(
        num_scalar_prefetch=1,
        grid=(t // MOE_TILE, ne),
        in_specs=[pl.BlockSpec((MOE_TILE, d), lambda i, e, s: (i, 0)),
                  pl.BlockSpec((MOE_TILE, LANES), lambda i, e, s: (i, 0)),
                  pl.BlockSpec((MOE_TILE, LANES), lambda i, e, s: (i, 0)),
                  pl.BlockSpec((None, d, MOE_HID), lambda i, e, s: (e, 0, 0)),
                  pl.BlockSpec((None, d, MOE_HID), lambda i, e, s: (e, 0, 0)),
                  pl.BlockSpec((None, MOE_HID, d), lambda i, e, s: (e, 0, 0)),
                  pl.BlockSpec((MOE_TILE, d), lambda i, e, s: (i, 0)),
                  pl.BlockSpec((None, 6, d), lambda i, e, s: (i // tps, 0, 0)),
                  pl.BlockSpec((1, d), lambda i, e, s: (0, 0))],
        out_specs=pl.BlockSpec((MOE_TILE, d), lambda i, e, s: (i, 0)),
        scratch_shapes=[pltpu.VMEM((MOE_TILE, d), F32)],
    )
    return pl.pallas_call(
        functools.partial(_moe_kernel, final=final),
        grid_spec=grid_spec,
        out_shape=jax.ShapeDtypeStruct((t, d), F32),
        compiler_params=_cparams(2),
        name="moe_experts",
    )(seg, h2, comb, pos, wg, wu, wd, xn, mod_l, fg)


def _permute_w_in(w):
    gate0 = D_LRU * 2 + D_ATT + 6 * LANES
    cv0 = gate0 + 3 * N_HEADS
    pad = jnp.zeros((w.shape[0], IN_PAD - w.shape[1]), w.dtype)
    return jnp.concatenate([w[:, :gate0], w[:, cv0:], w[:, gate0:cv0], pad], axis=1).astype(BF16)


def _router_weights(rg_w, rg_b, re_w, re_b):
    d = rg_w.shape[0]
    ne = MOE_GROUPS * MOE_EXPERTS
    w = jnp.concatenate([rg_w, jnp.transpose(re_w, (1, 0, 2)).reshape(d, ne),
                         jnp.zeros((d, LANES - MOE_GROUPS - ne), F32)], axis=1)
    b = jnp.concatenate([rg_b, re_b.reshape(ne), jnp.zeros((LANES - MOE_GROUPS - ne,), F32)])[None, :]
    w_hi = w.astype(BF16)
    w_mid = (w - w_hi.astype(F32)).astype(BF16)
    return jnp.stack([w_hi, w_mid]), b


def kernel(x, c, positions, ada_w, ada_b, norm1_g, norm2_g, w_in, lru_conv_w, lru_conv_b, lru_wa, lru_ba, lru_wx, lru_bx, lru_lambda, cmp_k_w1, cmp_k_w2, cmp_v_w1, cmp_v_w2, cmp_pos_k, cmp_pos_v, cnv_dw_w, cnv_dw_b, cnv_ln_g, cnv_ln_b, out_norm_g, w_out, moe_rg_w, moe_rg_b, moe_re_w, moe_re_b, moe_w_gate, moe_w_up, moe_w_down, final_norm_g):
    batch, seq, d = x.shape
    depth = ada_w.shape[0]
    assert d == D_MODEL and seq == N_SLC * SLC_LEN and seq % KEY_TILE == 0
    t = batch * seq
    ne = MOE_GROUPS * MOE_EXPERTS

    cos, sin = _rope_tables(positions)
    mod = _modulation(c, ada_w, ada_b)
    x2 = x.reshape(t, d)
    fg = final_norm_g[None, :]
    for l in range(depth):
        (xl, gl, q, kc, vc, ksa, vsa, kw, vwa, gate, cv) = _inproj(
            x2, mod[l], norm1_g[l][None, :], _permute_w_in(w_in[l]), cos, sin, seq)
        y_lru, y_cnv = _mixer(xl, gl, cv, lru_conv_w[l], lru_conv_b[l], lru_wa[l], lru_ba[l], lru_wx[l], lru_bx[l],
                              lru_lambda[l], cnv_dw_w[l], cnv_dw_b[l], cnv_ln_g[l], cnv_ln_b[l], batch, seq)
        kcmp, vcmp = _compress(kc, vc, cmp_pos_k[l], cmp_k_w1[l], cmp_k_w2[l], cmp_pos_v[l], cmp_v_w1[l], cmp_v_w2[l],
                               batch, seq)
        y_att = _attention(q, kcmp, vcmp, ksa, vsa, kw, vwa, gate, batch, seq)
        rw, rb = _router_weights(moe_rg_w[l], moe_rg_b[l], moe_re_w[l], moe_re_b[l])
        xn, h2, comb, pos, seg = _outproj(x2, y_lru, y_att, y_cnv, out_norm_g[l][None, :], w_out[l].astype(BF16),
                                          mod[l], norm2_g[l][None, :], rw, rb, seq)
        seg = jnp.concatenate([seg[:, 0, :MOE_GROUPS], seg[:, 1, :MOE_GROUPS]], axis=1).reshape(-1)
        x2 = _moe(seg, h2, comb, pos,
                  moe_w_gate[l].reshape(ne, d, MOE_HID).astype(BF16),
                  moe_w_up[l].reshape(ne, d, MOE_HID).astype(BF16),
                  moe_w_down[l].reshape(ne, MOE_HID, d).astype(BF16),
                  xn, mod[l], fg, seq, final=(l == depth - 1))
    return x2.reshape(batch, seq, d)
```

```python
import functools

import jax
import jax.numpy as jnp
from jax import lax
from jax.experimental import pallas as pl
from jax.experimental.pallas import tpu as pltpu

F32 = jnp.float32
BF16 = jnp.bfloat16

D_MODEL = 1024
D_LRU = 256
D_ATT = 512
D_CONV = 256
LRU_BLOCKS = 4
LRU_CONV_W = 4
LRU_C = 8.0
HEAD_DIM = 64
N_HEADS = 8
N_KV = 2
GQA = 4
ROPE_THETA = 10000.0
CMP_LEN = 32
CMP_STRIDE = 16
SLC_LEN = 64
SLC_TOPN = 16
WINDOW = 512
CONV_K = 31
MOE_GROUPS = 4
MOE_EXPERTS = 4
MOE_HID = 512
EPS = 1e-6
NEG = -1e30
FORCE = 1e9
LOG2_E = 1.4426950408889634

LANES = 128
Q_BLOCK = 128
KEY_TILE = 512
N_SLC = 128
TM = 512
MOE_ALIGN_LOG2 = 4
MOE_ALIGN = 1 << MOE_ALIGN_LOG2
MOE_SLOTS = 640
MOE_PIECES = tuple(1 << b for b in range(9, MOE_ALIGN_LOG2 - 1, -1))
TC = 512
SCAN_ROWS = 64
LRU_TAIL = 8
CNV_TAIL = 32
IN_PAD = 2432
VMEM_LIMIT = 56 * 1024 * 1024


def _cparams(n_axes, vmem=VMEM_LIMIT):
    return pltpu.CompilerParams(dimension_semantics=("arbitrary",) * n_axes, vmem_limit_bytes=vmem)


def _dot(a, b):
    return jnp.dot(a, b, preferred_element_type=F32)


def _dot_nt(a, b):
    return lax.dot_general(a, b, (((1,), (1,)), ((), ())), preferred_element_type=F32)


def _rep(v, n, axis):
    return jnp.concatenate([v] * n, axis=axis)


def _rms(v, g):
    return v * lax.rsqrt(jnp.mean(v * v, axis=-1, keepdims=True) + EPS) * g


def _rope_kernel(pos_ref, inv_ref, sign_ref, cos_ref, sin_ref):
    ang = pos_ref[...].astype(F32) * inv_ref[...]
    cos_ref[...] = jnp.cos(ang)
    sin_ref[...] = jnp.sin(ang) * sign_ref[...]


def _rope_tables(positions):
    t = positions.size
    inv = ROPE_THETA ** (-jnp.arange(0, HEAD_DIM, 2, dtype=F32) / HEAD_DIM)
    inv128 = jnp.tile(inv, 4)[None, :]
    sign128 = jnp.tile(jnp.concatenate([-jnp.ones((32,), F32), jnp.ones((32,), F32)]), 2)[None, :]
    tr = 1024
    return pl.pallas_call(
        _rope_kernel,
        grid=(t // tr,),
        in_specs=[pl.BlockSpec((tr, 1), lambda i: (i, 0)),
                  pl.BlockSpec((1, LANES), lambda i: (0, 0)),
                  pl.BlockSpec((1, LANES), lambda i: (0, 0))],
        out_specs=[pl.BlockSpec((tr, LANES), lambda i: (i, 0))] * 2,
        out_shape=[jax.ShapeDtypeStruct((t, LANES), F32)] * 2,
        compiler_params=_cparams(1),
        name="rope_tables",
    )(positions.reshape(t, 1), inv128, sign128)


def _mod_kernel(c_ref, w_ref, b_ref, o_ref):
    sc = jax.nn.silu(c_ref[...])
    o_ref[...] = _dot(sc.astype(BF16), w_ref[...].astype(BF16)) + b_ref[...]


def _modulation(c, ada_w, ada_b):
    nl, d, n6 = ada_w.shape
    b = c.shape[0]
    rows = 16
    cp = jnp.zeros((rows, d), F32).at[:b].set(c)
    tn = 1536
    out = pl.pallas_call(
        _mod_kernel,
        grid=(nl, n6 // tn),
        in_specs=[pl.BlockSpec((rows, d), lambda l, j: (0, 0)),
                  pl.BlockSpec((None, d, tn), lambda l, j: (l, 0, j)),
                  pl.BlockSpec((None, 1, tn), lambda l, j: (l, 0, j))],
        out_specs=pl.BlockSpec((None, rows, tn), lambda l, j: (l, 0, j)),
        out_shape=jax.ShapeDtypeStruct((nl, rows, n6), F32),
        compiler_params=_cparams(2),
        name="adaln_mod",
    )(cp, ada_w, ada_b.reshape(nl, 1, n6))
    return out[:, :b].reshape(nl, b, 6, d)


def _inproj_kernel(x_ref, mod_ref, g_ref, w_ref, cos_ref, sin_ref,
                   xl_ref, gl_ref, q_ref, kc_ref, vc_ref, ksa_ref, vsa_ref, kw_ref, vwa_ref, gate_ref, cv_ref,
                   *, tiles_per_seq):
    tm = x_ref.shape[0]
    x = x_ref[...]
    h = _rms(x, g_ref[...]) * (1.0 + mod_ref[1:2, :]) + mod_ref[0:1, :]
    p = _dot(h.astype(BF16), w_ref[...])

    cos = cos_ref[...]
    sin = sin_ref[...]
    lane = lax.broadcasted_iota(jnp.int32, (tm, LANES), 1)
    first_half = (lane & 63) < 32

    def rope(v):
        rot = jnp.where(first_half, pltpu.roll(v, 96, 1), pltpu.roll(v, 32, 1))
        return v * cos + rot * sin

    xl_ref[...] = p[:, 0:256]
    gl_ref[...] = p[:, 256:512]
    scale = HEAD_DIM ** -0.5 * LOG2_E
    low = lane < 64
    for m in range(4):
        slab = rope(p[:, 512 + 128 * m:640 + 128 * m]) * scale
        swapped = pltpu.roll(slab, 64, 1)
        for hh in range(2):
            head = 2 * m + hh
            kv = head // GQA
            src = slab if hh == kv else swapped
            keep = low if kv == 0 else jnp.logical_not(low)
            q_ref[:, head * LANES:(head + 1) * LANES] = jnp.where(keep, src, 0.0).astype(BF16)
    kc_ref[...] = rope(p[:, 1024:1152])
    vc_ref[...] = p[:, 1152:1280]
    row = lax.broadcasted_iota(jnp.int32, (tm, LANES), 0)
    s_base = (pl.program_id(0) % tiles_per_seq) * tm
    onehot = jnp.where(lane == ((s_base + row) >> 6), 1.0, 0.0).astype(BF16)
    ones = jnp.ones((tm, LANES), BF16)
    ksa_ref[:, 0:LANES] = rope(p[:, 1280:1408]).astype(BF16)
    ksa_ref[:, LANES:2 * LANES] = onehot
    vsa_ref[:, 0:LANES] = p[:, 1408:1536].astype(BF16)
    vsa_ref[:, LANES:2 * LANES] = ones
    kw_ref[...] = rope(p[:, 1536:1664]).astype(BF16)
    vwa_ref[:, 0:LANES] = p[:, 1664:1792].astype(BF16)
    vwa_ref[:, LANES:2 * LANES] = ones
    cv_ref[...] = p[:, 1792:2304]
    gate_ref[...] = jax.nn.sigmoid(p[:, 2304:2432])


def _inproj(x2, mod_l, g, w_p, cos, sin, seq):
    t, d = x2.shape
    tps = seq // TM
    row = lambda w: pl.BlockSpec((TM, w), lambda i: (i, 0))
    outs = [(256, F32), (256, F32), (N_HEADS * LANES, BF16), (LANES, F32), (LANES, F32), (2 * LANES, BF16),
            (2 * LANES, BF16), (LANES, BF16), (2 * LANES, BF16), (LANES, F32), (2 * D_CONV, F32)]
    return pl.pallas_call(
        functools.partial(_inproj_kernel, tiles_per_seq=tps),
        grid=(t // TM,),
        in_specs=[row(d),
                  pl.BlockSpec((None, 6, d), lambda i: (i // tps, 0, 0)),
                  pl.BlockSpec((1, d), lambda i: (0, 0)),
                  pl.BlockSpec((d, IN_PAD), lambda i: (0, 0)),
                  row(LANES), row(LANES)],
        out_specs=[row(w) for w, _ in outs],
        out_shape=[jax.ShapeDtypeStruct((t, w), dt) for w, dt in outs],
        compiler_params=_cparams(1),
        name="in_proj",
    )(x2, mod_l, g, w_p, cos, sin)


def _shift_rows(v, s, fill, row):
    return jnp.where(row < s, fill, pltpu.roll(v, s, 0))


def _causal_taps(ext, w_ref, bias, width, tail, tc):
    acc = bias
    for b in range(min(8, width)):
        shifted = pltpu.roll(ext, b, 0) if b else ext
        for k in range(width):
            back = width - 1 - k
            if back % 8 == b:
                start = tail - (back - b)
                acc = acc + w_ref[k:k + 1, :] * shifted[start:start + tc, :]
    return acc


def _mixer_kernel(xl_ref, gl_ref, cv_ref, lcw_ref, lcb_ref, wa_ref, ba_ref, wx_ref, bx_ref, lam_ref,
                  cw_ref, cb_ref, lng_ref, lnb_ref, ylru_ref, ycnv_ref,
                  extl, extc, abuf, ubuf, hcar):
    tc = xl_ref.shape[0]

    @pl.when(pl.program_id(1) == 0)
    def _():
        extl[0:LRU_TAIL, :] = jnp.zeros((LRU_TAIL, D_LRU), F32)
        extc[0:CNV_TAIL, :] = jnp.zeros((CNV_TAIL, D_CONV), F32)
        hcar[...] = jnp.zeros_like(hcar)

    xl = xl_ref[...]
    extl[LRU_TAIL:LRU_TAIL + tc, :] = xl
    xc = _causal_taps(extl[...], lcw_ref, lcb_ref[...], LRU_CONV_W, LRU_TAIL, tc)
    extl[0:LRU_TAIL, :] = xl[tc - LRU_TAIL:tc, :]
    xcb = xc.astype(BF16)
    r = jax.nn.sigmoid(_dot(xcb, wa_ref[...]) + ba_ref[...])
    gi = jax.nn.sigmoid(_dot(xcb, wx_ref[...]) + bx_ref[...])
    log_a = LRU_C * r * jax.nn.log_sigmoid(lam_ref[...])
    th = jnp.tanh(log_a)
    one_minus_a2 = -2.0 * th / (1.0 - th)
    abuf[...] = jnp.exp(log_a)
    ubuf[...] = jnp.sqrt(one_minus_a2) * (gi * xc)

    row = lax.broadcasted_iota(jnp.int32, (SCAN_ROWS, D_LRU), 0)

    def scan_chunk(c, h_prev):
        off = pl.multiple_of(c * SCAN_ROWS, SCAN_ROWS)
        a = abuf[pl.ds(off, SCAN_ROWS), :]
        b = ubuf[pl.ds(off, SCAN_ROWS), :]
        s = 1
        while s < SCAN_ROWS:
            b = b + a * _shift_rows(b, s, 0.0, row)
            a = a * _shift_rows(a, s, 1.0, row)
            s *= 2
        h = b + a * h_prev
        ylru_ref[pl.ds(off, SCAN_ROWS), :] = h * jax.nn.gelu(gl_ref[pl.ds(off, SCAN_ROWS), :])
        return h[SCAN_ROWS - 1:SCAN_ROWS, :]

    hcar[0:1, :] = lax.fori_loop(0, tc // SCAN_ROWS, scan_chunk, hcar[0:1, :])

    u = cv_ref[...]
    v = u[:, 0:D_CONV] * jax.nn.sigmoid(u[:, D_CONV:2 * D_CONV])
    extc[CNV_TAIL:CNV_TAIL + tc, :] = v
    acc = _causal_taps(extc[...], cw_ref, cb_ref[...], CONV_K, CNV_TAIL, tc)
    extc[0:CNV_TAIL, :] = v[tc - CNV_TAIL:tc, :]
    mu = jnp.mean(acc, axis=-1, keepdims=True)
    var = jnp.mean(jnp.square(acc - mu), axis=-1, keepdims=True)
    y = (acc - mu) * lax.rsqrt(var + EPS) * lng_ref[...] + lnb_ref[...]
    ycnv_ref[...] = jax.nn.silu(y)


def _block_diag(w):
    n, bw, _ = w.shape
    out = jnp.zeros((n * bw, n * bw), w.dtype)
    for i in range(n):
        out = out.at[i * bw:(i + 1) * bw, i * bw:(i + 1) * bw].set(w[i])
    return out


def _mixer(xl, gl, cv, lcw, lcb, wa, ba, wx, bx, lam, cw, cb, lng, lnb, batch, seq):
    seqrow = lambda w: pl.BlockSpec((None, TC, w), lambda b, t: (b, t, 0))
    full = lambda a: pl.BlockSpec(a.shape, lambda b, t: (0,) * a.ndim)
    small = [lcw, lcb[None, :], _block_diag(wa).astype(BF16), ba[None, :], _block_diag(wx).astype(BF16), bx[None, :],
             lam[None, :], jnp.zeros((32, D_CONV), F32).at[:CONV_K].set(cw), cb[None, :], lng[None, :], lnb[None, :]]
    return pl.pallas_call(
        _mixer_kernel,
        grid=(batch, seq // TC),
        in_specs=[seqrow(D_LRU), seqrow(D_LRU), seqrow(2 * D_CONV)] + [full(a) for a in small],
        out_specs=[seqrow(D_LRU), seqrow(D_CONV)],
        out_shape=[jax.ShapeDtypeStruct((batch, seq, D_LRU), F32), jax.ShapeDtypeStruct((batch, seq, D_CONV), F32)],
        scratch_shapes=[pltpu.VMEM((TC + LRU_TAIL, D_LRU), F32), pltpu.VMEM((TC + CNV_TAIL, D_CONV), F32),
                        pltpu.VMEM((TC, D_LRU), F32), pltpu.VMEM((TC, D_LRU), F32), pltpu.VMEM((8, D_LRU), F32)],
        compiler_params=_cparams(2),
        name="mixer_stream",
    )(xl.reshape(batch, seq, D_LRU), gl.reshape(batch, seq, D_LRU), cv.reshape(batch, seq, 2 * D_CONV), *small)


def _cmp_kernel(kc_ref, vc_ref, posk_ref, posv_ref, wk1_ref, wk2_ref, wv1_ref, wv2_ref, ko_ref, vo_ref, nat_ref):
    nh = ko_ref.shape[0]
    quarter = nh // 4

    def one(t_ref, pos_ref, w1_ref, w2_ref, o_ref):
        first = jnp.zeros((nh, w1_ref.shape[2]), F32)
        second = jnp.zeros((nh, w1_ref.shape[2]), F32)
        for l in range(CMP_STRIDE):
            x = t_ref[pl.ds(l, nh, stride=CMP_STRIDE), :]
            first = first + _dot((x + pos_ref[l:l + 1, :]).astype(BF16), w1_ref[l])
            second = second + _dot((x + pos_ref[CMP_STRIDE + l:CMP_STRIDE + l + 1, :]).astype(BF16),
                                   w1_ref[CMP_STRIDE + l])
        hid = jax.nn.gelu(first + pltpu.roll(second, nh - 1, 0))
        nat_ref[...] = _dot(hid.astype(BF16), w2_ref[...])
        for r in range(4):
            o_ref[r * quarter:(r + 1) * quarter, :] = nat_ref[pl.ds(r, quarter, stride=4), :].astype(o_ref.dtype)

    one(kc_ref, posk_ref, wk1_ref, wk2_ref, ko_ref)
    one(vc_ref, posv_ref, wv1_ref, wv2_ref, vo_ref)


def _cmp_weights(pos, w1, w2):
    hid = w1.shape[1]
    w1 = w1.reshape(CMP_LEN, HEAD_DIM, hid)
    z1 = jnp.zeros_like(w1)
    w1x = jnp.concatenate([jnp.concatenate([w1, z1], axis=2), jnp.concatenate([z1, w1], axis=2)], axis=1)
    z2 = jnp.zeros_like(w2)
    w2x = jnp.concatenate([jnp.concatenate([w2, z2], axis=1), jnp.concatenate([z2, w2], axis=1)], axis=0)
    return jnp.tile(pos, (1, N_KV)), w1x.astype(BF16), w2x.astype(BF16)


def _compress(kc, vc, pos_k, wk1, wk2, pos_v, wv1, wv2, batch, seq):
    nh = seq // CMP_STRIDE
    wide = N_KV * HEAD_DIM
    pk, wk1x, wk2x = _cmp_weights(pos_k, wk1, wk2)
    pv, wv1x, wv2x = _cmp_weights(pos_v, wv1, wv2)
    full = lambda a: pl.BlockSpec(a.shape, lambda b: (0,) * a.ndim)
    seqs = pl.BlockSpec((None, seq, wide), lambda b: (b, 0, 0))
    outs = pl.BlockSpec((None, nh, wide), lambda b: (b, 0, 0))
    return pl.pallas_call(
        _cmp_kernel,
        grid=(batch,),
        in_specs=[seqs, seqs, full(pk), full(pv), full(wk1x), full(wk2x), full(wv1x), full(wv2x)],
        out_specs=[outs, outs],
        out_shape=[jax.ShapeDtypeStruct((batch, nh, wide), BF16)] * 2,
        scratch_shapes=[pltpu.VMEM((nh, wide), F32)],
        compiler_params=_cparams(1),
        name="nsa_compress",
    )(kc.reshape(batch, seq, wide), vc.reshape(batch, seq, wide), pk, pv, wk1x, wk2x, wv1x, wv2x)


def _attn_kernel(q_ref, kcmp_ref, vcmp_ref, ksa_ref, vsa_ref, kw_ref, vwa_ref, gate_ref, o_ref,
                 qa_ref, part_ref, acc_ref, m_ref, accw_ref, mw_ref, sa_ref, sb_ref):
    rows = GQA * Q_BLOCK
    reps = KEY_TILE // LANES
    qb = pl.program_id(1)
    s0 = qb * Q_BLOCK
    kd = s0 // KEY_TILE
    off_d = pl.multiple_of(kd * KEY_TILE, KEY_TILE)
    off_p = pl.multiple_of(jnp.maximum(kd - 1, 0) * KEY_TILE, KEY_TILE)
    d0 = s0 - off_d

    lane = lax.broadcasted_iota(jnp.int32, (Q_BLOCK, LANES), 1)
    sub = lax.broadcasted_iota(jnp.int32, (Q_BLOCK, LANES), 0)
    r_q = lax.broadcasted_iota(jnp.int32, (Q_BLOCK, KEY_TILE), 0)
    c_k = lax.broadcasted_iota(jnp.int32, (Q_BLOCK, KEY_TILE), 1)
    rel = c_k - r_q
    gate = gate_ref[...]

    def tile_rows(plane):
        return _rep(plane, GQA, 0)

    def flash_step(acc, m, kv, s, v_tile):
        m_old = m[kv]
        m_new = jnp.maximum(m_old, jnp.broadcast_to(jnp.max(s, axis=-1, keepdims=True), m_old.shape))
        p = jnp.exp2(s - _rep(m_new, reps, 1))
        acc[kv] = acc[kv] * _rep(jnp.exp2(m_old - m_new), 2, 1) + _dot(p.astype(BF16), v_tile)
        m[kv] = m_new

    def flash_result(acc, kv):
        a = acc[kv]
        return a[:, 0:LANES] / a[:, LANES:2 * LANES]

    q128 = [jnp.concatenate([q_ref[:, (kv * GQA + g) * LANES:(kv * GQA + g + 1) * LANES] for g in range(GQA)], axis=0)
            for kv in range(N_KV)]

    kcmp = kcmp_ref[...]
    vcmp = vcmp_ref[...]
    cmp_end = SLC_LEN * (c_k & (N_SLC - 1)) + CMP_STRIDE * (c_k >> 7) + (CMP_LEN - 1)
    cmp_bias = tile_rows(jnp.where(cmp_end - r_q <= s0, 0.0, NEG))
    has_cmp = tile_rows(jnp.where(s0 + sub >= CMP_LEN - 1, 1.0, 0.0))
    o_cmp, imp_t = [], []
    for kv in range(N_KV):
        s_c = _dot_nt(q128[kv], kcmp) + cmp_bias
        mb = jnp.broadcast_to(jnp.max(s_c, axis=-1, keepdims=True), (rows, LANES))
        e = jnp.exp2(s_c - _rep(mb, reps, 1))
        lb = jnp.broadcast_to(jnp.sum(e, axis=-1, keepdims=True), (rows, LANES))
        inv = has_cmp / lb
        o_cmp.append(_dot(e.astype(BF16), vcmp) * inv)
        e4 = (e[:, 0:N_SLC] + e[:, N_SLC:2 * N_SLC] + e[:, 2 * N_SLC:3 * N_SLC] + e[:, 3 * N_SLC:4 * N_SLC]) * inv
        imp = e4[0:Q_BLOCK] + e4[Q_BLOCK:2 * Q_BLOCK] + e4[2 * Q_BLOCK:3 * Q_BLOCK] + e4[3 * Q_BLOCK:4 * Q_BLOCK]
        imp_t.append(imp.T)

    accw_ref[...] = jnp.zeros_like(accw_ref)
    mw_ref[...] = jnp.full_like(mw_ref, NEG)
    prev_bias = tile_rows(jnp.where(rel > jnp.where(kd > 0, d0, KEY_TILE), 0.0, NEG))
    diag_bias = tile_rows(jnp.where(rel <= d0, 0.0, NEG))
    kw_p, vw_p = kw_ref[pl.ds(off_p, KEY_TILE), :], vwa_ref[pl.ds(off_p, KEY_TILE), :]
    kw_d, vw_d = kw_ref[pl.ds(off_d, KEY_TILE), :], vwa_ref[pl.ds(off_d, KEY_TILE), :]
    for kv in range(N_KV):
        flash_step(accw_ref, mw_ref, kv, _dot_nt(q128[kv], kw_p) + prev_bias, vw_p)
    for kv in range(N_KV):
        flash_step(accw_ref, mw_ref, kv, _dot_nt(q128[kv], kw_d) + diag_bias, vw_d)
    for kv in range(N_KV):
        o_win = flash_result(accw_ref, kv)
        for g in range(GQA):
            col = 3 * (kv * GQA + g)
            rs = slice(g * Q_BLOCK, (g + 1) * Q_BLOCK)
            part_ref[kv, rs, :] = gate[:, col:col + 1] * o_cmp[kv][rs] + gate[:, col + 2:col + 3] * o_win[rs]

    t_q = s0 + lane
    forced = (sub == 0) | (sub == (t_q >> 6))
    val = [jnp.where(forced, FORCE, jnp.where(sub * SLC_LEN <= t_q, imp_t[kv], -1.0)) for kv in range(N_KV)]
    sel_t = [jnp.zeros((N_SLC, Q_BLOCK), F32) for _ in range(N_KV)]
    sub_f = sub.astype(F32)
    for _ in range(SLC_TOPN):
        for kv in range(N_KV):
            mx = jnp.max(val[kv], axis=0, keepdims=True)
            first = jnp.min(jnp.where(val[kv] == mx, sub_f, float(N_SLC)), axis=0, keepdims=True)
            pick = sub_f == first
            sel_t[kv] = jnp.where(pick, 1.0, sel_t[kv])
            val[kv] = jnp.where(pick, -jnp.inf, val[kv])
    for kv in range(N_KV):
        bias = ((sel_t[kv].T - 1.0) * (-NEG)).astype(BF16)
        qa_ref[kv] = jnp.concatenate([q128[kv], jnp.concatenate([bias] * GQA, axis=0)], axis=1)

    acc_ref[...] = jnp.zeros_like(acc_ref)
    m_ref[...] = jnp.full_like(m_ref, NEG)

    def scores_into(dst, kt):
        k_t = ksa_ref[pl.ds(pl.multiple_of(kt * KEY_TILE, KEY_TILE), KEY_TILE), :]
        for kv in range(N_KV):
            dst[kv] = _dot_nt(qa_ref[kv], k_t)

    def consume(src, kt, bias):
        v_t = vsa_ref[pl.ds(pl.multiple_of(kt * KEY_TILE, KEY_TILE), KEY_TILE), :]
        for kv in range(N_KV):
            flash_step(acc_ref, m_ref, kv, src[kv] if bias is None else src[kv] + bias, v_t)

    n_pairs = kd // 2
    scores_into(sa_ref, 0)

    def slc_pair(j, carry):
        a = 2 * j
        scores_into(sb_ref, a + 1)
        consume(sa_ref, a, None)
        scores_into(sa_ref, a + 2)
        consume(sb_ref, a + 1, None)
        return carry

    lax.fori_loop(0, n_pairs, slc_pair, 0)
    left = 2 * n_pairs
    scores_into(sb_ref, kd)
    consume(sa_ref, left, jnp.where(left < kd, 0.0, NEG))
    consume(sb_ref, kd, tile_rows(jnp.where(rel <= d0, 0.0, NEG)))

    for kv in range(N_KV):
        o_slc = flash_result(acc_ref, kv)
        pieces = []
        for g in range(GQA):
            col = 3 * (kv * GQA + g)
            rs = slice(g * Q_BLOCK, (g + 1) * Q_BLOCK)
            pieces.append(part_ref[kv, rs, :] + gate[:, col + 1:col + 2] * o_slc[rs])
        for mm in range(2):
            even, odd = pieces[2 * mm], pieces[2 * mm + 1]
            if kv == 0:
                odd = pltpu.roll(odd, 64, 1)
            else:
                even = pltpu.roll(even, 64, 1)
            slab = kv * 2 + mm
            o_ref[:, slab * LANES:(slab + 1) * LANES] = jnp.where(lane < 64, even, odd)


def _attention(q, kcmp, vcmp, ksa, vsa, kw, vwa, gate, batch, seq):
    nh = kcmp.shape[1]
    assert nh == KEY_TILE
    rows = GQA * Q_BLOCK
    res = lambda w: pl.BlockSpec((None, seq, w), lambda b, i: (b, 0, 0))
    blk = lambda w: pl.BlockSpec((None, Q_BLOCK, w), lambda b, i: (b, i, 0))
    r3 = lambda a, w: a.reshape(batch, seq, w)
    return pl.pallas_call(
        _attn_kernel,
        grid=(batch, seq // Q_BLOCK),
        in_specs=[blk(N_HEADS * LANES),
                  pl.BlockSpec((None, nh, LANES), lambda b, i: (b, 0, 0)),
                  pl.BlockSpec((None, nh, LANES), lambda b, i: (b, 0, 0)),
                  res(2 * LANES), res(2 * LANES), res(LANES), res(2 * LANES), blk(LANES)],
        out_specs=blk(D_ATT),
        out_shape=jax.ShapeDtypeStruct((batch, seq, D_ATT), F32),
        scratch_shapes=[pltpu.VMEM((N_KV, rows, 2 * LANES), BF16), pltpu.VMEM((N_KV, rows, LANES), F32),
                        pltpu.VMEM((N_KV, rows, 2 * LANES), F32), pltpu.VMEM((N_KV, rows, LANES), F32),
                        pltpu.VMEM((N_KV, rows, 2 * LANES), F32), pltpu.VMEM((N_KV, rows, LANES), F32),
                        pltpu.VMEM((N_KV, rows, KEY_TILE), F32), pltpu.VMEM((N_KV, rows, KEY_TILE), F32)],
        compiler_params=_cparams(2),
        name="nsa_attention",
    )(r3(q, N_HEADS * LANES), kcmp, vcmp, r3(ksa, 2 * LANES), r3(vsa, 2 * LANES), r3(kw, LANES), r3(vwa, 2 * LANES),
      r3(gate, LANES))


def _outproj_kernel(x_ref, yl_ref, ya_ref, yc_ref, gn_ref, w_ref, mod_ref, n2_ref, rw_ref, rb_ref,
                    xn_ref, h2_ref, comb_ref, pos_ref, seg_ref):
    tm = x_ref.shape[0]
    a, b = D_LRU, D_LRU + D_ATT
    yl = _rms(yl_ref[...], gn_ref[:, 0:a]).astype(BF16)
    ya = _rms(ya_ref[...], gn_ref[:, a:b]).astype(BF16)
    yc = _rms(yc_ref[...], gn_ref[:, b:D_MODEL]).astype(BF16)
    y = _dot(yl, w_ref[0:a, :]) + _dot(ya, w_ref[a:b, :]) + _dot(yc, w_ref[b:D_MODEL, :])
    xn = x_ref[...] + mod_ref[2:3, :] * y
    xn_ref[...] = xn
    h2 = _rms(xn, n2_ref[...]) * (1.0 + mod_ref[4:5, :]) + mod_ref[3:4, :]

    h_hi = h2.astype(BF16)
    h_mid = (h2 - h_hi.astype(F32)).astype(BF16)
    logit = _dot(h_hi, rw_ref[0]) + (_dot(h_hi, rw_ref[1]) + _dot(h_mid, rw_ref[0])) + rb_ref[...]
    lane = lax.broadcasted_iota(jnp.int32, (tm, LANES), 1)
    ninf = -jnp.inf
    is_g = lane < MOE_GROUPS
    is_e = (lane >= MOE_GROUPS) & (lane < MOE_GROUPS + MOE_GROUPS * MOE_EXPERTS)
    lg_max = jnp.max(jnp.where(is_g, logit, ninf), axis=-1, keepdims=True)
    g_star = jnp.min(jnp.where(is_g & (logit == lg_max), lane, LANES), axis=-1, keepdims=True)
    pg_star = 1.0 / jnp.sum(jnp.where(is_g, jnp.exp(logit - lg_max), 0.0), axis=-1, keepdims=True)
    in_grp = is_e & (((lane - MOE_GROUPS) >> 2) == g_star)
    v1 = jnp.max(jnp.where(in_grp, logit, ninf), axis=-1, keepdims=True)
    i1 = jnp.min(jnp.where(in_grp & (logit == v1), lane, LANES), axis=-1, keepdims=True)
    rest = in_grp & (lane != i1)
    v2 = jnp.max(jnp.where(rest, logit, ninf), axis=-1, keepdims=True)
    i2 = jnp.min(jnp.where(rest & (logit == v2), lane, LANES), axis=-1, keepdims=True)
    d = jnp.exp(v2 - v1)
    pe1 = 1.0 / (1.0 + d)
    pe2 = d / (1.0 + d)
    comb = jnp.where(lane == i1, pe1, jnp.where(lane == i2, pe2, 0.0)) * pg_star

    onehot = jnp.where(lane == g_star, 1.0, 0.0)
    r_i = lax.broadcasted_iota(jnp.int32, (tm, tm), 0)
    c_i = lax.broadcasted_iota(jnp.int32, (tm, tm), 1)
    earlier = jnp.where(c_i < r_i, 1.0, 0.0).astype(BF16)
    rank = jnp.sum(onehot * _dot(earlier, onehot.astype(BF16)), axis=-1, keepdims=True)
    n_g = jnp.sum(onehot, axis=0, keepdims=True)
    n_pad = (((n_g.astype(jnp.int32) + (MOE_ALIGN - 1)) >> MOE_ALIGN_LOG2) << MOE_ALIGN_LOG2).astype(F32)
    lane1 = lax.broadcasted_iota(jnp.int32, (1, LANES), 1)
    start = jnp.zeros((1, LANES), F32)
    below = jnp.zeros((1, 1), F32)
    for g in range(1, MOE_GROUPS):
        below = below + jnp.sum(jnp.where(lane1 == g - 1, n_pad, 0.0), axis=-1, keepdims=True)
        start = start + jnp.where(lane1 == g, below, 0.0)
    pos = jnp.sum(onehot * start, axis=-1, keepdims=True) + rank
    pos_rep = jnp.broadcast_to(pos, (tm, LANES))
    slot = lax.broadcasted_iota(jnp.int32, (tm, MOE_SLOTS), 1).astype(F32)
    take_t = jnp.where(_rep(pos_rep, MOE_SLOTS // LANES, 1) == slot, 1.0, 0.0)
    take = take_t.T
    take = take.astype(BF16)
    h2_ref[...] = _dot(take, h_hi).astype(BF16)
    c_hi = comb.astype(BF16)
    c_rest = comb - c_hi.astype(F32)
    c_mid = c_rest.astype(BF16)
    c_lo = (c_rest - c_mid.astype(F32)).astype(BF16)
    comb_ref[...] = (_dot(take, c_hi) + _dot(take, c_mid)) + _dot(take, c_lo)
    pos_ref[...] = pos_rep
    seg_ref[...] = jnp.zeros(seg_ref.shape, jnp.int32)
    seg_ref[0:1, :] = n_pad.astype(jnp.int32)
    seg_ref[1:2, :] = start.astype(jnp.int32)


def _outproj(x2, yl, ya, yc, gn, w_out, mod_l, n2g, rw, rb, seq):
    t, d = x2.shape
    tps = seq // TM
    row = lambda w: pl.BlockSpec((TM, w), lambda i: (i, 0))
    full = lambda a: pl.BlockSpec(a.shape, lambda i: (0,) * a.ndim)
    return pl.pallas_call(
        _outproj_kernel,
        grid=(t // TM,),
        in_specs=[row(d), row(D_LRU), row(D_ATT), row(D_CONV), full(gn), full(w_out),
                  pl.BlockSpec((None, 6, d), lambda i: (i // tps, 0, 0)), full(n2g), full(rw), full(rb)],
        out_specs=[row(d), pl.BlockSpec((MOE_SLOTS, d), lambda i: (i, 0)), pl.BlockSpec((MOE_SLOTS, LANES), lambda i: (i, 0)),
                   row(LANES), pl.BlockSpec((None, 8, LANES), lambda i: (i, 0, 0))],
        out_shape=[jax.ShapeDtypeStruct((t, d), F32), jax.ShapeDtypeStruct((t // TM * MOE_SLOTS, d), BF16),
                   jax.ShapeDtypeStruct((t // TM * MOE_SLOTS, LANES), F32), jax.ShapeDtypeStruct((t, LANES), F32),
                   jax.ShapeDtypeStruct((t // TM, 8, LANES), jnp.int32)],
        compiler_params=_cparams(1),
        name="out_proj_router",
    )(x2, yl.reshape(t, D_LRU), ya.reshape(t, D_ATT), yc.reshape(t, D_CONV), gn, w_out, mod_l, n2g, rw, rb)


def _moe_plan(seg, n_chunks, n_gtiles):
    rows = seg[:, 0, :MOE_GROUPS]
    start = seg[:, 1, :MOE_GROUPS]
    tiles_g = (jnp.sum(rows, axis=0) + (TM - 1)) // TM
    ends = jnp.cumsum(tiles_g)
    base = (ends - tiles_g) * TM
    dst = base[None, :] + jnp.cumsum(rows, axis=0) - rows
    src = jnp.arange(n_chunks, dtype=jnp.int32)[:, None] * MOE_SLOTS + start
    table = jnp.concatenate([src.reshape(-1), dst.reshape(-1), rows.reshape(-1)]).astype(jnp.int32)
    k = jnp.arange(n_gtiles, dtype=jnp.int32)
    group_of = jnp.minimum(jnp.sum((k[:, None] >= ends[None, :]).astype(jnp.int32), axis=1), MOE_GROUPS - 1)
    tiles = jnp.concatenate([group_of, ends[-1:]]).astype(jnp.int32)
    return table, tiles


def _for_each_piece(rows, act):
    done = jnp.int32(0)
    for size in MOE_PIECES:
        take = (rows & size) != 0

        @pl.when(take)
        def _(done=done, size=size):
            act(done, size)

        done = done + jnp.where(take, size, 0)


def _dispatch_kernel(tab_ref, h2s_ref, combs_ref, h2z_ref, combz_ref, h2g_ref, combg_ref, sem):
    del h2z_ref, combz_ref
    nseg = tab_ref.shape[0] // 3

    def segment(i, go):
        src, dst = tab_ref[i], tab_ref[nseg + i]

        def piece(first, size):
            s = pl.multiple_of(src + first, MOE_ALIGN)
            t = pl.multiple_of(dst + first, MOE_ALIGN)
            go(pltpu.make_async_copy(h2s_ref.at[pl.ds(s, size)], h2g_ref.at[pl.ds(t, size)], sem.at[0]))
            go(pltpu.make_async_copy(combs_ref.at[pl.ds(s, size)], combg_ref.at[pl.ds(t, size)], sem.at[1]))

        _for_each_piece(tab_ref[2 * nseg + i], piece)

    def start(i, carry):
        segment(i, lambda cp: cp.start())
        return carry

    def wait(i, carry):
        segment(i, lambda cp: cp.wait())
        return carry

    lax.fori_loop(0, nseg, start, 0)
    lax.fori_loop(0, nseg, wait, 0)


def _dispatch(table, h2s, combs, n_gtiles):
    d = h2s.shape[1]
    any_spec = pl.BlockSpec(memory_space=pl.ANY)
    grid_spec = pltpu.PrefetchScalarGridSpec(
        num_scalar_prefetch=1, grid=(1,), in_specs=[any_spec] * 4, out_specs=[any_spec] * 2,
        scratch_shapes=[pltpu.SemaphoreType.DMA((2,))])
    return pl.pallas_call(
        _dispatch_kernel,
        grid_spec=grid_spec,
        out_shape=[jax.ShapeDtypeStruct((n_gtiles * TM, d), BF16), jax.ShapeDtypeStruct((n_gtiles * TM, LANES), F32)],
        input_output_aliases={3: 0, 4: 1},
        compiler_params=_cparams(1),
        name="moe_dispatch",
    )(table, h2s, combs, jnp.zeros((n_gtiles * TM, d), BF16), jnp.zeros((n_gtiles * TM, LANES), F32))


def _experts_kernel(tiles_ref, h2_ref, comb_ref, wg_ref, wu_ref, wd_ref, y_ref, acc_ref):
    k = pl.program_id(0)
    e = pl.program_id(1)

    @pl.when(e == 0)
    def _():
        acc_ref[...] = jnp.zeros_like(acc_ref)

    @pl.when(k < tiles_ref[pl.num_programs(0)])
    def _():
        h2 = h2_ref[...]
        hid = jax.nn.silu(_dot(h2, wg_ref[...])) * _dot(h2, wu_ref[...])
        lane = lax.broadcasted_iota(jnp.int32, comb_ref.shape, 1)
        col = MOE_GROUPS + tiles_ref[k] * MOE_EXPERTS + e
        c = jnp.sum(jnp.where(lane == col, comb_ref[...], 0.0), axis=-1, keepdims=True)
        acc_ref[...] += _dot((hid * c).astype(BF16), wd_ref[...])

    @pl.when(e == pl.num_programs(1) - 1)
    def _():
        y_ref[...] = acc_ref[...]


def _experts(tiles, h2g, combg, wg, wu, wd):
    rows, d = h2g.shape
    weights = lambda s: pl.BlockSpec((None,) + s, lambda k, e, tl: (tl[k] * MOE_EXPERTS + e, 0, 0))
    grid_spec = pltpu.PrefetchScalarGridSpec(
        num_scalar_prefetch=1,
        grid=(rows // TM, MOE_EXPERTS),
        in_specs=[pl.BlockSpec((TM, d), lambda k, e, tl: (k, 0)),
                  pl.BlockSpec((TM, LANES), lambda k, e, tl: (k, 0)),
                  weights((d, MOE_HID)), weights((d, MOE_HID)), weights((MOE_HID, d))],
        out_specs=pl.BlockSpec((TM, d), lambda k, e, tl: (k, 0)),
        scratch_shapes=[pltpu.VMEM((TM, d), F32)],
    )
    return pl.pallas_call(
        _experts_kernel,
        grid_spec=grid_spec,
        out_shape=jax.ShapeDtypeStruct((rows, d), F32),
        compiler_params=_cparams(2),
        name="moe_experts",
    )(tiles, h2g, combg, wg, wu, wd)


def _finalize_kernel(tab_ref, yg_ref, pos_ref, xn_ref, mod_ref, fg_ref, o_ref, ys_ref, sem, *, final):
    c = pl.program_id(0)
    nseg = tab_ref.shape[0] // 3

    @pl.when(c == 0)
    def _():
        ys_ref[...] = jnp.zeros_like(ys_ref)

    def segments(go):
        for g in range(MOE_GROUPS):
            i = c * MOE_GROUPS + g
            slot0, src = tab_ref[i] - c * MOE_SLOTS, tab_ref[nseg + i]

            def piece(first, size, slot0=slot0, src=src):
                s = pl.multiple_of(src + first, MOE_ALIGN)
                t = pl.multiple_of(slot0 + first, MOE_ALIGN)
                go(pltpu.make_async_copy(yg_ref.at[pl.ds(s, size)], ys_ref.at[pl.ds(t, size)], sem.at[0]))

            _for_each_piece(tab_ref[2 * nseg + i], piece)

    segments(lambda cp: cp.start())
    segments(lambda cp: cp.wait())

    tm = xn_ref.shape[0]
    y = ys_ref[...]
    hi = y.astype(BF16)
    rest = y - hi.astype(F32)
    mid = rest.astype(BF16)
    lo = (rest - mid.astype(F32)).astype(BF16)
    slot = lax.broadcasted_iota(jnp.int32, (tm, MOE_SLOTS), 1).astype(F32)
    take_t = jnp.where(_rep(pos_ref[...], MOE_SLOTS // LANES, 1) == slot, 1.0, 0.0).astype(BF16)
    y_tok = (_dot(take_t, hi) + _dot(take_t, mid)) + _dot(take_t, lo)
    xo = xn_ref[...] + mod_ref[5:6, :] * y_tok
    if final:
        xo = _rms(xo, fg_ref[...])
    o_ref[...] = xo


def _finalize(table, yg, pos, xn, mod_l, fg, seq, final):
    t, d = xn.shape
    tps = seq // TM
    grid_spec = pltpu.PrefetchScalarGridSpec(
        num_scalar_prefetch=1,
        grid=(t // TM,),
        in_specs=[pl.BlockSpec(memory_space=pl.ANY),
                  pl.BlockSpec((TM, LANES), lambda i, tb: (i, 0)),
                  pl.BlockSpec((TM, d), lambda i, tb: (i, 0)),
                  pl.BlockSpec((None, 6, d), lambda i, tb: (i // tps, 0, 0)),
                  pl.BlockSpec((1, d), lambda i, tb: (0, 0))],
        out_specs=pl.BlockSpec((TM, d), lambda i, tb: (i, 0)),
        scratch_shapes=[pltpu.VMEM((MOE_SLOTS, d), F32), pltpu.SemaphoreType.DMA((1,))],
    )
    return pl.pallas_call(
        functools.partial(_finalize_kernel, final=final),
        grid_spec=grid_spec,
        out_shape=jax.ShapeDtypeStruct((t, d), F32),
        compiler_params=_cparams(1),
        name="moe_finalize",
    )(table, yg, pos, xn, mod_l, fg)


def _permute_w_in(w):
    gate0 = D_LRU * 2 + D_ATT + 6 * LANES
    cv0 = gate0 + 3 * N_HEADS
    pad = jnp.zeros((w.shape[0], IN_PAD - w.shape[1]), w.dtype)
    return jnp.concatenate([w[:, :gate0], w[:, cv0:], w[:, gate0:cv0], pad], axis=1).astype(BF16)


def _router_weights(rg_w, rg_b, re_w, re_b):
    d = rg_w.shape[0]
    ne = MOE_GROUPS * MOE_EXPERTS
    w = jnp.concatenate([rg_w, jnp.transpose(re_w, (1, 0, 2)).reshape(d, ne),
                         jnp.zeros((d, LANES - MOE_GROUPS - ne), F32)], axis=1)
    b = jnp.concatenate([rg_b, re_b.reshape(ne), jnp.zeros((LANES - MOE_GROUPS - ne,), F32)])[None, :]
    w_hi = w.astype(BF16)
    w_mid = (w - w_hi.astype(F32)).astype(BF16)
    return jnp.stack([w_hi, w_mid]), b


def kernel(x, c, positions, ada_w, ada_b, norm1_g, norm2_g, w_in, lru_conv_w, lru_conv_b, lru_wa, lru_ba, lru_wx, lru_bx, lru_lambda, cmp_k_w1, cmp_k_w2, cmp_v_w1, cmp_v_w2, cmp_pos_k, cmp_pos_v, cnv_dw_w, cnv_dw_b, cnv_ln_g, cnv_ln_b, out_norm_g, w_out, moe_rg_w, moe_rg_b, moe_re_w, moe_re_b, moe_w_gate, moe_w_up, moe_w_down, final_norm_g):
    batch, seq, d = x.shape
    depth = ada_w.shape[0]
    assert d == D_MODEL and seq == N_SLC * SLC_LEN and seq % KEY_TILE == 0
    t = batch * seq
    ne = MOE_GROUPS * MOE_EXPERTS
    n_chunks = t // TM
    n_gtiles = n_chunks + -(-n_chunks * MOE_GROUPS * (MOE_ALIGN - 1) // TM) + MOE_GROUPS

    cos, sin = _rope_tables(positions)
    mod = _modulation(c, ada_w, ada_b)
    x2 = x.reshape(t, d)
    fg = final_norm_g[None, :]
    for l in range(depth):
        (xl, gl, q, kc, vc, ksa, vsa, kw, vwa, gate, cv) = _inproj(
            x2, mod[l], norm1_g[l][None, :], _permute_w_in(w_in[l]), cos, sin, seq)
        y_lru, y_cnv = _mixer(xl, gl, cv, lru_conv_w[l], lru_conv_b[l], lru_wa[l], lru_ba[l], lru_wx[l], lru_bx[l],
                              lru_lambda[l], cnv_dw_w[l], cnv_dw_b[l], cnv_ln_g[l], cnv_ln_b[l], batch, seq)
        kcmp, vcmp = _compress(kc, vc, cmp_pos_k[l], cmp_k_w1[l], cmp_k_w2[l], cmp_pos_v[l], cmp_v_w1[l], cmp_v_w2[l],
                               batch, seq)
        y_att = _attention(q, kcmp, vcmp, ksa, vsa, kw, vwa, gate, batch, seq)
        rw, rb = _router_weights(moe_rg_w[l], moe_rg_b[l], moe_re_w[l], moe_re_b[l])
        xn, h2, comb, pos, seg = _outproj(x2, y_lru, y_att, y_cnv, out_norm_g[l][None, :], w_out[l].astype(BF16),
                                          mod[l], norm2_g[l][None, :], rw, rb, seq)
        table, tiles = _moe_plan(seg, n_chunks, n_gtiles)
        h2g, combg = _dispatch(table, h2, comb, n_gtiles)
        yg = _experts(tiles, h2g, combg,
                      moe_w_gate[l].reshape(ne, d, MOE_HID).astype(BF16),
                      moe_w_up[l].reshape(ne, d, MOE_HID).astype(BF16),
                      moe_w_down[l].reshape(ne, MOE_HID, d).astype(BF16))
        x2 = _finalize(table, yg, pos, xn, mod[l], fg, seq, final=(l == depth - 1))
    return x2.reshape(batch, seq, d)
```

```python
import functools

import jax
import jax.numpy as jnp
from jax import lax
from jax.experimental import pallas as pl
from jax.experimental.pallas import tpu as pltpu

F32 = jnp.float32
BF16 = jnp.bfloat16

D_MODEL = 1024
D_LRU = 256
D_ATT = 512
D_CONV = 256
LRU_BLOCKS = 4
LRU_CONV_W = 4
LRU_C = 8.0
HEAD_DIM = 64
N_HEADS = 8
N_KV = 2
GQA = 4
ROPE_THETA = 10000.0
CMP_LEN = 32
CMP_STRIDE = 16
SLC_LEN = 64
SLC_TOPN = 16
WINDOW = 512
CONV_K = 31
MOE_GROUPS = 4
MOE_EXPERTS = 4
MOE_HID = 512
EPS = 1e-6
NEG = -1e30
FORCE = 1e9
LOG2_E = 1.4426950408889634

LANES = 128
Q_BLOCK = 128
KEY_TILE = 512
N_SLC = 128
TM = 512
MOE_ALIGN_LOG2 = 4
MOE_ALIGN = 1 << MOE_ALIGN_LOG2
MOE_SLOTS = 640
TC = 512
SCAN_ROWS = 64
LRU_TAIL = 8
CNV_TAIL = 32
IN_PAD = 2432
VMEM_LIMIT = 56 * 1024 * 1024


def _cparams(n_axes, vmem=VMEM_LIMIT):
    return pltpu.CompilerParams(dimension_semantics=("arbitrary",) * n_axes, vmem_limit_bytes=vmem)


def _dot(a, b):
    return jnp.dot(a, b, preferred_element_type=F32)


def _dot_nt(a, b):
    return lax.dot_general(a, b, (((1,), (1,)), ((), ())), preferred_element_type=F32)


def _rep(v, n, axis):
    return jnp.concatenate([v] * n, axis=axis)


def _rms(v, g):
    return v * lax.rsqrt(jnp.mean(v * v, axis=-1, keepdims=True) + EPS) * g


def _rope_kernel(pos_ref, inv_ref, sign_ref, cos_ref, sin_ref):
    ang = pos_ref[...].astype(F32) * inv_ref[...]
    cos_ref[...] = jnp.cos(ang)
    sin_ref[...] = jnp.sin(ang) * sign_ref[...]


def _rope_tables(positions):
    t = positions.size
    inv = ROPE_THETA ** (-jnp.arange(0, HEAD_DIM, 2, dtype=F32) / HEAD_DIM)
    inv128 = jnp.tile(inv, 4)[None, :]
    sign128 = jnp.tile(jnp.concatenate([-jnp.ones((32,), F32), jnp.ones((32,), F32)]), 2)[None, :]
    tr = 1024
    return pl.pallas_call(
        _rope_kernel,
        grid=(t // tr,),
        in_specs=[pl.BlockSpec((tr, 1), lambda i: (i, 0)),
                  pl.BlockSpec((1, LANES), lambda i: (0, 0)),
                  pl.BlockSpec((1, LANES), lambda i: (0, 0))],
        out_specs=[pl.BlockSpec((tr, LANES), lambda i: (i, 0))] * 2,
        out_shape=[jax.ShapeDtypeStruct((t, LANES), F32)] * 2,
        compiler_params=_cparams(1),
        name="rope_tables",
    )(positions.reshape(t, 1), inv128, sign128)


def _mod_kernel(c_ref, w_ref, b_ref, o_ref):
    sc = jax.nn.silu(c_ref[...])
    o_ref[...] = _dot(sc.astype(BF16), w_ref[...].astype(BF16)) + b_ref[...]


def _modulation(c, ada_w, ada_b):
    nl, d, n6 = ada_w.shape
    b = c.shape[0]
    rows = 16
    cp = jnp.zeros((rows, d), F32).at[:b].set(c)
    tn = 1536
    out = pl.pallas_call(
        _mod_kernel,
        grid=(nl, n6 // tn),
        in_specs=[pl.BlockSpec((rows, d), lambda l, j: (0, 0)),
                  pl.BlockSpec((None, d, tn), lambda l, j: (l, 0, j)),
                  pl.BlockSpec((None, 1, tn), lambda l, j: (l, 0, j))],
        out_specs=pl.BlockSpec((None, rows, tn), lambda l, j: (l, 0, j)),
        out_shape=jax.ShapeDtypeStruct((nl, rows, n6), F32),
        compiler_params=_cparams(2),
        name="adaln_mod",
    )(cp, ada_w, ada_b.reshape(nl, 1, n6))
    return out[:, :b].reshape(nl, b, 6, d)


def _inproj_kernel(x_ref, mod_ref, g_ref, w_ref, cos_ref, sin_ref,
                   xl_ref, gl_ref, q_ref, kc_ref, vc_ref, ksa_ref, vsa_ref, kw_ref, vwa_ref, gate_ref, cv_ref,
                   *, tiles_per_seq):
    tm = x_ref.shape[0]
    x = x_ref[...]
    h = _rms(x, g_ref[...]) * (1.0 + mod_ref[1:2, :]) + mod_ref[0:1, :]
    p = _dot(h.astype(BF16), w_ref[...])

    cos = cos_ref[...]
    sin = sin_ref[...]
    lane = lax.broadcasted_iota(jnp.int32, (tm, LANES), 1)
    first_half = (lane & 63) < 32

    def rope(v):
        rot = jnp.where(first_half, pltpu.roll(v, 96, 1), pltpu.roll(v, 32, 1))
        return v * cos + rot * sin

    xl_ref[...] = p[:, 0:256]
    gl_ref[...] = p[:, 256:512]
    scale = HEAD_DIM ** -0.5 * LOG2_E
    low = lane < 64
    for m in range(4):
        slab = rope(p[:, 512 + 128 * m:640 + 128 * m]) * scale
        swapped = pltpu.roll(slab, 64, 1)
        for hh in range(2):
            head = 2 * m + hh
            kv = head // GQA
            src = slab if hh == kv else swapped
            keep = low if kv == 0 else jnp.logical_not(low)
            q_ref[:, head * LANES:(head + 1) * LANES] = jnp.where(keep, src, 0.0).astype(BF16)
    kc_ref[...] = rope(p[:, 1024:1152])
    vc_ref[...] = p[:, 1152:1280]
    row = lax.broadcasted_iota(jnp.int32, (tm, LANES), 0)
    s_base = (pl.program_id(0) % tiles_per_seq) * tm
    onehot = jnp.where(lane == ((s_base + row) >> 6), 1.0, 0.0).astype(BF16)
    ones = jnp.ones((tm, LANES), BF16)
    ksa_ref[:, 0:LANES] = rope(p[:, 1280:1408]).astype(BF16)
    ksa_ref[:, LANES:2 * LANES] = onehot
    vsa_ref[:, 0:LANES] = p[:, 1408:1536].astype(BF16)
    vsa_ref[:, LANES:2 * LANES] = ones
    kw_ref[...] = rope(p[:, 1536:1664]).astype(BF16)
    vwa_ref[:, 0:LANES] = p[:, 1664:1792].astype(BF16)
    vwa_ref[:, LANES:2 * LANES] = ones
    cv_ref[...] = p[:, 1792:2304]
    gate_ref[...] = jax.nn.sigmoid(p[:, 2304:2432])


def _inproj(x2, mod_l, g, w_p, cos, sin, seq):
    t, d = x2.shape
    tps = seq // TM
    row = lambda w: pl.BlockSpec((TM, w), lambda i: (i, 0))
    outs = [(256, F32), (256, F32), (N_HEADS * LANES, BF16), (LANES, F32), (LANES, F32), (2 * LANES, BF16),
            (2 * LANES, BF16), (LANES, BF16), (2 * LANES, BF16), (LANES, F32), (2 * D_CONV, F32)]
    return pl.pallas_call(
        functools.partial(_inproj_kernel, tiles_per_seq=tps),
        grid=(t // TM,),
        in_specs=[row(d),
                  pl.BlockSpec((None, 6, d), lambda i: (i // tps, 0, 0)),
                  pl.BlockSpec((1, d), lambda i: (0, 0)),
                  pl.BlockSpec((d, IN_PAD), lambda i: (0, 0)),
                  row(LANES), row(LANES)],
        out_specs=[row(w) for w, _ in outs],
        out_shape=[jax.ShapeDtypeStruct((t, w), dt) for w, dt in outs],
        compiler_params=_cparams(1),
        name="in_proj",
    )(x2, mod_l, g, w_p, cos, sin)


def _shift_rows(v, s, fill, row):
    return jnp.where(row < s, fill, pltpu.roll(v, s, 0))


def _causal_taps(ext, w_ref, bias, width, tail, tc):
    acc = bias
    for b in range(min(8, width)):
        shifted = pltpu.roll(ext, b, 0) if b else ext
        for k in range(width):
            back = width - 1 - k
            if back % 8 == b:
                start = tail - (back - b)
                acc = acc + w_ref[k:k + 1, :] * shifted[start:start + tc, :]
    return acc


def _mixer_kernel(xl_ref, gl_ref, cv_ref, lcw_ref, lcb_ref, wa_ref, ba_ref, wx_ref, bx_ref, lam_ref,
                  cw_ref, cb_ref, lng_ref, lnb_ref, ylru_ref, ycnv_ref,
                  extl, extc, abuf, ubuf, hcar):
    tc = xl_ref.shape[0]

    @pl.when(pl.program_id(1) == 0)
    def _():
        extl[0:LRU_TAIL, :] = jnp.zeros((LRU_TAIL, D_LRU), F32)
        extc[0:CNV_TAIL, :] = jnp.zeros((CNV_TAIL, D_CONV), F32)
        hcar[...] = jnp.zeros_like(hcar)

    xl = xl_ref[...]
    extl[LRU_TAIL:LRU_TAIL + tc, :] = xl
    xc = _causal_taps(extl[...], lcw_ref, lcb_ref[...], LRU_CONV_W, LRU_TAIL, tc)
    extl[0:LRU_TAIL, :] = xl[tc - LRU_TAIL:tc, :]
    xcb = xc.astype(BF16)
    r = jax.nn.sigmoid(_dot(xcb, wa_ref[...]) + ba_ref[...])
    gi = jax.nn.sigmoid(_dot(xcb, wx_ref[...]) + bx_ref[...])
    log_a = LRU_C * r * jax.nn.log_sigmoid(lam_ref[...])
    th = jnp.tanh(log_a)
    one_minus_a2 = -2.0 * th / (1.0 - th)
    abuf[...] = jnp.exp(log_a)
    ubuf[...] = jnp.sqrt(one_minus_a2) * (gi * xc)

    row = lax.broadcasted_iota(jnp.int32, (SCAN_ROWS, D_LRU), 0)

    def scan_chunk(c, h_prev):
        off = pl.multiple_of(c * SCAN_ROWS, SCAN_ROWS)
        a = abuf[pl.ds(off, SCAN_ROWS), :]
        b = ubuf[pl.ds(off, SCAN_ROWS), :]
        s = 1
        while s < SCAN_ROWS:
            b = b + a * _shift_rows(b, s, 0.0, row)
            a = a * _shift_rows(a, s, 1.0, row)
            s *= 2
        h = b + a * h_prev
        ylru_ref[pl.ds(off, SCAN_ROWS), :] = h * jax.nn.gelu(gl_ref[pl.ds(off, SCAN_ROWS), :])
        return h[SCAN_ROWS - 1:SCAN_ROWS, :]

    hcar[0:1, :] = lax.fori_loop(0, tc // SCAN_ROWS, scan_chunk, hcar[0:1, :])

    u = cv_ref[...]
    v = u[:, 0:D_CONV] * jax.nn.sigmoid(u[:, D_CONV:2 * D_CONV])
    extc[CNV_TAIL:CNV_TAIL + tc, :] = v
    acc = _causal_taps(extc[...], cw_ref, cb_ref[...], CONV_K, CNV_TAIL, tc)
    extc[0:CNV_TAIL, :] = v[tc - CNV_TAIL:tc, :]
    mu = jnp.mean(acc, axis=-1, keepdims=True)
    var = jnp.mean(jnp.square(acc - mu), axis=-1, keepdims=True)
    y = (acc - mu) * lax.rsqrt(var + EPS) * lng_ref[...] + lnb_ref[...]
    ycnv_ref[...] = jax.nn.silu(y)


def _block_diag(w):
    n, bw, _ = w.shape
    out = jnp.zeros((n * bw, n * bw), w.dtype)
    for i in range(n):
        out = out.at[i * bw:(i + 1) * bw, i * bw:(i + 1) * bw].set(w[i])
    return out


def _mixer(xl, gl, cv, lcw, lcb, wa, ba, wx, bx, lam, cw, cb, lng, lnb, batch, seq):
    seqrow = lambda w: pl.BlockSpec((None, TC, w), lambda b, t: (b, t, 0))
    full = lambda a: pl.BlockSpec(a.shape, lambda b, t: (0,) * a.ndim)
    small = [lcw, lcb[None, :], _block_diag(wa).astype(BF16), ba[None, :], _block_diag(wx).astype(BF16), bx[None, :],
             lam[None, :], jnp.zeros((32, D_CONV), F32).at[:CONV_K].set(cw), cb[None, :], lng[None, :], lnb[None, :]]
    return pl.pallas_call(
        _mixer_kernel,
        grid=(batch, seq // TC),
        in_specs=[seqrow(D_LRU), seqrow(D_LRU), seqrow(2 * D_CONV)] + [full(a) for a in small],
        out_specs=[seqrow(D_LRU), seqrow(D_CONV)],
        out_shape=[jax.ShapeDtypeStruct((batch, seq, D_LRU), F32), jax.ShapeDtypeStruct((batch, seq, D_CONV), F32)],
        scratch_shapes=[pltpu.VMEM((TC + LRU_TAIL, D_LRU), F32), pltpu.VMEM((TC + CNV_TAIL, D_CONV), F32),
                        pltpu.VMEM((TC, D_LRU), F32), pltpu.VMEM((TC, D_LRU), F32), pltpu.VMEM((8, D_LRU), F32)],
        compiler_params=_cparams(2),
        name="mixer_stream",
    )(xl.reshape(batch, seq, D_LRU), gl.reshape(batch, seq, D_LRU), cv.reshape(batch, seq, 2 * D_CONV), *small)


def _cmp_kernel(kc_ref, vc_ref, posk_ref, posv_ref, wk1_ref, wk2_ref, wv1_ref, wv2_ref, ko_ref, vo_ref, nat_ref):
    nh = ko_ref.shape[0]
    quarter = nh // 4

    def one(t_ref, pos_ref, w1_ref, w2_ref, o_ref):
        first = jnp.zeros((nh, w1_ref.shape[2]), F32)
        second = jnp.zeros((nh, w1_ref.shape[2]), F32)
        for l in range(CMP_STRIDE):
            x = t_ref[pl.ds(l, nh, stride=CMP_STRIDE), :]
            first = first + _dot((x + pos_ref[l:l + 1, :]).astype(BF16), w1_ref[l])
            second = second + _dot((x + pos_ref[CMP_STRIDE + l:CMP_STRIDE + l + 1, :]).astype(BF16),
                                   w1_ref[CMP_STRIDE + l])
        hid = jax.nn.gelu(first + pltpu.roll(second, nh - 1, 0))
        nat_ref[...] = _dot(hid.astype(BF16), w2_ref[...])
        for r in range(4):
            o_ref[r * quarter:(r + 1) * quarter, :] = nat_ref[pl.ds(r, quarter, stride=4), :].astype(o_ref.dtype)

    one(kc_ref, posk_ref, wk1_ref, wk2_ref, ko_ref)
    one(vc_ref, posv_ref, wv1_ref, wv2_ref, vo_ref)


def _cmp_weights(pos, w1, w2):
    hid = w1.shape[1]
    w1 = w1.reshape(CMP_LEN, HEAD_DIM, hid)
    z1 = jnp.zeros_like(w1)
    w1x = jnp.concatenate([jnp.concatenate([w1, z1], axis=2), jnp.concatenate([z1, w1], axis=2)], axis=1)
    z2 = jnp.zeros_like(w2)
    w2x = jnp.concatenate([jnp.concatenate([w2, z2], axis=1), jnp.concatenate([z2, w2], axis=1)], axis=0)
    return jnp.tile(pos, (1, N_KV)), w1x.astype(BF16), w2x.astype(BF16)


def _compress(kc, vc, pos_k, wk1, wk2, pos_v, wv1, wv2, batch, seq):
    nh = seq // CMP_STRIDE
    wide = N_KV * HEAD_DIM
    pk, wk1x, wk2x = _cmp_weights(pos_k, wk1, wk2)
    pv, wv1x, wv2x = _cmp_weights(pos_v, wv1, wv2)
    full = lambda a: pl.BlockSpec(a.shape, lambda b: (0,) * a.ndim)
    seqs = pl.BlockSpec((None, seq, wide), lambda b: (b, 0, 0))
    outs = pl.BlockSpec((None, nh, wide), lambda b: (b, 0, 0))
    return pl.pallas_call(
        _cmp_kernel,
        grid=(batch,),
        in_specs=[seqs, seqs, full(pk), full(pv), full(wk1x), full(wk2x), full(wv1x), full(wv2x)],
        out_specs=[outs, outs],
        out_shape=[jax.ShapeDtypeStruct((batch, nh, wide), BF16)] * 2,
        scratch_shapes=[pltpu.VMEM((nh, wide), F32)],
        compiler_params=_cparams(1),
        name="nsa_compress",
    )(kc.reshape(batch, seq, wide), vc.reshape(batch, seq, wide), pk, pv, wk1x, wk2x, wv1x, wv2x)


def _attn_kernel(q_ref, kcmp_ref, vcmp_ref, ksa_ref, vsa_ref, kw_ref, vwa_ref, gate_ref, o_ref,
                 qa_ref, part_ref, acc_ref, m_ref, accw_ref, mw_ref, sa_ref, sb_ref):
    rows = GQA * Q_BLOCK
    reps = KEY_TILE // LANES
    qb = pl.program_id(1)
    s0 = qb * Q_BLOCK
    kd = s0 // KEY_TILE
    off_d = pl.multiple_of(kd * KEY_TILE, KEY_TILE)
    off_p = pl.multiple_of(jnp.maximum(kd - 1, 0) * KEY_TILE, KEY_TILE)
    d0 = s0 - off_d

    lane = lax.broadcasted_iota(jnp.int32, (Q_BLOCK, LANES), 1)
    sub = lax.broadcasted_iota(jnp.int32, (Q_BLOCK, LANES), 0)
    r_q = lax.broadcasted_iota(jnp.int32, (Q_BLOCK, KEY_TILE), 0)
    c_k = lax.broadcasted_iota(jnp.int32, (Q_BLOCK, KEY_TILE), 1)
    rel = c_k - r_q
    gate = gate_ref[...]

    def tile_rows(plane):
        return _rep(plane, GQA, 0)

    def flash_step(acc, m, kv, s, v_tile):
        m_old = m[kv]
        m_new = jnp.maximum(m_old, jnp.broadcast_to(jnp.max(s, axis=-1, keepdims=True), m_old.shape))
        p = jnp.exp2(s - _rep(m_new, reps, 1))
        acc[kv] = acc[kv] * _rep(jnp.exp2(m_old - m_new), 2, 1) + _dot(p.astype(BF16), v_tile)
        m[kv] = m_new

    def flash_result(acc, kv):
        a = acc[kv]
        return a[:, 0:LANES] / a[:, LANES:2 * LANES]

    q128 = [jnp.concatenate([q_ref[:, (kv * GQA + g) * LANES:(kv * GQA + g + 1) * LANES] for g in range(GQA)], axis=0)
            for kv in range(N_KV)]

    kcmp = kcmp_ref[...]
    vcmp = vcmp_ref[...]
    cmp_end = SLC_LEN * (c_k & (N_SLC - 1)) + CMP_STRIDE * (c_k >> 7) + (CMP_LEN - 1)
    cmp_bias = tile_rows(jnp.where(cmp_end - r_q <= s0, 0.0, NEG))
    has_cmp = tile_rows(jnp.where(s0 + sub >= CMP_LEN - 1, 1.0, 0.0))
    o_cmp, imp_t = [], []
    for kv in range(N_KV):
        s_c = _dot_nt(q128[kv], kcmp) + cmp_bias
        mb = jnp.broadcast_to(jnp.max(s_c, axis=-1, keepdims=True), (rows, LANES))
        e = jnp.exp2(s_c - _rep(mb, reps, 1))
        lb = jnp.broadcast_to(jnp.sum(e, axis=-1, keepdims=True), (rows, LANES))
        inv = has_cmp / lb
        o_cmp.append(_dot(e.astype(BF16), vcmp) * inv)
        e4 = (e[:, 0:N_SLC] + e[:, N_SLC:2 * N_SLC] + e[:, 2 * N_SLC:3 * N_SLC] + e[:, 3 * N_SLC:4 * N_SLC]) * inv
        imp = e4[0:Q_BLOCK] + e4[Q_BLOCK:2 * Q_BLOCK] + e4[2 * Q_BLOCK:3 * Q_BLOCK] + e4[3 * Q_BLOCK:4 * Q_BLOCK]
        imp_t.append(imp.T)

    accw_ref[...] = jnp.zeros_like(accw_ref)
    mw_ref[...] = jnp.full_like(mw_ref, NEG)
    prev_bias = tile_rows(jnp.where(rel > jnp.where(kd > 0, d0, KEY_TILE), 0.0, NEG))
    diag_bias = tile_rows(jnp.where(rel <= d0, 0.0, NEG))
    kw_p, vw_p = kw_ref[pl.ds(off_p, KEY_TILE), :], vwa_ref[pl.ds(off_p, KEY_TILE), :]
    kw_d, vw_d = kw_ref[pl.ds(off_d, KEY_TILE), :], vwa_ref[pl.ds(off_d, KEY_TILE), :]
    for kv in range(N_KV):
        flash_step(accw_ref, mw_ref, kv, _dot_nt(q128[kv], kw_p) + prev_bias, vw_p)
    for kv in range(N_KV):
        flash_step(accw_ref, mw_ref, kv, _dot_nt(q128[kv], kw_d) + diag_bias, vw_d)
    for kv in range(N_KV):
        o_win = flash_result(accw_ref, kv)
        for g in range(GQA):
            col = 3 * (kv * GQA + g)
            rs = slice(g * Q_BLOCK, (g + 1) * Q_BLOCK)
            part_ref[kv, rs, :] = gate[:, col:col + 1] * o_cmp[kv][rs] + gate[:, col + 2:col + 3] * o_win[rs]

    t_q = s0 + lane
    forced = (sub == 0) | (sub == (t_q >> 6))
    val = [jnp.where(forced, FORCE, jnp.where(sub * SLC_LEN <= t_q, imp_t[kv], -1.0)) for kv in range(N_KV)]
    sel_t = [jnp.zeros((N_SLC, Q_BLOCK), F32) for _ in range(N_KV)]
    sub_f = sub.astype(F32)
    for _ in range(SLC_TOPN):
        for kv in range(N_KV):
            mx = jnp.max(val[kv], axis=0, keepdims=True)
            first = jnp.min(jnp.where(val[kv] == mx, sub_f, float(N_SLC)), axis=0, keepdims=True)
            pick = sub_f == first
            sel_t[kv] = jnp.where(pick, 1.0, sel_t[kv])
            val[kv] = jnp.where(pick, -jnp.inf, val[kv])
    for kv in range(N_KV):
        bias = ((sel_t[kv].T - 1.0) * (-NEG)).astype(BF16)
        qa_ref[kv] = jnp.concatenate([q128[kv], jnp.concatenate([bias] * GQA, axis=0)], axis=1)

    acc_ref[...] = jnp.zeros_like(acc_ref)
    m_ref[...] = jnp.full_like(m_ref, NEG)

    def scores_into(dst, kt):
        k_t = ksa_ref[pl.ds(pl.multiple_of(kt * KEY_TILE, KEY_TILE), KEY_TILE), :]
        for kv in range(N_KV):
            dst[kv] = _dot_nt(qa_ref[kv], k_t)

    def consume(src, kt, bias):
        v_t = vsa_ref[pl.ds(pl.multiple_of(kt * KEY_TILE, KEY_TILE), KEY_TILE), :]
        for kv in range(N_KV):
            flash_step(acc_ref, m_ref, kv, src[kv] if bias is None else src[kv] + bias, v_t)

    n_pairs = kd // 2
    scores_into(sa_ref, 0)

    def slc_pair(j, carry):
        a = 2 * j
        scores_into(sb_ref, a + 1)
        consume(sa_ref, a, None)
        scores_into(sa_ref, a + 2)
        consume(sb_ref, a + 1, None)
        return carry

    lax.fori_loop(0, n_pairs, slc_pair, 0)
    left = 2 * n_pairs
    scores_into(sb_ref, kd)
    consume(sa_ref, left, jnp.where(left < kd, 0.0, NEG))
    consume(sb_ref, kd, tile_rows(jnp.where(rel <= d0, 0.0, NEG)))

    for kv in range(N_KV):
        o_slc = flash_result(acc_ref, kv)
        pieces = []
        for g in range(GQA):
            col = 3 * (kv * GQA + g)
            rs = slice(g * Q_BLOCK, (g + 1) * Q_BLOCK)
            pieces.append(part_ref[kv, rs, :] + gate[:, col + 1:col + 2] * o_slc[rs])
        for mm in range(2):
            even, odd = pieces[2 * mm], pieces[2 * mm + 1]
            if kv == 0:
                odd = pltpu.roll(odd, 64, 1)
            else:
                even = pltpu.roll(even, 64, 1)
            slab = kv * 2 + mm
            o_ref[:, slab * LANES:(slab + 1) * LANES] = jnp.where(lane < 64, even, odd)


def _attention(q, kcmp, vcmp, ksa, vsa, kw, vwa, gate, batch, seq):
    nh = kcmp.shape[1]
    assert nh == KEY_TILE
    rows = GQA * Q_BLOCK
    res = lambda w: pl.BlockSpec((None, seq, w), lambda b, i: (b, 0, 0))
    blk = lambda w: pl.BlockSpec((None, Q_BLOCK, w), lambda b, i: (b, i, 0))
    r3 = lambda a, w: a.reshape(batch, seq, w)
    return pl.pallas_call(
        _attn_kernel,
        grid=(batch, seq // Q_BLOCK),
        in_specs=[blk(N_HEADS * LANES),
                  pl.BlockSpec((None, nh, LANES), lambda b, i: (b, 0, 0)),
                  pl.BlockSpec((None, nh, LANES), lambda b, i: (b, 0, 0)),
                  res(2 * LANES), res(2 * LANES), res(LANES), res(2 * LANES), blk(LANES)],
        out_specs=blk(D_ATT),
        out_shape=jax.ShapeDtypeStruct((batch, seq, D_ATT), F32),
        scratch_shapes=[pltpu.VMEM((N_KV, rows, 2 * LANES), BF16), pltpu.VMEM((N_KV, rows, LANES), F32),
                        pltpu.VMEM((N_KV, rows, 2 * LANES), F32), pltpu.VMEM((N_KV, rows, LANES), F32),
                        pltpu.VMEM((N_KV, rows, 2 * LANES), F32), pltpu.VMEM((N_KV, rows, LANES), F32),
                        pltpu.VMEM((N_KV, rows, KEY_TILE), F32), pltpu.VMEM((N_KV, rows, KEY_TILE), F32)],
        compiler_params=_cparams(2),
        name="nsa_attention",
    )(r3(q, N_HEADS * LANES), kcmp, vcmp, r3(ksa, 2 * LANES), r3(vsa, 2 * LANES), r3(kw, LANES), r3(vwa, 2 * LANES),
      r3(gate, LANES))


def _outproj_kernel(x_ref, yl_ref, ya_ref, yc_ref, gn_ref, w_ref, mod_ref, n2_ref, rw_ref, rb_ref,
                    xn_ref, h2_ref, comb_ref, pos_ref, seg_ref):
    tm = x_ref.shape[0]
    a, b = D_LRU, D_LRU + D_ATT
    yl = _rms(yl_ref[...], gn_ref[:, 0:a]).astype(BF16)
    ya = _rms(ya_ref[...], gn_ref[:, a:b]).astype(BF16)
    yc = _rms(yc_ref[...], gn_ref[:, b:D_MODEL]).astype(BF16)
    y = _dot(yl, w_ref[0:a, :]) + _dot(ya, w_ref[a:b, :]) + _dot(yc, w_ref[b:D_MODEL, :])
    xn = x_ref[...] + mod_ref[2:3, :] * y
    xn_ref[...] = xn
    h2 = _rms(xn, n2_ref[...]) * (1.0 + mod_ref[4:5, :]) + mod_ref[3:4, :]

    h_hi = h2.astype(BF16)
    h_mid = (h2 - h_hi.astype(F32)).astype(BF16)
    logit = _dot(h_hi, rw_ref[0]) + (_dot(h_hi, rw_ref[1]) + _dot(h_mid, rw_ref[0])) + rb_ref[...]
    lane = lax.broadcasted_iota(jnp.int32, (tm, LANES), 1)
    ninf = -jnp.inf
    is_g = lane < MOE_GROUPS
    is_e = (lane >= MOE_GROUPS) & (lane < MOE_GROUPS + MOE_GROUPS * MOE_EXPERTS)
    lg_max = jnp.max(jnp.where(is_g, logit, ninf), axis=-1, keepdims=True)
    g_star = jnp.min(jnp.where(is_g & (logit == lg_max), lane, LANES), axis=-1, keepdims=True)
    pg_star = 1.0 / jnp.sum(jnp.where(is_g, jnp.exp(logit - lg_max), 0.0), axis=-1, keepdims=True)
    in_grp = is_e & (((lane - MOE_GROUPS) >> 2) == g_star)
    v1 = jnp.max(jnp.where(in_grp, logit, ninf), axis=-1, keepdims=True)
    i1 = jnp.min(jnp.where(in_grp & (logit == v1), lane, LANES), axis=-1, keepdims=True)
    rest = in_grp & (lane != i1)
    v2 = jnp.max(jnp.where(rest, logit, ninf), axis=-1, keepdims=True)
    i2 = jnp.min(jnp.where(rest & (logit == v2), lane, LANES), axis=-1, keepdims=True)
    d = jnp.exp(v2 - v1)
    pe1 = 1.0 / (1.0 + d)
    pe2 = d / (1.0 + d)
    comb = jnp.where(lane == i1, pe1, jnp.where(lane == i2, pe2, 0.0)) * pg_star

    onehot = jnp.where(lane == g_star, 1.0, 0.0)
    r_i = lax.broadcasted_iota(jnp.int32, (tm, tm), 0)
    c_i = lax.broadcasted_iota(jnp.int32, (tm, tm), 1)
    earlier = jnp.where(c_i < r_i, 1.0, 0.0).astype(BF16)
    rank = jnp.sum(onehot * _dot(earlier, onehot.astype(BF16)), axis=-1, keepdims=True)
    n_g = jnp.sum(onehot, axis=0, keepdims=True)
    n_pad = (((n_g.astype(jnp.int32) + (MOE_ALIGN - 1)) >> MOE_ALIGN_LOG2) << MOE_ALIGN_LOG2).astype(F32)
    lane1 = lax.broadcasted_iota(jnp.int32, (1, LANES), 1)
    start = jnp.zeros((1, LANES), F32)
    below = jnp.zeros((1, 1), F32)
    for g in range(1, MOE_GROUPS):
        below = below + jnp.sum(jnp.where(lane1 == g - 1, n_pad, 0.0), axis=-1, keepdims=True)
        start = start + jnp.where(lane1 == g, below, 0.0)
    pos = jnp.sum(onehot * start, axis=-1, keepdims=True) + rank
    pos_rep = jnp.broadcast_to(pos, (tm, LANES))
    slot = lax.broadcasted_iota(jnp.int32, (tm, MOE_SLOTS), 1).astype(F32)
    take_t = jnp.where(_rep(pos_rep, MOE_SLOTS // LANES, 1) == slot, 1.0, 0.0)
    take = take_t.T
    take = take.astype(BF16)
    h2_ref[...] = _dot(take, h_hi).astype(BF16)
    c_hi = comb.astype(BF16)
    c_rest = comb - c_hi.astype(F32)
    c_mid = c_rest.astype(BF16)
    c_lo = (c_rest - c_mid.astype(F32)).astype(BF16)
    comb_ref[...] = (_dot(take, c_hi) + _dot(take, c_mid)) + _dot(take, c_lo)
    pos_ref[...] = pos_rep
    seg_ref[...] = jnp.zeros(seg_ref.shape, jnp.int32)
    seg_ref[0:1, :] = n_pad.astype(jnp.int32)
    seg_ref[1:2, :] = start.astype(jnp.int32)


def _outproj(x2, yl, ya, yc, gn, w_out, mod_l, n2g, rw, rb, seq):
    t, d = x2.shape
    tps = seq // TM
    row = lambda w: pl.BlockSpec((TM, w), lambda i: (i, 0))
    full = lambda a: pl.BlockSpec(a.shape, lambda i: (0,) * a.ndim)
    return pl.pallas_call(
        _outproj_kernel,
        grid=(t // TM,),
        in_specs=[row(d), row(D_LRU), row(D_ATT), row(D_CONV), full(gn), full(w_out),
                  pl.BlockSpec((None, 6, d), lambda i: (i // tps, 0, 0)), full(n2g), full(rw), full(rb)],
        out_specs=[row(d), pl.BlockSpec((MOE_SLOTS, d), lambda i: (i, 0)), pl.BlockSpec((MOE_SLOTS, LANES), lambda i: (i, 0)),
                   row(LANES), pl.BlockSpec((None, 8, LANES), lambda i: (i, 0, 0))],
        out_shape=[jax.ShapeDtypeStruct((t, d), F32), jax.ShapeDtypeStruct((t // TM * MOE_SLOTS, d), BF16),
                   jax.ShapeDtypeStruct((t // TM * MOE_SLOTS, LANES), F32), jax.ShapeDtypeStruct((t, LANES), F32),
                   jax.ShapeDtypeStruct((t // TM, 8, LANES), jnp.int32)],
        compiler_params=_cparams(1),
        name="out_proj_router",
    )(x2, yl.reshape(t, D_LRU), ya.reshape(t, D_ATT), yc.reshape(t, D_CONV), gn, w_out, mod_l, n2g, rw, rb)


def _moe_plan(seg, n_chunks, n_gtiles):
    rows = seg[:, 0, :MOE_GROUPS]
    start = seg[:, 1, :MOE_GROUPS]
    cum = jnp.cumsum(rows, axis=0)
    before = cum - rows
    total = cum[-1]
    tiles_g = (total + (TM - 1)) // TM
    ends = jnp.cumsum(tiles_g)
    base = (ends - tiles_g) * TM
    k = jnp.arange(n_gtiles, dtype=jnp.int32)
    group_of = jnp.minimum(jnp.sum((k[:, None] >= ends[None, :]).astype(jnp.int32), axis=1), MOE_GROUPS - 1)
    tiles = jnp.concatenate([group_of, ends[-1:]]).astype(jnp.int32)

    per_tile = TM // MOE_ALIGN
    row = jnp.arange(n_gtiles * per_tile, dtype=jnp.int32) * MOE_ALIGN
    grp = jnp.repeat(group_of, per_tile)
    p = row - base[grp]
    chunk = jnp.minimum(jnp.sum((p[:, None] >= cum.T[grp]).astype(jnp.int32), axis=1), n_chunks - 1)
    pick = lambda a: jnp.take_along_axis(a.T[grp], chunk[:, None], axis=1)[:, 0]
    src = chunk * MOE_SLOTS + pick(start) + p - pick(before)
    zero_row = TM + MOE_GROUPS * MOE_ALIGN
    gather = jnp.where(p < total[grp], src, zero_row) // MOE_ALIGN

    slot = jnp.arange(MOE_SLOTS // MOE_ALIGN, dtype=jnp.int32) * MOE_ALIGN
    inside = (slot[None, :, None] >= start[:, None, :]) & (slot[None, :, None] < (start + rows)[:, None, :])
    g_of = jnp.argmax(inside, axis=-1).astype(jnp.int32)
    at = lambda a: jnp.take_along_axis(a, g_of, axis=1)
    dst = base[g_of] + at(before) + slot[None, :] - at(start)
    scatter = jnp.where(jnp.any(inside, axis=-1), dst // MOE_ALIGN, -1)
    return gather.astype(jnp.int32), scatter.reshape(-1).astype(jnp.int32), tiles


def _experts_kernel(gather_ref, tiles_ref, h2s_ref, combs_ref, wg_ref, wu_ref, wd_ref, y_ref,
                    hbuf, cbuf, acc_ref, sem):
    k = pl.program_id(0)
    e = pl.program_id(1)
    n_used = tiles_ref[pl.num_programs(0)]
    per_tile = TM // MOE_ALIGN

    def tile_copies(tile, go):
        slot = tile % 2
        for j in range(per_tile):
            src = pl.multiple_of(gather_ref[tile * per_tile + j] * MOE_ALIGN, MOE_ALIGN)
            dst = pl.ds(j * MOE_ALIGN, MOE_ALIGN)
            go(pltpu.make_async_copy(h2s_ref.at[pl.ds(src, MOE_ALIGN)], hbuf.at[slot, dst], sem.at[0, slot]))
            go(pltpu.make_async_copy(combs_ref.at[pl.ds(src, MOE_ALIGN)], cbuf.at[slot, dst], sem.at[1, slot]))

    @pl.when(e == 0)
    def _():
        acc_ref[...] = jnp.zeros_like(acc_ref)

        @pl.when((k == 0) & (n_used > 0))
        def _():
            tile_copies(k, lambda cp: cp.start())

        @pl.when(k + 1 < n_used)
        def _():
            tile_copies(k + 1, lambda cp: cp.start())

        @pl.when(k < n_used)
        def _():
            tile_copies(k, lambda cp: cp.wait())

    @pl.when(k < n_used)
    def _():
        slot = k % 2
        h2 = hbuf[slot]
        hid = jax.nn.silu(_dot(h2, wg_ref[...])) * _dot(h2, wu_ref[...])
        lane = lax.broadcasted_iota(jnp.int32, (TM, LANES), 1)
        col = MOE_GROUPS + tiles_ref[k] * MOE_EXPERTS + e
        c = jnp.sum(jnp.where(lane == col, cbuf[slot], 0.0), axis=-1, keepdims=True)
        acc_ref[...] += _dot((hid * c).astype(BF16), wd_ref[...])

    @pl.when(e == pl.num_programs(1) - 1)
    def _():
        y_ref[...] = acc_ref[...]


def _experts(gather, tiles, h2s, combs, wg, wu, wd, n_gtiles):
    d = h2s.shape[1]
    weights = lambda s: pl.BlockSpec((None,) + s, lambda k, e, ga, tl: (tl[k] * MOE_EXPERTS + e, 0, 0))
    grid_spec = pltpu.PrefetchScalarGridSpec(
        num_scalar_prefetch=2,
        grid=(n_gtiles, MOE_EXPERTS),
        in_specs=[pl.BlockSpec(memory_space=pl.ANY), pl.BlockSpec(memory_space=pl.ANY),
                  weights((d, MOE_HID)), weights((d, MOE_HID)), weights((MOE_HID, d))],
        out_specs=pl.BlockSpec((TM, d), lambda k, e, ga, tl: (k, 0)),
        scratch_shapes=[pltpu.VMEM((2, TM, d), BF16), pltpu.VMEM((2, TM, LANES), F32), pltpu.VMEM((TM, d), F32),
                        pltpu.SemaphoreType.DMA((2, 2))],
    )
    return pl.pallas_call(
        _experts_kernel,
        grid_spec=grid_spec,
        out_shape=jax.ShapeDtypeStruct((n_gtiles * TM, d), F32),
        compiler_params=_cparams(2),
        name="moe_experts",
    )(gather, tiles, h2s, combs, wg, wu, wd)


def _finalize_kernel(scatter_ref, yg_ref, pos_ref, xn_ref, mod_ref, fg_ref, o_ref, ys_ref, sem, *, final):
    c = pl.program_id(0)
    per_chunk = MOE_SLOTS // MOE_ALIGN

    def chunk_copies(chunk, go):
        buf = chunk % 2
        for j in range(per_chunk):
            src = scatter_ref[chunk * per_chunk + j]

            @pl.when(src >= 0)
            def _(j=j, src=src):
                rows = pl.ds(pl.multiple_of(src * MOE_ALIGN, MOE_ALIGN), MOE_ALIGN)
                go(pltpu.make_async_copy(yg_ref.at[rows], ys_ref.at[buf, pl.ds(j * MOE_ALIGN, MOE_ALIGN)],
                                         sem.at[buf]))

    @pl.when(c == 0)
    def _():
        ys_ref[...] = jnp.zeros_like(ys_ref)
        chunk_copies(c, lambda cp: cp.start())

    @pl.when(c + 1 < pl.num_programs(0))
    def _():
        chunk_copies(c + 1, lambda cp: cp.start())

    chunk_copies(c, lambda cp: cp.wait())

    tm = xn_ref.shape[0]
    y = ys_ref[c % 2]
    hi = y.astype(BF16)
    rest = y - hi.astype(F32)
    mid = rest.astype(BF16)
    lo = (rest - mid.astype(F32)).astype(BF16)
    slot = lax.broadcasted_iota(jnp.int32, (tm, MOE_SLOTS), 1).astype(F32)
    take_t = jnp.where(_rep(pos_ref[...], MOE_SLOTS // LANES, 1) == slot, 1.0, 0.0).astype(BF16)
    y_tok = (_dot(take_t, hi) + _dot(take_t, mid)) + _dot(take_t, lo)
    xo = xn_ref[...] + mod_ref[5:6, :] * y_tok
    if final:
        xo = _rms(xo, fg_ref[...])
    o_ref[...] = xo


def _finalize(scatter, yg, pos, xn, mod_l, fg, seq, final):
    t, d = xn.shape
    tps = seq // TM
    grid_spec = pltpu.PrefetchScalarGridSpec(
        num_scalar_prefetch=1,
        grid=(t // TM,),
        in_specs=[pl.BlockSpec(memory_space=pl.ANY),
                  pl.BlockSpec((TM, LANES), lambda i, tb: (i, 0)),
                  pl.BlockSpec((TM, d), lambda i, tb: (i, 0)),
                  pl.BlockSpec((None, 6, d), lambda i, tb: (i // tps, 0, 0)),
                  pl.BlockSpec((1, d), lambda i, tb: (0, 0))],
        out_specs=pl.BlockSpec((TM, d), lambda i, tb: (i, 0)),
        scratch_shapes=[pltpu.VMEM((2, MOE_SLOTS, d), F32), pltpu.SemaphoreType.DMA((2,))],
    )
    return pl.pallas_call(
        functools.partial(_finalize_kernel, final=final),
        grid_spec=grid_spec,
        out_shape=jax.ShapeDtypeStruct((t, d), F32),
        compiler_params=_cparams(1),
        name="moe_finalize",
    )(scatter, yg, pos, xn, mod_l, fg)


def _permute_w_in(w):
    gate0 = D_LRU * 2 + D_ATT + 6 * LANES
    cv0 = gate0 + 3 * N_HEADS
    pad = jnp.zeros((w.shape[0], IN_PAD - w.shape[1]), w.dtype)
    return jnp.concatenate([w[:, :gate0], w[:, cv0:], w[:, gate0:cv0], pad], axis=1).astype(BF16)


def _router_weights(rg_w, rg_b, re_w, re_b):
    d = rg_w.shape[0]
    ne = MOE_GROUPS * MOE_EXPERTS
    w = jnp.concatenate([rg_w, jnp.transpose(re_w, (1, 0, 2)).reshape(d, ne),
                         jnp.zeros((d, LANES - MOE_GROUPS - ne), F32)], axis=1)
    b = jnp.concatenate([rg_b, re_b.reshape(ne), jnp.zeros((LANES - MOE_GROUPS - ne,), F32)])[None, :]
    w_hi = w.astype(BF16)
    w_mid = (w - w_hi.astype(F32)).astype(BF16)
    return jnp.stack([w_hi, w_mid]), b


def kernel(x, c, positions, ada_w, ada_b, norm1_g, norm2_g, w_in, lru_conv_w, lru_conv_b, lru_wa, lru_ba, lru_wx, lru_bx, lru_lambda, cmp_k_w1, cmp_k_w2, cmp_v_w1, cmp_v_w2, cmp_pos_k, cmp_pos_v, cnv_dw_w, cnv_dw_b, cnv_ln_g, cnv_ln_b, out_norm_g, w_out, moe_rg_w, moe_rg_b, moe_re_w, moe_re_b, moe_w_gate, moe_w_up, moe_w_down, final_norm_g):
    batch, seq, d = x.shape
    depth = ada_w.shape[0]
    assert d == D_MODEL and seq == N_SLC * SLC_LEN and seq % KEY_TILE == 0
    t = batch * seq
    ne = MOE_GROUPS * MOE_EXPERTS
    n_chunks = t // TM
    n_gtiles = n_chunks + -(-n_chunks * MOE_GROUPS * (MOE_ALIGN - 1) // TM) + MOE_GROUPS

    cos, sin = _rope_tables(positions)
    mod = _modulation(c, ada_w, ada_b)
    x2 = x.reshape(t, d)
    fg = final_norm_g[None, :]
    for l in range(depth):
        (xl, gl, q, kc, vc, ksa, vsa, kw, vwa, gate, cv) = _inproj(
            x2, mod[l], norm1_g[l][None, :], _permute_w_in(w_in[l]), cos, sin, seq)
        y_lru, y_cnv = _mixer(xl, gl, cv, lru_conv_w[l], lru_conv_b[l], lru_wa[l], lru_ba[l], lru_wx[l], lru_bx[l],
                              lru_lambda[l], cnv_dw_w[l], cnv_dw_b[l], cnv_ln_g[l], cnv_ln_b[l], batch, seq)
        kcmp, vcmp = _compress(kc, vc, cmp_pos_k[l], cmp_k_w1[l], cmp_k_w2[l], cmp_pos_v[l], cmp_v_w1[l], cmp_v_w2[l],
                               batch, seq)
        y_att = _attention(q, kcmp, vcmp, ksa, vsa, kw, vwa, gate, batch, seq)
        rw, rb = _router_weights(moe_rg_w[l], moe_rg_b[l], moe_re_w[l], moe_re_b[l])
        xn, h2, comb, pos, seg = _outproj(x2, y_lru, y_att, y_cnv, out_norm_g[l][None, :], w_out[l].astype(BF16),
                                          mod[l], norm2_g[l][None, :], rw, rb, seq)
        gather, scatter, tiles = _moe_plan(seg, n_chunks, n_gtiles)
        yg = _experts(gather, tiles, h2, comb,
                      moe_w_gate[l].reshape(ne, d, MOE_HID).astype(BF16),
                      moe_w_up[l].reshape(ne, d, MOE_HID).astype(BF16),
                      moe_w_down[l].reshape(ne, MOE_HID, d).astype(BF16), n_gtiles)
        x2 = _finalize(scatter, yg, pos, xn, mod[l], fg, seq, final=(l == depth - 1))
    return x2.reshape(batch, seq, d)
```

```python
import functools

import jax
import jax.numpy as jnp
from jax import lax
from jax.experimental import pallas as pl
from jax.experimental.pallas import tpu as pltpu

F32 = jnp.float32
BF16 = jnp.bfloat16

D_MODEL = 1024
D_LRU = 256
D_ATT = 512
D_CONV = 256
LRU_BLOCKS = 4
LRU_CONV_W = 4
LRU_C = 8.0
HEAD_DIM = 64
N_HEADS = 8
N_KV = 2
GQA = 4
ROPE_THETA = 10000.0
CMP_LEN = 32
CMP_STRIDE = 16
SLC_LEN = 64
SLC_TOPN = 16
WINDOW = 512
CONV_K = 31
MOE_GROUPS = 4
MOE_EXPERTS = 4
MOE_HID = 512
EPS = 1e-6
NEG = -1e30
FORCE = 1e9
LOG2_E = 1.4426950408889634

LANES = 128
Q_BLOCK = 128
KEY_TILE = 512
N_SLC = 128
TM = 512
MOE_ALIGN_LOG2 = 4
MOE_ALIGN = 1 << MOE_ALIGN_LOG2
MOE_SLOTS = 640
TC = 512
SCAN_ROWS = 64
LRU_TAIL = 8
CNV_TAIL = 32
IN_PAD = 2432
VMEM_LIMIT = 56 * 1024 * 1024


def _cparams(n_axes, vmem=VMEM_LIMIT):
    return pltpu.CompilerParams(dimension_semantics=("arbitrary",) * n_axes, vmem_limit_bytes=vmem)


def _dot(a, b):
    return jnp.dot(a, b, preferred_element_type=F32)


def _dot_nt(a, b):
    return lax.dot_general(a, b, (((1,), (1,)), ((), ())), preferred_element_type=F32)


def _rep(v, n, axis):
    return jnp.concatenate([v] * n, axis=axis)


def _rms(v, g):
    return v * lax.rsqrt(jnp.mean(v * v, axis=-1, keepdims=True) + EPS) * g


def _rope_kernel(pos_ref, inv_ref, sign_ref, cos_ref, sin_ref):
    ang = pos_ref[...].astype(F32) * inv_ref[...]
    cos_ref[...] = jnp.cos(ang)
    sin_ref[...] = jnp.sin(ang) * sign_ref[...]


def _rope_tables(positions):
    t = positions.size
    inv = ROPE_THETA ** (-jnp.arange(0, HEAD_DIM, 2, dtype=F32) / HEAD_DIM)
    inv128 = jnp.tile(inv, 4)[None, :]
    sign128 = jnp.tile(jnp.concatenate([-jnp.ones((32,), F32), jnp.ones((32,), F32)]), 2)[None, :]
    tr = 1024
    return pl.pallas_call(
        _rope_kernel,
        grid=(t // tr,),
        in_specs=[pl.BlockSpec((tr, 1), lambda i: (i, 0)),
                  pl.BlockSpec((1, LANES), lambda i: (0, 0)),
                  pl.BlockSpec((1, LANES), lambda i: (0, 0))],
        out_specs=[pl.BlockSpec((tr, LANES), lambda i: (i, 0))] * 2,
        out_shape=[jax.ShapeDtypeStruct((t, LANES), F32)] * 2,
        compiler_params=_cparams(1),
        name="rope_tables",
    )(positions.reshape(t, 1), inv128, sign128)


def _mod_kernel(c_ref, w_ref, b_ref, o_ref):
    sc = jax.nn.silu(c_ref[...])
    o_ref[...] = _dot(sc.astype(BF16), w_ref[...].astype(BF16)) + b_ref[...]


def _modulation(c, ada_w, ada_b):
    nl, d, n6 = ada_w.shape
    b = c.shape[0]
    rows = 16
    cp = jnp.zeros((rows, d), F32).at[:b].set(c)
    tn = 1536
    out = pl.pallas_call(
        _mod_kernel,
        grid=(nl, n6 // tn),
        in_specs=[pl.BlockSpec((rows, d), lambda l, j: (0, 0)),
                  pl.BlockSpec((None, d, tn), lambda l, j: (l, 0, j)),
                  pl.BlockSpec((None, 1, tn), lambda l, j: (l, 0, j))],
        out_specs=pl.BlockSpec((None, rows, tn), lambda l, j: (l, 0, j)),
        out_shape=jax.ShapeDtypeStruct((nl, rows, n6), F32),
        compiler_params=_cparams(2),
        name="adaln_mod",
    )(cp, ada_w, ada_b.reshape(nl, 1, n6))
    return out[:, :b].reshape(nl, b, 6, d)


def _inproj_kernel(x_ref, mod_ref, g_ref, w_ref, cos_ref, sin_ref,
                   xl_ref, gl_ref, q_ref, kc_ref, vc_ref, ksa_ref, vsa_ref, kw_ref, vwa_ref, gate_ref, cv_ref,
                   *, tiles_per_seq):
    tm = x_ref.shape[0]
    x = x_ref[...]
    h = _rms(x, g_ref[...]) * (1.0 + mod_ref[1:2, :]) + mod_ref[0:1, :]
    p = _dot(h.astype(BF16), w_ref[...])

    cos = cos_ref[...]
    sin = sin_ref[...]
    lane = lax.broadcasted_iota(jnp.int32, (tm, LANES), 1)
    first_half = (lane & 63) < 32

    def rope(v):
        rot = jnp.where(first_half, pltpu.roll(v, 96, 1), pltpu.roll(v, 32, 1))
        return v * cos + rot * sin

    xl_ref[...] = p[:, 0:256]
    gl_ref[...] = p[:, 256:512]
    scale = HEAD_DIM ** -0.5 * LOG2_E
    low = lane < 64
    for m in range(4):
        slab = rope(p[:, 512 + 128 * m:640 + 128 * m]) * scale
        swapped = pltpu.roll(slab, 64, 1)
        for hh in range(2):
            head = 2 * m + hh
            kv = head // GQA
            src = slab if hh == kv else swapped
            keep = low if kv == 0 else jnp.logical_not(low)
            q_ref[:, head * LANES:(head + 1) * LANES] = jnp.where(keep, src, 0.0).astype(BF16)
    kc_ref[...] = rope(p[:, 1024:1152])
    vc_ref[...] = p[:, 1152:1280]
    row = lax.broadcasted_iota(jnp.int32, (tm, LANES), 0)
    s_base = (pl.program_id(0) % tiles_per_seq) * tm
    onehot = jnp.where(lane == ((s_base + row) >> 6), 1.0, 0.0).astype(BF16)
    ones = jnp.ones((tm, LANES), BF16)
    ksa_ref[:, 0:LANES] = rope(p[:, 1280:1408]).astype(BF16)
    ksa_ref[:, LANES:2 * LANES] = onehot
    vsa_ref[:, 0:LANES] = p[:, 1408:1536].astype(BF16)
    vsa_ref[:, LANES:2 * LANES] = ones
    kw_ref[...] = rope(p[:, 1536:1664]).astype(BF16)
    vwa_ref[:, 0:LANES] = p[:, 1664:1792].astype(BF16)
    vwa_ref[:, LANES:2 * LANES] = ones
    cv_ref[...] = p[:, 1792:2304]
    gate_ref[...] = jax.nn.sigmoid(p[:, 2304:2432])


def _inproj(x2, mod_l, g, w_p, cos, sin, seq):
    t, d = x2.shape
    tps = seq // TM
    row = lambda w: pl.BlockSpec((TM, w), lambda i: (i, 0))
    outs = [(256, F32), (256, F32), (N_HEADS * LANES, BF16), (LANES, F32), (LANES, F32), (2 * LANES, BF16),
            (2 * LANES, BF16), (LANES, BF16), (2 * LANES, BF16), (LANES, F32), (2 * D_CONV, F32)]
    return pl.pallas_call(
        functools.partial(_inproj_kernel, tiles_per_seq=tps),
        grid=(t // TM,),
        in_specs=[row(d),
                  pl.BlockSpec((None, 6, d), lambda i: (i // tps, 0, 0)),
                  pl.BlockSpec((1, d), lambda i: (0, 0)),
                  pl.BlockSpec((d, IN_PAD), lambda i: (0, 0)),
                  row(LANES), row(LANES)],
        out_specs=[row(w) for w, _ in outs],
        out_shape=[jax.ShapeDtypeStruct((t, w), dt) for w, dt in outs],
        compiler_params=_cparams(1),
        name="in_proj",
    )(x2, mod_l, g, w_p, cos, sin)


def _shift_rows(v, s, fill, row):
    return jnp.where(row < s, fill, pltpu.roll(v, s, 0))


def _causal_taps(ext, w_ref, bias, width, tail, tc):
    acc = bias
    for b in range(min(8, width)):
        shifted = pltpu.roll(ext, b, 0) if b else ext
        for k in range(width):
            back = width - 1 - k
            if back % 8 == b:
                start = tail - (back - b)
                acc = acc + w_ref[k:k + 1, :] * shifted[start:start + tc, :]
    return acc


def _mixer_kernel(xl_ref, gl_ref, cv_ref, lcw_ref, lcb_ref, wa_ref, ba_ref, wx_ref, bx_ref, lam_ref,
                  cw_ref, cb_ref, lng_ref, lnb_ref, ylru_ref, ycnv_ref,
                  extl, extc, abuf, ubuf, hcar):
    tc = xl_ref.shape[0]

    @pl.when(pl.program_id(1) == 0)
    def _():
        extl[0:LRU_TAIL, :] = jnp.zeros((LRU_TAIL, D_LRU), F32)
        extc[0:CNV_TAIL, :] = jnp.zeros((CNV_TAIL, D_CONV), F32)
        hcar[...] = jnp.zeros_like(hcar)

    xl = xl_ref[...]
    extl[LRU_TAIL:LRU_TAIL + tc, :] = xl
    xc = _causal_taps(extl[...], lcw_ref, lcb_ref[...], LRU_CONV_W, LRU_TAIL, tc)
    extl[0:LRU_TAIL, :] = xl[tc - LRU_TAIL:tc, :]
    xcb = xc.astype(BF16)
    r = jax.nn.sigmoid(_dot(xcb, wa_ref[...]) + ba_ref[...])
    gi = jax.nn.sigmoid(_dot(xcb, wx_ref[...]) + bx_ref[...])
    log_a = LRU_C * r * jax.nn.log_sigmoid(lam_ref[...])
    th = jnp.tanh(log_a)
    one_minus_a2 = -2.0 * th / (1.0 - th)
    abuf[...] = jnp.exp(log_a)
    ubuf[...] = jnp.sqrt(one_minus_a2) * (gi * xc)

    row = lax.broadcasted_iota(jnp.int32, (SCAN_ROWS, D_LRU), 0)

    def scan_chunk(c, h_prev):
        off = pl.multiple_of(c * SCAN_ROWS, SCAN_ROWS)
        a = abuf[pl.ds(off, SCAN_ROWS), :]
        b = ubuf[pl.ds(off, SCAN_ROWS), :]
        s = 1
        while s < SCAN_ROWS:
            b = b + a * _shift_rows(b, s, 0.0, row)
            a = a * _shift_rows(a, s, 1.0, row)
            s *= 2
        h = b + a * h_prev
        ylru_ref[pl.ds(off, SCAN_ROWS), :] = h * jax.nn.gelu(gl_ref[pl.ds(off, SCAN_ROWS), :])
        return h[SCAN_ROWS - 1:SCAN_ROWS, :]

    hcar[0:1, :] = lax.fori_loop(0, tc // SCAN_ROWS, scan_chunk, hcar[0:1, :])

    u = cv_ref[...]
    v = u[:, 0:D_CONV] * jax.nn.sigmoid(u[:, D_CONV:2 * D_CONV])
    extc[CNV_TAIL:CNV_TAIL + tc, :] = v
    acc = _causal_taps(extc[...], cw_ref, cb_ref[...], CONV_K, CNV_TAIL, tc)
    extc[0:CNV_TAIL, :] = v[tc - CNV_TAIL:tc, :]
    mu = jnp.mean(acc, axis=-1, keepdims=True)
    var = jnp.mean(jnp.square(acc - mu), axis=-1, keepdims=True)
    y = (acc - mu) * lax.rsqrt(var + EPS) * lng_ref[...] + lnb_ref[...]
    ycnv_ref[...] = jax.nn.silu(y)


def _block_diag(w):
    n, rows, cols = w.shape
    eye = jnp.eye(n, dtype=bool)
    return jnp.where(eye[:, None, :, None], w[:, :, None, :], 0.0).reshape(n * rows, n * cols)


def _mixer(xl, gl, cv, lcw, lcb, wa, ba, wx, bx, lam, cw, cb, lng, lnb, batch, seq):
    seqrow = lambda w: pl.BlockSpec((None, TC, w), lambda b, t: (b, t, 0))
    full = lambda a: pl.BlockSpec(a.shape, lambda b, t: (0,) * a.ndim)
    small = [lcw, lcb[None, :], _block_diag(wa).astype(BF16), ba[None, :], _block_diag(wx).astype(BF16), bx[None, :],
             lam[None, :], jnp.zeros((32, D_CONV), F32).at[:CONV_K].set(cw), cb[None, :], lng[None, :], lnb[None, :]]
    return pl.pallas_call(
        _mixer_kernel,
        grid=(batch, seq // TC),
        in_specs=[seqrow(D_LRU), seqrow(D_LRU), seqrow(2 * D_CONV)] + [full(a) for a in small],
        out_specs=[seqrow(D_LRU), seqrow(D_CONV)],
        out_shape=[jax.ShapeDtypeStruct((batch, seq, D_LRU), F32), jax.ShapeDtypeStruct((batch, seq, D_CONV), F32)],
        scratch_shapes=[pltpu.VMEM((TC + LRU_TAIL, D_LRU), F32), pltpu.VMEM((TC + CNV_TAIL, D_CONV), F32),
                        pltpu.VMEM((TC, D_LRU), F32), pltpu.VMEM((TC, D_LRU), F32), pltpu.VMEM((8, D_LRU), F32)],
        compiler_params=_cparams(2),
        name="mixer_stream",
    )(xl.reshape(batch, seq, D_LRU), gl.reshape(batch, seq, D_LRU), cv.reshape(batch, seq, 2 * D_CONV), *small)


def _cmp_kernel(kc_ref, vc_ref, posk_ref, posv_ref, wk1_ref, wk2_ref, wv1_ref, wv2_ref, ko_ref, vo_ref, nat_ref):
    nh = ko_ref.shape[0]
    quarter = nh // 4

    def one(t_ref, pos_ref, w1_ref, w2_ref, o_ref):
        first = jnp.zeros((nh, w1_ref.shape[2]), F32)
        second = jnp.zeros((nh, w1_ref.shape[2]), F32)
        for l in range(CMP_STRIDE):
            x = t_ref[pl.ds(l, nh, stride=CMP_STRIDE), :]
            first = first + _dot((x + pos_ref[l:l + 1, :]).astype(BF16), w1_ref[l])
            second = second + _dot((x + pos_ref[CMP_STRIDE + l:CMP_STRIDE + l + 1, :]).astype(BF16),
                                   w1_ref[CMP_STRIDE + l])
        hid = jax.nn.gelu(first + pltpu.roll(second, nh - 1, 0))
        nat_ref[...] = _dot(hid.astype(BF16), w2_ref[...])
        for r in range(4):
            o_ref[r * quarter:(r + 1) * quarter, :] = nat_ref[pl.ds(r, quarter, stride=4), :].astype(o_ref.dtype)

    one(kc_ref, posk_ref, wk1_ref, wk2_ref, ko_ref)
    one(vc_ref, posv_ref, wv1_ref, wv2_ref, vo_ref)


def _cmp_weights(pos, w1, w2):
    hid = w1.shape[1]
    eye = jnp.eye(N_KV, dtype=bool)
    w1 = w1.reshape(CMP_LEN, HEAD_DIM, hid)
    w1x = jnp.where(eye[None, :, None, :, None], w1[:, None, :, None, :], 0.0)
    w1x = w1x.reshape(CMP_LEN, N_KV * HEAD_DIM, N_KV * hid)
    w2x = _block_diag(jnp.broadcast_to(w2, (N_KV,) + w2.shape))
    return jnp.tile(pos, (1, N_KV)), w1x.astype(BF16), w2x.astype(BF16)


def _compress(kc, vc, pos_k, wk1, wk2, pos_v, wv1, wv2, batch, seq):
    nh = seq // CMP_STRIDE
    wide = N_KV * HEAD_DIM
    pk, wk1x, wk2x = _cmp_weights(pos_k, wk1, wk2)
    pv, wv1x, wv2x = _cmp_weights(pos_v, wv1, wv2)
    full = lambda a: pl.BlockSpec(a.shape, lambda b: (0,) * a.ndim)
    seqs = pl.BlockSpec((None, seq, wide), lambda b: (b, 0, 0))
    outs = pl.BlockSpec((None, nh, wide), lambda b: (b, 0, 0))
    return pl.pallas_call(
        _cmp_kernel,
        grid=(batch,),
        in_specs=[seqs, seqs, full(pk), full(pv), full(wk1x), full(wk2x), full(wv1x), full(wv2x)],
        out_specs=[outs, outs],
        out_shape=[jax.ShapeDtypeStruct((batch, nh, wide), BF16)] * 2,
        scratch_shapes=[pltpu.VMEM((nh, wide), F32)],
        compiler_params=_cparams(1),
        name="nsa_compress",
    )(kc.reshape(batch, seq, wide), vc.reshape(batch, seq, wide), pk, pv, wk1x, wk2x, wv1x, wv2x)


def _attn_kernel(q_ref, kcmp_ref, vcmp_ref, ksa_ref, vsa_ref, kw_ref, vwa_ref, gate_ref, o_ref,
                 qa_ref, part_ref, acc_ref, m_ref, accw_ref, mw_ref, sa_ref, sb_ref):
    rows = GQA * Q_BLOCK
    reps = KEY_TILE // LANES
    qb = pl.program_id(1)
    s0 = qb * Q_BLOCK
    kd = s0 // KEY_TILE
    off_d = pl.multiple_of(kd * KEY_TILE, KEY_TILE)
    off_p = pl.multiple_of(jnp.maximum(kd - 1, 0) * KEY_TILE, KEY_TILE)
    d0 = s0 - off_d

    lane = lax.broadcasted_iota(jnp.int32, (Q_BLOCK, LANES), 1)
    sub = lax.broadcasted_iota(jnp.int32, (Q_BLOCK, LANES), 0)
    r_q = lax.broadcasted_iota(jnp.int32, (Q_BLOCK, KEY_TILE), 0)
    c_k = lax.broadcasted_iota(jnp.int32, (Q_BLOCK, KEY_TILE), 1)
    rel = c_k - r_q
    gate = gate_ref[...]

    def tile_rows(plane):
        return _rep(plane, GQA, 0)

    def flash_step(acc, m, kv, s, v_tile):
        m_old = m[kv]
        m_new = jnp.maximum(m_old, jnp.broadcast_to(jnp.max(s, axis=-1, keepdims=True), m_old.shape))
        p = jnp.exp2(s - _rep(m_new, reps, 1))
        acc[kv] = acc[kv] * _rep(jnp.exp2(m_old - m_new), 2, 1) + _dot(p.astype(BF16), v_tile)
        m[kv] = m_new

    def flash_result(acc, kv):
        a = acc[kv]
        return a[:, 0:LANES] / a[:, LANES:2 * LANES]

    q128 = [jnp.concatenate([q_ref[:, (kv * GQA + g) * LANES:(kv * GQA + g + 1) * LANES] for g in range(GQA)], axis=0)
            for kv in range(N_KV)]

    kcmp = kcmp_ref[...]
    vcmp = vcmp_ref[...]
    cmp_end = SLC_LEN * (c_k & (N_SLC - 1)) + CMP_STRIDE * (c_k >> 7) + (CMP_LEN - 1)
    cmp_bias = tile_rows(jnp.where(cmp_end - r_q <= s0, 0.0, NEG))
    has_cmp = tile_rows(jnp.where(s0 + sub >= CMP_LEN - 1, 1.0, 0.0))
    o_cmp, imp_t = [], []
    for kv in range(N_KV):
        s_c = _dot_nt(q128[kv], kcmp) + cmp_bias
        mb = jnp.broadcast_to(jnp.max(s_c, axis=-1, keepdims=True), (rows, LANES))
        e = jnp.exp2(s_c - _rep(mb, reps, 1))
        lb = jnp.broadcast_to(jnp.sum(e, axis=-1, keepdims=True), (rows, LANES))
        inv = has_cmp / lb
        o_cmp.append(_dot(e.astype(BF16), vcmp) * inv)
        e4 = (e[:, 0:N_SLC] + e[:, N_SLC:2 * N_SLC] + e[:, 2 * N_SLC:3 * N_SLC] + e[:, 3 * N_SLC:4 * N_SLC]) * inv
        imp = e4[0:Q_BLOCK] + e4[Q_BLOCK:2 * Q_BLOCK] + e4[2 * Q_BLOCK:3 * Q_BLOCK] + e4[3 * Q_BLOCK:4 * Q_BLOCK]
        imp_t.append(imp.T)

    accw_ref[...] = jnp.zeros_like(accw_ref)
    mw_ref[...] = jnp.full_like(mw_ref, NEG)
    prev_bias = tile_rows(jnp.where(rel > jnp.where(kd > 0, d0, KEY_TILE), 0.0, NEG))
    diag_bias = tile_rows(jnp.where(rel <= d0, 0.0, NEG))
    kw_p, vw_p = kw_ref[pl.ds(off_p, KEY_TILE), :], vwa_ref[pl.ds(off_p, KEY_TILE), :]
    kw_d, vw_d = kw_ref[pl.ds(off_d, KEY_TILE), :], vwa_ref[pl.ds(off_d, KEY_TILE), :]
    for kv in range(N_KV):
        flash_step(accw_ref, mw_ref, kv, _dot_nt(q128[kv], kw_p) + prev_bias, vw_p)
    for kv in range(N_KV):
        flash_step(accw_ref, mw_ref, kv, _dot_nt(q128[kv], kw_d) + diag_bias, vw_d)
    for kv in range(N_KV):
        o_win = flash_result(accw_ref, kv)
        for g in range(GQA):
            col = 3 * (kv * GQA + g)
            rs = slice(g * Q_BLOCK, (g + 1) * Q_BLOCK)
            part_ref[kv, rs, :] = gate[:, col:col + 1] * o_cmp[kv][rs] + gate[:, col + 2:col + 3] * o_win[rs]

    t_q = s0 + lane
    forced = (sub == 0) | (sub == (t_q >> 6))
    val = [jnp.where(forced, FORCE, jnp.where(sub * SLC_LEN <= t_q, imp_t[kv], -1.0)) for kv in range(N_KV)]
    sel_t = [jnp.zeros((N_SLC, Q_BLOCK), F32) for _ in range(N_KV)]
    sub_f = sub.astype(F32)
    for _ in range(SLC_TOPN):
        for kv in range(N_KV):
            mx = jnp.max(val[kv], axis=0, keepdims=True)
            first = jnp.min(jnp.where(val[kv] == mx, sub_f, float(N_SLC)), axis=0, keepdims=True)
            pick = sub_f == first
            sel_t[kv] = jnp.where(pick, 1.0, sel_t[kv])
            val[kv] = jnp.where(pick, -jnp.inf, val[kv])
    for kv in range(N_KV):
        bias = ((sel_t[kv].T - 1.0) * (-NEG)).astype(BF16)
        qa_ref[kv] = jnp.concatenate([q128[kv], jnp.concatenate([bias] * GQA, axis=0)], axis=1)

    acc_ref[...] = jnp.zeros_like(acc_ref)
    m_ref[...] = jnp.full_like(m_ref, NEG)

    def scores_into(dst, kt):
        k_t = ksa_ref[pl.ds(pl.multiple_of(kt * KEY_TILE, KEY_TILE), KEY_TILE), :]
        for kv in range(N_KV):
            dst[kv] = _dot_nt(qa_ref[kv], k_t)

    def consume(src, kt, bias):
        v_t = vsa_ref[pl.ds(pl.multiple_of(kt * KEY_TILE, KEY_TILE), KEY_TILE), :]
        for kv in range(N_KV):
            flash_step(acc_ref, m_ref, kv, src[kv] if bias is None else src[kv] + bias, v_t)

    n_pairs = kd // 2
    scores_into(sa_ref, 0)

    def slc_pair(j, carry):
        a = 2 * j
        scores_into(sb_ref, a + 1)
        consume(sa_ref, a, None)
        scores_into(sa_ref, a + 2)
        consume(sb_ref, a + 1, None)
        return carry

    lax.fori_loop(0, n_pairs, slc_pair, 0)
    causal_bias = tile_rows(jnp.where(rel <= d0, 0.0, NEG))
    kd_odd = (kd & 1) == 1

    @pl.when(kd_odd)
    def _():
        scores_into(sb_ref, kd)
        consume(sa_ref, kd - 1, None)
        consume(sb_ref, kd, causal_bias)

    @pl.when(jnp.logical_not(kd_odd))
    def _():
        consume(sa_ref, kd, causal_bias)

    for kv in range(N_KV):
        o_slc = flash_result(acc_ref, kv)
        pieces = []
        for g in range(GQA):
            col = 3 * (kv * GQA + g)
            rs = slice(g * Q_BLOCK, (g + 1) * Q_BLOCK)
            pieces.append(part_ref[kv, rs, :] + gate[:, col + 1:col + 2] * o_slc[rs])
        for mm in range(2):
            even, odd = pieces[2 * mm], pieces[2 * mm + 1]
            if kv == 0:
                odd = pltpu.roll(odd, 64, 1)
            else:
                even = pltpu.roll(even, 64, 1)
            slab = kv * 2 + mm
            o_ref[:, slab * LANES:(slab + 1) * LANES] = jnp.where(lane < 64, even, odd)


def _attention(q, kcmp, vcmp, ksa, vsa, kw, vwa, gate, batch, seq):
    nh = kcmp.shape[1]
    assert nh == KEY_TILE
    rows = GQA * Q_BLOCK
    res = lambda w: pl.BlockSpec((None, seq, w), lambda b, i: (b, 0, 0))
    blk = lambda w: pl.BlockSpec((None, Q_BLOCK, w), lambda b, i: (b, i, 0))
    r3 = lambda a, w: a.reshape(batch, seq, w)
    return pl.pallas_call(
        _attn_kernel,
        grid=(batch, seq // Q_BLOCK),
        in_specs=[blk(N_HEADS * LANES),
                  pl.BlockSpec((None, nh, LANES), lambda b, i: (b, 0, 0)),
                  pl.BlockSpec((None, nh, LANES), lambda b, i: (b, 0, 0)),
                  res(2 * LANES), res(2 * LANES), res(LANES), res(2 * LANES), blk(LANES)],
        out_specs=blk(D_ATT),
        out_shape=jax.ShapeDtypeStruct((batch, seq, D_ATT), F32),
        scratch_shapes=[pltpu.VMEM((N_KV, rows, 2 * LANES), BF16), pltpu.VMEM((N_KV, rows, LANES), F32),
                        pltpu.VMEM((N_KV, rows, 2 * LANES), F32), pltpu.VMEM((N_KV, rows, LANES), F32),
                        pltpu.VMEM((N_KV, rows, 2 * LANES), F32), pltpu.VMEM((N_KV, rows, LANES), F32),
                        pltpu.VMEM((N_KV, rows, KEY_TILE), F32), pltpu.VMEM((N_KV, rows, KEY_TILE), F32)],
        compiler_params=_cparams(2),
        name="nsa_attention",
    )(r3(q, N_HEADS * LANES), kcmp, vcmp, r3(ksa, 2 * LANES), r3(vsa, 2 * LANES), r3(kw, LANES), r3(vwa, 2 * LANES),
      r3(gate, LANES))


def _outproj_kernel(x_ref, yl_ref, ya_ref, yc_ref, gn_ref, w_ref, mod_ref, n2_ref, rw_ref, rb_ref,
                    xn_ref, h2_ref, comb_ref, pos_ref, seg_ref):
    tm = x_ref.shape[0]
    a, b = D_LRU, D_LRU + D_ATT
    yl = _rms(yl_ref[...], gn_ref[:, 0:a]).astype(BF16)
    ya = _rms(ya_ref[...], gn_ref[:, a:b]).astype(BF16)
    yc = _rms(yc_ref[...], gn_ref[:, b:D_MODEL]).astype(BF16)
    y = _dot(yl, w_ref[0:a, :]) + _dot(ya, w_ref[a:b, :]) + _dot(yc, w_ref[b:D_MODEL, :])
    xn = x_ref[...] + mod_ref[2:3, :] * y
    xn_ref[...] = xn
    h2 = _rms(xn, n2_ref[...]) * (1.0 + mod_ref[4:5, :]) + mod_ref[3:4, :]

    h_hi = h2.astype(BF16)
    h_mid = (h2 - h_hi.astype(F32)).astype(BF16)
    logit = _dot(h_hi, rw_ref[0]) + (_dot(h_hi, rw_ref[1]) + _dot(h_mid, rw_ref[0])) + rb_ref[...]
    lane = lax.broadcasted_iota(jnp.int32, (tm, LANES), 1)
    ninf = -jnp.inf
    is_g = lane < MOE_GROUPS
    is_e = (lane >= MOE_GROUPS) & (lane < MOE_GROUPS + MOE_GROUPS * MOE_EXPERTS)
    lg_max = jnp.max(jnp.where(is_g, logit, ninf), axis=-1, keepdims=True)
    g_star = jnp.min(jnp.where(is_g & (logit == lg_max), lane, LANES), axis=-1, keepdims=True)
    pg_star = 1.0 / jnp.sum(jnp.where(is_g, jnp.exp(logit - lg_max), 0.0), axis=-1, keepdims=True)
    in_grp = is_e & (((lane - MOE_GROUPS) >> 2) == g_star)
    v1 = jnp.max(jnp.where(in_grp, logit, ninf), axis=-1, keepdims=True)
    i1 = jnp.min(jnp.where(in_grp & (logit == v1), lane, LANES), axis=-1, keepdims=True)
    rest = in_grp & (lane != i1)
    v2 = jnp.max(jnp.where(rest, logit, ninf), axis=-1, keepdims=True)
    i2 = jnp.min(jnp.where(rest & (logit == v2), lane, LANES), axis=-1, keepdims=True)
    d = jnp.exp(v2 - v1)
    pe1 = 1.0 / (1.0 + d)
    pe2 = d / (1.0 + d)
    comb = jnp.where(lane == i1, pe1, jnp.where(lane == i2, pe2, 0.0)) * pg_star

    onehot = jnp.where(lane == g_star, 1.0, 0.0)
    r_i = lax.broadcasted_iota(jnp.int32, (tm, tm), 0)
    c_i = lax.broadcasted_iota(jnp.int32, (tm, tm), 1)
    earlier = jnp.where(c_i < r_i, 1.0, 0.0).astype(BF16)
    rank = jnp.sum(onehot * _dot(earlier, onehot.astype(BF16)), axis=-1, keepdims=True)
    n_g = jnp.sum(onehot, axis=0, keepdims=True)
    n_pad = (((n_g.astype(jnp.int32) + (MOE_ALIGN - 1)) >> MOE_ALIGN_LOG2) << MOE_ALIGN_LOG2).astype(F32)
    lane1 = lax.broadcasted_iota(jnp.int32, (1, LANES), 1)
    start = jnp.zeros((1, LANES), F32)
    below = jnp.zeros((1, 1), F32)
    for g in range(1, MOE_GROUPS):
        below = below + jnp.sum(jnp.where(lane1 == g - 1, n_pad, 0.0), axis=-1, keepdims=True)
        start = start + jnp.where(lane1 == g, below, 0.0)
    pos = jnp.sum(onehot * start, axis=-1, keepdims=True) + rank
    pos_rep = jnp.broadcast_to(pos, (tm, LANES))
    slot = lax.broadcasted_iota(jnp.int32, (tm, MOE_SLOTS), 1).astype(F32)
    take_t = jnp.where(_rep(pos_rep, MOE_SLOTS // LANES, 1) == slot, 1.0, 0.0)
    take = take_t.T
    take = take.astype(BF16)
    h2_ref[...] = _dot(take, h_hi).astype(BF16)
    c_hi = comb.astype(BF16)
    c_rest = comb - c_hi.astype(F32)
    c_mid = c_rest.astype(BF16)
    c_lo = (c_rest - c_mid.astype(F32)).astype(BF16)
    comb_ref[...] = (_dot(take, c_hi) + _dot(take, c_mid)) + _dot(take, c_lo)
    pos_ref[...] = pos_rep
    seg_ref[...] = jnp.zeros(seg_ref.shape, jnp.int32)
    seg_ref[0:1, :] = n_pad.astype(jnp.int32)
    seg_ref[1:2, :] = start.astype(jnp.int32)


def _outproj(x2, yl, ya, yc, gn, w_out, mod_l, n2g, rw, rb, seq):
    t, d = x2.shape
    tps = seq // TM
    row = lambda w: pl.BlockSpec((TM, w), lambda i: (i, 0))
    full = lambda a: pl.BlockSpec(a.shape, lambda i: (0,) * a.ndim)
    return pl.pallas_call(
        _outproj_kernel,
        grid=(t // TM,),
        in_specs=[row(d), row(D_LRU), row(D_ATT), row(D_CONV), full(gn), full(w_out),
                  pl.BlockSpec((None, 6, d), lambda i: (i // tps, 0, 0)), full(n2g), full(rw), full(rb)],
        out_specs=[row(d), pl.BlockSpec((MOE_SLOTS, d), lambda i: (i, 0)), pl.BlockSpec((MOE_SLOTS, LANES), lambda i: (i, 0)),
                   row(LANES), pl.BlockSpec((None, 8, LANES), lambda i: (i, 0, 0))],
        out_shape=[jax.ShapeDtypeStruct((t, d), F32), jax.ShapeDtypeStruct((t // TM * MOE_SLOTS, d), BF16),
                   jax.ShapeDtypeStruct((t // TM * MOE_SLOTS, LANES), F32), jax.ShapeDtypeStruct((t, LANES), F32),
                   jax.ShapeDtypeStruct((t // TM, 8, LANES), jnp.int32)],
        compiler_params=_cparams(1),
        name="out_proj_router",
    )(x2, yl.reshape(t, D_LRU), ya.reshape(t, D_ATT), yc.reshape(t, D_CONV), gn, w_out, mod_l, n2g, rw, rb)


def _moe_plan(seg, n_chunks, n_gtiles):
    rows = seg[:, 0, :MOE_GROUPS]
    start = seg[:, 1, :MOE_GROUPS]
    ci = jnp.arange(n_chunks, dtype=jnp.int32)
    gi = jnp.arange(MOE_GROUPS, dtype=jnp.int32)
    cum = jnp.sum(jnp.where((ci[None, :] <= ci[:, None])[:, :, None], rows[None, :, :], 0), axis=1)
    before = cum - rows
    total = jnp.sum(rows, axis=0)
    tiles_g = (total + (TM - 1)) // TM
    ends = jnp.sum(jnp.where(gi[None, :] <= gi[:, None], tiles_g[None, :], 0), axis=1)
    base = (ends - tiles_g) * TM
    group_at = lambda tile: jnp.minimum(jnp.sum((tile[:, None] >= ends[None, :]).astype(jnp.int32), axis=1),
                                        MOE_GROUPS - 1)
    tiles = jnp.concatenate([group_at(jnp.arange(n_gtiles, dtype=jnp.int32)), ends[-1:]]).astype(jnp.int32)

    row = jnp.arange(n_gtiles * (TM // MOE_ALIGN), dtype=jnp.int32) * MOE_ALIGN
    is_g = group_at(row // TM)[:, None] == gi[None, :]
    of_group = lambda v: jnp.sum(jnp.where(is_g, v[None, :], 0), axis=1)
    p = row - of_group(base)
    cum_g = jnp.sum(jnp.where(is_g[:, None, :], cum[None, :, :], 0), axis=2)
    chunk = jnp.minimum(jnp.sum((p[:, None] >= cum_g).astype(jnp.int32), axis=1), n_chunks - 1)
    is_cg = (chunk[:, None] == ci[None, :])[:, :, None] & is_g[:, None, :]
    of_segment = lambda a: jnp.sum(jnp.where(is_cg, a[None, :, :], 0), axis=(1, 2))
    src = chunk * MOE_SLOTS + of_segment(start - before) + p
    zero_row = TM + MOE_GROUPS * MOE_ALIGN
    gather = jnp.where(p < of_group(total), src, zero_row) // MOE_ALIGN

    slot = jnp.arange(MOE_SLOTS // MOE_ALIGN, dtype=jnp.int32) * MOE_ALIGN
    inside = (slot[None, :, None] >= start[:, None, :]) & (slot[None, :, None] < (start + rows)[:, None, :])
    dst = jnp.sum(jnp.where(inside, (base[None, :] + before - start)[:, None, :], 0), axis=-1) + slot[None, :]
    scatter = jnp.where(jnp.any(inside, axis=-1), dst // MOE_ALIGN, -1)
    return gather.astype(jnp.int32), scatter.reshape(-1).astype(jnp.int32), tiles


def _experts_kernel(gather_ref, tiles_ref, h2s_ref, combs_ref, wg_ref, wu_ref, wd_ref, y_ref,
                    hbuf, cbuf, acc_ref, sem):
    k = pl.program_id(0)
    e = pl.program_id(1)
    n_used = tiles_ref[pl.num_programs(0)]
    per_tile = TM // MOE_ALIGN

    def tile_copies(tile, go):
        slot = tile % 2
        for j in range(per_tile):
            src = pl.multiple_of(gather_ref[tile * per_tile + j] * MOE_ALIGN, MOE_ALIGN)
            dst = pl.ds(j * MOE_ALIGN, MOE_ALIGN)
            go(pltpu.make_async_copy(h2s_ref.at[pl.ds(src, MOE_ALIGN)], hbuf.at[slot, dst], sem.at[0, slot]))
            go(pltpu.make_async_copy(combs_ref.at[pl.ds(src, MOE_ALIGN)], cbuf.at[slot, dst], sem.at[1, slot]))

    @pl.when(e == 0)
    def _():
        acc_ref[...] = jnp.zeros_like(acc_ref)

        @pl.when((k == 0) & (n_used > 0))
        def _():
            tile_copies(k, lambda cp: cp.start())

        @pl.when(k + 1 < n_used)
        def _():
            tile_copies(k + 1, lambda cp: cp.start())

        @pl.when(k < n_used)
        def _():
            tile_copies(k, lambda cp: cp.wait())

    @pl.when(k < n_used)
    def _():
        slot = k % 2
        h2 = hbuf[slot]
        hid = jax.nn.silu(_dot(h2, wg_ref[...])) * _dot(h2, wu_ref[...])
        lane = lax.broadcasted_iota(jnp.int32, (TM, LANES), 1)
        col = MOE_GROUPS + tiles_ref[k] * MOE_EXPERTS + e
        c = jnp.sum(jnp.where(lane == col, cbuf[slot], 0.0), axis=-1, keepdims=True)
        acc_ref[...] += _dot((hid * c).astype(BF16), wd_ref[...])

    @pl.when(e == pl.num_programs(1) - 1)
    def _():
        y_ref[...] = acc_ref[...]


def _experts(gather, tiles, h2s, combs, wg, wu, wd, n_gtiles):
    d = h2s.shape[1]
    weights = lambda s: pl.BlockSpec((None,) + s, lambda k, e, ga, tl: (tl[k] * MOE_EXPERTS + e, 0, 0))
    grid_spec = pltpu.PrefetchScalarGridSpec(
        num_scalar_prefetch=2,
        grid=(n_gtiles, MOE_EXPERTS),
        in_specs=[pl.BlockSpec(memory_space=pl.ANY), pl.BlockSpec(memory_space=pl.ANY),
                  weights((d, MOE_HID)), weights((d, MOE_HID)), weights((MOE_HID, d))],
        out_specs=pl.BlockSpec((TM, d), lambda k, e, ga, tl: (k, 0)),
        scratch_shapes=[pltpu.VMEM((2, TM, d), BF16), pltpu.VMEM((2, TM, LANES), F32), pltpu.VMEM((TM, d), F32),
                        pltpu.SemaphoreType.DMA((2, 2))],
    )
    return pl.pallas_call(
        _experts_kernel,
        grid_spec=grid_spec,
        out_shape=jax.ShapeDtypeStruct((n_gtiles * TM, d), F32),
        compiler_params=_cparams(2),
        name="moe_experts",
    )(gather, tiles, h2s, combs, wg, wu, wd)


def _finalize_kernel(scatter_ref, yg_ref, pos_ref, xn_ref, mod_ref, fg_ref, o_ref, ys_ref, sem, *, final):
    c = pl.program_id(0)
    per_chunk = MOE_SLOTS // MOE_ALIGN

    def chunk_copies(chunk, go):
        buf = chunk % 2
        for j in range(per_chunk):
            src = scatter_ref[chunk * per_chunk + j]

            @pl.when(src >= 0)
            def _(j=j, src=src):
                rows = pl.ds(pl.multiple_of(src * MOE_ALIGN, MOE_ALIGN), MOE_ALIGN)
                go(pltpu.make_async_copy(yg_ref.at[rows], ys_ref.at[buf, pl.ds(j * MOE_ALIGN, MOE_ALIGN)],
                                         sem.at[buf]))

    @pl.when(c == 0)
    def _():
        ys_ref[...] = jnp.zeros_like(ys_ref)
        chunk_copies(c, lambda cp: cp.start())

    @pl.when(c + 1 < pl.num_programs(0))
    def _():
        chunk_copies(c + 1, lambda cp: cp.start())

    chunk_copies(c, lambda cp: cp.wait())

    tm = xn_ref.shape[0]
    y = ys_ref[c % 2]
    hi = y.astype(BF16)
    rest = y - hi.astype(F32)
    mid = rest.astype(BF16)
    lo = (rest - mid.astype(F32)).astype(BF16)
    slot = lax.broadcasted_iota(jnp.int32, (tm, MOE_SLOTS), 1).astype(F32)
    take_t = jnp.where(_rep(pos_ref[...], MOE_SLOTS // LANES, 1) == slot, 1.0, 0.0).astype(BF16)
    y_tok = (_dot(take_t, hi) + _dot(take_t, mid)) + _dot(take_t, lo)
    xo = xn_ref[...] + mod_ref[5:6, :] * y_tok
    if final:
        xo = _rms(xo, fg_ref[...])
    o_ref[...] = xo


def _finalize(scatter, yg, pos, xn, mod_l, fg, seq, final):
    t, d = xn.shape
    tps = seq // TM
    grid_spec = pltpu.PrefetchScalarGridSpec(
        num_scalar_prefetch=1,
        grid=(t // TM,),
        in_specs=[pl.BlockSpec(memory_space=pl.ANY),
                  pl.BlockSpec((TM, LANES), lambda i, tb: (i, 0)),
                  pl.BlockSpec((TM, d), lambda i, tb: (i, 0)),
                  pl.BlockSpec((None, 6, d), lambda i, tb: (i // tps, 0, 0)),
                  pl.BlockSpec((1, d), lambda i, tb: (0, 0))],
        out_specs=pl.BlockSpec((TM, d), lambda i, tb: (i, 0)),
        scratch_shapes=[pltpu.VMEM((2, MOE_SLOTS, d), F32), pltpu.SemaphoreType.DMA((2,))],
    )
    return pl.pallas_call(
        functools.partial(_finalize_kernel, final=final),
        grid_spec=grid_spec,
        out_shape=jax.ShapeDtypeStruct((t, d), F32),
        compiler_params=_cparams(1),
        name="moe_finalize",
    )(scatter, yg, pos, xn, mod_l, fg)


def _permute_w_in(w):
    gate0 = D_LRU * 2 + D_ATT + 6 * LANES
    cv0 = gate0 + 3 * N_HEADS
    pad = jnp.zeros((w.shape[0], IN_PAD - w.shape[1]), w.dtype)
    return jnp.concatenate([w[:, :gate0], w[:, cv0:], w[:, gate0:cv0], pad], axis=1).astype(BF16)


def _router_weights(rg_w, rg_b, re_w, re_b):
    d = rg_w.shape[0]
    ne = MOE_GROUPS * MOE_EXPERTS
    w = jnp.concatenate([rg_w, jnp.transpose(re_w, (1, 0, 2)).reshape(d, ne),
                         jnp.zeros((d, LANES - MOE_GROUPS - ne), F32)], axis=1)
    b = jnp.concatenate([rg_b, re_b.reshape(ne), jnp.zeros((LANES - MOE_GROUPS - ne,), F32)])[None, :]
    w_hi = w.astype(BF16)
    w_mid = (w - w_hi.astype(F32)).astype(BF16)
    return jnp.stack([w_hi, w_mid]), b


def kernel(x, c, positions, ada_w, ada_b, norm1_g, norm2_g, w_in, lru_conv_w, lru_conv_b, lru_wa, lru_ba, lru_wx, lru_bx, lru_lambda, cmp_k_w1, cmp_k_w2, cmp_v_w1, cmp_v_w2, cmp_pos_k, cmp_pos_v, cnv_dw_w, cnv_dw_b, cnv_ln_g, cnv_ln_b, out_norm_g, w_out, moe_rg_w, moe_rg_b, moe_re_w, moe_re_b, moe_w_gate, moe_w_up, moe_w_down, final_norm_g):
    batch, seq, d = x.shape
    depth = ada_w.shape[0]
    assert d == D_MODEL and seq == N_SLC * SLC_LEN and seq % KEY_TILE == 0
    t = batch * seq
    ne = MOE_GROUPS * MOE_EXPERTS
    n_chunks = t // TM
    n_gtiles = n_chunks + -(-n_chunks * MOE_GROUPS * (MOE_ALIGN - 1) // TM) + MOE_GROUPS

    cos, sin = _rope_tables(positions)
    mod = _modulation(c, ada_w, ada_b)
    x2 = x.reshape(t, d)
    fg = final_norm_g[None, :]
    for l in range(depth):
        (xl, gl, q, kc, vc, ksa, vsa, kw, vwa, gate, cv) = _inproj(
            x2, mod[l], norm1_g[l][None, :], _permute_w_in(w_in[l]), cos, sin, seq)
        y_lru, y_cnv = _mixer(xl, gl, cv, lru_conv_w[l], lru_conv_b[l], lru_wa[l], lru_ba[l], lru_wx[l], lru_bx[l],
                              lru_lambda[l], cnv_dw_w[l], cnv_dw_b[l], cnv_ln_g[l], cnv_ln_b[l], batch, seq)
        kcmp, vcmp = _compress(kc, vc, cmp_pos_k[l], cmp_k_w1[l], cmp_k_w2[l], cmp_pos_v[l], cmp_v_w1[l], cmp_v_w2[l],
                               batch, seq)
        y_att = _attention(q, kcmp, vcmp, ksa, vsa, kw, vwa, gate, batch, seq)
        rw, rb = _router_weights(moe_rg_w[l], moe_rg_b[l], moe_re_w[l], moe_re_b[l])
        xn, h2, comb, pos, seg = _outproj(x2, y_lru, y_att, y_cnv, out_norm_g[l][None, :], w_out[l].astype(BF16),
                                          mod[l], norm2_g[l][None, :], rw, rb, seq)
        gather, scatter, tiles = _moe_plan(seg, n_chunks, n_gtiles)
        yg = _experts(gather, tiles, h2, comb,
                      moe_w_gate[l].reshape(ne, d, MOE_HID).astype(BF16),
                      moe_w_up[l].reshape(ne, d, MOE_HID).astype(BF16),
                      moe_w_down[l].reshape(ne, MOE_HID, d).astype(BF16), n_gtiles)
        x2 = _finalize(scatter, yg, pos, xn, mod[l], fg, seq, final=(l == depth - 1))
    return x2.reshape(batch, seq, d)
```

```python
import functools

import jax
import jax.numpy as jnp
from jax import lax
from jax.experimental import pallas as pl
from jax.experimental.pallas import tpu as pltpu

F32 = jnp.float32
BF16 = jnp.bfloat16

D_MODEL = 1024
D_LRU = 256
D_ATT = 512
D_CONV = 256
LRU_BLOCKS = 4
LRU_CONV_W = 4
LRU_C = 8.0
HEAD_DIM = 64
N_HEADS = 8
N_KV = 2
GQA = 4
ROPE_THETA = 10000.0
CMP_LEN = 32
CMP_STRIDE = 16
SLC_LEN = 64
SLC_TOPN = 16
WINDOW = 512
CONV_K = 31
MOE_GROUPS = 4
MOE_EXPERTS = 4
MOE_HID = 512
EPS = 1e-6
NEG = -1e30
FORCE = 1e9
LOG2_E = 1.4426950408889634

LANES = 128
Q_BLOCK = 128
KEY_TILE = 512
N_SLC = 128
TM = 512
MOE_ALIGN_LOG2 = 4
MOE_ALIGN = 1 << MOE_ALIGN_LOG2
MOE_SLOTS = 640
TC = 512
SCAN_ROWS = 64
LRU_TAIL = 8
CNV_TAIL = 32
IN_PAD = 2432
VMEM_LIMIT = 56 * 1024 * 1024


def _cparams(n_axes, vmem=VMEM_LIMIT):
    return pltpu.CompilerParams(dimension_semantics=("arbitrary",) * n_axes, vmem_limit_bytes=vmem)


def _dot(a, b):
    return jnp.dot(a, b, preferred_element_type=F32)


def _dot_nt(a, b):
    return lax.dot_general(a, b, (((1,), (1,)), ((), ())), preferred_element_type=F32)


def _rep(v, n, axis):
    return jnp.concatenate([v] * n, axis=axis)


def _rms(v, g):
    return v * lax.rsqrt(jnp.mean(v * v, axis=-1, keepdims=True) + EPS) * g


def _rope_kernel(pos_ref, inv_ref, sign_ref, cos_ref, sin_ref):
    ang = pos_ref[...].astype(F32) * inv_ref[...]
    cos_ref[...] = jnp.cos(ang)
    sin_ref[...] = jnp.sin(ang) * sign_ref[...]


def _rope_tables(positions):
    t = positions.size
    inv = ROPE_THETA ** (-jnp.arange(0, HEAD_DIM, 2, dtype=F32) / HEAD_DIM)
    inv128 = jnp.tile(inv, 4)[None, :]
    sign128 = jnp.tile(jnp.concatenate([-jnp.ones((32,), F32), jnp.ones((32,), F32)]), 2)[None, :]
    tr = 1024
    return pl.pallas_call(
        _rope_kernel,
        grid=(t // tr,),
        in_specs=[pl.BlockSpec((tr, 1), lambda i: (i, 0)),
                  pl.BlockSpec((1, LANES), lambda i: (0, 0)),
                  pl.BlockSpec((1, LANES), lambda i: (0, 0))],
        out_specs=[pl.BlockSpec((tr, LANES), lambda i: (i, 0))] * 2,
        out_shape=[jax.ShapeDtypeStruct((t, LANES), F32)] * 2,
        compiler_params=_cparams(1),
        name="rope_tables",
    )(positions.reshape(t, 1), inv128, sign128)


def _mod_kernel(c_ref, w_ref, b_ref, o_ref):
    sc = jax.nn.silu(c_ref[...])
    o_ref[...] = _dot(sc.astype(BF16), w_ref[...].astype(BF16)) + b_ref[...]


def _modulation(c, ada_w, ada_b):
    nl, d, n6 = ada_w.shape
    b = c.shape[0]
    rows = 16
    cp = jnp.zeros((rows, d), F32).at[:b].set(c)
    tn = 1536
    out = pl.pallas_call(
        _mod_kernel,
        grid=(nl, n6 // tn),
        in_specs=[pl.BlockSpec((rows, d), lambda l, j: (0, 0)),
                  pl.BlockSpec((None, d, tn), lambda l, j: (l, 0, j)),
                  pl.BlockSpec((None, 1, tn), lambda l, j: (l, 0, j))],
        out_specs=pl.BlockSpec((None, rows, tn), lambda l, j: (l, 0, j)),
        out_shape=jax.ShapeDtypeStruct((nl, rows, n6), F32),
        compiler_params=_cparams(2),
        name="adaln_mod",
    )(cp, ada_w, ada_b.reshape(nl, 1, n6))
    return out[:, :b].reshape(nl, b, 6, d)


def _inproj_kernel(x_ref, mod_ref, g_ref, w_ref, cos_ref, sin_ref,
                   xl_ref, gl_ref, q_ref, kc_ref, vc_ref, ksa_ref, vsa_ref, kw_ref, vwa_ref, gate_ref, cv_ref,
                   *, tiles_per_seq):
    tm = x_ref.shape[0]
    x = x_ref[...]
    h = _rms(x, g_ref[...]) * (1.0 + mod_ref[1:2, :]) + mod_ref[0:1, :]
    p = _dot(h.astype(BF16), w_ref[...])

    cos = cos_ref[...]
    sin = sin_ref[...]
    lane = lax.broadcasted_iota(jnp.int32, (tm, LANES), 1)
    first_half = (lane & 63) < 32

    def rope(v):
        rot = jnp.where(first_half, pltpu.roll(v, 96, 1), pltpu.roll(v, 32, 1))
        return v * cos + rot * sin

    xl_ref[...] = p[:, 0:256]
    gl_ref[...] = p[:, 256:512]
    scale = HEAD_DIM ** -0.5 * LOG2_E
    low = lane < 64
    for m in range(4):
        slab = rope(p[:, 512 + 128 * m:640 + 128 * m]) * scale
        swapped = pltpu.roll(slab, 64, 1)
        for hh in range(2):
            head = 2 * m + hh
            kv = head // GQA
            src = slab if hh == kv else swapped
            keep = low if kv == 0 else jnp.logical_not(low)
            q_ref[:, head * LANES:(head + 1) * LANES] = jnp.where(keep, src, 0.0).astype(BF16)
    kc_ref[...] = rope(p[:, 1024:1152])
    vc_ref[...] = p[:, 1152:1280]
    row = lax.broadcasted_iota(jnp.int32, (tm, LANES), 0)
    s_base = (pl.program_id(0) % tiles_per_seq) * tm
    onehot = jnp.where(lane == ((s_base + row) >> 6), 1.0, 0.0).astype(BF16)
    ones = jnp.ones((tm, LANES), BF16)
    ksa_ref[:, 0:LANES] = rope(p[:, 1280:1408]).astype(BF16)
    ksa_ref[:, LANES:2 * LANES] = onehot
    vsa_ref[:, 0:LANES] = p[:, 1408:1536].astype(BF16)
    vsa_ref[:, LANES:2 * LANES] = ones
    kw_ref[...] = rope(p[:, 1536:1664]).astype(BF16)
    vwa_ref[:, 0:LANES] = p[:, 1664:1792].astype(BF16)
    vwa_ref[:, LANES:2 * LANES] = ones
    cv_ref[...] = p[:, 1792:2304]
    gate_ref[...] = jax.nn.sigmoid(p[:, 2304:2432])


def _inproj(x2, mod_l, g, w_p, cos, sin, seq):
    t, d = x2.shape
    tps = seq // TM
    row = lambda w: pl.BlockSpec((TM, w), lambda i: (i, 0))
    outs = [(256, F32), (256, F32), (N_HEADS * LANES, BF16), (LANES, F32), (LANES, F32), (2 * LANES, BF16),
            (2 * LANES, BF16), (LANES, BF16), (2 * LANES, BF16), (LANES, F32), (2 * D_CONV, F32)]
    return pl.pallas_call(
        functools.partial(_inproj_kernel, tiles_per_seq=tps),
        grid=(t // TM,),
        in_specs=[row(d),
                  pl.BlockSpec((None, 6, d), lambda i: (i // tps, 0, 0)),
                  pl.BlockSpec((1, d), lambda i: (0, 0)),
                  pl.BlockSpec((d, IN_PAD), lambda i: (0, 0)),
                  row(LANES), row(LANES)],
        out_specs=[row(w) for w, _ in outs],
        out_shape=[jax.ShapeDtypeStruct((t, w), dt) for w, dt in outs],
        compiler_params=_cparams(1),
        name="in_proj",
    )(x2, mod_l, g, w_p, cos, sin)


def _shift_rows(v, s, fill, row):
    return jnp.where(row < s, fill, pltpu.roll(v, s, 0))


def _causal_taps(ext, w_ref, bias, width, tail, tc):
    acc = bias
    for b in range(min(8, width)):
        shifted = pltpu.roll(ext, b, 0) if b else ext
        for k in range(width):
            back = width - 1 - k
            if back % 8 == b:
                start = tail - (back - b)
                acc = acc + w_ref[k:k + 1, :] * shifted[start:start + tc, :]
    return acc


def _mixer_kernel(xl_ref, gl_ref, cv_ref, lcw_ref, lcb_ref, wa_ref, ba_ref, wx_ref, bx_ref, lam_ref,
                  cw_ref, cb_ref, lng_ref, lnb_ref, ylru_ref, ycnv_ref,
                  extl, extc, abuf, ubuf, hcar):
    tc = xl_ref.shape[0]

    @pl.when(pl.program_id(1) == 0)
    def _():
        extl[0:LRU_TAIL, :] = jnp.zeros((LRU_TAIL, D_LRU), F32)
        extc[0:CNV_TAIL, :] = jnp.zeros((CNV_TAIL, D_CONV), F32)
        hcar[...] = jnp.zeros_like(hcar)

    xl = xl_ref[...]
    extl[LRU_TAIL:LRU_TAIL + tc, :] = xl
    xc = _causal_taps(extl[...], lcw_ref, lcb_ref[...], LRU_CONV_W, LRU_TAIL, tc)
    extl[0:LRU_TAIL, :] = xl[tc - LRU_TAIL:tc, :]
    xcb = xc.astype(BF16)
    r = jax.nn.sigmoid(_dot(xcb, wa_ref[...]) + ba_ref[...])
    gi = jax.nn.sigmoid(_dot(xcb, wx_ref[...]) + bx_ref[...])
    log_a = LRU_C * r * jax.nn.log_sigmoid(lam_ref[...])
    th = jnp.tanh(log_a)
    one_minus_a2 = -2.0 * th / (1.0 - th)
    abuf[...] = jnp.exp(log_a)
    ubuf[...] = jnp.sqrt(one_minus_a2) * (gi * xc)

    row = lax.broadcasted_iota(jnp.int32, (SCAN_ROWS, D_LRU), 0)

    def scan_chunk(c, h_prev):
        off = pl.multiple_of(c * SCAN_ROWS, SCAN_ROWS)
        a = abuf[pl.ds(off, SCAN_ROWS), :]
        b = ubuf[pl.ds(off, SCAN_ROWS), :]
        s = 1
        while s < SCAN_ROWS:
            b = b + a * _shift_rows(b, s, 0.0, row)
            a = a * _shift_rows(a, s, 1.0, row)
            s *= 2
        h = b + a * h_prev
        ylru_ref[pl.ds(off, SCAN_ROWS), :] = h * jax.nn.gelu(gl_ref[pl.ds(off, SCAN_ROWS), :])
        return h[SCAN_ROWS - 1:SCAN_ROWS, :]

    hcar[0:1, :] = lax.fori_loop(0, tc // SCAN_ROWS, scan_chunk, hcar[0:1, :])

    u = cv_ref[...]
    v = u[:, 0:D_CONV] * jax.nn.sigmoid(u[:, D_CONV:2 * D_CONV])
    extc[CNV_TAIL:CNV_TAIL + tc, :] = v
    acc = _causal_taps(extc[...], cw_ref, cb_ref[...], CONV_K, CNV_TAIL, tc)
    extc[0:CNV_TAIL, :] = v[tc - CNV_TAIL:tc, :]
    mu = jnp.mean(acc, axis=-1, keepdims=True)
    var = jnp.mean(jnp.square(acc - mu), axis=-1, keepdims=True)
    y = (acc - mu) * lax.rsqrt(var + EPS) * lng_ref[...] + lnb_ref[...]
    ycnv_ref[...] = jax.nn.silu(y)


def _block_diag(w):
    n, rows, cols = w.shape
    eye = jnp.eye(n, dtype=bool)
    return jnp.where(eye[:, None, :, None], w[:, :, None, :], 0.0).reshape(n * rows, n * cols)


def _mixer(xl, gl, cv, lcw, lcb, wa, ba, wx, bx, lam, cw, cb, lng, lnb, batch, seq):
    seqrow = lambda w: pl.BlockSpec((None, TC, w), lambda b, t: (b, t, 0))
    full = lambda a: pl.BlockSpec(a.shape, lambda b, t: (0,) * a.ndim)
    small = [lcw, lcb[None, :], _block_diag(wa).astype(BF16), ba[None, :], _block_diag(wx).astype(BF16), bx[None, :],
             lam[None, :], jnp.zeros((32, D_CONV), F32).at[:CONV_K].set(cw), cb[None, :], lng[None, :], lnb[None, :]]
    return pl.pallas_call(
        _mixer_kernel,
        grid=(batch, seq // TC),
        in_specs=[seqrow(D_LRU), seqrow(D_LRU), seqrow(2 * D_CONV)] + [full(a) for a in small],
        out_specs=[seqrow(D_LRU), seqrow(D_CONV)],
        out_shape=[jax.ShapeDtypeStruct((batch, seq, D_LRU), F32), jax.ShapeDtypeStruct((batch, seq, D_CONV), F32)],
        scratch_shapes=[pltpu.VMEM((TC + LRU_TAIL, D_LRU), F32), pltpu.VMEM((TC + CNV_TAIL, D_CONV), F32),
                        pltpu.VMEM((TC, D_LRU), F32), pltpu.VMEM((TC, D_LRU), F32), pltpu.VMEM((8, D_LRU), F32)],
        compiler_params=_cparams(2),
        name="mixer_stream",
    )(xl.reshape(batch, seq, D_LRU), gl.reshape(batch, seq, D_LRU), cv.reshape(batch, seq, 2 * D_CONV), *small)


def _cmp_kernel(kc_ref, vc_ref, posk_ref, posv_ref, wk1_ref, wk2_ref, wv1_ref, wv2_ref, ko_ref, vo_ref, nat_ref):
    nh = ko_ref.shape[0]
    quarter = nh // 4

    def one(t_ref, pos_ref, w1_ref, w2_ref, o_ref):
        first = jnp.zeros((nh, w1_ref.shape[2]), F32)
        second = jnp.zeros((nh, w1_ref.shape[2]), F32)
        for l in range(CMP_STRIDE):
            x = t_ref[pl.ds(l, nh, stride=CMP_STRIDE), :]
            first = first + _dot((x + pos_ref[l:l + 1, :]).astype(BF16), w1_ref[l])
            second = second + _dot((x + pos_ref[CMP_STRIDE + l:CMP_STRIDE + l + 1, :]).astype(BF16),
                                   w1_ref[CMP_STRIDE + l])
        hid = jax.nn.gelu(first + pltpu.roll(second, nh - 1, 0))
        nat_ref[...] = _dot(hid.astype(BF16), w2_ref[...])
        for r in range(4):
            o_ref[r * quarter:(r + 1) * quarter, :] = nat_ref[pl.ds(r, quarter, stride=4), :].astype(o_ref.dtype)

    one(kc_ref, posk_ref, wk1_ref, wk2_ref, ko_ref)
    one(vc_ref, posv_ref, wv1_ref, wv2_ref, vo_ref)


def _cmp_weights(pos, w1, w2):
    hid = w1.shape[1]
    eye = jnp.eye(N_KV, dtype=bool)
    w1 = w1.reshape(CMP_LEN, HEAD_DIM, hid)
    w1x = jnp.where(eye[None, :, None, :, None], w1[:, None, :, None, :], 0.0)
    w1x = w1x.reshape(CMP_LEN, N_KV * HEAD_DIM, N_KV * hid)
    w2x = _block_diag(jnp.broadcast_to(w2, (N_KV,) + w2.shape))
    return jnp.tile(pos, (1, N_KV)), w1x.astype(BF16), w2x.astype(BF16)


def _compress(kc, vc, pos_k, wk1, wk2, pos_v, wv1, wv2, batch, seq):
    nh = seq // CMP_STRIDE
    wide = N_KV * HEAD_DIM
    pk, wk1x, wk2x = _cmp_weights(pos_k, wk1, wk2)
    pv, wv1x, wv2x = _cmp_weights(pos_v, wv1, wv2)
    full = lambda a: pl.BlockSpec(a.shape, lambda b: (0,) * a.ndim)
    seqs = pl.BlockSpec((None, seq, wide), lambda b: (b, 0, 0))
    outs = pl.BlockSpec((None, nh, wide), lambda b: (b, 0, 0))
    return pl.pallas_call(
        _cmp_kernel,
        grid=(batch,),
        in_specs=[seqs, seqs, full(pk), full(pv), full(wk1x), full(wk2x), full(wv1x), full(wv2x)],
        out_specs=[outs, outs],
        out_shape=[jax.ShapeDtypeStruct((batch, nh, wide), BF16)] * 2,
        scratch_shapes=[pltpu.VMEM((nh, wide), F32)],
        compiler_params=_cparams(1),
        name="nsa_compress",
    )(kc.reshape(batch, seq, wide), vc.reshape(batch, seq, wide), pk, pv, wk1x, wk2x, wv1x, wv2x)


def _attn_kernel(q_ref, kcmp_ref, vcmp_ref, ksa_ref, vsa_ref, kw_ref, vwa_ref, gate_ref, o_ref,
                 qa_ref, part_ref, acc_ref, m_ref, accw_ref, mw_ref, sa_ref, sb_ref):
    rows = GQA * Q_BLOCK
    reps = KEY_TILE // LANES
    qb = pl.program_id(1)
    s0 = qb * Q_BLOCK
    kd = s0 // KEY_TILE
    off_d = pl.multiple_of(kd * KEY_TILE, KEY_TILE)
    off_p = pl.multiple_of(jnp.maximum(kd - 1, 0) * KEY_TILE, KEY_TILE)
    d0 = s0 - off_d

    lane = lax.broadcasted_iota(jnp.int32, (Q_BLOCK, LANES), 1)
    sub = lax.broadcasted_iota(jnp.int32, (Q_BLOCK, LANES), 0)
    r_q = lax.broadcasted_iota(jnp.int32, (Q_BLOCK, KEY_TILE), 0)
    c_k = lax.broadcasted_iota(jnp.int32, (Q_BLOCK, KEY_TILE), 1)
    rel = c_k - r_q
    gate = gate_ref[...]

    def tile_rows(plane):
        return _rep(plane, GQA, 0)

    def flash_step(acc, m, kv, s, v_tile):
        m_old = m[kv]
        m_new = jnp.maximum(m_old, jnp.broadcast_to(jnp.max(s, axis=-1, keepdims=True), m_old.shape))
        p = jnp.exp2(s - _rep(m_new, reps, 1))
        acc[kv] = acc[kv] * _rep(jnp.exp2(m_old - m_new), 2, 1) + _dot(p.astype(BF16), v_tile)
        m[kv] = m_new

    def flash_result(acc, kv):
        a = acc[kv]
        return a[:, 0:LANES] / a[:, LANES:2 * LANES]

    q128 = [jnp.concatenate([q_ref[:, (kv * GQA + g) * LANES:(kv * GQA + g + 1) * LANES] for g in range(GQA)], axis=0)
            for kv in range(N_KV)]

    kcmp = kcmp_ref[...]
    vcmp = vcmp_ref[...]
    cmp_end = SLC_LEN * (c_k & (N_SLC - 1)) + CMP_STRIDE * (c_k >> 7) + (CMP_LEN - 1)
    cmp_bias = tile_rows(jnp.where(cmp_end - r_q <= s0, 0.0, NEG))
    has_cmp = tile_rows(jnp.where(s0 + sub >= CMP_LEN - 1, 1.0, 0.0))
    o_cmp, imp_t = [], []
    for kv in range(N_KV):
        s_c = _dot_nt(q128[kv], kcmp) + cmp_bias
        mb = jnp.broadcast_to(jnp.max(s_c, axis=-1, keepdims=True), (rows, LANES))
        e = jnp.exp2(s_c - _rep(mb, reps, 1))
        lb = jnp.broadcast_to(jnp.sum(e, axis=-1, keepdims=True), (rows, LANES))
        inv = has_cmp / lb
        o_cmp.append(_dot(e.astype(BF16), vcmp) * inv)
        e4 = (e[:, 0:N_SLC] + e[:, N_SLC:2 * N_SLC] + e[:, 2 * N_SLC:3 * N_SLC] + e[:, 3 * N_SLC:4 * N_SLC]) * inv
        imp = e4[0:Q_BLOCK] + e4[Q_BLOCK:2 * Q_BLOCK] + e4[2 * Q_BLOCK:3 * Q_BLOCK] + e4[3 * Q_BLOCK:4 * Q_BLOCK]
        imp_t.append(imp.T)

    accw_ref[...] = jnp.zeros_like(accw_ref)
    mw_ref[...] = jnp.full_like(mw_ref, NEG)
    prev_bias = tile_rows(jnp.where(rel > jnp.where(kd > 0, d0, KEY_TILE), 0.0, NEG))
    diag_bias = tile_rows(jnp.where(rel <= d0, 0.0, NEG))
    kw_p, vw_p = kw_ref[pl.ds(off_p, KEY_TILE), :], vwa_ref[pl.ds(off_p, KEY_TILE), :]
    kw_d, vw_d = kw_ref[pl.ds(off_d, KEY_TILE), :], vwa_ref[pl.ds(off_d, KEY_TILE), :]
    for kv in range(N_KV):
        flash_step(accw_ref, mw_ref, kv, _dot_nt(q128[kv], kw_p) + prev_bias, vw_p)
    for kv in range(N_KV):
        flash_step(accw_ref, mw_ref, kv, _dot_nt(q128[kv], kw_d) + diag_bias, vw_d)
    for kv in range(N_KV):
        o_win = flash_result(accw_ref, kv)
        for g in range(GQA):
            col = 3 * (kv * GQA + g)
            rs = slice(g * Q_BLOCK, (g + 1) * Q_BLOCK)
            part_ref[kv, rs, :] = gate[:, col:col + 1] * o_cmp[kv][rs] + gate[:, col + 2:col + 3] * o_win[rs]

    blk = lax.broadcasted_iota(jnp.int32, (N_SLC, Q_BLOCK), 0)
    t_q = s0 + lax.broadcasted_iota(jnp.int32, (N_SLC, Q_BLOCK), 1)
    forced = (blk == 0) | (blk == (t_q >> 6))
    val = [jnp.where(forced, -jnp.inf, jnp.where(blk * SLC_LEN <= t_q, imp_t[kv], -1.0)) for kv in range(N_KV)]
    blk_f = blk.astype(F32)
    for _ in range(SLC_TOPN - 2):
        for kv in range(N_KV):
            mx = jnp.max(val[kv], axis=0, keepdims=True)
            first = jnp.min(jnp.where(val[kv] == mx, blk_f, float(N_SLC)), axis=0, keepdims=True)
            val[kv] = jnp.where(blk_f == first, -jnp.inf, val[kv])
    for kv in range(N_KV):
        bias = jnp.where(val[kv].T == -jnp.inf, 0.0, NEG).astype(BF16)
        qa_ref[kv] = jnp.concatenate([q128[kv], jnp.concatenate([bias] * GQA, axis=0)], axis=1)

    acc_ref[...] = jnp.zeros_like(acc_ref)
    m_ref[...] = jnp.full_like(m_ref, NEG)

    def scores_into(dst, kt):
        k_t = ksa_ref[pl.ds(pl.multiple_of(kt * KEY_TILE, KEY_TILE), KEY_TILE), :]
        for kv in range(N_KV):
            dst[kv] = _dot_nt(qa_ref[kv], k_t)

    def consume(src, kt, bias):
        v_t = vsa_ref[pl.ds(pl.multiple_of(kt * KEY_TILE, KEY_TILE), KEY_TILE), :]
        for kv in range(N_KV):
            flash_step(acc_ref, m_ref, kv, src[kv] if bias is None else src[kv] + bias, v_t)

    scores_into(sa_ref, 0)

    def slc_pairs(first, count):
        for i in range(count):
            a = first + 2 * i
            scores_into(sb_ref, a + 1)
            consume(sa_ref, a, None)
            scores_into(sa_ref, a + 2)
            consume(sb_ref, a + 1, None)

    def slc_quad(j, carry):
        slc_pairs(4 * j, 2)
        return carry

    def slc_pair(j, carry):
        slc_pairs(4 * n_quads + 2 * j, 1)
        return carry

    n_quads = kd // 4
    lax.fori_loop(0, n_quads, slc_quad, 0)
    lax.fori_loop(0, (kd // 2) & 1, slc_pair, 0)
    causal_bias = tile_rows(jnp.where(rel <= d0, 0.0, NEG))
    kd_odd = (kd & 1) == 1

    @pl.when(kd_odd)
    def _():
        scores_into(sb_ref, kd)
        consume(sa_ref, kd - 1, None)
        consume(sb_ref, kd, causal_bias)

    @pl.when(jnp.logical_not(kd_odd))
    def _():
        consume(sa_ref, kd, causal_bias)

    for kv in range(N_KV):
        o_slc = flash_result(acc_ref, kv)
        pieces = []
        for g in range(GQA):
            col = 3 * (kv * GQA + g)
            rs = slice(g * Q_BLOCK, (g + 1) * Q_BLOCK)
            pieces.append(part_ref[kv, rs, :] + gate[:, col + 1:col + 2] * o_slc[rs])
        for mm in range(2):
            even, odd = pieces[2 * mm], pieces[2 * mm + 1]
            if kv == 0:
                odd = pltpu.roll(odd, 64, 1)
            else:
                even = pltpu.roll(even, 64, 1)
            slab = kv * 2 + mm
            o_ref[:, slab * LANES:(slab + 1) * LANES] = jnp.where(lane < 64, even, odd)


def _attention(q, kcmp, vcmp, ksa, vsa, kw, vwa, gate, batch, seq):
    nh = kcmp.shape[1]
    assert nh == KEY_TILE
    rows = GQA * Q_BLOCK
    res = lambda w: pl.BlockSpec((None, seq, w), lambda b, i: (b, 0, 0))
    blk = lambda w: pl.BlockSpec((None, Q_BLOCK, w), lambda b, i: (b, i, 0))
    r3 = lambda a, w: a.reshape(batch, seq, w)
    return pl.pallas_call(
        _attn_kernel,
        grid=(batch, seq // Q_BLOCK),
        in_specs=[blk(N_HEADS * LANES),
                  pl.BlockSpec((None, nh, LANES), lambda b, i: (b, 0, 0)),
                  pl.BlockSpec((None, nh, LANES), lambda b, i: (b, 0, 0)),
                  res(2 * LANES), res(2 * LANES), res(LANES), res(2 * LANES), blk(LANES)],
        out_specs=blk(D_ATT),
        out_shape=jax.ShapeDtypeStruct((batch, seq, D_ATT), F32),
        scratch_shapes=[pltpu.VMEM((N_KV, rows, 2 * LANES), BF16), pltpu.VMEM((N_KV, rows, LANES), F32),
                        pltpu.VMEM((N_KV, rows, 2 * LANES), F32), pltpu.VMEM((N_KV, rows, LANES), F32),
                        pltpu.VMEM((N_KV, rows, 2 * LANES), F32), pltpu.VMEM((N_KV, rows, LANES), F32),
                        pltpu.VMEM((N_KV, rows, KEY_TILE), F32), pltpu.VMEM((N_KV, rows, KEY_TILE), F32)],
        compiler_params=_cparams(2),
        name="nsa_attention",
    )(r3(q, N_HEADS * LANES), kcmp, vcmp, r3(ksa, 2 * LANES), r3(vsa, 2 * LANES), r3(kw, LANES), r3(vwa, 2 * LANES),
      r3(gate, LANES))


def _outproj_kernel(x_ref, yl_ref, ya_ref, yc_ref, gn_ref, w_ref, mod_ref, n2_ref, rw_ref, rb_ref,
                    xn_ref, h2_ref, comb_ref, pos_ref, seg_ref):
    tm = x_ref.shape[0]
    a, b = D_LRU, D_LRU + D_ATT
    yl = _rms(yl_ref[...], gn_ref[:, 0:a]).astype(BF16)
    ya = _rms(ya_ref[...], gn_ref[:, a:b]).astype(BF16)
    yc = _rms(yc_ref[...], gn_ref[:, b:D_MODEL]).astype(BF16)
    y = _dot(yl, w_ref[0:a, :]) + _dot(ya, w_ref[a:b, :]) + _dot(yc, w_ref[b:D_MODEL, :])
    xn = x_ref[...] + mod_ref[2:3, :] * y
    xn_ref[...] = xn
    h2 = _rms(xn, n2_ref[...]) * (1.0 + mod_ref[4:5, :]) + mod_ref[3:4, :]

    h_hi = h2.astype(BF16)
    h_mid = (h2 - h_hi.astype(F32)).astype(BF16)
    logit = _dot(h_hi, rw_ref[0]) + (_dot(h_hi, rw_ref[1]) + _dot(h_mid, rw_ref[0])) + rb_ref[...]
    lane = lax.broadcasted_iota(jnp.int32, (tm, LANES), 1)
    ninf = -jnp.inf
    is_g = lane < MOE_GROUPS
    is_e = (lane >= MOE_GROUPS) & (lane < MOE_GROUPS + MOE_GROUPS * MOE_EXPERTS)
    lg_max = jnp.max(jnp.where(is_g, logit, ninf), axis=-1, keepdims=True)
    g_star = jnp.min(jnp.where(is_g & (logit == lg_max), lane, LANES), axis=-1, keepdims=True)
    pg_star = 1.0 / jnp.sum(jnp.where(is_g, jnp.exp(logit - lg_max), 0.0), axis=-1, keepdims=True)
    in_grp = is_e & (((lane - MOE_GROUPS) >> 2) == g_star)
    v1 = jnp.max(jnp.where(in_grp, logit, ninf), axis=-1, keepdims=True)
    i1 = jnp.min(jnp.where(in_grp & (logit == v1), lane, LANES), axis=-1, keepdims=True)
    rest = in_grp & (lane != i1)
    v2 = jnp.max(jnp.where(rest, logit, ninf), axis=-1, keepdims=True)
    i2 = jnp.min(jnp.where(rest & (logit == v2), lane, LANES), axis=-1, keepdims=True)
    d = jnp.exp(v2 - v1)
    pe1 = 1.0 / (1.0 + d)
    pe2 = d / (1.0 + d)
    comb = jnp.where(lane == i1, pe1, jnp.where(lane == i2, pe2, 0.0)) * pg_star

    onehot = jnp.where(lane == g_star, 1.0, 0.0)
    r_i = lax.broadcasted_iota(jnp.int32, (tm, tm), 0)
    c_i = lax.broadcasted_iota(jnp.int32, (tm, tm), 1)
    earlier = jnp.where(c_i < r_i, 1.0, 0.0).astype(BF16)
    rank = jnp.sum(onehot * _dot(earlier, onehot.astype(BF16)), axis=-1, keepdims=True)
    n_g = jnp.sum(onehot, axis=0, keepdims=True)
    n_pad = (((n_g.astype(jnp.int32) + (MOE_ALIGN - 1)) >> MOE_ALIGN_LOG2) << MOE_ALIGN_LOG2).astype(F32)
    lane1 = lax.broadcasted_iota(jnp.int32, (1, LANES), 1)
    start = jnp.zeros((1, LANES), F32)
    below = jnp.zeros((1, 1), F32)
    for g in range(1, MOE_GROUPS):
        below = below + jnp.sum(jnp.where(lane1 == g - 1, n_pad, 0.0), axis=-1, keepdims=True)
        start = start + jnp.where(lane1 == g, below, 0.0)
    pos = jnp.sum(onehot * start, axis=-1, keepdims=True) + rank
    pos_rep = jnp.broadcast_to(pos, (tm, LANES))
    slot = lax.broadcasted_iota(jnp.int32, (tm, MOE_SLOTS), 1).astype(F32)
    take_t = jnp.where(_rep(pos_rep, MOE_SLOTS // LANES, 1) == slot, 1.0, 0.0)
    take = take_t.T
    take = take.astype(BF16)
    h2_ref[...] = _dot(take, h_hi).astype(BF16)
    c_hi = comb.astype(BF16)
    c_rest = comb - c_hi.astype(F32)
    c_mid = c_rest.astype(BF16)
    c_lo = (c_rest - c_mid.astype(F32)).astype(BF16)
    comb_ref[...] = (_dot(take, c_hi) + _dot(take, c_mid)) + _dot(take, c_lo)
    pos_ref[...] = pos_rep
    seg_ref[...] = jnp.zeros(seg_ref.shape, jnp.int32)
    seg_ref[0:1, :] = n_pad.astype(jnp.int32)
    seg_ref[1:2, :] = start.astype(jnp.int32)


def _outproj(x2, yl, ya, yc, gn, w_out, mod_l, n2g, rw, rb, seq):
    t, d = x2.shape
    tps = seq // TM
    row = lambda w: pl.BlockSpec((TM, w), lambda i: (i, 0))
    full = lambda a: pl.BlockSpec(a.shape, lambda i: (0,) * a.ndim)
    return pl.pallas_call(
        _outproj_kernel,
        grid=(t // TM,),
        in_specs=[row(d), row(D_LRU), row(D_ATT), row(D_CONV), full(gn), full(w_out),
                  pl.BlockSpec((None, 6, d), lambda i: (i // tps, 0, 0)), full(n2g), full(rw), full(rb)],
        out_specs=[row(d), pl.BlockSpec((MOE_SLOTS, d), lambda i: (i, 0)), pl.BlockSpec((MOE_SLOTS, LANES), lambda i: (i, 0)),
                   row(LANES), pl.BlockSpec((None, 8, LANES), lambda i: (i, 0, 0))],
        out_shape=[jax.ShapeDtypeStruct((t, d), F32), jax.ShapeDtypeStruct((t // TM * MOE_SLOTS, d), BF16),
                   jax.ShapeDtypeStruct((t // TM * MOE_SLOTS, LANES), F32), jax.ShapeDtypeStruct((t, LANES), F32),
                   jax.ShapeDtypeStruct((t // TM, 8, LANES), jnp.int32)],
        compiler_params=_cparams(1),
        name="out_proj_router",
    )(x2, yl.reshape(t, D_LRU), ya.reshape(t, D_ATT), yc.reshape(t, D_CONV), gn, w_out, mod_l, n2g, rw, rb)


def _moe_plan(seg, n_chunks, n_gtiles):
    rows = seg[:, 0, :MOE_GROUPS]
    start = seg[:, 1, :MOE_GROUPS]
    ci = jnp.arange(n_chunks, dtype=jnp.int32)
    gi = jnp.arange(MOE_GROUPS, dtype=jnp.int32)
    cum = jnp.sum(jnp.where((ci[None, :] <= ci[:, None])[:, :, None], rows[None, :, :], 0), axis=1)
    before = cum - rows
    total = jnp.sum(rows, axis=0)
    tiles_g = (total + (TM - 1)) // TM
    ends = jnp.sum(jnp.where(gi[None, :] <= gi[:, None], tiles_g[None, :], 0), axis=1)
    base = (ends - tiles_g) * TM
    group_at = lambda tile: jnp.minimum(jnp.sum((tile[:, None] >= ends[None, :]).astype(jnp.int32), axis=1),
                                        MOE_GROUPS - 1)
    tiles = jnp.concatenate([group_at(jnp.arange(n_gtiles, dtype=jnp.int32)), ends[-1:]]).astype(jnp.int32)

    row = jnp.arange(n_gtiles * (TM // MOE_ALIGN), dtype=jnp.int32) * MOE_ALIGN
    is_g = group_at(row // TM)[:, None] == gi[None, :]
    of_group = lambda v: jnp.sum(jnp.where(is_g, v[None, :], 0), axis=1)
    p = row - of_group(base)
    cum_g = jnp.sum(jnp.where(is_g[:, None, :], cum[None, :, :], 0), axis=2)
    chunk = jnp.minimum(jnp.sum((p[:, None] >= cum_g).astype(jnp.int32), axis=1), n_chunks - 1)
    is_cg = (chunk[:, None] == ci[None, :])[:, :, None] & is_g[:, None, :]
    of_segment = lambda a: jnp.sum(jnp.where(is_cg, a[None, :, :], 0), axis=(1, 2))
    src = chunk * MOE_SLOTS + of_segment(start - before) + p
    zero_row = TM + MOE_GROUPS * MOE_ALIGN
    gather = jnp.where(p < of_group(total), src, zero_row) // MOE_ALIGN

    slot = jnp.arange(MOE_SLOTS // MOE_ALIGN, dtype=jnp.int32) * MOE_ALIGN
    inside = (slot[None, :, None] >= start[:, None, :]) & (slot[None, :, None] < (start + rows)[:, None, :])
    dst = jnp.sum(jnp.where(inside, (base[None, :] + before - start)[:, None, :], 0), axis=-1) + slot[None, :]
    scatter = jnp.where(jnp.any(inside, axis=-1), dst // MOE_ALIGN, -1)
    return gather.astype(jnp.int32), scatter.reshape(-1).astype(jnp.int32), tiles


def _experts_kernel(gather_ref, tiles_ref, h2s_ref, combs_ref, wg_ref, wu_ref, wd_ref, y_ref,
                    hbuf, cbuf, acc_ref, sem):
    k = pl.program_id(0)
    e = pl.program_id(1)
    n_used = tiles_ref[pl.num_programs(0)]
    per_tile = TM // MOE_ALIGN

    def tile_copies(tile, go):
        slot = tile % 2
        for j in range(per_tile):
            src = pl.multiple_of(gather_ref[tile * per_tile + j] * MOE_ALIGN, MOE_ALIGN)
            dst = pl.ds(j * MOE_ALIGN, MOE_ALIGN)
            go(pltpu.make_async_copy(h2s_ref.at[pl.ds(src, MOE_ALIGN)], hbuf.at[slot, dst], sem.at[0, slot]))
            go(pltpu.make_async_copy(combs_ref.at[pl.ds(src, MOE_ALIGN)], cbuf.at[slot, dst], sem.at[1, slot]))

    @pl.when(e == 0)
    def _():
        acc_ref[...] = jnp.zeros_like(acc_ref)

        @pl.when((k == 0) & (n_used > 0))
        def _():
            tile_copies(k, lambda cp: cp.start())

        @pl.when(k + 1 < n_used)
        def _():
            tile_copies(k + 1, lambda cp: cp.start())

        @pl.when(k < n_used)
        def _():
            tile_copies(k, lambda cp: cp.wait())

    @pl.when(k < n_used)
    def _():
        slot = k % 2
        h2 = hbuf[slot]
        hid = jax.nn.silu(_dot(h2, wg_ref[...].astype(BF16))) * _dot(h2, wu_ref[...].astype(BF16))
        lane = lax.broadcasted_iota(jnp.int32, (TM, LANES), 1)
        col = MOE_GROUPS + tiles_ref[k] * MOE_EXPERTS + e
        c = jnp.sum(jnp.where(lane == col, cbuf[slot], 0.0), axis=-1, keepdims=True)
        acc_ref[...] += _dot((hid * c).astype(BF16), wd_ref[...].astype(BF16))

    @pl.when(e == pl.num_programs(1) - 1)
    def _():
        y_ref[...] = acc_ref[...]


def _experts(gather, tiles, h2s, combs, wg, wu, wd, n_gtiles, first_expert):
    d = h2s.shape[1]
    weights = lambda s: pl.BlockSpec((None,) + s,
                                     lambda k, e, ga, tl: (first_expert + tl[k] * MOE_EXPERTS + e, 0, 0))
    grid_spec = pltpu.PrefetchScalarGridSpec(
        num_scalar_prefetch=2,
        grid=(n_gtiles, MOE_EXPERTS),
        in_specs=[pl.BlockSpec(memory_space=pl.ANY), pl.BlockSpec(memory_space=pl.ANY),
                  weights((d, MOE_HID)), weights((d, MOE_HID)), weights((MOE_HID, d))],
        out_specs=pl.BlockSpec((TM, d), lambda k, e, ga, tl: (k, 0)),
        scratch_shapes=[pltpu.VMEM((2, TM, d), BF16), pltpu.VMEM((2, TM, LANES), F32), pltpu.VMEM((TM, d), F32),
                        pltpu.SemaphoreType.DMA((2, 2))],
    )
    return pl.pallas_call(
        _experts_kernel,
        grid_spec=grid_spec,
        out_shape=jax.ShapeDtypeStruct((n_gtiles * TM, d), F32),
        compiler_params=_cparams(2),
        name="moe_experts",
    )(gather, tiles, h2s, combs, wg, wu, wd)


def _finalize_kernel(scatter_ref, yg_ref, pos_ref, xn_ref, mod_ref, fg_ref, o_ref, ys_ref, sem, *, final):
    c = pl.program_id(0)
    per_chunk = MOE_SLOTS // MOE_ALIGN

    def chunk_copies(chunk, go):
        buf = chunk % 2
        for j in range(per_chunk):
            src = scatter_ref[chunk * per_chunk + j]

            @pl.when(src >= 0)
            def _(j=j, src=src):
                rows = pl.ds(pl.multiple_of(src * MOE_ALIGN, MOE_ALIGN), MOE_ALIGN)
                go(pltpu.make_async_copy(yg_ref.at[rows], ys_ref.at[buf, pl.ds(j * MOE_ALIGN, MOE_ALIGN)],
                                         sem.at[buf]))

    @pl.when(c == 0)
    def _():
        ys_ref[...] = jnp.zeros_like(ys_ref)
        chunk_copies(c, lambda cp: cp.start())

    @pl.when(c + 1 < pl.num_programs(0))
    def _():
        chunk_copies(c + 1, lambda cp: cp.start())

    chunk_copies(c, lambda cp: cp.wait())

    tm = xn_ref.shape[0]
    y = ys_ref[c % 2]
    hi = y.astype(BF16)
    rest = y - hi.astype(F32)
    mid = rest.astype(BF16)
    lo = (rest - mid.astype(F32)).astype(BF16)
    slot = lax.broadcasted_iota(jnp.int32, (tm, MOE_SLOTS), 1).astype(F32)
    take_t = jnp.where(_rep(pos_ref[...], MOE_SLOTS // LANES, 1) == slot, 1.0, 0.0).astype(BF16)
    y_tok = (_dot(take_t, hi) + _dot(take_t, mid)) + _dot(take_t, lo)
    xo = xn_ref[...] + mod_ref[5:6, :] * y_tok
    if final:
        xo = _rms(xo, fg_ref[...])
    o_ref[...] = xo


def _finalize(scatter, yg, pos, xn, mod_l, fg, seq, final):
    t, d = xn.shape
    tps = seq // TM
    grid_spec = pltpu.PrefetchScalarGridSpec(
        num_scalar_prefetch=1,
        grid=(t // TM,),
        in_specs=[pl.BlockSpec(memory_space=pl.ANY),
                  pl.BlockSpec((TM, LANES), lambda i, tb: (i, 0)),
                  pl.BlockSpec((TM, d), lambda i, tb: (i, 0)),
                  pl.BlockSpec((None, 6, d), lambda i, tb: (i // tps, 0, 0)),
                  pl.BlockSpec((1, d), lambda i, tb: (0, 0))],
        out_specs=pl.BlockSpec((TM, d), lambda i, tb: (i, 0)),
        scratch_shapes=[pltpu.VMEM((2, MOE_SLOTS, d), F32), pltpu.SemaphoreType.DMA((2,))],
    )
    return pl.pallas_call(
        functools.partial(_finalize_kernel, final=final),
        grid_spec=grid_spec,
        out_shape=jax.ShapeDtypeStruct((t, d), F32),
        compiler_params=_cparams(1),
        name="moe_finalize",
    )(scatter, yg, pos, xn, mod_l, fg)


def _permute_w_in(w):
    gate0 = D_LRU * 2 + D_ATT + 6 * LANES
    cv0 = gate0 + 3 * N_HEADS
    pad = jnp.zeros((w.shape[0], IN_PAD - w.shape[1]), w.dtype)
    return jnp.concatenate([w[:, :gate0], w[:, cv0:], w[:, gate0:cv0], pad], axis=1).astype(BF16)


def _router_weights(rg_w, rg_b, re_w, re_b):
    d = rg_w.shape[0]
    ne = MOE_GROUPS * MOE_EXPERTS
    w = jnp.concatenate([rg_w, jnp.transpose(re_w, (1, 0, 2)).reshape(d, ne),
                         jnp.zeros((d, LANES - MOE_GROUPS - ne), F32)], axis=1)
    b = jnp.concatenate([rg_b, re_b.reshape(ne), jnp.zeros((LANES - MOE_GROUPS - ne,), F32)])[None, :]
    w_hi = w.astype(BF16)
    w_mid = (w - w_hi.astype(F32)).astype(BF16)
    return jnp.stack([w_hi, w_mid]), b


def kernel(x, c, positions, ada_w, ada_b, norm1_g, norm2_g, w_in, lru_conv_w, lru_conv_b, lru_wa, lru_ba, lru_wx, lru_bx, lru_lambda, cmp_k_w1, cmp_k_w2, cmp_v_w1, cmp_v_w2, cmp_pos_k, cmp_pos_v, cnv_dw_w, cnv_dw_b, cnv_ln_g, cnv_ln_b, out_norm_g, w_out, moe_rg_w, moe_rg_b, moe_re_w, moe_re_b, moe_w_gate, moe_w_up, moe_w_down, final_norm_g):
    batch, seq, d = x.shape
    depth = ada_w.shape[0]
    assert d == D_MODEL and seq == N_SLC * SLC_LEN and seq % KEY_TILE == 0
    t = batch * seq
    ne = MOE_GROUPS * MOE_EXPERTS
    n_chunks = t // TM
    n_gtiles = n_chunks + -(-n_chunks * MOE_GROUPS * (MOE_ALIGN - 1) // TM) + MOE_GROUPS

    cos, sin = _rope_tables(positions)
    mod = _modulation(c, ada_w, ada_b)
    x2 = x.reshape(t, d)
    fg = final_norm_g[None, :]
    for l in range(depth):
        (xl, gl, q, kc, vc, ksa, vsa, kw, vwa, gate, cv) = _inproj(
            x2, mod[l], norm1_g[l][None, :], _permute_w_in(w_in[l]), cos, sin, seq)
        y_lru, y_cnv = _mixer(xl, gl, cv, lru_conv_w[l], lru_conv_b[l], lru_wa[l], lru_ba[l], lru_wx[l], lru_bx[l],
                              lru_lambda[l], cnv_dw_w[l], cnv_dw_b[l], cnv_ln_g[l], cnv_ln_b[l], batch, seq)
        kcmp, vcmp = _compress(kc, vc, cmp_pos_k[l], cmp_k_w1[l], cmp_k_w2[l], cmp_pos_v[l], cmp_v_w1[l], cmp_v_w2[l],
                               batch, seq)
        y_att = _attention(q, kcmp, vcmp, ksa, vsa, kw, vwa, gate, batch, seq)
        rw, rb = _router_weights(moe_rg_w[l], moe_rg_b[l], moe_re_w[l], moe_re_b[l])
        xn, h2, comb, pos, seg = _outproj(x2, y_lru, y_att, y_cnv, out_norm_g[l][None, :], w_out[l].astype(BF16),
                                          mod[l], norm2_g[l][None, :], rw, rb, seq)
        gather, scatter, tiles = _moe_plan(seg, n_chunks, n_gtiles)
        yg = _experts(gather, tiles, h2, comb,
                      moe_w_gate.reshape(depth * ne, d, MOE_HID), moe_w_up.reshape(depth * ne, d, MOE_HID),
                      moe_w_down.reshape(depth * ne, MOE_HID, d), n_gtiles, l * ne)
        x2 = _finalize(scatter, yg, pos, xn, mod[l], fg, seq, final=(l == depth - 1))
    return x2.reshape(batch, seq, d)
```

```python
import functools

import jax
import jax.numpy as jnp
from jax import lax
from jax.experimental import pallas as pl
from jax.experimental.pallas import tpu as pltpu

F32 = jnp.float32
BF16 = jnp.bfloat16

D_MODEL = 1024
D_LRU = 256
D_ATT = 512
D_CONV = 256
LRU_BLOCKS = 4
LRU_CONV_W = 4
LRU_C = 8.0
HEAD_DIM = 64
N_HEADS = 8
N_KV = 2
GQA = 4
ROPE_THETA = 10000.0
CMP_LEN = 32
CMP_STRIDE = 16
SLC_LEN = 64
SLC_TOPN = 16
WINDOW = 512
CONV_K = 31
MOE_GROUPS = 4
MOE_EXPERTS = 4
MOE_HID = 512
EPS = 1e-6
NEG = -1e30
FORCE = 1e9
LOG2_E = 1.4426950408889634

LANES = 128
Q_BLOCK = 128
KEY_TILE = 512
N_SLC = 128
TM = 512
MOE_ALIGN_LOG2 = 4
MOE_ALIGN = 1 << MOE_ALIGN_LOG2
MOE_SLOTS = 640
MOE_SUPER = 8
TC = 512
SCAN_ROWS = 64
LRU_TAIL = 8
CNV_TAIL = 32
IN_PAD = 2432
VMEM_LIMIT = 56 * 1024 * 1024


def _cparams(n_axes, vmem=VMEM_LIMIT):
    return pltpu.CompilerParams(dimension_semantics=("arbitrary",) * n_axes, vmem_limit_bytes=vmem)


def _dot(a, b):
    return jnp.dot(a, b, preferred_element_type=F32)


def _dot_nt(a, b):
    return lax.dot_general(a, b, (((1,), (1,)), ((), ())), preferred_element_type=F32)


def _rep(v, n, axis):
    return jnp.concatenate([v] * n, axis=axis)


def _rms(v, g):
    return v * lax.rsqrt(jnp.mean(v * v, axis=-1, keepdims=True) + EPS) * g


def _rope_kernel(pos_ref, inv_ref, sign_ref, cos_ref, sin_ref):
    ang = pos_ref[...].astype(F32) * inv_ref[...]
    cos_ref[...] = jnp.cos(ang)
    sin_ref[...] = jnp.sin(ang) * sign_ref[...]


def _rope_tables(positions):
    t = positions.size
    inv = ROPE_THETA ** (-jnp.arange(0, HEAD_DIM, 2, dtype=F32) / HEAD_DIM)
    inv128 = jnp.tile(inv, 4)[None, :]
    sign128 = jnp.tile(jnp.concatenate([-jnp.ones((32,), F32), jnp.ones((32,), F32)]), 2)[None, :]
    tr = 1024
    return pl.pallas_call(
        _rope_kernel,
        grid=(t // tr,),
        in_specs=[pl.BlockSpec((tr, 1), lambda i: (i, 0)),
                  pl.BlockSpec((1, LANES), lambda i: (0, 0)),
                  pl.BlockSpec((1, LANES), lambda i: (0, 0))],
        out_specs=[pl.BlockSpec((tr, LANES), lambda i: (i, 0))] * 2,
        out_shape=[jax.ShapeDtypeStruct((t, LANES), F32)] * 2,
        compiler_params=_cparams(1),
        name="rope_tables",
    )(positions.reshape(t, 1), inv128, sign128)


def _mod_kernel(c_ref, w_ref, b_ref, o_ref):
    sc = jax.nn.silu(c_ref[...])
    o_ref[...] = _dot(sc.astype(BF16), w_ref[...].astype(BF16)) + b_ref[...]


def _modulation(c, ada_w, ada_b):
    nl, d, n6 = ada_w.shape
    b = c.shape[0]
    rows = 16
    cp = jnp.zeros((rows, d), F32).at[:b].set(c)
    tn = 1536
    out = pl.pallas_call(
        _mod_kernel,
        grid=(nl, n6 // tn),
        in_specs=[pl.BlockSpec((rows, d), lambda l, j: (0, 0)),
                  pl.BlockSpec((None, d, tn), lambda l, j: (l, 0, j)),
                  pl.BlockSpec((None, 1, tn), lambda l, j: (l, 0, j))],
        out_specs=pl.BlockSpec((None, rows, tn), lambda l, j: (l, 0, j)),
        out_shape=jax.ShapeDtypeStruct((nl, rows, n6), F32),
        compiler_params=_cparams(2),
        name="adaln_mod",
    )(cp, ada_w, ada_b.reshape(nl, 1, n6))
    return out[:, :b].reshape(nl, b, 6, d)


def _inproj_kernel(x_ref, mod_ref, g_ref, w_ref, cos_ref, sin_ref,
                   xl_ref, gl_ref, q_ref, kc_ref, vc_ref, ksa_ref, vsa_ref, kw_ref, vwa_ref, gate_ref, cv_ref,
                   *, tiles_per_seq):
    tm = x_ref.shape[0]
    x = x_ref[...]
    h = _rms(x, g_ref[...]) * (1.0 + mod_ref[1:2, :]) + mod_ref[0:1, :]
    p = _dot(h.astype(BF16), w_ref[...])

    cos = cos_ref[...]
    sin = sin_ref[...]
    lane = lax.broadcasted_iota(jnp.int32, (tm, LANES), 1)
    first_half = (lane & 63) < 32

    def rope(v):
        rot = jnp.where(first_half, pltpu.roll(v, 96, 1), pltpu.roll(v, 32, 1))
        return v * cos + rot * sin

    xl_ref[...] = p[:, 0:256]
    gl_ref[...] = p[:, 256:512]
    scale = HEAD_DIM ** -0.5 * LOG2_E
    low = lane < 64
    for m in range(4):
        slab = rope(p[:, 512 + 128 * m:640 + 128 * m]) * scale
        swapped = pltpu.roll(slab, 64, 1)
        for hh in range(2):
            head = 2 * m + hh
            kv = head // GQA
            src = slab if hh == kv else swapped
            keep = low if kv == 0 else jnp.logical_not(low)
            q_ref[:, head * LANES:(head + 1) * LANES] = jnp.where(keep, src, 0.0).astype(BF16)
    kc_ref[...] = rope(p[:, 1024:1152])
    vc_ref[...] = p[:, 1152:1280]
    row = lax.broadcasted_iota(jnp.int32, (tm, LANES), 0)
    s_base = (pl.program_id(0) % tiles_per_seq) * tm
    onehot = jnp.where(lane == ((s_base + row) >> 6), 1.0, 0.0).astype(BF16)
    ones = jnp.ones((tm, LANES), BF16)
    ksa_ref[:, 0:LANES] = rope(p[:, 1280:1408]).astype(BF16)
    ksa_ref[:, LANES:2 * LANES] = onehot
    vsa_ref[:, 0:LANES] = p[:, 1408:1536].astype(BF16)
    vsa_ref[:, LANES:2 * LANES] = ones
    kw_ref[...] = rope(p[:, 1536:1664]).astype(BF16)
    vwa_ref[:, 0:LANES] = p[:, 1664:1792].astype(BF16)
    vwa_ref[:, LANES:2 * LANES] = ones
    cv_ref[...] = p[:, 1792:2304]
    gate_ref[...] = jax.nn.sigmoid(p[:, 2304:2432])


def _inproj(x2, mod_l, g, w_p, cos, sin, seq):
    t, d = x2.shape
    tps = seq // TM
    row = lambda w: pl.BlockSpec((TM, w), lambda i: (i, 0))
    outs = [(256, F32), (256, F32), (N_HEADS * LANES, BF16), (LANES, F32), (LANES, F32), (2 * LANES, BF16),
            (2 * LANES, BF16), (LANES, BF16), (2 * LANES, BF16), (LANES, F32), (2 * D_CONV, F32)]
    return pl.pallas_call(
        functools.partial(_inproj_kernel, tiles_per_seq=tps),
        grid=(t // TM,),
        in_specs=[row(d),
                  pl.BlockSpec((None, 6, d), lambda i: (i // tps, 0, 0)),
                  pl.BlockSpec((1, d), lambda i: (0, 0)),
                  pl.BlockSpec((d, IN_PAD), lambda i: (0, 0)),
                  row(LANES), row(LANES)],
        out_specs=[row(w) for w, _ in outs],
        out_shape=[jax.ShapeDtypeStruct((t, w), dt) for w, dt in outs],
        compiler_params=_cparams(1),
        name="in_proj",
    )(x2, mod_l, g, w_p, cos, sin)


def _shift_rows(v, s, fill, row):
    return jnp.where(row < s, fill, pltpu.roll(v, s, 0))


def _causal_taps(ext, w_ref, bias, width, tail, tc):
    acc = bias
    for b in range(min(8, width)):
        shifted = pltpu.roll(ext, b, 0) if b else ext
        for k in range(width):
            back = width - 1 - k
            if back % 8 == b:
                start = tail - (back - b)
                acc = acc + w_ref[k:k + 1, :] * shifted[start:start + tc, :]
    return acc


def _mixer_kernel(xl_ref, gl_ref, cv_ref, lcw_ref, lcb_ref, wa_ref, ba_ref, wx_ref, bx_ref, lam_ref,
                  cw_ref, cb_ref, lng_ref, lnb_ref, ylru_ref, ycnv_ref,
                  extl, extc, abuf, ubuf, hcar):
    tc = xl_ref.shape[0]

    @pl.when(pl.program_id(1) == 0)
    def _():
        extl[0:LRU_TAIL, :] = jnp.zeros((LRU_TAIL, D_LRU), F32)
        extc[0:CNV_TAIL, :] = jnp.zeros((CNV_TAIL, D_CONV), F32)
        hcar[...] = jnp.zeros_like(hcar)

    xl = xl_ref[...]
    extl[LRU_TAIL:LRU_TAIL + tc, :] = xl
    xc = _causal_taps(extl[...], lcw_ref, lcb_ref[...], LRU_CONV_W, LRU_TAIL, tc)
    extl[0:LRU_TAIL, :] = xl[tc - LRU_TAIL:tc, :]
    xcb = xc.astype(BF16)
    r = jax.nn.sigmoid(_dot(xcb, wa_ref[...]) + ba_ref[...])
    gi = jax.nn.sigmoid(_dot(xcb, wx_ref[...]) + bx_ref[...])
    log_a = LRU_C * r * jax.nn.log_sigmoid(lam_ref[...])
    th = jnp.tanh(log_a)
    one_minus_a2 = -2.0 * th / (1.0 - th)
    abuf[...] = jnp.exp(log_a)
    ubuf[...] = jnp.sqrt(one_minus_a2) * (gi * xc)

    row = lax.broadcasted_iota(jnp.int32, (SCAN_ROWS, D_LRU), 0)

    def scan_chunk(c, h_prev):
        off = pl.multiple_of(c * SCAN_ROWS, SCAN_ROWS)
        a = abuf[pl.ds(off, SCAN_ROWS), :]
        b = ubuf[pl.ds(off, SCAN_ROWS), :]
        s = 1
        while s < SCAN_ROWS:
            b = b + a * _shift_rows(b, s, 0.0, row)
            a = a * _shift_rows(a, s, 1.0, row)
            s *= 2
        h = b + a * h_prev
        ylru_ref[pl.ds(off, SCAN_ROWS), :] = h * jax.nn.gelu(gl_ref[pl.ds(off, SCAN_ROWS), :])
        return h[SCAN_ROWS - 1:SCAN_ROWS, :]

    hcar[0:1, :] = lax.fori_loop(0, tc // SCAN_ROWS, scan_chunk, hcar[0:1, :])

    u = cv_ref[...]
    v = u[:, 0:D_CONV] * jax.nn.sigmoid(u[:, D_CONV:2 * D_CONV])
    extc[CNV_TAIL:CNV_TAIL + tc, :] = v
    acc = _causal_taps(extc[...], cw_ref, cb_ref[...], CONV_K, CNV_TAIL, tc)
    extc[0:CNV_TAIL, :] = v[tc - CNV_TAIL:tc, :]
    mu = jnp.mean(acc, axis=-1, keepdims=True)
    var = jnp.mean(jnp.square(acc - mu), axis=-1, keepdims=True)
    y = (acc - mu) * lax.rsqrt(var + EPS) * lng_ref[...] + lnb_ref[...]
    ycnv_ref[...] = jax.nn.silu(y)


def _block_diag(w):
    n, rows, cols = w.shape
    eye = jnp.eye(n, dtype=bool)
    return jnp.where(eye[:, None, :, None], w[:, :, None, :], 0.0).reshape(n * rows, n * cols)


def _mixer(xl, gl, cv, lcw, lcb, wa, ba, wx, bx, lam, cw, cb, lng, lnb, batch, seq):
    seqrow = lambda w: pl.BlockSpec((None, TC, w), lambda b, t: (b, t, 0))
    full = lambda a: pl.BlockSpec(a.shape, lambda b, t: (0,) * a.ndim)
    small = [lcw, lcb[None, :], _block_diag(wa).astype(BF16), ba[None, :], _block_diag(wx).astype(BF16), bx[None, :],
             lam[None, :], jnp.zeros((32, D_CONV), F32).at[:CONV_K].set(cw), cb[None, :], lng[None, :], lnb[None, :]]
    return pl.pallas_call(
        _mixer_kernel,
        grid=(batch, seq // TC),
        in_specs=[seqrow(D_LRU), seqrow(D_LRU), seqrow(2 * D_CONV)] + [full(a) for a in small],
        out_specs=[seqrow(D_LRU), seqrow(D_CONV)],
        out_shape=[jax.ShapeDtypeStruct((batch, seq, D_LRU), F32), jax.ShapeDtypeStruct((batch, seq, D_CONV), F32)],
        scratch_shapes=[pltpu.VMEM((TC + LRU_TAIL, D_LRU), F32), pltpu.VMEM((TC + CNV_TAIL, D_CONV), F32),
                        pltpu.VMEM((TC, D_LRU), F32), pltpu.VMEM((TC, D_LRU), F32), pltpu.VMEM((8, D_LRU), F32)],
        compiler_params=_cparams(2),
        name="mixer_stream",
    )(xl.reshape(batch, seq, D_LRU), gl.reshape(batch, seq, D_LRU), cv.reshape(batch, seq, 2 * D_CONV), *small)


def _cmp_kernel(kc_ref, vc_ref, posk_ref, posv_ref, wk1_ref, wk2_ref, wv1_ref, wv2_ref, ko_ref, vo_ref, nat_ref):
    nh = ko_ref.shape[0]
    quarter = nh // 4

    def one(t_ref, pos_ref, w1_ref, w2_ref, o_ref):
        first = jnp.zeros((nh, w1_ref.shape[2]), F32)
        second = jnp.zeros((nh, w1_ref.shape[2]), F32)
        for l in range(CMP_STRIDE):
            x = t_ref[pl.ds(l, nh, stride=CMP_STRIDE), :]
            first = first + _dot((x + pos_ref[l:l + 1, :]).astype(BF16), w1_ref[l])
            second = second + _dot((x + pos_ref[CMP_STRIDE + l:CMP_STRIDE + l + 1, :]).astype(BF16),
                                   w1_ref[CMP_STRIDE + l])
        hid = jax.nn.gelu(first + pltpu.roll(second, nh - 1, 0))
        nat_ref[...] = _dot(hid.astype(BF16), w2_ref[...])
        for r in range(4):
            o_ref[r * quarter:(r + 1) * quarter, :] = nat_ref[pl.ds(r, quarter, stride=4), :].astype(o_ref.dtype)

    one(kc_ref, posk_ref, wk1_ref, wk2_ref, ko_ref)
    one(vc_ref, posv_ref, wv1_ref, wv2_ref, vo_ref)


def _cmp_weights(pos, w1, w2):
    hid = w1.shape[1]
    eye = jnp.eye(N_KV, dtype=bool)
    w1 = w1.astype(BF16).reshape(CMP_LEN, HEAD_DIM, hid)
    w1x = jnp.where(eye[None, :, None, :, None], w1[:, None, :, None, :], 0.0)
    w1x = w1x.reshape(CMP_LEN, N_KV * HEAD_DIM, N_KV * hid)
    w2x = _block_diag(jnp.broadcast_to(w2.astype(BF16), (N_KV,) + w2.shape))
    return jnp.tile(pos, (1, N_KV)), w1x, w2x


def _compress(kc, vc, pos_k, wk1, wk2, pos_v, wv1, wv2, batch, seq):
    nh = seq // CMP_STRIDE
    wide = N_KV * HEAD_DIM
    pk, wk1x, wk2x = _cmp_weights(pos_k, wk1, wk2)
    pv, wv1x, wv2x = _cmp_weights(pos_v, wv1, wv2)
    full = lambda a: pl.BlockSpec(a.shape, lambda b: (0,) * a.ndim)
    seqs = pl.BlockSpec((None, seq, wide), lambda b: (b, 0, 0))
    outs = pl.BlockSpec((None, nh, wide), lambda b: (b, 0, 0))
    return pl.pallas_call(
        _cmp_kernel,
        grid=(batch,),
        in_specs=[seqs, seqs, full(pk), full(pv), full(wk1x), full(wk2x), full(wv1x), full(wv2x)],
        out_specs=[outs, outs],
        out_shape=[jax.ShapeDtypeStruct((batch, nh, wide), BF16)] * 2,
        scratch_shapes=[pltpu.VMEM((nh, wide), F32)],
        compiler_params=_cparams(1),
        name="nsa_compress",
    )(kc.reshape(batch, seq, wide), vc.reshape(batch, seq, wide), pk, pv, wk1x, wk2x, wv1x, wv2x)


def _attn_kernel(q_ref, kcmp_ref, vcmp_ref, ksa_ref, vsa_ref, kw_ref, vwa_ref, gate_ref, o_ref,
                 qa_ref, part_ref, acc_ref, m_ref, accw_ref, mw_ref, sa_ref, sb_ref):
    rows = GQA * Q_BLOCK
    reps = KEY_TILE // LANES
    qb = pl.program_id(1)
    s0 = qb * Q_BLOCK
    kd = s0 // KEY_TILE
    off_d = pl.multiple_of(kd * KEY_TILE, KEY_TILE)
    off_p = pl.multiple_of(jnp.maximum(kd - 1, 0) * KEY_TILE, KEY_TILE)
    d0 = s0 - off_d

    lane = lax.broadcasted_iota(jnp.int32, (Q_BLOCK, LANES), 1)
    sub = lax.broadcasted_iota(jnp.int32, (Q_BLOCK, LANES), 0)
    r_q = lax.broadcasted_iota(jnp.int32, (Q_BLOCK, KEY_TILE), 0)
    c_k = lax.broadcasted_iota(jnp.int32, (Q_BLOCK, KEY_TILE), 1)
    rel = c_k - r_q
    gate = gate_ref[...]

    def tile_rows(plane):
        return _rep(plane, GQA, 0)

    def flash_step(acc, m, kv, s, v_tile):
        m_old = m[kv]
        m_new = jnp.maximum(m_old, jnp.broadcast_to(jnp.max(s, axis=-1, keepdims=True), m_old.shape))
        p = jnp.exp2(s - _rep(m_new, reps, 1))
        acc[kv] = acc[kv] * _rep(jnp.exp2(m_old - m_new), 2, 1) + _dot(p.astype(BF16), v_tile)
        m[kv] = m_new

    def flash_result(acc, kv):
        a = acc[kv]
        return a[:, 0:LANES] / a[:, LANES:2 * LANES]

    q128 = [jnp.concatenate([q_ref[:, (kv * GQA + g) * LANES:(kv * GQA + g + 1) * LANES] for g in range(GQA)], axis=0)
            for kv in range(N_KV)]

    kcmp = kcmp_ref[...]
    vcmp = vcmp_ref[...]
    cmp_end = SLC_LEN * (c_k & (N_SLC - 1)) + CMP_STRIDE * (c_k >> 7) + (CMP_LEN - 1)
    cmp_bias = tile_rows(jnp.where(cmp_end - r_q <= s0, 0.0, NEG))
    has_cmp = tile_rows(jnp.where(s0 + sub >= CMP_LEN - 1, 1.0, 0.0))
    o_cmp, imp_t = [], []
    for kv in range(N_KV):
        s_c = _dot_nt(q128[kv], kcmp) + cmp_bias
        mb = jnp.broadcast_to(jnp.max(s_c, axis=-1, keepdims=True), (rows, LANES))
        e = jnp.exp2(s_c - _rep(mb, reps, 1))
        lb = jnp.broadcast_to(jnp.sum(e, axis=-1, keepdims=True), (rows, LANES))
        inv = has_cmp / lb
        o_cmp.append(_dot(e.astype(BF16), vcmp) * inv)
        e4 = (e[:, 0:N_SLC] + e[:, N_SLC:2 * N_SLC] + e[:, 2 * N_SLC:3 * N_SLC] + e[:, 3 * N_SLC:4 * N_SLC]) * inv
        imp = e4[0:Q_BLOCK] + e4[Q_BLOCK:2 * Q_BLOCK] + e4[2 * Q_BLOCK:3 * Q_BLOCK] + e4[3 * Q_BLOCK:4 * Q_BLOCK]
        imp_t.append(imp.T)

    accw_ref[...] = jnp.zeros_like(accw_ref)
    mw_ref[...] = jnp.full_like(mw_ref, NEG)
    prev_bias = tile_rows(jnp.where(rel > jnp.where(kd > 0, d0, KEY_TILE), 0.0, NEG))
    diag_bias = tile_rows(jnp.where(rel <= d0, 0.0, NEG))
    kw_p, vw_p = kw_ref[pl.ds(off_p, KEY_TILE), :], vwa_ref[pl.ds(off_p, KEY_TILE), :]
    kw_d, vw_d = kw_ref[pl.ds(off_d, KEY_TILE), :], vwa_ref[pl.ds(off_d, KEY_TILE), :]
    for kv in range(N_KV):
        flash_step(accw_ref, mw_ref, kv, _dot_nt(q128[kv], kw_p) + prev_bias, vw_p)
    for kv in range(N_KV):
        flash_step(accw_ref, mw_ref, kv, _dot_nt(q128[kv], kw_d) + diag_bias, vw_d)
    for kv in range(N_KV):
        o_win = flash_result(accw_ref, kv)
        for g in range(GQA):
            col = 3 * (kv * GQA + g)
            rs = slice(g * Q_BLOCK, (g + 1) * Q_BLOCK)
            part_ref[kv, rs, :] = gate[:, col:col + 1] * o_cmp[kv][rs] + gate[:, col + 2:col + 3] * o_win[rs]

    blk = lax.broadcasted_iota(jnp.int32, (N_SLC, Q_BLOCK), 0)
    t_q = s0 + lax.broadcasted_iota(jnp.int32, (N_SLC, Q_BLOCK), 1)
    forced = (blk == 0) | (blk == (t_q >> 6))
    val = [jnp.where(forced, -jnp.inf, jnp.where(blk * SLC_LEN <= t_q, imp_t[kv], -1.0)) for kv in range(N_KV)]
    blk_f = blk.astype(F32)
    for _ in range(SLC_TOPN - 2):
        for kv in range(N_KV):
            mx = jnp.max(val[kv], axis=0, keepdims=True)
            first = jnp.min(jnp.where(val[kv] == mx, blk_f, float(N_SLC)), axis=0, keepdims=True)
            val[kv] = jnp.where(blk_f == first, -jnp.inf, val[kv])
    for kv in range(N_KV):
        bias = jnp.where(val[kv].T == -jnp.inf, 0.0, NEG).astype(BF16)
        qa_ref[kv] = jnp.concatenate([q128[kv], jnp.concatenate([bias] * GQA, axis=0)], axis=1)

    acc_ref[...] = jnp.zeros_like(acc_ref)
    m_ref[...] = jnp.full_like(m_ref, NEG)

    def scores_into(dst, kt):
        k_t = ksa_ref[pl.ds(pl.multiple_of(kt * KEY_TILE, KEY_TILE), KEY_TILE), :]
        for kv in range(N_KV):
            dst[kv] = _dot_nt(qa_ref[kv], k_t)

    def consume(src, kt, bias):
        v_t = vsa_ref[pl.ds(pl.multiple_of(kt * KEY_TILE, KEY_TILE), KEY_TILE), :]
        for kv in range(N_KV):
            flash_step(acc_ref, m_ref, kv, src[kv] if bias is None else src[kv] + bias, v_t)

    scores_into(sa_ref, 0)

    def slc_pairs(first, count):
        for i in range(count):
            a = first + 2 * i
            scores_into(sb_ref, a + 1)
            consume(sa_ref, a, None)
            scores_into(sa_ref, a + 2)
            consume(sb_ref, a + 1, None)

    def slc_quad(j, carry):
        slc_pairs(4 * j, 2)
        return carry

    def slc_pair(j, carry):
        slc_pairs(4 * n_quads + 2 * j, 1)
        return carry

    n_quads = kd // 4
    lax.fori_loop(0, n_quads, slc_quad, 0)
    lax.fori_loop(0, (kd // 2) & 1, slc_pair, 0)
    causal_bias = tile_rows(jnp.where(rel <= d0, 0.0, NEG))
    kd_odd = (kd & 1) == 1

    @pl.when(kd_odd)
    def _():
        scores_into(sb_ref, kd)
        consume(sa_ref, kd - 1, None)
        consume(sb_ref, kd, causal_bias)

    @pl.when(jnp.logical_not(kd_odd))
    def _():
        consume(sa_ref, kd, causal_bias)

    for kv in range(N_KV):
        o_slc = flash_result(acc_ref, kv)
        pieces = []
        for g in range(GQA):
            col = 3 * (kv * GQA + g)
            rs = slice(g * Q_BLOCK, (g + 1) * Q_BLOCK)
            pieces.append(part_ref[kv, rs, :] + gate[:, col + 1:col + 2] * o_slc[rs])
        for mm in range(2):
            even, odd = pieces[2 * mm], pieces[2 * mm + 1]
            if kv == 0:
                odd = pltpu.roll(odd, 64, 1)
            else:
                even = pltpu.roll(even, 64, 1)
            slab = kv * 2 + mm
            o_ref[:, slab * LANES:(slab + 1) * LANES] = jnp.where(lane < 64, even, odd)


def _attention(q, kcmp, vcmp, ksa, vsa, kw, vwa, gate, batch, seq):
    nh = kcmp.shape[1]
    assert nh == KEY_TILE
    rows = GQA * Q_BLOCK
    res = lambda w: pl.BlockSpec((None, seq, w), lambda b, i: (b, 0, 0))
    blk = lambda w: pl.BlockSpec((None, Q_BLOCK, w), lambda b, i: (b, i, 0))
    r3 = lambda a, w: a.reshape(batch, seq, w)
    return pl.pallas_call(
        _attn_kernel,
        grid=(batch, seq // Q_BLOCK),
        in_specs=[blk(N_HEADS * LANES),
                  pl.BlockSpec((None, nh, LANES), lambda b, i: (b, 0, 0)),
                  pl.BlockSpec((None, nh, LANES), lambda b, i: (b, 0, 0)),
                  res(2 * LANES), res(2 * LANES), res(LANES), res(2 * LANES), blk(LANES)],
        out_specs=blk(D_ATT),
        out_shape=jax.ShapeDtypeStruct((batch, seq, D_ATT), F32),
        scratch_shapes=[pltpu.VMEM((N_KV, rows, 2 * LANES), BF16), pltpu.VMEM((N_KV, rows, LANES), F32),
                        pltpu.VMEM((N_KV, rows, 2 * LANES), F32), pltpu.VMEM((N_KV, rows, LANES), F32),
                        pltpu.VMEM((N_KV, rows, 2 * LANES), F32), pltpu.VMEM((N_KV, rows, LANES), F32),
                        pltpu.VMEM((N_KV, rows, KEY_TILE), F32), pltpu.VMEM((N_KV, rows, KEY_TILE), F32)],
        compiler_params=_cparams(2),
        name="nsa_attention",
    )(r3(q, N_HEADS * LANES), kcmp, vcmp, r3(ksa, 2 * LANES), r3(vsa, 2 * LANES), r3(kw, LANES), r3(vwa, 2 * LANES),
      r3(gate, LANES))


def _outproj_kernel(x_ref, yl_ref, ya_ref, yc_ref, gn_ref, w_ref, mod_ref, n2_ref, rw_ref, rb_ref,
                    xn_ref, h2_ref, comb_ref, pos_ref, seg_ref):
    tm = x_ref.shape[0]
    a, b = D_LRU, D_LRU + D_ATT
    yl = _rms(yl_ref[...], gn_ref[:, 0:a]).astype(BF16)
    ya = _rms(ya_ref[...], gn_ref[:, a:b]).astype(BF16)
    yc = _rms(yc_ref[...], gn_ref[:, b:D_MODEL]).astype(BF16)
    y = _dot(yl, w_ref[0:a, :]) + _dot(ya, w_ref[a:b, :]) + _dot(yc, w_ref[b:D_MODEL, :])
    xn = x_ref[...] + mod_ref[2:3, :] * y
    xn_ref[...] = xn
    h2 = _rms(xn, n2_ref[...]) * (1.0 + mod_ref[4:5, :]) + mod_ref[3:4, :]

    h_hi = h2.astype(BF16)
    h_mid = (h2 - h_hi.astype(F32)).astype(BF16)
    logit = _dot(h_hi, rw_ref[0]) + (_dot(h_hi, rw_ref[1]) + _dot(h_mid, rw_ref[0])) + rb_ref[...]
    lane = lax.broadcasted_iota(jnp.int32, (tm, LANES), 1)
    ninf = -jnp.inf
    is_g = lane < MOE_GROUPS
    is_e = (lane >= MOE_GROUPS) & (lane < MOE_GROUPS + MOE_GROUPS * MOE_EXPERTS)
    lg_max = jnp.max(jnp.where(is_g, logit, ninf), axis=-1, keepdims=True)
    g_star = jnp.min(jnp.where(is_g & (logit == lg_max), lane, LANES), axis=-1, keepdims=True)
    pg_star = 1.0 / jnp.sum(jnp.where(is_g, jnp.exp(logit - lg_max), 0.0), axis=-1, keepdims=True)
    in_grp = is_e & (((lane - MOE_GROUPS) >> 2) == g_star)
    v1 = jnp.max(jnp.where(in_grp, logit, ninf), axis=-1, keepdims=True)
    i1 = jnp.min(jnp.where(in_grp & (logit == v1), lane, LANES), axis=-1, keepdims=True)
    rest = in_grp & (lane != i1)
    v2 = jnp.max(jnp.where(rest, logit, ninf), axis=-1, keepdims=True)
    i2 = jnp.min(jnp.where(rest & (logit == v2), lane, LANES), axis=-1, keepdims=True)
    d = jnp.exp(v2 - v1)
    pe1 = 1.0 / (1.0 + d)
    pe2 = d / (1.0 + d)
    comb = jnp.where(lane == i1, pe1, jnp.where(lane == i2, pe2, 0.0)) * pg_star

    onehot = jnp.where(lane == g_star, 1.0, 0.0)
    r_i = lax.broadcasted_iota(jnp.int32, (tm, tm), 0)
    c_i = lax.broadcasted_iota(jnp.int32, (tm, tm), 1)
    earlier = jnp.where(c_i < r_i, 1.0, 0.0).astype(BF16)
    rank = jnp.sum(onehot * _dot(earlier, onehot.astype(BF16)), axis=-1, keepdims=True)
    n_g = jnp.sum(onehot, axis=0, keepdims=True)
    n_pad = (((n_g.astype(jnp.int32) + (MOE_ALIGN - 1)) >> MOE_ALIGN_LOG2) << MOE_ALIGN_LOG2).astype(F32)
    lane1 = lax.broadcasted_iota(jnp.int32, (1, LANES), 1)
    start = jnp.zeros((1, LANES), F32)
    below = jnp.zeros((1, 1), F32)
    for g in range(1, MOE_GROUPS):
        below = below + jnp.sum(jnp.where(lane1 == g - 1, n_pad, 0.0), axis=-1, keepdims=True)
        start = start + jnp.where(lane1 == g, below, 0.0)
    pos = jnp.sum(onehot * start, axis=-1, keepdims=True) + rank
    pos_rep = jnp.broadcast_to(pos, (tm, LANES))
    slot = lax.broadcasted_iota(jnp.int32, (tm, MOE_SLOTS), 1).astype(F32)
    take_t = jnp.where(_rep(pos_rep, MOE_SLOTS // LANES, 1) == slot, 1.0, 0.0)
    take = take_t.T
    take = take.astype(BF16)
    h2_ref[...] = _dot(take, h_hi).astype(BF16)
    c_hi = comb.astype(BF16)
    c_rest = comb - c_hi.astype(F32)
    c_mid = c_rest.astype(BF16)
    c_lo = (c_rest - c_mid.astype(F32)).astype(BF16)
    comb_ref[...] = (_dot(take, c_hi) + _dot(take, c_mid)) + _dot(take, c_lo)
    pos_ref[...] = pos_rep
    seg_ref[...] = jnp.zeros(seg_ref.shape, jnp.int32)
    seg_ref[0:1, :] = n_pad.astype(jnp.int32)
    seg_ref[1:2, :] = start.astype(jnp.int32)


def _outproj(x2, yl, ya, yc, gn, w_out, mod_l, n2g, rw, rb, seq):
    t, d = x2.shape
    tps = seq // TM
    row = lambda w: pl.BlockSpec((TM, w), lambda i: (i, 0))
    full = lambda a: pl.BlockSpec(a.shape, lambda i: (0,) * a.ndim)
    return pl.pallas_call(
        _outproj_kernel,
        grid=(t // TM,),
        in_specs=[row(d), row(D_LRU), row(D_ATT), row(D_CONV), full(gn), full(w_out),
                  pl.BlockSpec((None, 6, d), lambda i: (i // tps, 0, 0)), full(n2g), full(rw), full(rb)],
        out_specs=[row(d), pl.BlockSpec((MOE_SLOTS, d), lambda i: (i, 0)), pl.BlockSpec((MOE_SLOTS, LANES), lambda i: (i, 0)),
                   row(LANES), pl.BlockSpec((None, 8, LANES), lambda i: (i, 0, 0))],
        out_shape=[jax.ShapeDtypeStruct((t, d), F32), jax.ShapeDtypeStruct((t // TM * MOE_SLOTS, d), BF16),
                   jax.ShapeDtypeStruct((t // TM * MOE_SLOTS, LANES), F32), jax.ShapeDtypeStruct((t, LANES), F32),
                   jax.ShapeDtypeStruct((t // TM, 8, LANES), jnp.int32)],
        compiler_params=_cparams(1),
        name="out_proj_router",
    )(x2, yl.reshape(t, D_LRU), ya.reshape(t, D_ATT), yc.reshape(t, D_CONV), gn, w_out, mod_l, n2g, rw, rb)


def _moe_plan(seg, n_chunks, n_gtiles):
    rows = seg[:, 0, :MOE_GROUPS]
    start = seg[:, 1, :MOE_GROUPS]
    ci = jnp.arange(n_chunks, dtype=jnp.int32)
    gi = jnp.arange(MOE_GROUPS, dtype=jnp.int32)
    cum = jnp.sum(jnp.where((ci[None, :] <= ci[:, None])[:, :, None], rows[None, :, :], 0), axis=1)
    before = cum - rows
    total = jnp.sum(rows, axis=0)
    tiles_g = (total + (TM - 1)) // TM
    ends = jnp.sum(jnp.where(gi[None, :] <= gi[:, None], tiles_g[None, :], 0), axis=1)
    base = (ends - tiles_g) * TM
    group_at = lambda tile: jnp.minimum(jnp.sum((tile[:, None] >= ends[None, :]).astype(jnp.int32), axis=1),
                                        MOE_GROUPS - 1)
    n_sup = -(-n_gtiles // MOE_SUPER) + MOE_GROUPS
    listed_g = jnp.where(gi == MOE_GROUPS - 1, n_gtiles - (ends[-1] - tiles_g[-1]), tiles_g)
    sup_g = (listed_g + (MOE_SUPER - 1)) // MOE_SUPER
    sup_ends = jnp.sum(jnp.where(gi[None, :] <= gi[:, None], sup_g[None, :], 0), axis=1)
    si = jnp.arange(n_sup, dtype=jnp.int32)
    s_is_g = jnp.minimum(jnp.sum((si[:, None] >= sup_ends[None, :]).astype(jnp.int32), axis=1),
                         MOE_GROUPS - 1)[:, None] == gi[None, :]
    s_of = lambda v: jnp.sum(jnp.where(s_is_g, v[None, :], 0), axis=1)
    inner = (si - s_of(sup_ends - sup_g)) * MOE_SUPER
    used = si < sup_ends[-1]
    sup = jnp.concatenate([
        s_of(gi),
        jnp.where(used, s_of(ends - tiles_g) + inner, n_gtiles - 1),
        jnp.where(used, jnp.clip(s_of(listed_g) - inner, 0, MOE_SUPER), 0),
        ends[-1:]]).astype(jnp.int32)

    row = jnp.arange(n_gtiles * (TM // MOE_ALIGN), dtype=jnp.int32) * MOE_ALIGN
    is_g = group_at(row // TM)[:, None] == gi[None, :]
    of_group = lambda v: jnp.sum(jnp.where(is_g, v[None, :], 0), axis=1)
    p = row - of_group(base)
    cum_g = jnp.sum(jnp.where(is_g[:, None, :], cum[None, :, :], 0), axis=2)
    chunk = jnp.minimum(jnp.sum((p[:, None] >= cum_g).astype(jnp.int32), axis=1), n_chunks - 1)
    is_cg = (chunk[:, None] == ci[None, :])[:, :, None] & is_g[:, None, :]
    of_segment = lambda a: jnp.sum(jnp.where(is_cg, a[None, :, :], 0), axis=(1, 2))
    src = chunk * MOE_SLOTS + of_segment(start - before) + p
    zero_row = TM + MOE_GROUPS * MOE_ALIGN
    gather = jnp.where(p < of_group(total), src, zero_row) // MOE_ALIGN

    slot = jnp.arange(MOE_SLOTS // MOE_ALIGN, dtype=jnp.int32) * MOE_ALIGN
    inside = (slot[None, :, None] >= start[:, None, :]) & (slot[None, :, None] < (start + rows)[:, None, :])
    dst = jnp.sum(jnp.where(inside, (base[None, :] + before - start)[:, None, :], 0), axis=-1) + slot[None, :]
    scatter = jnp.where(jnp.any(inside, axis=-1), dst // MOE_ALIGN, -1)
    return gather.astype(jnp.int32), scatter.reshape(-1).astype(jnp.int32), sup


def _experts_kernel(gather_ref, sup_ref, h2s_ref, combs_ref, wg_ref, wu_ref, wd_ref, y_ref,
                    hbuf, cbuf, acc_ref, sem):
    s = pl.program_id(0)
    e = pl.program_id(1)
    j = pl.program_id(2)
    n_sup = pl.num_programs(0)
    grp, first, count, n_used = sup_ref[s], sup_ref[n_sup + s], sup_ref[2 * n_sup + s], sup_ref[3 * n_sup]
    tile = first + j
    listed = j < count
    real = listed & (tile < n_used)
    last_e = e == pl.num_programs(1) - 1
    per_tile = TM // MOE_ALIGN

    def tile_copies(t, slot, go):
        for g in range(per_tile):
            src = pl.multiple_of(gather_ref[t * per_tile + g] * MOE_ALIGN, MOE_ALIGN)
            dst = pl.ds(g * MOE_ALIGN, MOE_ALIGN)
            go(pltpu.make_async_copy(h2s_ref.at[pl.ds(src, MOE_ALIGN)], hbuf.at[slot, dst], sem.at[0, slot]))
            go(pltpu.make_async_copy(combs_ref.at[pl.ds(src, MOE_ALIGN)], cbuf.at[slot, dst], sem.at[1, slot]))

    @pl.when(real & (e == 0))
    def _():
        @pl.when(j == 0)
        def _():
            tile_copies(tile, j, lambda cp: cp.start())

        @pl.when((j + 1 < count) & (tile + 1 < n_used))
        def _():
            tile_copies(tile + 1, j + 1, lambda cp: cp.start())

        tile_copies(tile, j, lambda cp: cp.wait())
        acc_ref[j] = jnp.zeros(acc_ref.shape[1:], F32)

    @pl.when(real)
    def _():
        h2 = hbuf[j]
        hid = jax.nn.silu(_dot(h2, wg_ref[...].astype(BF16))) * _dot(h2, wu_ref[...].astype(BF16))
        lane = lax.broadcasted_iota(jnp.int32, (TM, LANES), 1)
        col = MOE_GROUPS + grp * MOE_EXPERTS + e
        c = jnp.sum(jnp.where(lane == col, cbuf[j], 0.0), axis=-1, keepdims=True)
        acc_ref[j] += _dot((hid * c).astype(BF16), wd_ref[...].astype(BF16))

    @pl.when(real & last_e)
    def _():
        y_ref[...] = acc_ref[j]

    @pl.when(listed & jnp.logical_not(real) & last_e)
    def _():
        y_ref[...] = jnp.zeros_like(y_ref)


def _experts(gather, sup, h2s, combs, wg, wu, wd, n_gtiles, first_expert):
    d = h2s.shape[1]
    n_sup = (sup.shape[0] - 1) // 3
    weights = lambda shape: pl.BlockSpec(
        (None,) + shape, lambda s, e, j, ga, sp: (first_expert + sp[s] * MOE_EXPERTS + e, 0, 0))

    def out_tile(s, e, j, ga, sp):
        count = sp[2 * n_sup + s]
        step = jnp.where(e == MOE_EXPERTS - 1, jnp.clip(j, 0, jnp.maximum(count - 1, 0)), 0)
        return (sp[n_sup + s] + step, 0)

    grid_spec = pltpu.PrefetchScalarGridSpec(
        num_scalar_prefetch=2,
        grid=(n_sup, MOE_EXPERTS, MOE_SUPER),
        in_specs=[pl.BlockSpec(memory_space=pl.ANY), pl.BlockSpec(memory_space=pl.ANY),
                  weights((d, MOE_HID)), weights((d, MOE_HID)), weights((MOE_HID, d))],
        out_specs=pl.BlockSpec((TM, d), out_tile),
        scratch_shapes=[pltpu.VMEM((MOE_SUPER, TM, d), BF16), pltpu.VMEM((MOE_SUPER, TM, LANES), F32),
                        pltpu.VMEM((MOE_SUPER, TM, d), F32), pltpu.SemaphoreType.DMA((2, MOE_SUPER))],
    )
    return pl.pallas_call(
        _experts_kernel,
        grid_spec=grid_spec,
        out_shape=jax.ShapeDtypeStruct((n_gtiles * TM, d), F32),
        compiler_params=_cparams(3),
        name="moe_experts",
    )(gather, sup, h2s, combs, wg, wu, wd)


def _finalize_kernel(scatter_ref, yg_ref, pos_ref, xn_ref, mod_ref, fg_ref, o_ref, ys_ref, sem, *, final):
    c = pl.program_id(0)
    per_chunk = MOE_SLOTS // MOE_ALIGN

    def chunk_copies(chunk, go):
        buf = chunk % 2
        for j in range(per_chunk):
            src = scatter_ref[chunk * per_chunk + j]

            @pl.when(src >= 0)
            def _(j=j, src=src):
                rows = pl.ds(pl.multiple_of(src * MOE_ALIGN, MOE_ALIGN), MOE_ALIGN)
                go(pltpu.make_async_copy(yg_ref.at[rows], ys_ref.at[buf, pl.ds(j * MOE_ALIGN, MOE_ALIGN)],
                                         sem.at[buf]))

    @pl.when(c == 0)
    def _():
        ys_ref[...] = jnp.zeros_like(ys_ref)
        chunk_copies(c, lambda cp: cp.start())

    @pl.when(c + 1 < pl.num_programs(0))
    def _():
        chunk_copies(c + 1, lambda cp: cp.start())

    chunk_copies(c, lambda cp: cp.wait())

    tm = xn_ref.shape[0]
    y = ys_ref[c % 2]
    hi = y.astype(BF16)
    rest = y - hi.astype(F32)
    mid = rest.astype(BF16)
    lo = (rest - mid.astype(F32)).astype(BF16)
    slot = lax.broadcasted_iota(jnp.int32, (tm, MOE_SLOTS), 1).astype(F32)
    take_t = jnp.where(_rep(pos_ref[...], MOE_SLOTS // LANES, 1) == slot, 1.0, 0.0).astype(BF16)
    y_tok = (_dot(take_t, hi) + _dot(take_t, mid)) + _dot(take_t, lo)
    xo = xn_ref[...] + mod_ref[5:6, :] * y_tok
    if final:
        xo = _rms(xo, fg_ref[...])
    o_ref[...] = xo


def _finalize(scatter, yg, pos, xn, mod_l, fg, seq, final):
    t, d = xn.shape
    tps = seq // TM
    grid_spec = pltpu.PrefetchScalarGridSpec(
        num_scalar_prefetch=1,
        grid=(t // TM,),
        in_specs=[pl.BlockSpec(memory_space=pl.ANY),
                  pl.BlockSpec((TM, LANES), lambda i, tb: (i, 0)),
                  pl.BlockSpec((TM, d), lambda i, tb: (i, 0)),
                  pl.BlockSpec((None, 6, d), lambda i, tb: (i // tps, 0, 0)),
                  pl.BlockSpec((1, d), lambda i, tb: (0, 0))],
        out_specs=pl.BlockSpec((TM, d), lambda i, tb: (i, 0)),
        scratch_shapes=[pltpu.VMEM((2, MOE_SLOTS, d), F32), pltpu.SemaphoreType.DMA((2,))],
    )
    return pl.pallas_call(
        functools.partial(_finalize_kernel, final=final),
        grid_spec=grid_spec,
        out_shape=jax.ShapeDtypeStruct((t, d), F32),
        compiler_params=_cparams(1),
        name="moe_finalize",
    )(scatter, yg, pos, xn, mod_l, fg)


def _permute_w_in(w):
    gate0 = D_LRU * 2 + D_ATT + 6 * LANES
    cv0 = gate0 + 3 * N_HEADS
    w = w.astype(BF16)
    pad = jnp.zeros((w.shape[0], IN_PAD - w.shape[1]), BF16)
    return jnp.concatenate([w[:, :gate0], w[:, cv0:], w[:, gate0:cv0], pad], axis=1)


def _router_weights(rg_w, rg_b, re_w, re_b):
    d = rg_w.shape[0]
    ne = MOE_GROUPS * MOE_EXPERTS
    w = jnp.concatenate([rg_w, jnp.transpose(re_w, (1, 0, 2)).reshape(d, ne),
                         jnp.zeros((d, LANES - MOE_GROUPS - ne), F32)], axis=1)
    b = jnp.concatenate([rg_b, re_b.reshape(ne), jnp.zeros((LANES - MOE_GROUPS - ne,), F32)])[None, :]
    w_hi = w.astype(BF16)
    w_mid = (w - w_hi.astype(F32)).astype(BF16)
    return jnp.stack([w_hi, w_mid]), b


def kernel(x, c, positions, ada_w, ada_b, norm1_g, norm2_g, w_in, lru_conv_w, lru_conv_b, lru_wa, lru_ba, lru_wx, lru_bx, lru_lambda, cmp_k_w1, cmp_k_w2, cmp_v_w1, cmp_v_w2, cmp_pos_k, cmp_pos_v, cnv_dw_w, cnv_dw_b, cnv_ln_g, cnv_ln_b, out_norm_g, w_out, moe_rg_w, moe_rg_b, moe_re_w, moe_re_b, moe_w_gate, moe_w_up, moe_w_down, final_norm_g):
    batch, seq, d = x.shape
    depth = ada_w.shape[0]
    assert d == D_MODEL and seq == N_SLC * SLC_LEN and seq % KEY_TILE == 0
    t = batch * seq
    ne = MOE_GROUPS * MOE_EXPERTS
    n_chunks = t // TM
    n_gtiles = n_chunks + -(-n_chunks * MOE_GROUPS * (MOE_ALIGN - 1) // TM) + MOE_GROUPS

    cos, sin = _rope_tables(positions)
    mod = _modulation(c, ada_w, ada_b)
    x2 = x.reshape(t, d)
    fg = final_norm_g[None, :]
    for l in range(depth):
        (xl, gl, q, kc, vc, ksa, vsa, kw, vwa, gate, cv) = _inproj(
            x2, mod[l], norm1_g[l][None, :], _permute_w_in(w_in[l]), cos, sin, seq)
        y_lru, y_cnv = _mixer(xl, gl, cv, lru_conv_w[l], lru_conv_b[l], lru_wa[l], lru_ba[l], lru_wx[l], lru_bx[l],
                              lru_lambda[l], cnv_dw_w[l], cnv_dw_b[l], cnv_ln_g[l], cnv_ln_b[l], batch, seq)
        kcmp, vcmp = _compress(kc, vc, cmp_pos_k[l], cmp_k_w1[l], cmp_k_w2[l], cmp_pos_v[l], cmp_v_w1[l], cmp_v_w2[l],
                               batch, seq)
        y_att = _attention(q, kcmp, vcmp, ksa, vsa, kw, vwa, gate, batch, seq)
        rw, rb = _router_weights(moe_rg_w[l], moe_rg_b[l], moe_re_w[l], moe_re_b[l])
        xn, h2, comb, pos, seg = _outproj(x2, y_lru, y_att, y_cnv, out_norm_g[l][None, :], w_out[l].astype(BF16),
                                          mod[l], norm2_g[l][None, :], rw, rb, seq)
        gather, scatter, sup = _moe_plan(seg, n_chunks, n_gtiles)
        yg = _experts(gather, sup, h2, comb,
                      moe_w_gate.reshape(depth * ne, d, MOE_HID), moe_w_up.reshape(depth * ne, d, MOE_HID),
                      moe_w_down.reshape(depth * ne, MOE_HID, d), n_gtiles, l * ne)
        x2 = _finalize(scatter, yg, pos, xn, mod[l], fg, seq, final=(l == depth - 1))
    return x2.reshape(batch, seq, d)
```

```python
import functools

import jax
import jax.numpy as jnp
from jax import lax
from jax.experimental import pallas as pl
from jax.experimental.pallas import tpu as pltpu

F32 = jnp.float32
BF16 = jnp.bfloat16

D_MODEL = 1024
D_LRU = 256
D_ATT = 512
D_CONV = 256
LRU_BLOCKS = 4
LRU_CONV_W = 4
LRU_C = 8.0
HEAD_DIM = 64
N_HEADS = 8
N_KV = 2
GQA = 4
ROPE_THETA = 10000.0
CMP_LEN = 32
CMP_STRIDE = 16
SLC_LEN = 64
SLC_TOPN = 16
WINDOW = 512
CONV_K = 31
MOE_GROUPS = 4
MOE_EXPERTS = 4
MOE_HID = 512
EPS = 1e-6
NEG = -1e30
FORCE = 1e9
LOG2_E = 1.4426950408889634

LANES = 128
Q_BLOCK = 128
KEY_TILE = 512
N_SLC = 128
TM = 512
MOE_ALIGN_LOG2 = 4
MOE_ALIGN = 1 << MOE_ALIGN_LOG2
MOE_SLOTS = 640
MOE_SUPER = 8
TC = 512
SCAN_ROWS = 64
LRU_TAIL = 8
CNV_TAIL = 32
IN_PAD = 2432
VMEM_LIMIT = 56 * 1024 * 1024


def _cparams(n_axes, vmem=VMEM_LIMIT):
    return pltpu.CompilerParams(dimension_semantics=("arbitrary",) * n_axes, vmem_limit_bytes=vmem)


def _dot(a, b):
    return jnp.dot(a, b, preferred_element_type=F32)


def _dot_nt(a, b):
    return lax.dot_general(a, b, (((1,), (1,)), ((), ())), preferred_element_type=F32)


def _rep(v, n, axis):
    return jnp.concatenate([v] * n, axis=axis)


def _rms(v, g):
    return v * lax.rsqrt(jnp.mean(v * v, axis=-1, keepdims=True) + EPS) * g


def _rope_kernel(pos_ref, inv_ref, sign_ref, cos_ref, sin_ref):
    ang = pos_ref[...].astype(F32) * inv_ref[...]
    cos_ref[...] = jnp.cos(ang)
    sin_ref[...] = jnp.sin(ang) * sign_ref[...]


def _rope_tables(positions):
    t = positions.size
    inv = ROPE_THETA ** (-jnp.arange(0, HEAD_DIM, 2, dtype=F32) / HEAD_DIM)
    inv128 = jnp.tile(inv, 4)[None, :]
    sign128 = jnp.tile(jnp.concatenate([-jnp.ones((32,), F32), jnp.ones((32,), F32)]), 2)[None, :]
    tr = 1024
    return pl.pallas_call(
        _rope_kernel,
        grid=(t // tr,),
        in_specs=[pl.BlockSpec((tr, 1), lambda i: (i, 0)),
                  pl.BlockSpec((1, LANES), lambda i: (0, 0)),
                  pl.BlockSpec((1, LANES), lambda i: (0, 0))],
        out_specs=[pl.BlockSpec((tr, LANES), lambda i: (i, 0))] * 2,
        out_shape=[jax.ShapeDtypeStruct((t, LANES), F32)] * 2,
        compiler_params=_cparams(1),
        name="rope_tables",
    )(positions.reshape(t, 1), inv128, sign128)


def _mod_kernel(c_ref, w_ref, b_ref, o_ref):
    sc = jax.nn.silu(c_ref[...])
    o_ref[...] = _dot(sc.astype(BF16), w_ref[...].astype(BF16)) + b_ref[...]


def _modulation(c, ada_w, ada_b):
    nl, d, n6 = ada_w.shape
    b = c.shape[0]
    rows = 16
    cp = jnp.zeros((rows, d), F32).at[:b].set(c)
    tn = 1536
    out = pl.pallas_call(
        _mod_kernel,
        grid=(nl, n6 // tn),
        in_specs=[pl.BlockSpec((rows, d), lambda l, j: (0, 0)),
                  pl.BlockSpec((None, d, tn), lambda l, j: (l, 0, j)),
                  pl.BlockSpec((None, 1, tn), lambda l, j: (l, 0, j))],
        out_specs=pl.BlockSpec((None, rows, tn), lambda l, j: (l, 0, j)),
        out_shape=jax.ShapeDtypeStruct((nl, rows, n6), F32),
        compiler_params=_cparams(2),
        name="adaln_mod",
    )(cp, ada_w, ada_b.reshape(nl, 1, n6))
    return out[:, :b].reshape(nl, b, 6, d)


def _inproj_kernel(x_ref, mod_ref, g_ref, w_ref, cos_ref, sin_ref,
                   xl_ref, gl_ref, q_ref, kc_ref, vc_ref, ksa_ref, vsa_ref, kw_ref, vwa_ref, gate_ref, cv_ref,
                   *, tiles_per_seq):
    tm = x_ref.shape[0]
    x = x_ref[...]
    h = _rms(x, g_ref[...]) * (1.0 + mod_ref[1:2, :]) + mod_ref[0:1, :]
    p = _dot(h.astype(BF16), w_ref[...])

    cos = cos_ref[...]
    sin = sin_ref[...]
    lane = lax.broadcasted_iota(jnp.int32, (tm, LANES), 1)
    first_half = (lane & 63) < 32

    def rope(v):
        rot = jnp.where(first_half, pltpu.roll(v, 96, 1), pltpu.roll(v, 32, 1))
        return v * cos + rot * sin

    xl_ref[...] = p[:, 0:256]
    gl_ref[...] = p[:, 256:512]
    scale = HEAD_DIM ** -0.5 * LOG2_E
    low = lane < 64
    for m in range(4):
        slab = rope(p[:, 512 + 128 * m:640 + 128 * m]) * scale
        swapped = pltpu.roll(slab, 64, 1)
        for hh in range(2):
            head = 2 * m + hh
            kv = head // GQA
            src = slab if hh == kv else swapped
            keep = low if kv == 0 else jnp.logical_not(low)
            q_ref[:, head * LANES:(head + 1) * LANES] = jnp.where(keep, src, 0.0).astype(BF16)
    kc_ref[...] = rope(p[:, 1024:1152])
    vc_ref[...] = p[:, 1152:1280]
    row = lax.broadcasted_iota(jnp.int32, (tm, LANES), 0)
    s_base = (pl.program_id(0) % tiles_per_seq) * tm
    onehot = jnp.where(lane == ((s_base + row) >> 6), 1.0, 0.0).astype(BF16)
    ones = jnp.ones((tm, LANES), BF16)
    ksa_ref[:, 0:LANES] = rope(p[:, 1280:1408]).astype(BF16)
    ksa_ref[:, LANES:2 * LANES] = onehot
    vsa_ref[:, 0:LANES] = p[:, 1408:1536].astype(BF16)
    vsa_ref[:, LANES:2 * LANES] = ones
    kw_ref[...] = rope(p[:, 1536:1664]).astype(BF16)
    vwa_ref[:, 0:LANES] = p[:, 1664:1792].astype(BF16)
    vwa_ref[:, LANES:2 * LANES] = ones
    cv_ref[...] = p[:, 1792:2304]
    gate_ref[...] = jax.nn.sigmoid(p[:, 2304:2432])


def _inproj(x2, mod_l, g, w_p, cos, sin, seq):
    t, d = x2.shape
    tps = seq // TM
    row = lambda w: pl.BlockSpec((TM, w), lambda i: (i, 0))
    outs = [(256, F32), (256, F32), (N_HEADS * LANES, BF16), (LANES, F32), (LANES, F32), (2 * LANES, BF16),
            (2 * LANES, BF16), (LANES, BF16), (2 * LANES, BF16), (LANES, F32), (2 * D_CONV, F32)]
    return pl.pallas_call(
        functools.partial(_inproj_kernel, tiles_per_seq=tps),
        grid=(t // TM,),
        in_specs=[row(d),
                  pl.BlockSpec((None, 6, d), lambda i: (i // tps, 0, 0)),
                  pl.BlockSpec((1, d), lambda i: (0, 0)),
                  pl.BlockSpec((d, IN_PAD), lambda i: (0, 0)),
                  row(LANES), row(LANES)],
        out_specs=[row(w) for w, _ in outs],
        out_shape=[jax.ShapeDtypeStruct((t, w), dt) for w, dt in outs],
        compiler_params=_cparams(1),
        name="in_proj",
    )(x2, mod_l, g, w_p, cos, sin)


def _shift_rows(v, s, fill, row):
    return jnp.where(row < s, fill, pltpu.roll(v, s, 0))


def _causal_taps(ext, w_ref, bias, width, tail, tc):
    acc = bias
    for b in range(min(8, width)):
        shifted = pltpu.roll(ext, b, 0) if b else ext
        for k in range(width):
            back = width - 1 - k
            if back % 8 == b:
                start = tail - (back - b)
                acc = acc + w_ref[k:k + 1, :] * shifted[start:start + tc, :]
    return acc


def _mixer_kernel(xl_ref, gl_ref, cv_ref, lcw_ref, lcb_ref, wa_ref, ba_ref, wx_ref, bx_ref, lam_ref,
                  cw_ref, cb_ref, lng_ref, lnb_ref, ylru_ref, ycnv_ref,
                  extl, extc, abuf, ubuf, hcar):
    tc = xl_ref.shape[0]

    @pl.when(pl.program_id(1) == 0)
    def _():
        extl[0:LRU_TAIL, :] = jnp.zeros((LRU_TAIL, D_LRU), F32)
        extc[0:CNV_TAIL, :] = jnp.zeros((CNV_TAIL, D_CONV), F32)
        hcar[...] = jnp.zeros_like(hcar)

    xl = xl_ref[...]
    extl[LRU_TAIL:LRU_TAIL + tc, :] = xl
    xc = _causal_taps(extl[...], lcw_ref, lcb_ref[...], LRU_CONV_W, LRU_TAIL, tc)
    extl[0:LRU_TAIL, :] = xl[tc - LRU_TAIL:tc, :]
    xcb = xc.astype(BF16)
    r = jax.nn.sigmoid(_dot(xcb, wa_ref[...]) + ba_ref[...])
    gi = jax.nn.sigmoid(_dot(xcb, wx_ref[...]) + bx_ref[...])
    log_a = LRU_C * r * jax.nn.log_sigmoid(lam_ref[...])
    th = jnp.tanh(log_a)
    one_minus_a2 = -2.0 * th / (1.0 - th)
    abuf[...] = jnp.exp(log_a)
    ubuf[...] = jnp.sqrt(one_minus_a2) * (gi * xc)

    row = lax.broadcasted_iota(jnp.int32, (SCAN_ROWS, D_LRU), 0)

    def scan_chunk(c, h_prev):
        off = pl.multiple_of(c * SCAN_ROWS, SCAN_ROWS)
        a = abuf[pl.ds(off, SCAN_ROWS), :]
        b = ubuf[pl.ds(off, SCAN_ROWS), :]
        s = 1
        while s < SCAN_ROWS:
            b = b + a * _shift_rows(b, s, 0.0, row)
            a = a * _shift_rows(a, s, 1.0, row)
            s *= 2
        h = b + a * h_prev
        ylru_ref[pl.ds(off, SCAN_ROWS), :] = h * jax.nn.gelu(gl_ref[pl.ds(off, SCAN_ROWS), :])
        return h[SCAN_ROWS - 1:SCAN_ROWS, :]

    hcar[0:1, :] = lax.fori_loop(0, tc // SCAN_ROWS, scan_chunk, hcar[0:1, :])

    u = cv_ref[...]
    v = u[:, 0:D_CONV] * jax.nn.sigmoid(u[:, D_CONV:2 * D_CONV])
    extc[CNV_TAIL:CNV_TAIL + tc, :] = v
    acc = _causal_taps(extc[...], cw_ref, cb_ref[...], CONV_K, CNV_TAIL, tc)
    extc[0:CNV_TAIL, :] = v[tc - CNV_TAIL:tc, :]
    mu = jnp.mean(acc, axis=-1, keepdims=True)
    var = jnp.mean(jnp.square(acc - mu), axis=-1, keepdims=True)
    y = (acc - mu) * lax.rsqrt(var + EPS) * lng_ref[...] + lnb_ref[...]
    ycnv_ref[...] = jax.nn.silu(y)


def _block_diag(w):
    n, rows, cols = w.shape
    eye = jnp.eye(n, dtype=bool)
    return jnp.where(eye[:, None, :, None], w[:, :, None, :], 0.0).reshape(n * rows, n * cols)


def _mixer(xl, gl, cv, lcw, lcb, wa, ba, wx, bx, lam, cw, cb, lng, lnb, batch, seq):
    seqrow = lambda w: pl.BlockSpec((None, TC, w), lambda b, t: (b, t, 0))
    full = lambda a: pl.BlockSpec(a.shape, lambda b, t: (0,) * a.ndim)
    small = [lcw, lcb[None, :], _block_diag(wa).astype(BF16), ba[None, :], _block_diag(wx).astype(BF16), bx[None, :],
             lam[None, :], jnp.zeros((32, D_CONV), F32).at[:CONV_K].set(cw), cb[None, :], lng[None, :], lnb[None, :]]
    return pl.pallas_call(
        _mixer_kernel,
        grid=(batch, seq // TC),
        in_specs=[seqrow(D_LRU), seqrow(D_LRU), seqrow(2 * D_CONV)] + [full(a) for a in small],
        out_specs=[seqrow(D_LRU), seqrow(D_CONV)],
        out_shape=[jax.ShapeDtypeStruct((batch, seq, D_LRU), F32), jax.ShapeDtypeStruct((batch, seq, D_CONV), F32)],
        scratch_shapes=[pltpu.VMEM((TC + LRU_TAIL, D_LRU), F32), pltpu.VMEM((TC + CNV_TAIL, D_CONV), F32),
                        pltpu.VMEM((TC, D_LRU), F32), pltpu.VMEM((TC, D_LRU), F32), pltpu.VMEM((8, D_LRU), F32)],
        compiler_params=_cparams(2),
        name="mixer_stream",
    )(xl.reshape(batch, seq, D_LRU), gl.reshape(batch, seq, D_LRU), cv.reshape(batch, seq, 2 * D_CONV), *small)


def _cmp_kernel(kc_ref, vc_ref, posk_ref, posv_ref, wk1_ref, wk2_ref, wv1_ref, wv2_ref, ko_ref, vo_ref, nat_ref):
    nh = ko_ref.shape[0]
    quarter = nh // 4

    def one(t_ref, pos_ref, w1_ref, w2_ref, o_ref):
        first = jnp.zeros((nh, w1_ref.shape[2]), F32)
        second = jnp.zeros((nh, w1_ref.shape[2]), F32)
        for l in range(CMP_STRIDE):
            x = t_ref[pl.ds(l, nh, stride=CMP_STRIDE), :]
            first = first + _dot((x + pos_ref[l:l + 1, :]).astype(BF16), w1_ref[l])
            second = second + _dot((x + pos_ref[CMP_STRIDE + l:CMP_STRIDE + l + 1, :]).astype(BF16),
                                   w1_ref[CMP_STRIDE + l])
        hid = jax.nn.gelu(first + pltpu.roll(second, nh - 1, 0))
        nat_ref[...] = _dot(hid.astype(BF16), w2_ref[...])
        for r in range(4):
            o_ref[r * quarter:(r + 1) * quarter, :] = nat_ref[pl.ds(r, quarter, stride=4), :].astype(o_ref.dtype)

    one(kc_ref, posk_ref, wk1_ref, wk2_ref, ko_ref)
    one(vc_ref, posv_ref, wv1_ref, wv2_ref, vo_ref)


def _cmp_weights(pos, w1, w2):
    hid = w1.shape[1]
    eye = jnp.eye(N_KV, dtype=bool)
    w1 = w1.astype(BF16).reshape(CMP_LEN, HEAD_DIM, hid)
    w1x = jnp.where(eye[None, :, None, :, None], w1[:, None, :, None, :], 0.0)
    w1x = w1x.reshape(CMP_LEN, N_KV * HEAD_DIM, N_KV * hid)
    w2x = _block_diag(jnp.broadcast_to(w2.astype(BF16), (N_KV,) + w2.shape))
    return jnp.tile(pos, (1, N_KV)), w1x, w2x


def _compress(kc, vc, pos_k, wk1, wk2, pos_v, wv1, wv2, batch, seq):
    nh = seq // CMP_STRIDE
    wide = N_KV * HEAD_DIM
    pk, wk1x, wk2x = _cmp_weights(pos_k, wk1, wk2)
    pv, wv1x, wv2x = _cmp_weights(pos_v, wv1, wv2)
    full = lambda a: pl.BlockSpec(a.shape, lambda b: (0,) * a.ndim)
    seqs = pl.BlockSpec((None, seq, wide), lambda b: (b, 0, 0))
    outs = pl.BlockSpec((None, nh, wide), lambda b: (b, 0, 0))
    return pl.pallas_call(
        _cmp_kernel,
        grid=(batch,),
        in_specs=[seqs, seqs, full(pk), full(pv), full(wk1x), full(wk2x), full(wv1x), full(wv2x)],
        out_specs=[outs, outs],
        out_shape=[jax.ShapeDtypeStruct((batch, nh, wide), BF16)] * 2,
        scratch_shapes=[pltpu.VMEM((nh, wide), F32)],
        compiler_params=_cparams(1),
        name="nsa_compress",
    )(kc.reshape(batch, seq, wide), vc.reshape(batch, seq, wide), pk, pv, wk1x, wk2x, wv1x, wv2x)


def _attn_kernel(q_ref, kcmp_ref, vcmp_ref, ksa_ref, vsa_ref, kw_ref, vwa_ref, gate_ref, o_ref,
                 qa_ref, part_ref, acc_ref, m_ref, accw_ref, mw_ref, sa_ref, sb_ref):
    rows = GQA * Q_BLOCK
    reps = KEY_TILE // LANES
    qb = pl.program_id(1)
    s0 = qb * Q_BLOCK
    kd = s0 // KEY_TILE
    off_d = pl.multiple_of(kd * KEY_TILE, KEY_TILE)
    off_p = pl.multiple_of(jnp.maximum(kd - 1, 0) * KEY_TILE, KEY_TILE)
    d0 = s0 - off_d

    lane = lax.broadcasted_iota(jnp.int32, (Q_BLOCK, LANES), 1)
    sub = lax.broadcasted_iota(jnp.int32, (Q_BLOCK, LANES), 0)
    r_q = lax.broadcasted_iota(jnp.int32, (Q_BLOCK, KEY_TILE), 0)
    c_k = lax.broadcasted_iota(jnp.int32, (Q_BLOCK, KEY_TILE), 1)
    rel = c_k - r_q
    gate = gate_ref[...]

    def tile_rows(plane):
        return _rep(plane, GQA, 0)

    def flash_step(acc, m, kv, s, v_tile):
        m_old = m[kv]
        m_new = jnp.maximum(m_old, jnp.broadcast_to(jnp.max(s, axis=-1, keepdims=True), m_old.shape))
        p = jnp.exp2(s - _rep(m_new, reps, 1))
        acc[kv] = acc[kv] * _rep(jnp.exp2(m_old - m_new), 2, 1) + _dot(p.astype(BF16), v_tile)
        m[kv] = m_new

    def flash_result(acc, kv):
        a = acc[kv]
        return a[:, 0:LANES] / a[:, LANES:2 * LANES]

    q128 = [jnp.concatenate([q_ref[:, (kv * GQA + g) * LANES:(kv * GQA + g + 1) * LANES] for g in range(GQA)], axis=0)
            for kv in range(N_KV)]

    kcmp = kcmp_ref[...]
    vcmp = vcmp_ref[...]
    cmp_end = SLC_LEN * (c_k & (N_SLC - 1)) + CMP_STRIDE * (c_k >> 7) + (CMP_LEN - 1)
    cmp_bias = tile_rows(jnp.where(cmp_end - r_q <= s0, 0.0, NEG))
    has_cmp = tile_rows(jnp.where(s0 + sub >= CMP_LEN - 1, 1.0, 0.0))
    o_cmp, imp_t = [], []
    for kv in range(N_KV):
        s_c = _dot_nt(q128[kv], kcmp) + cmp_bias
        mb = jnp.broadcast_to(jnp.max(s_c, axis=-1, keepdims=True), (rows, LANES))
        e = jnp.exp2(s_c - _rep(mb, reps, 1))
        lb = jnp.broadcast_to(jnp.sum(e, axis=-1, keepdims=True), (rows, LANES))
        inv = has_cmp / lb
        o_cmp.append(_dot(e.astype(BF16), vcmp) * inv)
        e4 = (e[:, 0:N_SLC] + e[:, N_SLC:2 * N_SLC] + e[:, 2 * N_SLC:3 * N_SLC] + e[:, 3 * N_SLC:4 * N_SLC]) * inv
        imp = e4[0:Q_BLOCK] + e4[Q_BLOCK:2 * Q_BLOCK] + e4[2 * Q_BLOCK:3 * Q_BLOCK] + e4[3 * Q_BLOCK:4 * Q_BLOCK]
        imp_t.append(imp.T)

    accw_ref[...] = jnp.zeros_like(accw_ref)
    mw_ref[...] = jnp.full_like(mw_ref, NEG)
    prev_bias = tile_rows(jnp.where(rel > jnp.where(kd > 0, d0, KEY_TILE), 0.0, NEG))
    diag_bias = tile_rows(jnp.where(rel <= d0, 0.0, NEG))
    kw_p, vw_p = kw_ref[pl.ds(off_p, KEY_TILE), :], vwa_ref[pl.ds(off_p, KEY_TILE), :]
    kw_d, vw_d = kw_ref[pl.ds(off_d, KEY_TILE), :], vwa_ref[pl.ds(off_d, KEY_TILE), :]
    for kv in range(N_KV):
        flash_step(accw_ref, mw_ref, kv, _dot_nt(q128[kv], kw_p) + prev_bias, vw_p)
    for kv in range(N_KV):
        flash_step(accw_ref, mw_ref, kv, _dot_nt(q128[kv], kw_d) + diag_bias, vw_d)
    for kv in range(N_KV):
        o_win = flash_result(accw_ref, kv)
        for g in range(GQA):
            col = 3 * (kv * GQA + g)
            rs = slice(g * Q_BLOCK, (g + 1) * Q_BLOCK)
            part_ref[kv, rs, :] = gate[:, col:col + 1] * o_cmp[kv][rs] + gate[:, col + 2:col + 3] * o_win[rs]

    blk = lax.broadcasted_iota(jnp.int32, (N_SLC, Q_BLOCK), 0)
    t_q = s0 + lax.broadcasted_iota(jnp.int32, (N_SLC, Q_BLOCK), 1)
    forced = (blk == 0) | (blk == (t_q >> 6))
    val = [jnp.where(forced, -jnp.inf, jnp.where(blk * SLC_LEN <= t_q, imp_t[kv], -1.0)) for kv in range(N_KV)]
    blk_f = blk.astype(F32)
    for _ in range(SLC_TOPN - 2):
        for kv in range(N_KV):
            mx = jnp.max(val[kv], axis=0, keepdims=True)
            first = jnp.min(jnp.where(val[kv] == mx, blk_f, float(N_SLC)), axis=0, keepdims=True)
            val[kv] = jnp.where(blk_f == first, -jnp.inf, val[kv])
    for kv in range(N_KV):
        bias = jnp.where(val[kv].T == -jnp.inf, 0.0, NEG).astype(BF16)
        qa_ref[kv] = jnp.concatenate([q128[kv], jnp.concatenate([bias] * GQA, axis=0)], axis=1)

    acc_ref[...] = jnp.zeros_like(acc_ref)
    m_ref[...] = jnp.full_like(m_ref, NEG)

    def scores_into(dst, kt):
        k_t = ksa_ref[pl.ds(pl.multiple_of(kt * KEY_TILE, KEY_TILE), KEY_TILE), :]
        for kv in range(N_KV):
            dst[kv] = _dot_nt(qa_ref[kv], k_t)

    def consume(src, kt, bias):
        v_t = vsa_ref[pl.ds(pl.multiple_of(kt * KEY_TILE, KEY_TILE), KEY_TILE), :]
        for kv in range(N_KV):
            flash_step(acc_ref, m_ref, kv, src[kv] if bias is None else src[kv] + bias, v_t)

    scores_into(sa_ref, 0)

    def slc_pairs(first, count):
        for i in range(count):
            a = first + 2 * i
            scores_into(sb_ref, a + 1)
            consume(sa_ref, a, None)
            scores_into(sa_ref, a + 2)
            consume(sb_ref, a + 1, None)

    def slc_quad(j, carry):
        slc_pairs(4 * j, 2)
        return carry

    def slc_pair(j, carry):
        slc_pairs(4 * n_quads + 2 * j, 1)
        return carry

    n_quads = kd // 4
    lax.fori_loop(0, n_quads, slc_quad, 0)
    lax.fori_loop(0, (kd // 2) & 1, slc_pair, 0)
    causal_bias = tile_rows(jnp.where(rel <= d0, 0.0, NEG))
    kd_odd = (kd & 1) == 1

    @pl.when(kd_odd)
    def _():
        scores_into(sb_ref, kd)
        consume(sa_ref, kd - 1, None)
        consume(sb_ref, kd, causal_bias)

    @pl.when(jnp.logical_not(kd_odd))
    def _():
        consume(sa_ref, kd, causal_bias)

    for kv in range(N_KV):
        o_slc = flash_result(acc_ref, kv)
        pieces = []
        for g in range(GQA):
            col = 3 * (kv * GQA + g)
            rs = slice(g * Q_BLOCK, (g + 1) * Q_BLOCK)
            pieces.append(part_ref[kv, rs, :] + gate[:, col + 1:col + 2] * o_slc[rs])
        for mm in range(2):
            even, odd = pieces[2 * mm], pieces[2 * mm + 1]
            if kv == 0:
                odd = pltpu.roll(odd, 64, 1)
            else:
                even = pltpu.roll(even, 64, 1)
            slab = kv * 2 + mm
            o_ref[:, slab * LANES:(slab + 1) * LANES] = jnp.where(lane < 64, even, odd)


def _attention(q, kcmp, vcmp, ksa, vsa, kw, vwa, gate, batch, seq):
    nh = kcmp.shape[1]
    assert nh == KEY_TILE
    rows = GQA * Q_BLOCK
    res = lambda w: pl.BlockSpec((None, seq, w), lambda b, i: (b, 0, 0))
    blk = lambda w: pl.BlockSpec((None, Q_BLOCK, w), lambda b, i: (b, i, 0))
    r3 = lambda a, w: a.reshape(batch, seq, w)
    return pl.pallas_call(
        _attn_kernel,
        grid=(batch, seq // Q_BLOCK),
        in_specs=[blk(N_HEADS * LANES),
                  pl.BlockSpec((None, nh, LANES), lambda b, i: (b, 0, 0)),
                  pl.BlockSpec((None, nh, LANES), lambda b, i: (b, 0, 0)),
                  res(2 * LANES), res(2 * LANES), res(LANES), res(2 * LANES), blk(LANES)],
        out_specs=blk(D_ATT),
        out_shape=jax.ShapeDtypeStruct((batch, seq, D_ATT), F32),
        scratch_shapes=[pltpu.VMEM((N_KV, rows, 2 * LANES), BF16), pltpu.VMEM((N_KV, rows, LANES), F32),
                        pltpu.VMEM((N_KV, rows, 2 * LANES), F32), pltpu.VMEM((N_KV, rows, LANES), F32),
                        pltpu.VMEM((N_KV, rows, 2 * LANES), F32), pltpu.VMEM((N_KV, rows, LANES), F32),
                        pltpu.VMEM((N_KV, rows, KEY_TILE), F32), pltpu.VMEM((N_KV, rows, KEY_TILE), F32)],
        compiler_params=_cparams(2),
        name="nsa_attention",
    )(r3(q, N_HEADS * LANES), kcmp, vcmp, r3(ksa, 2 * LANES), r3(vsa, 2 * LANES), r3(kw, LANES), r3(vwa, 2 * LANES),
      r3(gate, LANES))


def _outproj_kernel(x_ref, yl_ref, ya_ref, yc_ref, gn_ref, w_ref, mod_ref, n2_ref, rw_ref, rb_ref,
                    xn_ref, h2_ref, comb_ref, pos_ref, seg_ref):
    tm = x_ref.shape[0]
    a, b = D_LRU, D_LRU + D_ATT
    yl = _rms(yl_ref[...], gn_ref[:, 0:a]).astype(BF16)
    ya = _rms(ya_ref[...], gn_ref[:, a:b]).astype(BF16)
    yc = _rms(yc_ref[...], gn_ref[:, b:D_MODEL]).astype(BF16)
    y = _dot(yl, w_ref[0:a, :]) + _dot(ya, w_ref[a:b, :]) + _dot(yc, w_ref[b:D_MODEL, :])
    xn = x_ref[...] + mod_ref[2:3, :] * y
    xn_ref[...] = xn
    h2 = _rms(xn, n2_ref[...]) * (1.0 + mod_ref[4:5, :]) + mod_ref[3:4, :]

    h_hi = h2.astype(BF16)
    h_mid = (h2 - h_hi.astype(F32)).astype(BF16)
    logit = _dot(h_hi, rw_ref[0]) + (_dot(h_hi, rw_ref[1]) + _dot(h_mid, rw_ref[0])) + rb_ref[...]
    lane = lax.broadcasted_iota(jnp.int32, (tm, LANES), 1)
    ninf = -jnp.inf
    is_g = lane < MOE_GROUPS
    is_e = (lane >= MOE_GROUPS) & (lane < MOE_GROUPS + MOE_GROUPS * MOE_EXPERTS)
    lg_max = jnp.max(jnp.where(is_g, logit, ninf), axis=-1, keepdims=True)
    g_star = jnp.min(jnp.where(is_g & (logit == lg_max), lane, LANES), axis=-1, keepdims=True)
    pg_star = 1.0 / jnp.sum(jnp.where(is_g, jnp.exp(logit - lg_max), 0.0), axis=-1, keepdims=True)
    in_grp = is_e & (((lane - MOE_GROUPS) >> 2) == g_star)
    v1 = jnp.max(jnp.where(in_grp, logit, ninf), axis=-1, keepdims=True)
    i1 = jnp.min(jnp.where(in_grp & (logit == v1), lane, LANES), axis=-1, keepdims=True)
    rest = in_grp & (lane != i1)
    v2 = jnp.max(jnp.where(rest, logit, ninf), axis=-1, keepdims=True)
    i2 = jnp.min(jnp.where(rest & (logit == v2), lane, LANES), axis=-1, keepdims=True)
    d = jnp.exp(v2 - v1)
    pe1 = 1.0 / (1.0 + d)
    pe2 = d / (1.0 + d)
    comb = jnp.where(lane == i1, pe1, jnp.where(lane == i2, pe2, 0.0)) * pg_star

    onehot = jnp.where(lane == g_star, 1.0, 0.0)
    r_i = lax.broadcasted_iota(jnp.int32, (tm, tm), 0)
    c_i = lax.broadcasted_iota(jnp.int32, (tm, tm), 1)
    earlier = jnp.where(c_i < r_i, 1.0, 0.0).astype(BF16)
    rank = jnp.sum(onehot * _dot(earlier, onehot.astype(BF16)), axis=-1, keepdims=True)
    n_g = jnp.sum(onehot, axis=0, keepdims=True)
    n_pad = (((n_g.astype(jnp.int32) + (MOE_ALIGN - 1)) >> MOE_ALIGN_LOG2) << MOE_ALIGN_LOG2).astype(F32)
    lane1 = lax.broadcasted_iota(jnp.int32, (1, LANES), 1)
    start = jnp.zeros((1, LANES), F32)
    below = jnp.zeros((1, 1), F32)
    for g in range(1, MOE_GROUPS):
        below = below + jnp.sum(jnp.where(lane1 == g - 1, n_pad, 0.0), axis=-1, keepdims=True)
        start = start + jnp.where(lane1 == g, below, 0.0)
    pos = jnp.sum(onehot * start, axis=-1, keepdims=True) + rank
    pos_rep = jnp.broadcast_to(pos, (tm, LANES))
    slot = lax.broadcasted_iota(jnp.int32, (tm, MOE_SLOTS), 1).astype(F32)
    take_t = jnp.where(_rep(pos_rep, MOE_SLOTS // LANES, 1) == slot, 1.0, 0.0)
    take = take_t.T
    take = take.astype(BF16)
    h2_ref[...] = _dot(take, h_hi).astype(BF16)
    c_hi = comb.astype(BF16)
    c_rest = comb - c_hi.astype(F32)
    c_mid = c_rest.astype(BF16)
    c_lo = (c_rest - c_mid.astype(F32)).astype(BF16)
    comb_ref[...] = (_dot(take, c_hi) + _dot(take, c_mid)) + _dot(take, c_lo)
    pos_ref[...] = pos_rep
    seg_ref[...] = jnp.zeros(seg_ref.shape, jnp.int32)
    seg_ref[0:1, :] = n_pad.astype(jnp.int32)
    seg_ref[1:2, :] = start.astype(jnp.int32)


def _outproj(x2, yl, ya, yc, gn, w_out, mod_l, n2g, rw, rb, seq):
    t, d = x2.shape
    tps = seq // TM
    row = lambda w: pl.BlockSpec((TM, w), lambda i: (i, 0))
    full = lambda a: pl.BlockSpec(a.shape, lambda i: (0,) * a.ndim)
    return pl.pallas_call(
        _outproj_kernel,
        grid=(t // TM,),
        in_specs=[row(d), row(D_LRU), row(D_ATT), row(D_CONV), full(gn), full(w_out),
                  pl.BlockSpec((None, 6, d), lambda i: (i // tps, 0, 0)), full(n2g), full(rw), full(rb)],
        out_specs=[row(d), pl.BlockSpec((MOE_SLOTS, d), lambda i: (i, 0)), pl.BlockSpec((MOE_SLOTS, LANES), lambda i: (i, 0)),
                   row(LANES), pl.BlockSpec((None, 8, LANES), lambda i: (i, 0, 0))],
        out_shape=[jax.ShapeDtypeStruct((t, d), F32), jax.ShapeDtypeStruct((t // TM * MOE_SLOTS, d), BF16),
                   jax.ShapeDtypeStruct((t // TM * MOE_SLOTS, LANES), F32), jax.ShapeDtypeStruct((t, LANES), F32),
                   jax.ShapeDtypeStruct((t // TM, 8, LANES), jnp.int32)],
        compiler_params=_cparams(1),
        name="out_proj_router",
    )(x2, yl.reshape(t, D_LRU), ya.reshape(t, D_ATT), yc.reshape(t, D_CONV), gn, w_out, mod_l, n2g, rw, rb)


def _moe_plan(seg, n_chunks, n_gtiles):
    rows = seg[:, 0, :MOE_GROUPS]
    start = seg[:, 1, :MOE_GROUPS]
    ci = jnp.arange(n_chunks, dtype=jnp.int32)
    gi = jnp.arange(MOE_GROUPS, dtype=jnp.int32)
    cum = jnp.sum(jnp.where((ci[None, :] <= ci[:, None])[:, :, None], rows[None, :, :], 0), axis=1)
    before = cum - rows
    total = jnp.sum(rows, axis=0)
    tiles_g = (total + (TM - 1)) // TM
    ends = jnp.sum(jnp.where(gi[None, :] <= gi[:, None], tiles_g[None, :], 0), axis=1)
    base = (ends - tiles_g) * TM
    group_at = lambda tile: jnp.minimum(jnp.sum((tile[:, None] >= ends[None, :]).astype(jnp.int32), axis=1),
                                        MOE_GROUPS - 1)
    n_sup = -(-n_gtiles // MOE_SUPER) + MOE_GROUPS
    listed_g = jnp.where(gi == MOE_GROUPS - 1, n_gtiles - (ends[-1] - tiles_g[-1]), tiles_g)
    sup_g = (listed_g + (MOE_SUPER - 1)) // MOE_SUPER
    sup_ends = jnp.sum(jnp.where(gi[None, :] <= gi[:, None], sup_g[None, :], 0), axis=1)
    si = jnp.arange(n_sup, dtype=jnp.int32)
    s_is_g = jnp.minimum(jnp.sum((si[:, None] >= sup_ends[None, :]).astype(jnp.int32), axis=1),
                         MOE_GROUPS - 1)[:, None] == gi[None, :]
    s_of = lambda v: jnp.sum(jnp.where(s_is_g, v[None, :], 0), axis=1)
    inner = (si - s_of(sup_ends - sup_g)) * MOE_SUPER
    used = si < sup_ends[-1]
    sup = jnp.concatenate([
        s_of(gi),
        jnp.where(used, s_of(ends - tiles_g) + inner, n_gtiles - 1),
        jnp.where(used, jnp.clip(s_of(listed_g) - inner, 0, MOE_SUPER), 0),
        ends[-1:]]).astype(jnp.int32)

    row = jnp.arange(n_gtiles * (TM // MOE_ALIGN), dtype=jnp.int32) * MOE_ALIGN
    is_g = group_at(row // TM)[:, None] == gi[None, :]
    of_group = lambda v: jnp.sum(jnp.where(is_g, v[None, :], 0), axis=1)
    p = row - of_group(base)
    cum_g = jnp.sum(jnp.where(is_g[:, None, :], cum[None, :, :], 0), axis=2)
    chunk = jnp.minimum(jnp.sum((p[:, None] >= cum_g).astype(jnp.int32), axis=1), n_chunks - 1)
    is_cg = (chunk[:, None] == ci[None, :])[:, :, None] & is_g[:, None, :]
    of_segment = lambda a: jnp.sum(jnp.where(is_cg, a[None, :, :], 0), axis=(1, 2))
    src = chunk * MOE_SLOTS + of_segment(start - before) + p
    zero_row = TM + MOE_GROUPS * MOE_ALIGN
    gather = jnp.where(p < of_group(total), src, zero_row) // MOE_ALIGN

    slot = jnp.arange(MOE_SLOTS // MOE_ALIGN, dtype=jnp.int32) * MOE_ALIGN
    inside = (slot[None, :, None] >= start[:, None, :]) & (slot[None, :, None] < (start + rows)[:, None, :])
    dst = jnp.sum(jnp.where(inside, (base[None, :] + before - start)[:, None, :], 0), axis=-1) + slot[None, :]
    scatter = jnp.where(jnp.any(inside, axis=-1), dst // MOE_ALIGN, -1)
    return gather.astype(jnp.int32), scatter.reshape(-1).astype(jnp.int32), sup


def _experts_kernel(gather_ref, sup_ref, h2s_ref, combs_ref, wg_ref, wu_ref, wd_ref, y_ref,
                    hbuf, cbuf, acc_ref, wgb, wub, wdb, sem):
    s = pl.program_id(0)
    e = pl.program_id(1)
    j = pl.program_id(2)
    n_sup = pl.num_programs(0)
    grp, first, count, n_used = sup_ref[s], sup_ref[n_sup + s], sup_ref[2 * n_sup + s], sup_ref[3 * n_sup]
    tile = first + j
    listed = j < count
    real = listed & (tile < n_used)
    last_e = e == pl.num_programs(1) - 1
    per_tile = TM // MOE_ALIGN

    def tile_copies(t, slot, go):
        for g in range(per_tile):
            src = pl.multiple_of(gather_ref[t * per_tile + g] * MOE_ALIGN, MOE_ALIGN)
            dst = pl.ds(g * MOE_ALIGN, MOE_ALIGN)
            go(pltpu.make_async_copy(h2s_ref.at[pl.ds(src, MOE_ALIGN)], hbuf.at[slot, dst], sem.at[0, slot]))
            go(pltpu.make_async_copy(combs_ref.at[pl.ds(src, MOE_ALIGN)], cbuf.at[slot, dst], sem.at[1, slot]))

    @pl.when(real & (e == 0))
    def _():
        @pl.when(j == 0)
        def _():
            tile_copies(tile, j, lambda cp: cp.start())

        @pl.when((j + 1 < count) & (tile + 1 < n_used))
        def _():
            tile_copies(tile + 1, j + 1, lambda cp: cp.start())

        tile_copies(tile, j, lambda cp: cp.wait())
        acc_ref[j] = jnp.zeros(acc_ref.shape[1:], F32)

    @pl.when(real & (j == 0))
    def _():
        wgb[...] = wg_ref[...].astype(BF16)
        wub[...] = wu_ref[...].astype(BF16)
        wdb[...] = wd_ref[...].astype(BF16)

    @pl.when(real)
    def _():
        h2 = hbuf[j]
        hid = jax.nn.silu(_dot(h2, wgb[...])) * _dot(h2, wub[...])
        lane = lax.broadcasted_iota(jnp.int32, (TM, LANES), 1)
        col = MOE_GROUPS + grp * MOE_EXPERTS + e
        c = jnp.sum(jnp.where(lane == col, cbuf[j], 0.0), axis=-1, keepdims=True)
        acc_ref[j] += _dot((hid * c).astype(BF16), wdb[...])

    @pl.when(real & last_e)
    def _():
        y_ref[...] = acc_ref[j]

    @pl.when(listed & jnp.logical_not(real) & last_e)
    def _():
        y_ref[...] = jnp.zeros_like(y_ref)


def _experts(gather, sup, h2s, combs, wg, wu, wd, n_gtiles, first_expert):
    d = h2s.shape[1]
    n_sup = (sup.shape[0] - 1) // 3
    weights = lambda shape: pl.BlockSpec(
        (None,) + shape, lambda s, e, j, ga, sp: (first_expert + sp[s] * MOE_EXPERTS + e, 0, 0))

    def out_tile(s, e, j, ga, sp):
        count = sp[2 * n_sup + s]
        step = jnp.where(e == MOE_EXPERTS - 1, jnp.clip(j, 0, jnp.maximum(count - 1, 0)), 0)
        return (sp[n_sup + s] + step, 0)

    grid_spec = pltpu.PrefetchScalarGridSpec(
        num_scalar_prefetch=2,
        grid=(n_sup, MOE_EXPERTS, MOE_SUPER),
        in_specs=[pl.BlockSpec(memory_space=pl.ANY), pl.BlockSpec(memory_space=pl.ANY),
                  weights((d, MOE_HID)), weights((d, MOE_HID)), weights((MOE_HID, d))],
        out_specs=pl.BlockSpec((TM, d), out_tile),
        scratch_shapes=[pltpu.VMEM((MOE_SUPER, TM, d), BF16), pltpu.VMEM((MOE_SUPER, TM, LANES), F32),
                        pltpu.VMEM((MOE_SUPER, TM, d), F32), pltpu.VMEM((d, MOE_HID), BF16),
                        pltpu.VMEM((d, MOE_HID), BF16), pltpu.VMEM((MOE_HID, d), BF16),
                        pltpu.SemaphoreType.DMA((2, MOE_SUPER))],
    )
    return pl.pallas_call(
        _experts_kernel,
        grid_spec=grid_spec,
        out_shape=jax.ShapeDtypeStruct((n_gtiles * TM, d), F32),
        compiler_params=_cparams(3),
        name="moe_experts",
    )(gather, sup, h2s, combs, wg, wu, wd)


def _finalize_kernel(scatter_ref, yg_ref, pos_ref, xn_ref, mod_ref, fg_ref, o_ref, ys_ref, sem, *, final):
    c = pl.program_id(0)
    per_chunk = MOE_SLOTS // MOE_ALIGN

    def chunk_copies(chunk, go):
        buf = chunk % 2
        for j in range(per_chunk):
            src = scatter_ref[chunk * per_chunk + j]

            @pl.when(src >= 0)
            def _(j=j, src=src):
                rows = pl.ds(pl.multiple_of(src * MOE_ALIGN, MOE_ALIGN), MOE_ALIGN)
                go(pltpu.make_async_copy(yg_ref.at[rows], ys_ref.at[buf, pl.ds(j * MOE_ALIGN, MOE_ALIGN)],
                                         sem.at[buf]))

    @pl.when(c == 0)
    def _():
        ys_ref[...] = jnp.zeros_like(ys_ref)
        chunk_copies(c, lambda cp: cp.start())

    @pl.when(c + 1 < pl.num_programs(0))
    def _():
        chunk_copies(c + 1, lambda cp: cp.start())

    chunk_copies(c, lambda cp: cp.wait())

    tm = xn_ref.shape[0]
    y = ys_ref[c % 2]
    hi = y.astype(BF16)
    rest = y - hi.astype(F32)
    mid = rest.astype(BF16)
    lo = (rest - mid.astype(F32)).astype(BF16)
    slot = lax.broadcasted_iota(jnp.int32, (tm, MOE_SLOTS), 1).astype(F32)
    take_t = jnp.where(_rep(pos_ref[...], MOE_SLOTS // LANES, 1) == slot, 1.0, 0.0).astype(BF16)
    y_tok = (_dot(take_t, hi) + _dot(take_t, mid)) + _dot(take_t, lo)
    xo = xn_ref[...] + mod_ref[5:6, :] * y_tok
    if final:
        xo = _rms(xo, fg_ref[...])
    o_ref[...] = xo


def _finalize(scatter, yg, pos, xn, mod_l, fg, seq, final):
    t, d = xn.shape
    tps = seq // TM
    grid_spec = pltpu.PrefetchScalarGridSpec(
        num_scalar_prefetch=1,
        grid=(t // TM,),
        in_specs=[pl.BlockSpec(memory_space=pl.ANY),
                  pl.BlockSpec((TM, LANES), lambda i, tb: (i, 0)),
                  pl.BlockSpec((TM, d), lambda i, tb: (i, 0)),
                  pl.BlockSpec((None, 6, d), lambda i, tb: (i // tps, 0, 0)),
                  pl.BlockSpec((1, d), lambda i, tb: (0, 0))],
        out_specs=pl.BlockSpec((TM, d), lambda i, tb: (i, 0)),
        scratch_shapes=[pltpu.VMEM((2, MOE_SLOTS, d), F32), pltpu.SemaphoreType.DMA((2,))],
    )
    return pl.pallas_call(
        functools.partial(_finalize_kernel, final=final),
        grid_spec=grid_spec,
        out_shape=jax.ShapeDtypeStruct((t, d), F32),
        compiler_params=_cparams(1),
        name="moe_finalize",
    )(scatter, yg, pos, xn, mod_l, fg)


def _permute_w_in(w):
    gate0 = D_LRU * 2 + D_ATT + 6 * LANES
    cv0 = gate0 + 3 * N_HEADS
    w = w.astype(BF16)
    pad = jnp.zeros((w.shape[0], IN_PAD - w.shape[1]), BF16)
    return jnp.concatenate([w[:, :gate0], w[:, cv0:], w[:, gate0:cv0], pad], axis=1)


def _router_weights(rg_w, rg_b, re_w, re_b):
    d = rg_w.shape[0]
    ne = MOE_GROUPS * MOE_EXPERTS
    w = jnp.concatenate([rg_w, jnp.transpose(re_w, (1, 0, 2)).reshape(d, ne),
                         jnp.zeros((d, LANES - MOE_GROUPS - ne), F32)], axis=1)
    b = jnp.concatenate([rg_b, re_b.reshape(ne), jnp.zeros((LANES - MOE_GROUPS - ne,), F32)])[None, :]
    w_hi = w.astype(BF16)
    w_mid = (w - w_hi.astype(F32)).astype(BF16)
    return jnp.stack([w_hi, w_mid]), b


def kernel(x, c, positions, ada_w, ada_b, norm1_g, norm2_g, w_in, lru_conv_w, lru_conv_b, lru_wa, lru_ba, lru_wx, lru_bx, lru_lambda, cmp_k_w1, cmp_k_w2, cmp_v_w1, cmp_v_w2, cmp_pos_k, cmp_pos_v, cnv_dw_w, cnv_dw_b, cnv_ln_g, cnv_ln_b, out_norm_g, w_out, moe_rg_w, moe_rg_b, moe_re_w, moe_re_b, moe_w_gate, moe_w_up, moe_w_down, final_norm_g):
    batch, seq, d = x.shape
    depth = ada_w.shape[0]
    assert d == D_MODEL and seq == N_SLC * SLC_LEN and seq % KEY_TILE == 0
    t = batch * seq
    ne = MOE_GROUPS * MOE_EXPERTS
    n_chunks = t // TM
    n_gtiles = n_chunks + -(-n_chunks * MOE_GROUPS * (MOE_ALIGN - 1) // TM) + MOE_GROUPS

    cos, sin = _rope_tables(positions)
    mod = _modulation(c, ada_w, ada_b)
    x2 = x.reshape(t, d)
    fg = final_norm_g[None, :]
    for l in range(depth):
        (xl, gl, q, kc, vc, ksa, vsa, kw, vwa, gate, cv) = _inproj(
            x2, mod[l], norm1_g[l][None, :], _permute_w_in(w_in[l]), cos, sin, seq)
        y_lru, y_cnv = _mixer(xl, gl, cv, lru_conv_w[l], lru_conv_b[l], lru_wa[l], lru_ba[l], lru_wx[l], lru_bx[l],
                              lru_lambda[l], cnv_dw_w[l], cnv_dw_b[l], cnv_ln_g[l], cnv_ln_b[l], batch, seq)
        kcmp, vcmp = _compress(kc, vc, cmp_pos_k[l], cmp_k_w1[l], cmp_k_w2[l], cmp_pos_v[l], cmp_v_w1[l], cmp_v_w2[l],
                               batch, seq)
        y_att = _attention(q, kcmp, vcmp, ksa, vsa, kw, vwa, gate, batch, seq)
        rw, rb = _router_weights(moe_rg_w[l], moe_rg_b[l], moe_re_w[l], moe_re_b[l])
        xn, h2, comb, pos, seg = _outproj(x2, y_lru, y_att, y_cnv, out_norm_g[l][None, :], w_out[l].astype(BF16),
                                          mod[l], norm2_g[l][None, :], rw, rb, seq)
        gather, scatter, sup = _moe_plan(seg, n_chunks, n_gtiles)
        yg = _experts(gather, sup, h2, comb,
                      moe_w_gate.reshape(depth * ne, d, MOE_HID), moe_w_up.reshape(depth * ne, d, MOE_HID),
                      moe_w_down.reshape(depth * ne, MOE_HID, d), n_gtiles, l * ne)
        x2 = _finalize(scatter, yg, pos, xn, mod[l], fg, seq, final=(l == depth - 1))
    return x2.reshape(batch, seq, d)
```

```python
import functools

import jax
import jax.numpy as jnp
from jax import lax
from jax.experimental import pallas as pl
from jax.experimental.pallas import tpu as pltpu

F32 = jnp.float32
BF16 = jnp.bfloat16

D_MODEL = 1024
D_LRU = 256
D_ATT = 512
D_CONV = 256
LRU_BLOCKS = 4
LRU_CONV_W = 4
LRU_C = 8.0
HEAD_DIM = 64
N_HEADS = 8
N_KV = 2
GQA = 4
ROPE_THETA = 10000.0
CMP_LEN = 32
CMP_STRIDE = 16
SLC_LEN = 64
SLC_TOPN = 16
WINDOW = 512
CONV_K = 31
MOE_GROUPS = 4
MOE_EXPERTS = 4
MOE_HID = 512
EPS = 1e-6
NEG = -1e30
FORCE = 1e9
LOG2_E = 1.4426950408889634

LANES = 128
Q_BLOCK = 128
KEY_TILE = 512
N_SLC = 128
TM = 512
MOE_ALIGN_LOG2 = 4
MOE_ALIGN = 1 << MOE_ALIGN_LOG2
MOE_SLOTS = 640
TC = 512
SCAN_ROWS = 64
LRU_TAIL = 8
CNV_TAIL = 32
IN_PAD = 2432
VMEM_LIMIT = 56 * 1024 * 1024


def _cparams(n_axes, vmem=VMEM_LIMIT):
    return pltpu.CompilerParams(dimension_semantics=("arbitrary",) * n_axes, vmem_limit_bytes=vmem)


def _dot(a, b):
    return jnp.dot(a, b, preferred_element_type=F32)


def _dot_nt(a, b):
    return lax.dot_general(a, b, (((1,), (1,)), ((), ())), preferred_element_type=F32)


def _rep(v, n, axis):
    return jnp.concatenate([v] * n, axis=axis)


def _rms(v, g):
    return v * lax.rsqrt(jnp.mean(v * v, axis=-1, keepdims=True) + EPS) * g


def _rope_kernel(pos_ref, inv_ref, sign_ref, cos_ref, sin_ref):
    ang = pos_ref[...].astype(F32) * inv_ref[...]
    cos_ref[...] = jnp.cos(ang)
    sin_ref[...] = jnp.sin(ang) * sign_ref[...]


def _rope_tables(positions):
    t = positions.size
    inv = ROPE_THETA ** (-jnp.arange(0, HEAD_DIM, 2, dtype=F32) / HEAD_DIM)
    inv128 = jnp.tile(inv, 4)[None, :]
    sign128 = jnp.tile(jnp.concatenate([-jnp.ones((32,), F32), jnp.ones((32,), F32)]), 2)[None, :]
    tr = 1024
    return pl.pallas_call(
        _rope_kernel,
        grid=(t // tr,),
        in_specs=[pl.BlockSpec((tr, 1), lambda i: (i, 0)),
                  pl.BlockSpec((1, LANES), lambda i: (0, 0)),
                  pl.BlockSpec((1, LANES), lambda i: (0, 0))],
        out_specs=[pl.BlockSpec((tr, LANES), lambda i: (i, 0))] * 2,
        out_shape=[jax.ShapeDtypeStruct((t, LANES), F32)] * 2,
        compiler_params=_cparams(1),
        name="rope_tables",
    )(positions.reshape(t, 1), inv128, sign128)


def _mod_kernel(c_ref, w_ref, b_ref, o_ref):
    sc = jax.nn.silu(c_ref[...])
    o_ref[...] = _dot(sc.astype(BF16), w_ref[...].astype(BF16)) + b_ref[...]


def _modulation(c, ada_w, ada_b):
    nl, d, n6 = ada_w.shape
    b = c.shape[0]
    rows = 16
    cp = jnp.zeros((rows, d), F32).at[:b].set(c)
    tn = 1536
    out = pl.pallas_call(
        _mod_kernel,
        grid=(nl, n6 // tn),
        in_specs=[pl.BlockSpec((rows, d), lambda l, j: (0, 0)),
                  pl.BlockSpec((None, d, tn), lambda l, j: (l, 0, j)),
                  pl.BlockSpec((None, 1, tn), lambda l, j: (l, 0, j))],
        out_specs=pl.BlockSpec((None, rows, tn), lambda l, j: (l, 0, j)),
        out_shape=jax.ShapeDtypeStruct((nl, rows, n6), F32),
        compiler_params=_cparams(2),
        name="adaln_mod",
    )(cp, ada_w, ada_b.reshape(nl, 1, n6))
    return out[:, :b].reshape(nl, b, 6, d)


def _inproj_kernel(x_ref, mod_ref, g_ref, w_ref, cos_ref, sin_ref,
                   xl_ref, gl_ref, q_ref, kc_ref, vc_ref, ksa_ref, vsa_ref, kw_ref, vwa_ref, gate_ref, cv_ref,
                   *, tiles_per_seq):
    tm = x_ref.shape[0]
    x = x_ref[...]
    h = _rms(x, g_ref[...]) * (1.0 + mod_ref[1:2, :]) + mod_ref[0:1, :]
    p = _dot(h.astype(BF16), w_ref[...])

    cos = cos_ref[...]
    sin = sin_ref[...]
    lane = lax.broadcasted_iota(jnp.int32, (tm, LANES), 1)
    first_half = (lane & 63) < 32

    def rope(v):
        rot = jnp.where(first_half, pltpu.roll(v, 96, 1), pltpu.roll(v, 32, 1))
        return v * cos + rot * sin

    xl_ref[...] = p[:, 0:256]
    gl_ref[...] = p[:, 256:512]
    scale = HEAD_DIM ** -0.5 * LOG2_E
    low = lane < 64
    for m in range(4):
        slab = rope(p[:, 512 + 128 * m:640 + 128 * m]) * scale
        swapped = pltpu.roll(slab, 64, 1)
        for hh in range(2):
            head = 2 * m + hh
            kv = head // GQA
            src = slab if hh == kv else swapped
            keep = low if kv == 0 else jnp.logical_not(low)
            q_ref[:, head * LANES:(head + 1) * LANES] = jnp.where(keep, src, 0.0).astype(BF16)
    kc_ref[...] = rope(p[:, 1024:1152])
    vc_ref[...] = p[:, 1152:1280]
    row = lax.broadcasted_iota(jnp.int32, (tm, LANES), 0)
    s_base = (pl.program_id(0) % tiles_per_seq) * tm
    onehot = jnp.where(lane == ((s_base + row) >> 6), 1.0, 0.0).astype(BF16)
    ones = jnp.ones((tm, LANES), BF16)
    ksa_ref[:, 0:LANES] = rope(p[:, 1280:1408]).astype(BF16)
    ksa_ref[:, LANES:2 * LANES] = onehot
    vsa_ref[:, 0:LANES] = p[:, 1408:1536].astype(BF16)
    vsa_ref[:, LANES:2 * LANES] = ones
    kw_ref[...] = rope(p[:, 1536:1664]).astype(BF16)
    vwa_ref[:, 0:LANES] = p[:, 1664:1792].astype(BF16)
    vwa_ref[:, LANES:2 * LANES] = ones
    cv_ref[...] = p[:, 1792:2304]
    gate_ref[...] = jax.nn.sigmoid(p[:, 2304:2432])


def _inproj(x2, mod_l, g, w_p, cos, sin, seq):
    t, d = x2.shape
    tps = seq // TM
    row = lambda w: pl.BlockSpec((TM, w), lambda i: (i, 0))
    outs = [(256, F32), (256, F32), (N_HEADS * LANES, BF16), (LANES, F32), (LANES, F32), (2 * LANES, BF16),
            (2 * LANES, BF16), (LANES, BF16), (2 * LANES, BF16), (LANES, F32), (2 * D_CONV, F32)]
    return pl.pallas_call(
        functools.partial(_inproj_kernel, tiles_per_seq=tps),
        grid=(t // TM,),
        in_specs=[row(d),
                  pl.BlockSpec((None, 6, d), lambda i: (i // tps, 0, 0)),
                  pl.BlockSpec((1, d), lambda i: (0, 0)),
                  pl.BlockSpec((d, IN_PAD), lambda i: (0, 0)),
                  row(LANES), row(LANES)],
        out_specs=[row(w) for w, _ in outs],
        out_shape=[jax.ShapeDtypeStruct((t, w), dt) for w, dt in outs],
        compiler_params=_cparams(1),
        name="in_proj",
    )(x2, mod_l, g, w_p, cos, sin)


def _shift_rows(v, s, fill, row):
    return jnp.where(row < s, fill, pltpu.roll(v, s, 0))


def _causal_taps(ext, w_ref, bias, width, tail, tc):
    acc = bias
    for b in range(min(8, width)):
        shifted = pltpu.roll(ext, b, 0) if b else ext
        for k in range(width):
            back = width - 1 - k
            if back % 8 == b:
                start = tail - (back - b)
                acc = acc + w_ref[k:k + 1, :] * shifted[start:start + tc, :]
    return acc


def _mixer_kernel(xl_ref, gl_ref, cv_ref, lcw_ref, lcb_ref, wa_ref, ba_ref, wx_ref, bx_ref, lam_ref,
                  cw_ref, cb_ref, lng_ref, lnb_ref, ylru_ref, ycnv_ref,
                  extl, extc, abuf, ubuf, hcar):
    tc = xl_ref.shape[0]

    @pl.when(pl.program_id(1) == 0)
    def _():
        extl[0:LRU_TAIL, :] = jnp.zeros((LRU_TAIL, D_LRU), F32)
        extc[0:CNV_TAIL, :] = jnp.zeros((CNV_TAIL, D_CONV), F32)
        hcar[...] = jnp.zeros_like(hcar)

    xl = xl_ref[...]
    extl[LRU_TAIL:LRU_TAIL + tc, :] = xl
    xc = _causal_taps(extl[...], lcw_ref, lcb_ref[...], LRU_CONV_W, LRU_TAIL, tc)
    extl[0:LRU_TAIL, :] = xl[tc - LRU_TAIL:tc, :]
    xcb = xc.astype(BF16)
    r = jax.nn.sigmoid(_dot(xcb, wa_ref[...]) + ba_ref[...])
    gi = jax.nn.sigmoid(_dot(xcb, wx_ref[...]) + bx_ref[...])
    log_a = LRU_C * r * jax.nn.log_sigmoid(lam_ref[...])
    th = jnp.tanh(log_a)
    one_minus_a2 = -2.0 * th / (1.0 - th)
    abuf[...] = jnp.exp(log_a)
    ubuf[...] = jnp.sqrt(one_minus_a2) * (gi * xc)

    row = lax.broadcasted_iota(jnp.int32, (SCAN_ROWS, D_LRU), 0)

    def scan_chunk(c, h_prev):
        off = pl.multiple_of(c * SCAN_ROWS, SCAN_ROWS)
        a = abuf[pl.ds(off, SCAN_ROWS), :]
        b = ubuf[pl.ds(off, SCAN_ROWS), :]
        s = 1
        while s < SCAN_ROWS:
            b = b + a * _shift_rows(b, s, 0.0, row)
            a = a * _shift_rows(a, s, 1.0, row)
            s *= 2
        h = b + a * h_prev
        ylru_ref[pl.ds(off, SCAN_ROWS), :] = h * jax.nn.gelu(gl_ref[pl.ds(off, SCAN_ROWS), :])
        return h[SCAN_ROWS - 1:SCAN_ROWS, :]

    hcar[0:1, :] = lax.fori_loop(0, tc // SCAN_ROWS, scan_chunk, hcar[0:1, :])

    u = cv_ref[...]
    v = u[:, 0:D_CONV] * jax.nn.sigmoid(u[:, D_CONV:2 * D_CONV])
    extc[CNV_TAIL:CNV_TAIL + tc, :] = v
    acc = _causal_taps(extc[...], cw_ref, cb_ref[...], CONV_K, CNV_TAIL, tc)
    extc[0:CNV_TAIL, :] = v[tc - CNV_TAIL:tc, :]
    mu = jnp.mean(acc, axis=-1, keepdims=True)
    var = jnp.mean(jnp.square(acc - mu), axis=-1, keepdims=True)
    y = (acc - mu) * lax.rsqrt(var + EPS) * lng_ref[...] + lnb_ref[...]
    ycnv_ref[...] = jax.nn.silu(y)


def _block_diag(w):
    n, rows, cols = w.shape
    eye = jnp.eye(n, dtype=bool)
    return jnp.where(eye[:, None, :, None], w[:, :, None, :], 0.0).reshape(n * rows, n * cols)


def _mixer(xl, gl, cv, lcw, lcb, wa, ba, wx, bx, lam, cw, cb, lng, lnb, batch, seq):
    seqrow = lambda w: pl.BlockSpec((None, TC, w), lambda b, t: (b, t, 0))
    full = lambda a: pl.BlockSpec(a.shape, lambda b, t: (0,) * a.ndim)
    small = [lcw, lcb[None, :], _block_diag(wa).astype(BF16), ba[None, :], _block_diag(wx).astype(BF16), bx[None, :],
             lam[None, :], jnp.zeros((32, D_CONV), F32).at[:CONV_K].set(cw), cb[None, :], lng[None, :], lnb[None, :]]
    return pl.pallas_call(
        _mixer_kernel,
        grid=(batch, seq // TC),
        in_specs=[seqrow(D_LRU), seqrow(D_LRU), seqrow(2 * D_CONV)] + [full(a) for a in small],
        out_specs=[seqrow(D_LRU), seqrow(D_CONV)],
        out_shape=[jax.ShapeDtypeStruct((batch, seq, D_LRU), F32), jax.ShapeDtypeStruct((batch, seq, D_CONV), F32)],
        scratch_shapes=[pltpu.VMEM((TC + LRU_TAIL, D_LRU), F32), pltpu.VMEM((TC + CNV_TAIL, D_CONV), F32),
                        pltpu.VMEM((TC, D_LRU), F32), pltpu.VMEM((TC, D_LRU), F32), pltpu.VMEM((8, D_LRU), F32)],
        compiler_params=_cparams(2),
        name="mixer_stream",
    )(xl.reshape(batch, seq, D_LRU), gl.reshape(batch, seq, D_LRU), cv.reshape(batch, seq, 2 * D_CONV), *small)


def _cmp_kernel(kc_ref, vc_ref, posk_ref, posv_ref, wk1_ref, wk2_ref, wv1_ref, wv2_ref, ko_ref, vo_ref, nat_ref):
    nh = ko_ref.shape[0]
    quarter = nh // 4

    def one(t_ref, pos_ref, w1_ref, w2_ref, o_ref):
        first = jnp.zeros((nh, w1_ref.shape[2]), F32)
        second = jnp.zeros((nh, w1_ref.shape[2]), F32)
        for l in range(CMP_STRIDE):
            x = t_ref[pl.ds(l, nh, stride=CMP_STRIDE), :]
            first = first + _dot((x + pos_ref[l:l + 1, :]).astype(BF16), w1_ref[l])
            second = second + _dot((x + pos_ref[CMP_STRIDE + l:CMP_STRIDE + l + 1, :]).astype(BF16),
                                   w1_ref[CMP_STRIDE + l])
        hid = jax.nn.gelu(first + pltpu.roll(second, nh - 1, 0))
        nat_ref[...] = _dot(hid.astype(BF16), w2_ref[...])
        for r in range(4):
            o_ref[r * quarter:(r + 1) * quarter, :] = nat_ref[pl.ds(r, quarter, stride=4), :].astype(o_ref.dtype)

    one(kc_ref, posk_ref, wk1_ref, wk2_ref, ko_ref)
    one(vc_ref, posv_ref, wv1_ref, wv2_ref, vo_ref)


def _cmp_weights(pos, w1, w2):
    hid = w1.shape[1]
    eye = jnp.eye(N_KV, dtype=bool)
    w1 = w1.astype(BF16).reshape(CMP_LEN, HEAD_DIM, hid)
    w1x = jnp.where(eye[None, :, None, :, None], w1[:, None, :, None, :], 0.0)
    w1x = w1x.reshape(CMP_LEN, N_KV * HEAD_DIM, N_KV * hid)
    w2x = _block_diag(jnp.broadcast_to(w2.astype(BF16), (N_KV,) + w2.shape))
    return jnp.tile(pos, (1, N_KV)), w1x, w2x


def _compress(kc, vc, pos_k, wk1, wk2, pos_v, wv1, wv2, batch, seq):
    nh = seq // CMP_STRIDE
    wide = N_KV * HEAD_DIM
    pk, wk1x, wk2x = _cmp_weights(pos_k, wk1, wk2)
    pv, wv1x, wv2x = _cmp_weights(pos_v, wv1, wv2)
    full = lambda a: pl.BlockSpec(a.shape, lambda b: (0,) * a.ndim)
    seqs = pl.BlockSpec((None, seq, wide), lambda b: (b, 0, 0))
    outs = pl.BlockSpec((None, nh, wide), lambda b: (b, 0, 0))
    return pl.pallas_call(
        _cmp_kernel,
        grid=(batch,),
        in_specs=[seqs, seqs, full(pk), full(pv), full(wk1x), full(wk2x), full(wv1x), full(wv2x)],
        out_specs=[outs, outs],
        out_shape=[jax.ShapeDtypeStruct((batch, nh, wide), BF16)] * 2,
        scratch_shapes=[pltpu.VMEM((nh, wide), F32)],
        compiler_params=_cparams(1),
        name="nsa_compress",
    )(kc.reshape(batch, seq, wide), vc.reshape(batch, seq, wide), pk, pv, wk1x, wk2x, wv1x, wv2x)


def _attn_kernel(q_ref, kcmp_ref, vcmp_ref, ksa_ref, vsa_ref, kw_ref, vwa_ref, gate_ref, o_ref,
                 qa_ref, part_ref, acc_ref, m_ref, accw_ref, mw_ref, sa_ref, sb_ref):
    rows = GQA * Q_BLOCK
    reps = KEY_TILE // LANES
    qb = pl.program_id(1)
    s0 = qb * Q_BLOCK
    kd = s0 // KEY_TILE
    off_d = pl.multiple_of(kd * KEY_TILE, KEY_TILE)
    off_p = pl.multiple_of(jnp.maximum(kd - 1, 0) * KEY_TILE, KEY_TILE)
    d0 = s0 - off_d

    lane = lax.broadcasted_iota(jnp.int32, (Q_BLOCK, LANES), 1)
    sub = lax.broadcasted_iota(jnp.int32, (Q_BLOCK, LANES), 0)
    r_q = lax.broadcasted_iota(jnp.int32, (Q_BLOCK, KEY_TILE), 0)
    c_k = lax.broadcasted_iota(jnp.int32, (Q_BLOCK, KEY_TILE), 1)
    rel = c_k - r_q
    gate = gate_ref[...]

    def tile_rows(plane):
        return _rep(plane, GQA, 0)

    def flash_step(acc, m, kv, s, v_tile):
        m_old = m[kv]
        m_new = jnp.maximum(m_old, jnp.broadcast_to(jnp.max(s, axis=-1, keepdims=True), m_old.shape))
        p = jnp.exp2(s - _rep(m_new, reps, 1))
        acc[kv] = acc[kv] * _rep(jnp.exp2(m_old - m_new), 2, 1) + _dot(p.astype(BF16), v_tile)
        m[kv] = m_new

    def flash_result(acc, kv):
        a = acc[kv]
        return a[:, 0:LANES] / a[:, LANES:2 * LANES]

    q128 = [jnp.concatenate([q_ref[:, (kv * GQA + g) * LANES:(kv * GQA + g + 1) * LANES] for g in range(GQA)], axis=0)
            for kv in range(N_KV)]

    kcmp = kcmp_ref[...]
    vcmp = vcmp_ref[...]
    cmp_end = SLC_LEN * (c_k & (N_SLC - 1)) + CMP_STRIDE * (c_k >> 7) + (CMP_LEN - 1)
    cmp_bias = tile_rows(jnp.where(cmp_end - r_q <= s0, 0.0, NEG))
    has_cmp = tile_rows(jnp.where(s0 + sub >= CMP_LEN - 1, 1.0, 0.0))
    o_cmp, imp_t = [], []
    for kv in range(N_KV):
        s_c = _dot_nt(q128[kv], kcmp) + cmp_bias
        mb = jnp.broadcast_to(jnp.max(s_c, axis=-1, keepdims=True), (rows, LANES))
        e = jnp.exp2(s_c - _rep(mb, reps, 1))
        lb = jnp.broadcast_to(jnp.sum(e, axis=-1, keepdims=True), (rows, LANES))
        inv = has_cmp / lb
        o_cmp.append(_dot(e.astype(BF16), vcmp) * inv)
        e4 = (e[:, 0:N_SLC] + e[:, N_SLC:2 * N_SLC] + e[:, 2 * N_SLC:3 * N_SLC] + e[:, 3 * N_SLC:4 * N_SLC]) * inv
        imp = e4[0:Q_BLOCK] + e4[Q_BLOCK:2 * Q_BLOCK] + e4[2 * Q_BLOCK:3 * Q_BLOCK] + e4[3 * Q_BLOCK:4 * Q_BLOCK]
        imp_t.append(imp.T)

    accw_ref[...] = jnp.zeros_like(accw_ref)
    mw_ref[...] = jnp.full_like(mw_ref, NEG)
    prev_bias = tile_rows(jnp.where(rel > jnp.where(kd > 0, d0, KEY_TILE), 0.0, NEG))
    diag_bias = tile_rows(jnp.where(rel <= d0, 0.0, NEG))
    kw_p, vw_p = kw_ref[pl.ds(off_p, KEY_TILE), :], vwa_ref[pl.ds(off_p, KEY_TILE), :]
    kw_d, vw_d = kw_ref[pl.ds(off_d, KEY_TILE), :], vwa_ref[pl.ds(off_d, KEY_TILE), :]
    for kv in range(N_KV):
        flash_step(accw_ref, mw_ref, kv, _dot_nt(q128[kv], kw_p) + prev_bias, vw_p)
    for kv in range(N_KV):
        flash_step(accw_ref, mw_ref, kv, _dot_nt(q128[kv], kw_d) + diag_bias, vw_d)
    for kv in range(N_KV):
        o_win = flash_result(accw_ref, kv)
        for g in range(GQA):
            col = 3 * (kv * GQA + g)
            rs = slice(g * Q_BLOCK, (g + 1) * Q_BLOCK)
            part_ref[kv, rs, :] = gate[:, col:col + 1] * o_cmp[kv][rs] + gate[:, col + 2:col + 3] * o_win[rs]

    blk = lax.broadcasted_iota(jnp.int32, (N_SLC, Q_BLOCK), 0)
    t_q = s0 + lax.broadcasted_iota(jnp.int32, (N_SLC, Q_BLOCK), 1)
    forced = (blk == 0) | (blk == (t_q >> 6))
    val = [jnp.where(forced, -jnp.inf, jnp.where(blk * SLC_LEN <= t_q, imp_t[kv], -1.0)) for kv in range(N_KV)]
    blk_f = blk.astype(F32)
    for _ in range(SLC_TOPN - 2):
        for kv in range(N_KV):
            mx = jnp.max(val[kv], axis=0, keepdims=True)
            first = jnp.min(jnp.where(val[kv] == mx, blk_f, float(N_SLC)), axis=0, keepdims=True)
            val[kv] = jnp.where(blk_f == first, -jnp.inf, val[kv])
    for kv in range(N_KV):
        bias = jnp.where(val[kv].T == -jnp.inf, 0.0, NEG).astype(BF16)
        qa_ref[kv] = jnp.concatenate([q128[kv], jnp.concatenate([bias] * GQA, axis=0)], axis=1)

    acc_ref[...] = jnp.zeros_like(acc_ref)
    m_ref[...] = jnp.full_like(m_ref, NEG)

    def scores_into(dst, kt):
        k_t = ksa_ref[pl.ds(pl.multiple_of(kt * KEY_TILE, KEY_TILE), KEY_TILE), :]
        for kv in range(N_KV):
            dst[kv] = _dot_nt(qa_ref[kv], k_t)

    def consume(src, kt, bias):
        v_t = vsa_ref[pl.ds(pl.multiple_of(kt * KEY_TILE, KEY_TILE), KEY_TILE), :]
        for kv in range(N_KV):
            flash_step(acc_ref, m_ref, kv, src[kv] if bias is None else src[kv] + bias, v_t)

    scores_into(sa_ref, 0)

    def slc_pairs(first, count):
        for i in range(count):
            a = first + 2 * i
            scores_into(sb_ref, a + 1)
            consume(sa_ref, a, None)
            scores_into(sa_ref, a + 2)
            consume(sb_ref, a + 1, None)

    def slc_quad(j, carry):
        slc_pairs(4 * j, 2)
        return carry

    def slc_pair(j, carry):
        slc_pairs(4 * n_quads + 2 * j, 1)
        return carry

    n_quads = kd // 4
    lax.fori_loop(0, n_quads, slc_quad, 0)
    lax.fori_loop(0, (kd // 2) & 1, slc_pair, 0)
    causal_bias = tile_rows(jnp.where(rel <= d0, 0.0, NEG))
    kd_odd = (kd & 1) == 1

    @pl.when(kd_odd)
    def _():
        scores_into(sb_ref, kd)
        consume(sa_ref, kd - 1, None)
        consume(sb_ref, kd, causal_bias)

    @pl.when(jnp.logical_not(kd_odd))
    def _():
        consume(sa_ref, kd, causal_bias)

    for kv in range(N_KV):
        o_slc = flash_result(acc_ref, kv)
        pieces = []
        for g in range(GQA):
            col = 3 * (kv * GQA + g)
            rs = slice(g * Q_BLOCK, (g + 1) * Q_BLOCK)
            pieces.append(part_ref[kv, rs, :] + gate[:, col + 1:col + 2] * o_slc[rs])
        for mm in range(2):
            even, odd = pieces[2 * mm], pieces[2 * mm + 1]
            if kv == 0:
                odd = pltpu.roll(odd, 64, 1)
            else:
                even = pltpu.roll(even, 64, 1)
            slab = kv * 2 + mm
            o_ref[:, slab * LANES:(slab + 1) * LANES] = jnp.where(lane < 64, even, odd)


def _attention(q, kcmp, vcmp, ksa, vsa, kw, vwa, gate, batch, seq):
    nh = kcmp.shape[1]
    assert nh == KEY_TILE
    rows = GQA * Q_BLOCK
    res = lambda w: pl.BlockSpec((None, seq, w), lambda b, i: (b, 0, 0))
    blk = lambda w: pl.BlockSpec((None, Q_BLOCK, w), lambda b, i: (b, i, 0))
    r3 = lambda a, w: a.reshape(batch, seq, w)
    return pl.pallas_call(
        _attn_kernel,
        grid=(batch, seq // Q_BLOCK),
        in_specs=[blk(N_HEADS * LANES),
                  pl.BlockSpec((None, nh, LANES), lambda b, i: (b, 0, 0)),
                  pl.BlockSpec((None, nh, LANES), lambda b, i: (b, 0, 0)),
                  res(2 * LANES), res(2 * LANES), res(LANES), res(2 * LANES), blk(LANES)],
        out_specs=blk(D_ATT),
        out_shape=jax.ShapeDtypeStruct((batch, seq, D_ATT), F32),
        scratch_shapes=[pltpu.VMEM((N_KV, rows, 2 * LANES), BF16), pltpu.VMEM((N_KV, rows, LANES), F32),
                        pltpu.VMEM((N_KV, rows, 2 * LANES), F32), pltpu.VMEM((N_KV, rows, LANES), F32),
                        pltpu.VMEM((N_KV, rows, 2 * LANES), F32), pltpu.VMEM((N_KV, rows, LANES), F32),
                        pltpu.VMEM((N_KV, rows, KEY_TILE), F32), pltpu.VMEM((N_KV, rows, KEY_TILE), F32)],
        compiler_params=_cparams(2),
        name="nsa_attention",
    )(r3(q, N_HEADS * LANES), kcmp, vcmp, r3(ksa, 2 * LANES), r3(vsa, 2 * LANES), r3(kw, LANES), r3(vwa, 2 * LANES),
      r3(gate, LANES))


def _outproj_kernel(x_ref, yl_ref, ya_ref, yc_ref, gn_ref, w_ref, mod_ref, n2_ref, rw_ref, rb_ref,
                    xn_ref, h2_ref, comb_ref, pos_ref, seg_ref):
    tm = x_ref.shape[0]
    a, b = D_LRU, D_LRU + D_ATT
    yl = _rms(yl_ref[...], gn_ref[:, 0:a]).astype(BF16)
    ya = _rms(ya_ref[...], gn_ref[:, a:b]).astype(BF16)
    yc = _rms(yc_ref[...], gn_ref[:, b:D_MODEL]).astype(BF16)
    y = _dot(yl, w_ref[0:a, :]) + _dot(ya, w_ref[a:b, :]) + _dot(yc, w_ref[b:D_MODEL, :])
    xn = x_ref[...] + mod_ref[2:3, :] * y
    xn_ref[...] = xn
    h2 = _rms(xn, n2_ref[...]) * (1.0 + mod_ref[4:5, :]) + mod_ref[3:4, :]

    h_hi = h2.astype(BF16)
    h_mid = (h2 - h_hi.astype(F32)).astype(BF16)
    logit = _dot(h_hi, rw_ref[0]) + (_dot(h_hi, rw_ref[1]) + _dot(h_mid, rw_ref[0])) + rb_ref[...]
    lane = lax.broadcasted_iota(jnp.int32, (tm, LANES), 1)
    ninf = -jnp.inf
    is_g = lane < MOE_GROUPS
    is_e = (lane >= MOE_GROUPS) & (lane < MOE_GROUPS + MOE_GROUPS * MOE_EXPERTS)
    lg_max = jnp.max(jnp.where(is_g, logit, ninf), axis=-1, keepdims=True)
    g_star = jnp.min(jnp.where(is_g & (logit == lg_max), lane, LANES), axis=-1, keepdims=True)
    pg_star = 1.0 / jnp.sum(jnp.where(is_g, jnp.exp(logit - lg_max), 0.0), axis=-1, keepdims=True)
    in_grp = is_e & (((lane - MOE_GROUPS) >> 2) == g_star)
    v1 = jnp.max(jnp.where(in_grp, logit, ninf), axis=-1, keepdims=True)
    i1 = jnp.min(jnp.where(in_grp & (logit == v1), lane, LANES), axis=-1, keepdims=True)
    rest = in_grp & (lane != i1)
    v2 = jnp.max(jnp.where(rest, logit, ninf), axis=-1, keepdims=True)
    i2 = jnp.min(jnp.where(rest & (logit == v2), lane, LANES), axis=-1, keepdims=True)
    d = jnp.exp(v2 - v1)
    pe1 = 1.0 / (1.0 + d)
    pe2 = d / (1.0 + d)
    comb = jnp.where(lane == i1, pe1, jnp.where(lane == i2, pe2, 0.0)) * pg_star

    onehot = jnp.where(lane == g_star, 1.0, 0.0)
    r_i = lax.broadcasted_iota(jnp.int32, (tm, tm), 0)
    c_i = lax.broadcasted_iota(jnp.int32, (tm, tm), 1)
    earlier = jnp.where(c_i < r_i, 1.0, 0.0).astype(BF16)
    rank = jnp.sum(onehot * _dot(earlier, onehot.astype(BF16)), axis=-1, keepdims=True)
    n_g = jnp.sum(onehot, axis=0, keepdims=True)
    n_pad = (((n_g.astype(jnp.int32) + (MOE_ALIGN - 1)) >> MOE_ALIGN_LOG2) << MOE_ALIGN_LOG2).astype(F32)
    lane1 = lax.broadcasted_iota(jnp.int32, (1, LANES), 1)
    start = jnp.zeros((1, LANES), F32)
    below = jnp.zeros((1, 1), F32)
    for g in range(1, MOE_GROUPS):
        below = below + jnp.sum(jnp.where(lane1 == g - 1, n_pad, 0.0), axis=-1, keepdims=True)
        start = start + jnp.where(lane1 == g, below, 0.0)
    pos = jnp.sum(onehot * start, axis=-1, keepdims=True) + rank
    pos_rep = jnp.broadcast_to(pos, (tm, LANES))
    slot = lax.broadcasted_iota(jnp.int32, (tm, MOE_SLOTS), 1).astype(F32)
    take_t = jnp.where(_rep(pos_rep, MOE_SLOTS // LANES, 1) == slot, 1.0, 0.0)
    take = take_t.T
    take = take.astype(BF16)
    h2_ref[...] = _dot(take, h_hi).astype(BF16)
    c_hi = comb.astype(BF16)
    c_rest = comb - c_hi.astype(F32)
    c_mid = c_rest.astype(BF16)
    c_lo = (c_rest - c_mid.astype(F32)).astype(BF16)
    comb_ref[...] = (_dot(take, c_hi) + _dot(take, c_mid)) + _dot(take, c_lo)
    pos_ref[...] = pos_rep
    seg_ref[...] = jnp.zeros(seg_ref.shape, jnp.int32)
    seg_ref[0:1, :] = n_pad.astype(jnp.int32)
    seg_ref[1:2, :] = start.astype(jnp.int32)


def _outproj(x2, yl, ya, yc, gn, w_out, mod_l, n2g, rw, rb, seq):
    t, d = x2.shape
    tps = seq // TM
    row = lambda w: pl.BlockSpec((TM, w), lambda i: (i, 0))
    full = lambda a: pl.BlockSpec(a.shape, lambda i: (0,) * a.ndim)
    return pl.pallas_call(
        _outproj_kernel,
        grid=(t // TM,),
        in_specs=[row(d), row(D_LRU), row(D_ATT), row(D_CONV), full(gn), full(w_out),
                  pl.BlockSpec((None, 6, d), lambda i: (i // tps, 0, 0)), full(n2g), full(rw), full(rb)],
        out_specs=[row(d), pl.BlockSpec((MOE_SLOTS, d), lambda i: (i, 0)), pl.BlockSpec((MOE_SLOTS, LANES), lambda i: (i, 0)),
                   row(LANES), pl.BlockSpec((None, 8, LANES), lambda i: (i, 0, 0))],
        out_shape=[jax.ShapeDtypeStruct((t, d), F32), jax.ShapeDtypeStruct((t // TM * MOE_SLOTS, d), BF16),
                   jax.ShapeDtypeStruct((t // TM * MOE_SLOTS, LANES), F32), jax.ShapeDtypeStruct((t, LANES), F32),
                   jax.ShapeDtypeStruct((t // TM, 8, LANES), jnp.int32)],
        compiler_params=_cparams(1),
        name="out_proj_router",
    )(x2, yl.reshape(t, D_LRU), ya.reshape(t, D_ATT), yc.reshape(t, D_CONV), gn, w_out, mod_l, n2g, rw, rb)


def _moe_plan(seg, n_chunks, n_gtiles):
    rows = seg[:, 0, :MOE_GROUPS]
    start = seg[:, 1, :MOE_GROUPS]
    ci = jnp.arange(n_chunks, dtype=jnp.int32)
    gi = jnp.arange(MOE_GROUPS, dtype=jnp.int32)
    cum = jnp.sum(jnp.where((ci[None, :] <= ci[:, None])[:, :, None], rows[None, :, :], 0), axis=1)
    before = cum - rows
    total = jnp.sum(rows, axis=0)
    tiles_g = (total + (TM - 1)) // TM
    ends = jnp.sum(jnp.where(gi[None, :] <= gi[:, None], tiles_g[None, :], 0), axis=1)
    base = (ends - tiles_g) * TM
    group_at = lambda tile: jnp.minimum(jnp.sum((tile[:, None] >= ends[None, :]).astype(jnp.int32), axis=1),
                                        MOE_GROUPS - 1)
    tiles = jnp.concatenate([group_at(jnp.arange(n_gtiles, dtype=jnp.int32)), ends[-1:]]).astype(jnp.int32)

    row = jnp.arange(n_gtiles * (TM // MOE_ALIGN), dtype=jnp.int32) * MOE_ALIGN
    is_g = group_at(row // TM)[:, None] == gi[None, :]
    of_group = lambda v: jnp.sum(jnp.where(is_g, v[None, :], 0), axis=1)
    p = row - of_group(base)
    cum_g = jnp.sum(jnp.where(is_g[:, None, :], cum[None, :, :], 0), axis=2)
    chunk = jnp.minimum(jnp.sum((p[:, None] >= cum_g).astype(jnp.int32), axis=1), n_chunks - 1)
    is_cg = (chunk[:, None] == ci[None, :])[:, :, None] & is_g[:, None, :]
    of_segment = lambda a: jnp.sum(jnp.where(is_cg, a[None, :, :], 0), axis=(1, 2))
    src = chunk * MOE_SLOTS + of_segment(start - before) + p
    zero_row = TM + MOE_GROUPS * MOE_ALIGN
    gather = jnp.where(p < of_group(total), src, zero_row) // MOE_ALIGN

    slot = jnp.arange(MOE_SLOTS // MOE_ALIGN, dtype=jnp.int32) * MOE_ALIGN
    inside = (slot[None, :, None] >= start[:, None, :]) & (slot[None, :, None] < (start + rows)[:, None, :])
    dst = jnp.sum(jnp.where(inside, (base[None, :] + before - start)[:, None, :], 0), axis=-1) + slot[None, :]
    scatter = jnp.where(jnp.any(inside, axis=-1), dst // MOE_ALIGN, -1)
    return gather.astype(jnp.int32), scatter.reshape(-1).astype(jnp.int32), tiles


def _experts_kernel(gather_ref, tiles_ref, h2s_ref, combs_ref, wg_ref, wu_ref, wd_ref, y_ref,
                    hbuf, cbuf, acc_ref, sem):
    e = pl.program_id(1)
    s = pl.program_id(2)
    tile = 2 * pl.program_id(0) + s
    n_used = tiles_ref[2 * pl.num_programs(0)]
    real = tile < n_used
    slots = hbuf.shape[0]
    per_tile = TM // MOE_ALIGN

    def tile_copies(t, go):
        slot = t % slots
        for g in range(per_tile):
            src = pl.multiple_of(gather_ref[t * per_tile + g] * MOE_ALIGN, MOE_ALIGN)
            dst = pl.ds(g * MOE_ALIGN, MOE_ALIGN)
            go(pltpu.make_async_copy(h2s_ref.at[pl.ds(src, MOE_ALIGN)], hbuf.at[slot, dst], sem.at[0, slot]))
            go(pltpu.make_async_copy(combs_ref.at[pl.ds(src, MOE_ALIGN)], cbuf.at[slot, dst], sem.at[1, slot]))

    @pl.when(e == 0)
    def _():
        acc_ref[s] = jnp.zeros(acc_ref.shape[1:], F32)

        for t in range(2):
            @pl.when((tile == 0) & (t < n_used))
            def _(t=t):
                tile_copies(t, lambda cp: cp.start())

        @pl.when(tile + 2 < n_used)
        def _():
            tile_copies(tile + 2, lambda cp: cp.start())

        @pl.when(real)
        def _():
            tile_copies(tile, lambda cp: cp.wait())

    @pl.when(real)
    def _():
        slot = tile % slots
        h2 = hbuf[slot]
        hid = jax.nn.silu(_dot(h2, wg_ref[...].astype(BF16))) * _dot(h2, wu_ref[...].astype(BF16))
        lane = lax.broadcasted_iota(jnp.int32, (TM, LANES), 1)
        col = MOE_GROUPS + tiles_ref[tile] * MOE_EXPERTS + e
        c = jnp.sum(jnp.where(lane == col, cbuf[slot], 0.0), axis=-1, keepdims=True)
        acc_ref[s] += _dot((hid * c).astype(BF16), wd_ref[...].astype(BF16))

    @pl.when(e == pl.num_programs(1) - 1)
    def _():
        y_ref[...] = acc_ref[s]


def _experts(gather, tiles, h2s, combs, wg, wu, wd, n_gtiles, first_expert):
    d = h2s.shape[1]
    assert n_gtiles % 2 == 0
    weights = lambda shape: pl.BlockSpec(
        (None,) + shape, lambda p, e, s, ga, tl: (first_expert + tl[2 * p + s] * MOE_EXPERTS + e, 0, 0))

    def out_tile(p, e, s, ga, tl):
        return (2 * p + jnp.where(e == MOE_EXPERTS - 1, s, 0), 0)

    grid_spec = pltpu.PrefetchScalarGridSpec(
        num_scalar_prefetch=2,
        grid=(n_gtiles // 2, MOE_EXPERTS, 2),
        in_specs=[pl.BlockSpec(memory_space=pl.ANY), pl.BlockSpec(memory_space=pl.ANY),
                  weights((d, MOE_HID)), weights((d, MOE_HID)), weights((MOE_HID, d))],
        out_specs=pl.BlockSpec((TM, d), out_tile),
        scratch_shapes=[pltpu.VMEM((4, TM, d), BF16), pltpu.VMEM((4, TM, LANES), F32), pltpu.VMEM((2, TM, d), F32),
                        pltpu.SemaphoreType.DMA((2, 4))],
    )
    return pl.pallas_call(
        _experts_kernel,
        grid_spec=grid_spec,
        out_shape=jax.ShapeDtypeStruct((n_gtiles * TM, d), F32),
        compiler_params=_cparams(3),
        name="moe_experts",
    )(gather, tiles, h2s, combs, wg, wu, wd)


def _finalize_kernel(scatter_ref, yg_ref, pos_ref, xn_ref, mod_ref, fg_ref, o_ref, ys_ref, sem, *, final):
    c = pl.program_id(0)
    per_chunk = MOE_SLOTS // MOE_ALIGN

    def chunk_copies(chunk, go):
        buf = chunk % 2
        for j in range(per_chunk):
            src = scatter_ref[chunk * per_chunk + j]

            @pl.when(src >= 0)
            def _(j=j, src=src):
                rows = pl.ds(pl.multiple_of(src * MOE_ALIGN, MOE_ALIGN), MOE_ALIGN)
                go(pltpu.make_async_copy(yg_ref.at[rows], ys_ref.at[buf, pl.ds(j * MOE_ALIGN, MOE_ALIGN)],
                                         sem.at[buf]))

    @pl.when(c == 0)
    def _():
        ys_ref[...] = jnp.zeros_like(ys_ref)
        chunk_copies(c, lambda cp: cp.start())

    @pl.when(c + 1 < pl.num_programs(0))
    def _():
        chunk_copies(c + 1, lambda cp: cp.start())

    chunk_copies(c, lambda cp: cp.wait())

    tm = xn_ref.shape[0]
    y = ys_ref[c % 2]
    hi = y.astype(BF16)
    rest = y - hi.astype(F32)
    mid = rest.astype(BF16)
    lo = (rest - mid.astype(F32)).astype(BF16)
    slot = lax.broadcasted_iota(jnp.int32, (tm, MOE_SLOTS), 1).astype(F32)
    take_t = jnp.where(_rep(pos_ref[...], MOE_SLOTS // LANES, 1) == slot, 1.0, 0.0).astype(BF16)
    y_tok = (_dot(take_t, hi) + _dot(take_t, mid)) + _dot(take_t, lo)
    xo = xn_ref[...] + mod_ref[5:6, :] * y_tok
    if final:
        xo = _rms(xo, fg_ref[...])
    o_ref[...] = xo


def _finalize(scatter, yg, pos, xn, mod_l, fg, seq, final):
    t, d = xn.shape
    tps = seq // TM
    grid_spec = pltpu.PrefetchScalarGridSpec(
        num_scalar_prefetch=1,
        grid=(t // TM,),
        in_specs=[pl.BlockSpec(memory_space=pl.ANY),
                  pl.BlockSpec((TM, LANES), lambda i, tb: (i, 0)),
                  pl.BlockSpec((TM, d), lambda i, tb: (i, 0)),
                  pl.BlockSpec((None, 6, d), lambda i, tb: (i // tps, 0, 0)),
                  pl.BlockSpec((1, d), lambda i, tb: (0, 0))],
        out_specs=pl.BlockSpec((TM, d), lambda i, tb: (i, 0)),
        scratch_shapes=[pltpu.VMEM((2, MOE_SLOTS, d), F32), pltpu.SemaphoreType.DMA((2,))],
    )
    return pl.pallas_call(
        functools.partial(_finalize_kernel, final=final),
        grid_spec=grid_spec,
        out_shape=jax.ShapeDtypeStruct((t, d), F32),
        compiler_params=_cparams(1),
        name="moe_finalize",
    )(scatter, yg, pos, xn, mod_l, fg)


def _permute_w_in(w):
    gate0 = D_LRU * 2 + D_ATT + 6 * LANES
    cv0 = gate0 + 3 * N_HEADS
    w = w.astype(BF16)
    pad = jnp.zeros((w.shape[0], IN_PAD - w.shape[1]), BF16)
    return jnp.concatenate([w[:, :gate0], w[:, cv0:], w[:, gate0:cv0], pad], axis=1)


def _router_weights(rg_w, rg_b, re_w, re_b):
    d = rg_w.shape[0]
    ne = MOE_GROUPS * MOE_EXPERTS
    w = jnp.concatenate([rg_w, jnp.transpose(re_w, (1, 0, 2)).reshape(d, ne),
                         jnp.zeros((d, LANES - MOE_GROUPS - ne), F32)], axis=1)
    b = jnp.concatenate([rg_b, re_b.reshape(ne), jnp.zeros((LANES - MOE_GROUPS - ne,), F32)])[None, :]
    w_hi = w.astype(BF16)
    w_mid = (w - w_hi.astype(F32)).astype(BF16)
    return jnp.stack([w_hi, w_mid]), b


def kernel(x, c, positions, ada_w, ada_b, norm1_g, norm2_g, w_in, lru_conv_w, lru_conv_b, lru_wa, lru_ba, lru_wx, lru_bx, lru_lambda, cmp_k_w1, cmp_k_w2, cmp_v_w1, cmp_v_w2, cmp_pos_k, cmp_pos_v, cnv_dw_w, cnv_dw_b, cnv_ln_g, cnv_ln_b, out_norm_g, w_out, moe_rg_w, moe_rg_b, moe_re_w, moe_re_b, moe_w_gate, moe_w_up, moe_w_down, final_norm_g):
    batch, seq, d = x.shape
    depth = ada_w.shape[0]
    assert d == D_MODEL and seq == N_SLC * SLC_LEN and seq % KEY_TILE == 0
    t = batch * seq
    ne = MOE_GROUPS * MOE_EXPERTS
    n_chunks = t // TM
    n_gtiles = n_chunks + -(-n_chunks * MOE_GROUPS * (MOE_ALIGN - 1) // TM) + MOE_GROUPS

    cos, sin = _rope_tables(positions)
    mod = _modulation(c, ada_w, ada_b)
    x2 = x.reshape(t, d)
    fg = final_norm_g[None, :]
    for l in range(depth):
        (xl, gl, q, kc, vc, ksa, vsa, kw, vwa, gate, cv) = _inproj(
            x2, mod[l], norm1_g[l][None, :], _permute_w_in(w_in[l]), cos, sin, seq)
        y_lru, y_cnv = _mixer(xl, gl, cv, lru_conv_w[l], lru_conv_b[l], lru_wa[l], lru_ba[l], lru_wx[l], lru_bx[l],
                              lru_lambda[l], cnv_dw_w[l], cnv_dw_b[l], cnv_ln_g[l], cnv_ln_b[l], batch, seq)
        kcmp, vcmp = _compress(kc, vc, cmp_pos_k[l], cmp_k_w1[l], cmp_k_w2[l], cmp_pos_v[l], cmp_v_w1[l], cmp_v_w2[l],
                               batch, seq)
        y_att = _attention(q, kcmp, vcmp, ksa, vsa, kw, vwa, gate, batch, seq)
        rw, rb = _router_weights(moe_rg_w[l], moe_rg_b[l], moe_re_w[l], moe_re_b[l])
        xn, h2, comb, pos, seg = _outproj(x2, y_lru, y_att, y_cnv, out_norm_g[l][None, :], w_out[l].astype(BF16),
                                          mod[l], norm2_g[l][None, :], rw, rb, seq)
        gather, scatter, tiles = _moe_plan(seg, n_chunks, n_gtiles)
        yg = _experts(gather, tiles, h2, comb,
                      moe_w_gate.reshape(depth * ne, d, MOE_HID), moe_w_up.reshape(depth * ne, d, MOE_HID),
                      moe_w_down.reshape(depth * ne, MOE_HID, d), n_gtiles, l * ne)
        x2 = _finalize(scatter, yg, pos, xn, mod[l], fg, seq, final=(l == depth - 1))
    return x2.reshape(batch, seq, d)
```

```python
import functools

import jax
import jax.numpy as jnp
from jax import lax
from jax.experimental import pallas as pl
from jax.experimental.pallas import tpu as pltpu

F32 = jnp.float32
BF16 = jnp.bfloat16

D_MODEL = 1024
D_LRU = 256
D_ATT = 512
D_CONV = 256
LRU_BLOCKS = 4
LRU_CONV_W = 4
LRU_C = 8.0
HEAD_DIM = 64
N_HEADS = 8
N_KV = 2
GQA = 4
ROPE_THETA = 10000.0
CMP_LEN = 32
CMP_STRIDE = 16
SLC_LEN = 64
SLC_TOPN = 16
WINDOW = 512
CONV_K = 31
MOE_GROUPS = 4
MOE_EXPERTS = 4
MOE_HID = 512
EPS = 1e-6
NEG = -1e30
FORCE = 1e9
LOG2_E = 1.4426950408889634

LANES = 128
Q_BLOCK = 128
KEY_TILE = 512
N_SLC = 128
TM = 512
MOE_ALIGN_LOG2 = 4
MOE_ALIGN = 1 << MOE_ALIGN_LOG2
MOE_SLOTS = 640
TC = 512
SCAN_ROWS = 64
LRU_TAIL = 8
CNV_TAIL = 32
IN_PAD = 2432
VMEM_LIMIT = 56 * 1024 * 1024


def _cparams(n_axes, vmem=VMEM_LIMIT):
    return pltpu.CompilerParams(dimension_semantics=("arbitrary",) * n_axes, vmem_limit_bytes=vmem)


def _dot(a, b):
    return jnp.dot(a, b, preferred_element_type=F32)


def _dot_nt(a, b):
    return lax.dot_general(a, b, (((1,), (1,)), ((), ())), preferred_element_type=F32)


def _rep(v, n, axis):
    return jnp.concatenate([v] * n, axis=axis)


def _rms(v, g):
    return v * lax.rsqrt(jnp.mean(v * v, axis=-1, keepdims=True) + EPS) * g


def _rope_kernel(pos_ref, inv_ref, sign_ref, cos_ref, sin_ref):
    ang = pos_ref[...].astype(F32) * inv_ref[...]
    cos_ref[...] = jnp.cos(ang)
    sin_ref[...] = jnp.sin(ang) * sign_ref[...]


def _rope_tables(positions):
    t = positions.size
    inv = ROPE_THETA ** (-jnp.arange(0, HEAD_DIM, 2, dtype=F32) / HEAD_DIM)
    inv128 = jnp.tile(inv, 4)[None, :]
    sign128 = jnp.tile(jnp.concatenate([-jnp.ones((32,), F32), jnp.ones((32,), F32)]), 2)[None, :]
    tr = 1024
    return pl.pallas_call(
        _rope_kernel,
        grid=(t // tr,),
        in_specs=[pl.BlockSpec((tr, 1), lambda i: (i, 0)),
                  pl.BlockSpec((1, LANES), lambda i: (0, 0)),
                  pl.BlockSpec((1, LANES), lambda i: (0, 0))],
        out_specs=[pl.BlockSpec((tr, LANES), lambda i: (i, 0))] * 2,
        out_shape=[jax.ShapeDtypeStruct((t, LANES), F32)] * 2,
        compiler_params=_cparams(1),
        name="rope_tables",
    )(positions.reshape(t, 1), inv128, sign128)


def _mod_kernel(c_ref, w_ref, b_ref, o_ref):
    sc = jax.nn.silu(c_ref[...])
    o_ref[...] = _dot(sc.astype(BF16), w_ref[...].astype(BF16)) + b_ref[...]


def _modulation(c, ada_w, ada_b):
    nl, d, n6 = ada_w.shape
    b = c.shape[0]
    rows = 16
    cp = jnp.zeros((rows, d), F32).at[:b].set(c)
    tn = 1536
    out = pl.pallas_call(
        _mod_kernel,
        grid=(nl, n6 // tn),
        in_specs=[pl.BlockSpec((rows, d), lambda l, j: (0, 0)),
                  pl.BlockSpec((None, d, tn), lambda l, j: (l, 0, j)),
                  pl.BlockSpec((None, 1, tn), lambda l, j: (l, 0, j))],
        out_specs=pl.BlockSpec((None, rows, tn), lambda l, j: (l, 0, j)),
        out_shape=jax.ShapeDtypeStruct((nl, rows, n6), F32),
        compiler_params=_cparams(2),
        name="adaln_mod",
    )(cp, ada_w, ada_b.reshape(nl, 1, n6))
    return out[:, :b].reshape(nl, b, 6, d)


def _inproj_kernel(x_ref, mod_ref, g_ref, w_ref, cos_ref, sin_ref,
                   xl_ref, gl_ref, q_ref, kc_ref, vc_ref, ksa_ref, vsa_ref, kw_ref, vwa_ref, gate_ref, cv_ref,
                   *, tiles_per_seq):
    tm = x_ref.shape[0]
    x = x_ref[...]
    h = _rms(x, g_ref[...]) * (1.0 + mod_ref[1:2, :]) + mod_ref[0:1, :]
    p = _dot(h.astype(BF16), w_ref[...])

    cos = cos_ref[...]
    sin = sin_ref[...]
    lane = lax.broadcasted_iota(jnp.int32, (tm, LANES), 1)
    first_half = (lane & 63) < 32

    def rope(v):
        rot = jnp.where(first_half, pltpu.roll(v, 96, 1), pltpu.roll(v, 32, 1))
        return v * cos + rot * sin

    xl_ref[...] = p[:, 0:256]
    gl_ref[...] = p[:, 256:512]
    scale = HEAD_DIM ** -0.5 * LOG2_E
    low = lane < 64
    for m in range(4):
        slab = rope(p[:, 512 + 128 * m:640 + 128 * m]) * scale
        swapped = pltpu.roll(slab, 64, 1)
        for hh in range(2):
            head = 2 * m + hh
            kv = head // GQA
            src = slab if hh == kv else swapped
            keep = low if kv == 0 else jnp.logical_not(low)
            q_ref[:, head * LANES:(head + 1) * LANES] = jnp.where(keep, src, 0.0).astype(BF16)
    kc_ref[...] = rope(p[:, 1024:1152])
    vc_ref[...] = p[:, 1152:1280]
    row = lax.broadcasted_iota(jnp.int32, (tm, LANES), 0)
    s_base = (pl.program_id(0) % tiles_per_seq) * tm
    onehot = jnp.where(lane == ((s_base + row) >> 6), 1.0, 0.0).astype(BF16)
    ones = jnp.ones((tm, LANES), BF16)
    ksa_ref[:, 0:LANES] = rope(p[:, 1280:1408]).astype(BF16)
    ksa_ref[:, LANES:2 * LANES] = onehot
    vsa_ref[:, 0:LANES] = p[:, 1408:1536].astype(BF16)
    vsa_ref[:, LANES:2 * LANES] = ones
    kw_ref[...] = rope(p[:, 1536:1664]).astype(BF16)
    vwa_ref[:, 0:LANES] = p[:, 1664:1792].astype(BF16)
    vwa_ref[:, LANES:2 * LANES] = ones
    cv_ref[...] = p[:, 1792:2304]
    gate_ref[...] = jax.nn.sigmoid(p[:, 2304:2432])


def _inproj(x2, mod_l, g, w_p, cos, sin, seq):
    t, d = x2.shape
    tps = seq // TM
    row = lambda w: pl.BlockSpec((TM, w), lambda i: (i, 0))
    outs = [(256, F32), (256, F32), (N_HEADS * LANES, BF16), (LANES, F32), (LANES, F32), (2 * LANES, BF16),
            (2 * LANES, BF16), (LANES, BF16), (2 * LANES, BF16), (LANES, F32), (2 * D_CONV, F32)]
    return pl.pallas_call(
        functools.partial(_inproj_kernel, tiles_per_seq=tps),
        grid=(t // TM,),
        in_specs=[row(d),
                  pl.BlockSpec((None, 6, d), lambda i: (i // tps, 0, 0)),
                  pl.BlockSpec((1, d), lambda i: (0, 0)),
                  pl.BlockSpec((d, IN_PAD), lambda i: (0, 0)),
                  row(LANES), row(LANES)],
        out_specs=[row(w) for w, _ in outs],
        out_shape=[jax.ShapeDtypeStruct((t, w), dt) for w, dt in outs],
        compiler_params=_cparams(1),
        name="in_proj",
    )(x2, mod_l, g, w_p, cos, sin)


def _shift_rows(v, s, fill, row):
    return jnp.where(row < s, fill, pltpu.roll(v, s, 0))


def _causal_taps(ext, w_ref, bias, width, tail, tc):
    acc = bias
    for b in range(min(8, width)):
        shifted = pltpu.roll(ext, b, 0) if b else ext
        for k in range(width):
            back = width - 1 - k
            if back % 8 == b:
                start = tail - (back - b)
                acc = acc + w_ref[k:k + 1, :] * shifted[start:start + tc, :]
    return acc


def _mixer_kernel(xl_ref, gl_ref, cv_ref, lcw_ref, lcb_ref, wa_ref, ba_ref, wx_ref, bx_ref, lam_ref,
                  cw_ref, cb_ref, lng_ref, lnb_ref, ylru_ref, ycnv_ref,
                  extl, extc, abuf, ubuf, hcar):
    tc = xl_ref.shape[0]

    @pl.when(pl.program_id(1) == 0)
    def _():
        extl[0:LRU_TAIL, :] = jnp.zeros((LRU_TAIL, D_LRU), F32)
        extc[0:CNV_TAIL, :] = jnp.zeros((CNV_TAIL, D_CONV), F32)
        hcar[...] = jnp.zeros_like(hcar)

    xl = xl_ref[...]
    extl[LRU_TAIL:LRU_TAIL + tc, :] = xl
    xc = _causal_taps(extl[...], lcw_ref, lcb_ref[...], LRU_CONV_W, LRU_TAIL, tc)
    extl[0:LRU_TAIL, :] = xl[tc - LRU_TAIL:tc, :]
    xcb = xc.astype(BF16)
    r = jax.nn.sigmoid(_dot(xcb, wa_ref[...]) + ba_ref[...])
    gi = jax.nn.sigmoid(_dot(xcb, wx_ref[...]) + bx_ref[...])
    log_a = LRU_C * r * jax.nn.log_sigmoid(lam_ref[...])
    th = jnp.tanh(log_a)
    one_minus_a2 = -2.0 * th / (1.0 - th)
    abuf[...] = jnp.exp(log_a)
    ubuf[...] = jnp.sqrt(one_minus_a2) * (gi * xc)

    row = lax.broadcasted_iota(jnp.int32, (SCAN_ROWS, D_LRU), 0)

    def scan_chunk(c, h_prev):
        off = pl.multiple_of(c * SCAN_ROWS, SCAN_ROWS)
        a = abuf[pl.ds(off, SCAN_ROWS), :]
        b = ubuf[pl.ds(off, SCAN_ROWS), :]
        s = 1
        while s < SCAN_ROWS:
            b = b + a * _shift_rows(b, s, 0.0, row)
            a = a * _shift_rows(a, s, 1.0, row)
            s *= 2
        h = b + a * h_prev
        ylru_ref[pl.ds(off, SCAN_ROWS), :] = h * jax.nn.gelu(gl_ref[pl.ds(off, SCAN_ROWS), :])
        return h[SCAN_ROWS - 1:SCAN_ROWS, :]

    hcar[0:1, :] = lax.fori_loop(0, tc // SCAN_ROWS, scan_chunk, hcar[0:1, :])

    u = cv_ref[...]
    v = u[:, 0:D_CONV] * jax.nn.sigmoid(u[:, D_CONV:2 * D_CONV])
    extc[CNV_TAIL:CNV_TAIL + tc, :] = v
    acc = _causal_taps(extc[...], cw_ref, cb_ref[...], CONV_K, CNV_TAIL, tc)
    extc[0:CNV_TAIL, :] = v[tc - CNV_TAIL:tc, :]
    mu = jnp.mean(acc, axis=-1, keepdims=True)
    var = jnp.mean(jnp.square(acc - mu), axis=-1, keepdims=True)
    y = (acc - mu) * lax.rsqrt(var + EPS) * lng_ref[...] + lnb_ref[...]
    ycnv_ref[...] = jax.nn.silu(y)


def _block_diag(w):
    n, rows, cols = w.shape
    eye = jnp.eye(n, dtype=bool)
    return jnp.where(eye[:, None, :, None], w[:, :, None, :], 0.0).reshape(n * rows, n * cols)


def _mixer(xl, gl, cv, lcw, lcb, wa, ba, wx, bx, lam, cw, cb, lng, lnb, batch, seq):
    seqrow = lambda w: pl.BlockSpec((None, TC, w), lambda b, t: (b, t, 0))
    full = lambda a: pl.BlockSpec(a.shape, lambda b, t: (0,) * a.ndim)
    small = [lcw, lcb[None, :], _block_diag(wa).astype(BF16), ba[None, :], _block_diag(wx).astype(BF16), bx[None, :],
             lam[None, :], jnp.zeros((32, D_CONV), F32).at[:CONV_K].set(cw), cb[None, :], lng[None, :], lnb[None, :]]
    return pl.pallas_call(
        _mixer_kernel,
        grid=(batch, seq // TC),
        in_specs=[seqrow(D_LRU), seqrow(D_LRU), seqrow(2 * D_CONV)] + [full(a) for a in small],
        out_specs=[seqrow(D_LRU), seqrow(D_CONV)],
        out_shape=[jax.ShapeDtypeStruct((batch, seq, D_LRU), F32), jax.ShapeDtypeStruct((batch, seq, D_CONV), F32)],
        scratch_shapes=[pltpu.VMEM((TC + LRU_TAIL, D_LRU), F32), pltpu.VMEM((TC + CNV_TAIL, D_CONV), F32),
                        pltpu.VMEM((TC, D_LRU), F32), pltpu.VMEM((TC, D_LRU), F32), pltpu.VMEM((8, D_LRU), F32)],
        compiler_params=_cparams(2),
        name="mixer_stream",
    )(xl.reshape(batch, seq, D_LRU), gl.reshape(batch, seq, D_LRU), cv.reshape(batch, seq, 2 * D_CONV), *small)


def _cmp_kernel(kc_ref, vc_ref, posk_ref, posv_ref, wk1_ref, wk2_ref, wv1_ref, wv2_ref, ko_ref, vo_ref, nat_ref):
    nh = ko_ref.shape[0]
    quarter = nh // 4

    def one(t_ref, pos_ref, w1_ref, w2_ref, o_ref):
        first = jnp.zeros((nh, w1_ref.shape[2]), F32)
        second = jnp.zeros((nh, w1_ref.shape[2]), F32)
        for l in range(CMP_STRIDE):
            x = t_ref[pl.ds(l, nh, stride=CMP_STRIDE), :]
            first = first + _dot((x + pos_ref[l:l + 1, :]).astype(BF16), w1_ref[l])
            second = second + _dot((x + pos_ref[CMP_STRIDE + l:CMP_STRIDE + l + 1, :]).astype(BF16),
                                   w1_ref[CMP_STRIDE + l])
        hid = jax.nn.gelu(first + pltpu.roll(second, nh - 1, 0))
        nat_ref[...] = _dot(hid.astype(BF16), w2_ref[...])
        for r in range(4):
            o_ref[r * quarter:(r + 1) * quarter, :] = nat_ref[pl.ds(r, quarter, stride=4), :].astype(o_ref.dtype)

    one(kc_ref, posk_ref, wk1_ref, wk2_ref, ko_ref)
    one(vc_ref, posv_ref, wv1_ref, wv2_ref, vo_ref)


def _cmp_weights(pos, w1, w2):
    hid = w1.shape[1]
    eye = jnp.eye(N_KV, dtype=bool)
    w1 = w1.astype(BF16).reshape(CMP_LEN, HEAD_DIM, hid)
    w1x = jnp.where(eye[None, :, None, :, None], w1[:, None, :, None, :], 0.0)
    w1x = w1x.reshape(CMP_LEN, N_KV * HEAD_DIM, N_KV * hid)
    w2x = _block_diag(jnp.broadcast_to(w2.astype(BF16), (N_KV,) + w2.shape))
    return jnp.tile(pos, (1, N_KV)), w1x, w2x


def _compress(kc, vc, pos_k, wk1, wk2, pos_v, wv1, wv2, batch, seq):
    nh = seq // CMP_STRIDE
    wide = N_KV * HEAD_DIM
    pk, wk1x, wk2x = _cmp_weights(pos_k, wk1, wk2)
    pv, wv1x, wv2x = _cmp_weights(pos_v, wv1, wv2)
    full = lambda a: pl.BlockSpec(a.shape, lambda b: (0,) * a.ndim)
    seqs = pl.BlockSpec((None, seq, wide), lambda b: (b, 0, 0))
    outs = pl.BlockSpec((None, nh, wide), lambda b: (b, 0, 0))
    return pl.pallas_call(
        _cmp_kernel,
        grid=(batch,),
        in_specs=[seqs, seqs, full(pk), full(pv), full(wk1x), full(wk2x), full(wv1x), full(wv2x)],
        out_specs=[outs, outs],
        out_shape=[jax.ShapeDtypeStruct((batch, nh, wide), BF16)] * 2,
        scratch_shapes=[pltpu.VMEM((nh, wide), F32)],
        compiler_params=_cparams(1),
        name="nsa_compress",
    )(kc.reshape(batch, seq, wide), vc.reshape(batch, seq, wide), pk, pv, wk1x, wk2x, wv1x, wv2x)


def _attn_kernel(q_ref, kcmp_ref, vcmp_ref, ksa_ref, vsa_ref, kw_ref, vwa_ref, gate_ref, o_ref,
                 qa_ref, part_ref, acc_ref, m_ref, accw_ref, mw_ref, sa_ref, sb_ref):
    rows = GQA * Q_BLOCK
    reps = KEY_TILE // LANES
    qb = pl.program_id(1)
    s0 = qb * Q_BLOCK
    kd = s0 // KEY_TILE
    off_d = pl.multiple_of(kd * KEY_TILE, KEY_TILE)
    off_p = pl.multiple_of(jnp.maximum(kd - 1, 0) * KEY_TILE, KEY_TILE)
    d0 = s0 - off_d

    lane = lax.broadcasted_iota(jnp.int32, (Q_BLOCK, LANES), 1)
    sub = lax.broadcasted_iota(jnp.int32, (Q_BLOCK, LANES), 0)
    r_q = lax.broadcasted_iota(jnp.int32, (Q_BLOCK, KEY_TILE), 0)
    c_k = lax.broadcasted_iota(jnp.int32, (Q_BLOCK, KEY_TILE), 1)
    rel = c_k - r_q
    gate = gate_ref[...]

    def tile_rows(plane):
        return _rep(plane, GQA, 0)

    def flash_step(acc, m, kv, s, v_tile):
        m_old = m[kv]
        m_new = jnp.maximum(m_old, jnp.broadcast_to(jnp.max(s, axis=-1, keepdims=True), m_old.shape))
        p = jnp.exp2(s - _rep(m_new, reps, 1))
        acc[kv] = acc[kv] * _rep(jnp.exp2(m_old - m_new), 2, 1) + _dot(p.astype(BF16), v_tile)
        m[kv] = m_new

    def flash_result(acc, kv):
        a = acc[kv]
        return a[:, 0:LANES] / a[:, LANES:2 * LANES]

    q128 = [jnp.concatenate([q_ref[:, (kv * GQA + g) * LANES:(kv * GQA + g + 1) * LANES] for g in range(GQA)], axis=0)
            for kv in range(N_KV)]

    kcmp = kcmp_ref[...]
    vcmp = vcmp_ref[...]
    cmp_end = SLC_LEN * (c_k & (N_SLC - 1)) + CMP_STRIDE * (c_k >> 7) + (CMP_LEN - 1)
    cmp_bias = tile_rows(jnp.where(cmp_end - r_q <= s0, 0.0, NEG))
    has_cmp = tile_rows(jnp.where(s0 + sub >= CMP_LEN - 1, 1.0, 0.0))
    o_cmp, imp_t = [], []
    for kv in range(N_KV):
        s_c = _dot_nt(q128[kv], kcmp) + cmp_bias
        mb = jnp.broadcast_to(jnp.max(s_c, axis=-1, keepdims=True), (rows, LANES))
        e = jnp.exp2(s_c - _rep(mb, reps, 1))
        lb = jnp.broadcast_to(jnp.sum(e, axis=-1, keepdims=True), (rows, LANES))
        inv = has_cmp / lb
        o_cmp.append(_dot(e.astype(BF16), vcmp) * inv)
        e4 = (e[:, 0:N_SLC] + e[:, N_SLC:2 * N_SLC] + e[:, 2 * N_SLC:3 * N_SLC] + e[:, 3 * N_SLC:4 * N_SLC]) * inv
        imp = e4[0:Q_BLOCK] + e4[Q_BLOCK:2 * Q_BLOCK] + e4[2 * Q_BLOCK:3 * Q_BLOCK] + e4[3 * Q_BLOCK:4 * Q_BLOCK]
        imp_t.append(imp.T)

    accw_ref[...] = jnp.zeros_like(accw_ref)
    mw_ref[...] = jnp.full_like(mw_ref, NEG)
    prev_bias = tile_rows(jnp.where(rel > jnp.where(kd > 0, d0, KEY_TILE), 0.0, NEG))
    diag_bias = tile_rows(jnp.where(rel <= d0, 0.0, NEG))
    kw_p, vw_p = kw_ref[pl.ds(off_p, KEY_TILE), :], vwa_ref[pl.ds(off_p, KEY_TILE), :]
    kw_d, vw_d = kw_ref[pl.ds(off_d, KEY_TILE), :], vwa_ref[pl.ds(off_d, KEY_TILE), :]
    for kv in range(N_KV):
        flash_step(accw_ref, mw_ref, kv, _dot_nt(q128[kv], kw_p) + prev_bias, vw_p)
    for kv in range(N_KV):
        flash_step(accw_ref, mw_ref, kv, _dot_nt(q128[kv], kw_d) + diag_bias, vw_d)
    for kv in range(N_KV):
        o_win = flash_result(accw_ref, kv)
        for g in range(GQA):
            col = 3 * (kv * GQA + g)
            rs = slice(g * Q_BLOCK, (g + 1) * Q_BLOCK)
            part_ref[kv, rs, :] = gate[:, col:col + 1] * o_cmp[kv][rs] + gate[:, col + 2:col + 3] * o_win[rs]

    blk = lax.broadcasted_iota(jnp.int32, (N_SLC, Q_BLOCK), 0)
    t_q = s0 + lax.broadcasted_iota(jnp.int32, (N_SLC, Q_BLOCK), 1)
    forced = (blk == 0) | (blk == (t_q >> 6))
    val = [jnp.where(forced, -jnp.inf, jnp.where(blk * SLC_LEN <= t_q, imp_t[kv], -1.0)) for kv in range(N_KV)]
    blk_f = blk.astype(F32)
    for _ in range(SLC_TOPN - 2):
        for kv in range(N_KV):
            mx = jnp.max(val[kv], axis=0, keepdims=True)
            first = jnp.min(jnp.where(val[kv] == mx, blk_f, float(N_SLC)), axis=0, keepdims=True)
            val[kv] = jnp.where(blk_f == first, -jnp.inf, val[kv])
    for kv in range(N_KV):
        bias = jnp.where(val[kv].T == -jnp.inf, 0.0, NEG).astype(BF16)
        qa_ref[kv] = jnp.concatenate([q128[kv], jnp.concatenate([bias] * GQA, axis=0)], axis=1)

    acc_ref[...] = jnp.zeros_like(acc_ref)
    m_ref[...] = jnp.full_like(m_ref, NEG)

    def scores_into(dst, kt):
        k_t = ksa_ref[pl.ds(pl.multiple_of(kt * KEY_TILE, KEY_TILE), KEY_TILE), :]
        for kv in range(N_KV):
            dst[kv] = _dot_nt(qa_ref[kv], k_t)

    def consume(src, kt, bias):
        v_t = vsa_ref[pl.ds(pl.multiple_of(kt * KEY_TILE, KEY_TILE), KEY_TILE), :]
        for kv in range(N_KV):
            flash_step(acc_ref, m_ref, kv, src[kv] if bias is None else src[kv] + bias, v_t)

    scores_into(sa_ref, 0)

    def slc_pairs(first, count):
        for i in range(count):
            a = first + 2 * i
            scores_into(sb_ref, a + 1)
            consume(sa_ref, a, None)
            scores_into(sa_ref, a + 2)
            consume(sb_ref, a + 1, None)

    def slc_quad(j, carry):
        slc_pairs(4 * j, 2)
        return carry

    def slc_pair(j, carry):
        slc_pairs(4 * n_quads + 2 * j, 1)
        return carry

    n_quads = kd // 4
    lax.fori_loop(0, n_quads, slc_quad, 0)
    lax.fori_loop(0, (kd // 2) & 1, slc_pair, 0)
    causal_bias = tile_rows(jnp.where(rel <= d0, 0.0, NEG))
    kd_odd = (kd & 1) == 1

    @pl.when(kd_odd)
    def _():
        scores_into(sb_ref, kd)
        consume(sa_ref, kd - 1, None)
        consume(sb_ref, kd, causal_bias)

    @pl.when(jnp.logical_not(kd_odd))
    def _():
        consume(sa_ref, kd, causal_bias)

    for kv in range(N_KV):
        o_slc = flash_result(acc_ref, kv)
        pieces = []
        for g in range(GQA):
            col = 3 * (kv * GQA + g)
            rs = slice(g * Q_BLOCK, (g + 1) * Q_BLOCK)
            pieces.append(part_ref[kv, rs, :] + gate[:, col + 1:col + 2] * o_slc[rs])
        for mm in range(2):
            even, odd = pieces[2 * mm], pieces[2 * mm + 1]
            if kv == 0:
                odd = pltpu.roll(odd, 64, 1)
            else:
                even = pltpu.roll(even, 64, 1)
            slab = kv * 2 + mm
            o_ref[:, slab * LANES:(slab + 1) * LANES] = jnp.where(lane < 64, even, odd)


def _attention(q, kcmp, vcmp, ksa, vsa, kw, vwa, gate, batch, seq):
    nh = kcmp.shape[1]
    assert nh == KEY_TILE
    rows = GQA * Q_BLOCK
    res = lambda w: pl.BlockSpec((None, seq, w), lambda b, i: (b, 0, 0))
    blk = lambda w: pl.BlockSpec((None, Q_BLOCK, w), lambda b, i: (b, i, 0))
    r3 = lambda a, w: a.reshape(batch, seq, w)
    return pl.pallas_call(
        _attn_kernel,
        grid=(batch, seq // Q_BLOCK),
        in_specs=[blk(N_HEADS * LANES),
                  pl.BlockSpec((None, nh, LANES), lambda b, i: (b, 0, 0)),
                  pl.BlockSpec((None, nh, LANES), lambda b, i: (b, 0, 0)),
                  res(2 * LANES), res(2 * LANES), res(LANES), res(2 * LANES), blk(LANES)],
        out_specs=blk(D_ATT),
        out_shape=jax.ShapeDtypeStruct((batch, seq, D_ATT), F32),
        scratch_shapes=[pltpu.VMEM((N_KV, rows, 2 * LANES), BF16), pltpu.VMEM((N_KV, rows, LANES), F32),
                        pltpu.VMEM((N_KV, rows, 2 * LANES), F32), pltpu.VMEM((N_KV, rows, LANES), F32),
                        pltpu.VMEM((N_KV, rows, 2 * LANES), F32), pltpu.VMEM((N_KV, rows, LANES), F32),
                        pltpu.VMEM((N_KV, rows, KEY_TILE), F32), pltpu.VMEM((N_KV, rows, KEY_TILE), F32)],
        compiler_params=_cparams(2),
        name="nsa_attention",
    )(r3(q, N_HEADS * LANES), kcmp, vcmp, r3(ksa, 2 * LANES), r3(vsa, 2 * LANES), r3(kw, LANES), r3(vwa, 2 * LANES),
      r3(gate, LANES))


def _outproj_kernel(x_ref, yl_ref, ya_ref, yc_ref, gn_ref, w_ref, mod_ref, n2_ref, rw_ref, rb_ref,
                    xn_ref, h2_ref, comb_ref, pos_ref, seg_ref):
    tm = x_ref.shape[0]
    a, b = D_LRU, D_LRU + D_ATT
    yl = _rms(yl_ref[...], gn_ref[:, 0:a]).astype(BF16)
    ya = _rms(ya_ref[...], gn_ref[:, a:b]).astype(BF16)
    yc = _rms(yc_ref[...], gn_ref[:, b:D_MODEL]).astype(BF16)
    y = _dot(yl, w_ref[0:a, :]) + _dot(ya, w_ref[a:b, :]) + _dot(yc, w_ref[b:D_MODEL, :])
    xn = x_ref[...] + mod_ref[2:3, :] * y
    xn_ref[...] = xn
    h2 = _rms(xn, n2_ref[...]) * (1.0 + mod_ref[4:5, :]) + mod_ref[3:4, :]

    h_hi = h2.astype(BF16)
    h_mid = (h2 - h_hi.astype(F32)).astype(BF16)
    logit = _dot(h_hi, rw_ref[0]) + (_dot(h_hi, rw_ref[1]) + _dot(h_mid, rw_ref[0])) + rb_ref[...]
    lane = lax.broadcasted_iota(jnp.int32, (tm, LANES), 1)
    ninf = -jnp.inf
    is_g = lane < MOE_GROUPS
    is_e = (lane >= MOE_GROUPS) & (lane < MOE_GROUPS + MOE_GROUPS * MOE_EXPERTS)
    lg_max = jnp.max(jnp.where(is_g, logit, ninf), axis=-1, keepdims=True)
    g_star = jnp.min(jnp.where(is_g & (logit == lg_max), lane, LANES), axis=-1, keepdims=True)
    pg_star = 1.0 / jnp.sum(jnp.where(is_g, jnp.exp(logit - lg_max), 0.0), axis=-1, keepdims=True)
    in_grp = is_e & (((lane - MOE_GROUPS) >> 2) == g_star)
    v1 = jnp.max(jnp.where(in_grp, logit, ninf), axis=-1, keepdims=True)
    i1 = jnp.min(jnp.where(in_grp & (logit == v1), lane, LANES), axis=-1, keepdims=True)
    rest = in_grp & (lane != i1)
    v2 = jnp.max(jnp.where(rest, logit, ninf), axis=-1, keepdims=True)
    i2 = jnp.min(jnp.where(rest & (logit == v2), lane, LANES), axis=-1, keepdims=True)
    d = jnp.exp(v2 - v1)
    pe1 = 1.0 / (1.0 + d)
    pe2 = d / (1.0 + d)
    comb = jnp.where(lane == i1, pe1, jnp.where(lane == i2, pe2, 0.0)) * pg_star

    onehot = jnp.where(lane == g_star, 1.0, 0.0)
    r_i = lax.broadcasted_iota(jnp.int32, (tm, tm), 0)
    c_i = lax.broadcasted_iota(jnp.int32, (tm, tm), 1)
    earlier = jnp.where(c_i < r_i, 1.0, 0.0).astype(BF16)
    rank = jnp.sum(onehot * _dot(earlier, onehot.astype(BF16)), axis=-1, keepdims=True)
    n_g = jnp.sum(onehot, axis=0, keepdims=True)
    n_pad = (((n_g.astype(jnp.int32) + (MOE_ALIGN - 1)) >> MOE_ALIGN_LOG2) << MOE_ALIGN_LOG2).astype(F32)
    lane1 = lax.broadcasted_iota(jnp.int32, (1, LANES), 1)
    start = jnp.zeros((1, LANES), F32)
    below = jnp.zeros((1, 1), F32)
    for g in range(1, MOE_GROUPS):
        below = below + jnp.sum(jnp.where(lane1 == g - 1, n_pad, 0.0), axis=-1, keepdims=True)
        start = start + jnp.where(lane1 == g, below, 0.0)
    pos = jnp.sum(onehot * start, axis=-1, keepdims=True) + rank
    pos_rep = jnp.broadcast_to(pos, (tm, LANES))
    slot = lax.broadcasted_iota(jnp.int32, (tm, MOE_SLOTS), 1).astype(F32)
    take_t = jnp.where(_rep(pos_rep, MOE_SLOTS // LANES, 1) == slot, 1.0, 0.0)
    take = take_t.T
    take = take.astype(BF16)
    h2_ref[...] = _dot(take, h_hi).astype(BF16)
    c_hi = comb.astype(BF16)
    c_rest = comb - c_hi.astype(F32)
    c_mid = c_rest.astype(BF16)
    c_lo = (c_rest - c_mid.astype(F32)).astype(BF16)
    comb_ref[...] = (_dot(take, c_hi) + _dot(take, c_mid)) + _dot(take, c_lo)
    pos_ref[...] = pos_rep
    seg_ref[...] = jnp.zeros(seg_ref.shape, jnp.int32)
    seg_ref[0:1, :] = n_pad.astype(jnp.int32)
    seg_ref[1:2, :] = start.astype(jnp.int32)


def _outproj(x2, yl, ya, yc, gn, w_out, mod_l, n2g, rw, rb, seq):
    t, d = x2.shape
    tps = seq // TM
    row = lambda w: pl.BlockSpec((TM, w), lambda i: (i, 0))
    full = lambda a: pl.BlockSpec(a.shape, lambda i: (0,) * a.ndim)
    return pl.pallas_call(
        _outproj_kernel,
        grid=(t // TM,),
        in_specs=[row(d), row(D_LRU), row(D_ATT), row(D_CONV), full(gn), full(w_out),
                  pl.BlockSpec((None, 6, d), lambda i: (i // tps, 0, 0)), full(n2g), full(rw), full(rb)],
        out_specs=[row(d), pl.BlockSpec((MOE_SLOTS, d), lambda i: (i, 0)), pl.BlockSpec((MOE_SLOTS, LANES), lambda i: (i, 0)),
                   row(LANES), pl.BlockSpec((None, 8, LANES), lambda i: (i, 0, 0))],
        out_shape=[jax.ShapeDtypeStruct((t, d), F32), jax.ShapeDtypeStruct((t // TM * MOE_SLOTS, d), BF16),
                   jax.ShapeDtypeStruct((t // TM * MOE_SLOTS, LANES), F32), jax.ShapeDtypeStruct((t, LANES), F32),
                   jax.ShapeDtypeStruct((t // TM, 8, LANES), jnp.int32)],
        compiler_params=_cparams(1),
        name="out_proj_router",
    )(x2, yl.reshape(t, D_LRU), ya.reshape(t, D_ATT), yc.reshape(t, D_CONV), gn, w_out, mod_l, n2g, rw, rb)


def _moe_plan(seg, n_chunks, n_gtiles):
    rows = seg[:, 0, :MOE_GROUPS]
    start = seg[:, 1, :MOE_GROUPS]
    ci = jnp.arange(n_chunks, dtype=jnp.int32)
    gi = jnp.arange(MOE_GROUPS, dtype=jnp.int32)
    cum = jnp.sum(jnp.where((ci[None, :] <= ci[:, None])[:, :, None], rows[None, :, :], 0), axis=1)
    before = cum - rows
    total = jnp.sum(rows, axis=0)
    tiles_g = (total + (TM - 1)) // TM
    ends = jnp.sum(jnp.where(gi[None, :] <= gi[:, None], tiles_g[None, :], 0), axis=1)
    base = (ends - tiles_g) * TM
    group_at = lambda tile: jnp.minimum(jnp.sum((tile[:, None] >= ends[None, :]).astype(jnp.int32), axis=1),
                                        MOE_GROUPS - 1)
    tiles = jnp.concatenate([group_at(jnp.arange(n_gtiles, dtype=jnp.int32)), ends[-1:]]).astype(jnp.int32)

    row = jnp.arange(n_gtiles * (TM // MOE_ALIGN), dtype=jnp.int32) * MOE_ALIGN
    is_g = group_at(row // TM)[:, None] == gi[None, :]
    of_group = lambda v: jnp.sum(jnp.where(is_g, v[None, :], 0), axis=1)
    p = row - of_group(base)
    cum_g = jnp.sum(jnp.where(is_g[:, None, :], cum[None, :, :], 0), axis=2)
    chunk = jnp.minimum(jnp.sum((p[:, None] >= cum_g).astype(jnp.int32), axis=1), n_chunks - 1)
    is_cg = (chunk[:, None] == ci[None, :])[:, :, None] & is_g[:, None, :]
    of_segment = lambda a: jnp.sum(jnp.where(is_cg, a[None, :, :], 0), axis=(1, 2))
    src = chunk * MOE_SLOTS + of_segment(start - before) + p
    zero_row = TM + MOE_GROUPS * MOE_ALIGN
    gather = jnp.where(p < of_group(total), src, zero_row) // MOE_ALIGN

    slot = jnp.arange(MOE_SLOTS // MOE_ALIGN, dtype=jnp.int32) * MOE_ALIGN
    inside = (slot[None, :, None] >= start[:, None, :]) & (slot[None, :, None] < (start + rows)[:, None, :])
    dst = jnp.sum(jnp.where(inside, (base[None, :] + before - start)[:, None, :], 0), axis=-1) + slot[None, :]
    scatter = jnp.where(jnp.any(inside, axis=-1), dst // MOE_ALIGN, -1)
    return gather.astype(jnp.int32), scatter.reshape(-1).astype(jnp.int32), tiles


def _experts_kernel(gather_ref, tiles_ref, h2s_ref, combs_ref, wg_ref, wu_ref, wd_ref, y_ref,
                    hbuf, cbuf, acc_ref, wgb, wub, wdb, sem):
    e = pl.program_id(1)
    s = pl.program_id(2)
    tile = 2 * pl.program_id(0) + s
    n_used = tiles_ref[2 * pl.num_programs(0)]
    real = tile < n_used
    slots = hbuf.shape[0]
    per_tile = TM // MOE_ALIGN

    def tile_copies(t, go):
        slot = t % slots
        for g in range(per_tile):
            src = pl.multiple_of(gather_ref[t * per_tile + g] * MOE_ALIGN, MOE_ALIGN)
            dst = pl.ds(g * MOE_ALIGN, MOE_ALIGN)
            go(pltpu.make_async_copy(h2s_ref.at[pl.ds(src, MOE_ALIGN)], hbuf.at[slot, dst], sem.at[0, slot]))
            go(pltpu.make_async_copy(combs_ref.at[pl.ds(src, MOE_ALIGN)], cbuf.at[slot, dst], sem.at[1, slot]))

    @pl.when(e == 0)
    def _():
        acc_ref[s] = jnp.zeros(acc_ref.shape[1:], F32)

        for t in range(2):
            @pl.when((tile == 0) & (t < n_used))
            def _(t=t):
                tile_copies(t, lambda cp: cp.start())

        @pl.when(tile + 2 < n_used)
        def _():
            tile_copies(tile + 2, lambda cp: cp.start())

        @pl.when(real)
        def _():
            tile_copies(tile, lambda cp: cp.wait())

    @pl.when(real & ((s == 0) | (tiles_ref[tile] != tiles_ref[jnp.maximum(tile - 1, 0)])))
    def _():
        wgb[...] = wg_ref[...].astype(BF16)
        wub[...] = wu_ref[...].astype(BF16)
        wdb[...] = wd_ref[...].astype(BF16)

    @pl.when(real)
    def _():
        slot = tile % slots
        h2 = hbuf[slot]
        hid = jax.nn.silu(_dot(h2, wgb[...])) * _dot(h2, wub[...])
        lane = lax.broadcasted_iota(jnp.int32, (TM, LANES), 1)
        col = MOE_GROUPS + tiles_ref[tile] * MOE_EXPERTS + e
        c = jnp.sum(jnp.where(lane == col, cbuf[slot], 0.0), axis=-1, keepdims=True)
        acc_ref[s] += _dot((hid * c).astype(BF16), wdb[...])

    @pl.when(e == pl.num_programs(1) - 1)
    def _():
        y_ref[...] = acc_ref[s]


def _experts(gather, tiles, h2s, combs, wg, wu, wd, n_gtiles, first_expert):
    d = h2s.shape[1]
    assert n_gtiles % 2 == 0
    weights = lambda shape: pl.BlockSpec(
        (None,) + shape, lambda p, e, s, ga, tl: (first_expert + tl[2 * p + s] * MOE_EXPERTS + e, 0, 0))

    def out_tile(p, e, s, ga, tl):
        return (2 * p + jnp.where(e == MOE_EXPERTS - 1, s, 0), 0)

    grid_spec = pltpu.PrefetchScalarGridSpec(
        num_scalar_prefetch=2,
        grid=(n_gtiles // 2, MOE_EXPERTS, 2),
        in_specs=[pl.BlockSpec(memory_space=pl.ANY), pl.BlockSpec(memory_space=pl.ANY),
                  weights((d, MOE_HID)), weights((d, MOE_HID)), weights((MOE_HID, d))],
        out_specs=pl.BlockSpec((TM, d), out_tile),
        scratch_shapes=[pltpu.VMEM((4, TM, d), BF16), pltpu.VMEM((4, TM, LANES), F32), pltpu.VMEM((2, TM, d), F32),
                        pltpu.VMEM((d, MOE_HID), BF16), pltpu.VMEM((d, MOE_HID), BF16), pltpu.VMEM((MOE_HID, d), BF16),
                        pltpu.SemaphoreType.DMA((2, 4))],
    )
    return pl.pallas_call(
        _experts_kernel,
        grid_spec=grid_spec,
        out_shape=jax.ShapeDtypeStruct((n_gtiles * TM, d), F32),
        compiler_params=_cparams(3),
        name="moe_experts",
    )(gather, tiles, h2s, combs, wg, wu, wd)


def _finalize_kernel(scatter_ref, yg_ref, pos_ref, xn_ref, mod_ref, fg_ref, o_ref, ys_ref, sem, *, final):
    c = pl.program_id(0)
    per_chunk = MOE_SLOTS // MOE_ALIGN

    def chunk_copies(chunk, go):
        buf = chunk % 2
        for j in range(per_chunk):
            src = scatter_ref[chunk * per_chunk + j]

            @pl.when(src >= 0)
            def _(j=j, src=src):
                rows = pl.ds(pl.multiple_of(src * MOE_ALIGN, MOE_ALIGN), MOE_ALIGN)
                go(pltpu.make_async_copy(yg_ref.at[rows], ys_ref.at[buf, pl.ds(j * MOE_ALIGN, MOE_ALIGN)],
                                         sem.at[buf]))

    @pl.when(c == 0)
    def _():
        ys_ref[...] = jnp.zeros_like(ys_ref)
        chunk_copies(c, lambda cp: cp.start())

    @pl.when(c + 1 < pl.num_programs(0))
    def _():
        chunk_copies(c + 1, lambda cp: cp.start())

    chunk_copies(c, lambda cp: cp.wait())

    tm = xn_ref.shape[0]
    y = ys_ref[c % 2]
    hi = y.astype(BF16)
    rest = y - hi.astype(F32)
    mid = rest.astype(BF16)
    lo = (rest - mid.astype(F32)).astype(BF16)
    slot = lax.broadcasted_iota(jnp.int32, (tm, MOE_SLOTS), 1).astype(F32)
    take_t = jnp.where(_rep(pos_ref[...], MOE_SLOTS // LANES, 1) == slot, 1.0, 0.0).astype(BF16)
    y_tok = (_dot(take_t, hi) + _dot(take_t, mid)) + _dot(take_t, lo)
    xo = xn_ref[...] + mod_ref[5:6, :] * y_tok
    if final:
        xo = _rms(xo, fg_ref[...])
    o_ref[...] = xo


def _finalize(scatter, yg, pos, xn, mod_l, fg, seq, final):
    t, d = xn.shape
    tps = seq // TM
    grid_spec = pltpu.PrefetchScalarGridSpec(
        num_scalar_prefetch=1,
        grid=(t // TM,),
        in_specs=[pl.BlockSpec(memory_space=pl.ANY),
                  pl.BlockSpec((TM, LANES), lambda i, tb: (i, 0)),
                  pl.BlockSpec((TM, d), lambda i, tb: (i, 0)),
                  pl.BlockSpec((None, 6, d), lambda i, tb: (i // tps, 0, 0)),
                  pl.BlockSpec((1, d), lambda i, tb: (0, 0))],
        out_specs=pl.BlockSpec((TM, d), lambda i, tb: (i, 0)),
        scratch_shapes=[pltpu.VMEM((2, MOE_SLOTS, d), F32), pltpu.SemaphoreType.DMA((2,))],
    )
    return pl.pallas_call(
        functools.partial(_finalize_kernel, final=final),
        grid_spec=grid_spec,
        out_shape=jax.ShapeDtypeStruct((t, d), F32),
        compiler_params=_cparams(1),
        name="moe_finalize",
    )(scatter, yg, pos, xn, mod_l, fg)


def _permute_w_in(w):
    gate0 = D_LRU * 2 + D_ATT + 6 * LANES
    cv0 = gate0 + 3 * N_HEADS
    w = w.astype(BF16)
    pad = jnp.zeros((w.shape[0], IN_PAD - w.shape[1]), BF16)
    return jnp.concatenate([w[:, :gate0], w[:, cv0:], w[:, gate0:cv0], pad], axis=1)


def _router_weights(rg_w, rg_b, re_w, re_b):
    d = rg_w.shape[0]
    ne = MOE_GROUPS * MOE_EXPERTS
    w = jnp.concatenate([rg_w, jnp.transpose(re_w, (1, 0, 2)).reshape(d, ne),
                         jnp.zeros((d, LANES - MOE_GROUPS - ne), F32)], axis=1)
    b = jnp.concatenate([rg_b, re_b.reshape(ne), jnp.zeros((LANES - MOE_GROUPS - ne,), F32)])[None, :]
    w_hi = w.astype(BF16)
    w_mid = (w - w_hi.astype(F32)).astype(BF16)
    return jnp.stack([w_hi, w_mid]), b


def kernel(x, c, positions, ada_w, ada_b, norm1_g, norm2_g, w_in, lru_conv_w, lru_conv_b, lru_wa, lru_ba, lru_wx, lru_bx, lru_lambda, cmp_k_w1, cmp_k_w2, cmp_v_w1, cmp_v_w2, cmp_pos_k, cmp_pos_v, cnv_dw_w, cnv_dw_b, cnv_ln_g, cnv_ln_b, out_norm_g, w_out, moe_rg_w, moe_rg_b, moe_re_w, moe_re_b, moe_w_gate, moe_w_up, moe_w_down, final_norm_g):
    batch, seq, d = x.shape
    depth = ada_w.shape[0]
    assert d == D_MODEL and seq == N_SLC * SLC_LEN and seq % KEY_TILE == 0
    t = batch * seq
    ne = MOE_GROUPS * MOE_EXPERTS
    n_chunks = t // TM
    n_gtiles = n_chunks + -(-n_chunks * MOE_GROUPS * (MOE_ALIGN - 1) // TM) + MOE_GROUPS

    cos, sin = _rope_tables(positions)
    mod = _modulation(c, ada_w, ada_b)
    x2 = x.reshape(t, d)
    fg = final_norm_g[None, :]
    for l in range(depth):
        (xl, gl, q, kc, vc, ksa, vsa, kw, vwa, gate, cv) = _inproj(
            x2, mod[l], norm1_g[l][None, :], _permute_w_in(w_in[l]), cos, sin, seq)
        y_lru, y_cnv = _mixer(xl, gl, cv, lru_conv_w[l], lru_conv_b[l], lru_wa[l], lru_ba[l], lru_wx[l], lru_bx[l],
                              lru_lambda[l], cnv_dw_w[l], cnv_dw_b[l], cnv_ln_g[l], cnv_ln_b[l], batch, seq)
        kcmp, vcmp = _compress(kc, vc, cmp_pos_k[l], cmp_k_w1[l], cmp_k_w2[l], cmp_pos_v[l], cmp_v_w1[l], cmp_v_w2[l],
                               batch, seq)
        y_att = _attention(q, kcmp, vcmp, ksa, vsa, kw, vwa, gate, batch, seq)
        rw, rb = _router_weights(moe_rg_w[l], moe_rg_b[l], moe_re_w[l], moe_re_b[l])
        xn, h2, comb, pos, seg = _outproj(x2, y_lru, y_att, y_cnv, out_norm_g[l][None, :], w_out[l].astype(BF16),
                                          mod[l], norm2_g[l][None, :], rw, rb, seq)
        gather, scatter, tiles = _moe_plan(seg, n_chunks, n_gtiles)
        yg = _experts(gather, tiles, h2, comb,
                      moe_w_gate.reshape(depth * ne, d, MOE_HID), moe_w_up.reshape(depth * ne, d, MOE_HID),
                      moe_w_down.reshape(depth * ne, MOE_HID, d), n_gtiles, l * ne)
        x2 = _finalize(scatter, yg, pos, xn, mod[l], fg, seq, final=(l == depth - 1))
    return x2.reshape(batch, seq, d)
```

```python
import functools

import jax
import jax.numpy as jnp
from jax import lax
from jax.experimental import pallas as pl
from jax.experimental.pallas import tpu as pltpu

F32 = jnp.float32
BF16 = jnp.bfloat16

D_MODEL = 1024
D_LRU = 256
D_ATT = 512
D_CONV = 256
LRU_BLOCKS = 4
LRU_CONV_W = 4
LRU_C = 8.0
HEAD_DIM = 64
N_HEADS = 8
N_KV = 2
GQA = 4
ROPE_THETA = 10000.0
CMP_LEN = 32
CMP_STRIDE = 16
SLC_LEN = 64
SLC_TOPN = 16
WINDOW = 512
CONV_K = 31
MOE_GROUPS = 4
MOE_EXPERTS = 4
MOE_HID = 512
EPS = 1e-6
NEG = -1e30
FORCE = 1e9
LOG2_E = 1.4426950408889634

LANES = 128
Q_BLOCK = 128
KEY_TILE = 512
N_SLC = 128
TM = 512
MOE_ALIGN_LOG2 = 4
MOE_ALIGN = 1 << MOE_ALIGN_LOG2
MOE_SLOTS = 640
TC = 512
SCAN_ROWS = 64
LRU_TAIL = 8
CNV_TAIL = 32
IN_PAD = 2432
VMEM_LIMIT = 56 * 1024 * 1024


def _cparams(n_axes, vmem=VMEM_LIMIT):
    return pltpu.CompilerParams(dimension_semantics=("arbitrary",) * n_axes, vmem_limit_bytes=vmem)


def _dot(a, b):
    return jnp.dot(a, b, preferred_element_type=F32)


def _dot_nt(a, b):
    return lax.dot_general(a, b, (((1,), (1,)), ((), ())), preferred_element_type=F32)


def _rep(v, n, axis):
    return jnp.concatenate([v] * n, axis=axis)


def _rms(v, g):
    return v * lax.rsqrt(jnp.mean(v * v, axis=-1, keepdims=True) + EPS) * g


def _rope_kernel(pos_ref, inv_ref, sign_ref, cos_ref, sin_ref):
    ang = pos_ref[...].astype(F32) * inv_ref[...]
    cos_ref[...] = jnp.cos(ang)
    sin_ref[...] = jnp.sin(ang) * sign_ref[...]


def _rope_tables(positions):
    t = positions.size
    inv = ROPE_THETA ** (-jnp.arange(0, HEAD_DIM, 2, dtype=F32) / HEAD_DIM)
    inv128 = jnp.tile(inv, 4)[None, :]
    sign128 = jnp.tile(jnp.concatenate([-jnp.ones((32,), F32), jnp.ones((32,), F32)]), 2)[None, :]
    tr = 1024
    return pl.pallas_call(
        _rope_kernel,
        grid=(t // tr,),
        in_specs=[pl.BlockSpec((tr, 1), lambda i: (i, 0)),
                  pl.BlockSpec((1, LANES), lambda i: (0, 0)),
                  pl.BlockSpec((1, LANES), lambda i: (0, 0))],
        out_specs=[pl.BlockSpec((tr, LANES), lambda i: (i, 0))] * 2,
        out_shape=[jax.ShapeDtypeStruct((t, LANES), F32)] * 2,
        compiler_params=_cparams(1),
        name="rope_tables",
    )(positions.reshape(t, 1), inv128, sign128)


def _mod_kernel(c_ref, w_ref, b_ref, o_ref):
    sc = jax.nn.silu(c_ref[...])
    o_ref[...] = _dot(sc.astype(BF16), w_ref[...].astype(BF16)) + b_ref[...]


def _modulation(c, ada_w, ada_b):
    nl, d, n6 = ada_w.shape
    b = c.shape[0]
    rows = 16
    cp = jnp.zeros((rows, d), F32).at[:b].set(c)
    tn = 1536
    out = pl.pallas_call(
        _mod_kernel,
        grid=(nl, n6 // tn),
        in_specs=[pl.BlockSpec((rows, d), lambda l, j: (0, 0)),
                  pl.BlockSpec((None, d, tn), lambda l, j: (l, 0, j)),
                  pl.BlockSpec((None, 1, tn), lambda l, j: (l, 0, j))],
        out_specs=pl.BlockSpec((None, rows, tn), lambda l, j: (l, 0, j)),
        out_shape=jax.ShapeDtypeStruct((nl, rows, n6), F32),
        compiler_params=_cparams(2),
        name="adaln_mod",
    )(cp, ada_w, ada_b.reshape(nl, 1, n6))
    return out[:, :b].reshape(nl, b, 6, d)


def _inproj_kernel(x_ref, mod_ref, g_ref, w_ref, cos_ref, sin_ref,
                   xl_ref, gl_ref, q_ref, kc_ref, vc_ref, ksa_ref, vsa_ref, kw_ref, vwa_ref, gate_ref, cv_ref,
                   *, tiles_per_seq):
    tm = x_ref.shape[0]
    x = x_ref[...]
    h = _rms(x, g_ref[...]) * (1.0 + mod_ref[1:2, :]) + mod_ref[0:1, :]
    p = _dot(h.astype(BF16), w_ref[...])

    cos = cos_ref[...]
    sin = sin_ref[...]
    lane = lax.broadcasted_iota(jnp.int32, (tm, LANES), 1)
    first_half = (lane & 63) < 32

    def rope(v):
        rot = jnp.where(first_half, pltpu.roll(v, 96, 1), pltpu.roll(v, 32, 1))
        return v * cos + rot * sin

    xl_ref[...] = p[:, 0:256]
    gl_ref[...] = p[:, 256:512]
    scale = HEAD_DIM ** -0.5 * LOG2_E
    low = lane < 64
    for m in range(4):
        slab = rope(p[:, 512 + 128 * m:640 + 128 * m]) * scale
        swapped = pltpu.roll(slab, 64, 1)
        for hh in range(2):
            head = 2 * m + hh
            kv = head // GQA
            src = slab if hh == kv else swapped
            keep = low if kv == 0 else jnp.logical_not(low)
            q_ref[:, head * LANES:(head + 1) * LANES] = jnp.where(keep, src, 0.0).astype(BF16)
    kc_ref[...] = rope(p[:, 1024:1152])
    vc_ref[...] = p[:, 1152:1280]
    row = lax.broadcasted_iota(jnp.int32, (tm, LANES), 0)
    s_base = (pl.program_id(0) % tiles_per_seq) * tm
    onehot = jnp.where(lane == ((s_base + row) >> 6), 1.0, 0.0).astype(BF16)
    ones = jnp.ones((tm, LANES), BF16)
    ksa_ref[:, 0:LANES] = rope(p[:, 1280:1408]).astype(BF16)
    ksa_ref[:, LANES:2 * LANES] = onehot
    vsa_ref[:, 0:LANES] = p[:, 1408:1536].astype(BF16)
    vsa_ref[:, LANES:2 * LANES] = ones
    kw_ref[...] = rope(p[:, 1536:1664]).astype(BF16)
    vwa_ref[:, 0:LANES] = p[:, 1664:1792].astype(BF16)
    vwa_ref[:, LANES:2 * LANES] = ones
    cv_ref[...] = p[:, 1792:2304]
    gate_ref[...] = jax.nn.sigmoid(p[:, 2304:2432])


def _inproj(x2, mod_l, g, w_p, cos, sin, seq):
    t, d = x2.shape
    tps = seq // TM
    row = lambda w: pl.BlockSpec((TM, w), lambda i: (i, 0))
    outs = [(256, F32), (256, F32), (N_HEADS * LANES, BF16), (LANES, F32), (LANES, F32), (2 * LANES, BF16),
            (2 * LANES, BF16), (LANES, BF16), (2 * LANES, BF16), (LANES, F32), (2 * D_CONV, F32)]
    return pl.pallas_call(
        functools.partial(_inproj_kernel, tiles_per_seq=tps),
        grid=(t // TM,),
        in_specs=[row(d),
                  pl.BlockSpec((None, 6, d), lambda i: (i // tps, 0, 0)),
                  pl.BlockSpec((1, d), lambda i: (0, 0)),
                  pl.BlockSpec((d, IN_PAD), lambda i: (0, 0)),
                  row(LANES), row(LANES)],
        out_specs=[row(w) for w, _ in outs],
        out_shape=[jax.ShapeDtypeStruct((t, w), dt) for w, dt in outs],
        compiler_params=_cparams(1),
        name="in_proj",
    )(x2, mod_l, g, w_p, cos, sin)


def _shift_rows(v, s, fill, row):
    return jnp.where(row < s, fill, pltpu.roll(v, s, 0))


def _causal_taps(ext, w_ref, bias, width, tail, tc):
    acc = bias
    for b in range(min(8, width)):
        shifted = pltpu.roll(ext, b, 0) if b else ext
        for k in range(width):
            back = width - 1 - k
            if back % 8 == b:
                start = tail - (back - b)
                acc = acc + w_ref[k:k + 1, :] * shifted[start:start + tc, :]
    return acc


def _mixer_kernel(xl_ref, gl_ref, cv_ref, lcw_ref, lcb_ref, wa_ref, ba_ref, wx_ref, bx_ref, lam_ref,
                  cw_ref, cb_ref, lng_ref, lnb_ref, ylru_ref, ycnv_ref,
                  extl, extc, abuf, ubuf, hcar):
    tc = xl_ref.shape[0]

    @pl.when(pl.program_id(1) == 0)
    def _():
        extl[0:LRU_TAIL, :] = jnp.zeros((LRU_TAIL, D_LRU), F32)
        extc[0:CNV_TAIL, :] = jnp.zeros((CNV_TAIL, D_CONV), F32)
        hcar[...] = jnp.zeros_like(hcar)

    xl = xl_ref[...]
    extl[LRU_TAIL:LRU_TAIL + tc, :] = xl
    xc = _causal_taps(extl[...], lcw_ref, lcb_ref[...], LRU_CONV_W, LRU_TAIL, tc)
    extl[0:LRU_TAIL, :] = xl[tc - LRU_TAIL:tc, :]
    xcb = xc.astype(BF16)
    r = jax.nn.sigmoid(_dot(xcb, wa_ref[...]) + ba_ref[...])
    gi = jax.nn.sigmoid(_dot(xcb, wx_ref[...]) + bx_ref[...])
    log_a = LRU_C * r * jax.nn.log_sigmoid(lam_ref[...])
    th = jnp.tanh(log_a)
    one_minus_a2 = -2.0 * th / (1.0 - th)
    abuf[...] = jnp.exp(log_a)
    ubuf[...] = jnp.sqrt(one_minus_a2) * (gi * xc)

    row = lax.broadcasted_iota(jnp.int32, (SCAN_ROWS, D_LRU), 0)

    def scan_chunk(c, h_prev):
        off = pl.multiple_of(c * SCAN_ROWS, SCAN_ROWS)
        a = abuf[pl.ds(off, SCAN_ROWS), :]
        b = ubuf[pl.ds(off, SCAN_ROWS), :]
        s = 1
        while s < SCAN_ROWS:
            b = b + a * _shift_rows(b, s, 0.0, row)
            a = a * _shift_rows(a, s, 1.0, row)
            s *= 2
        h = b + a * h_prev
        ylru_ref[pl.ds(off, SCAN_ROWS), :] = h * jax.nn.gelu(gl_ref[pl.ds(off, SCAN_ROWS), :])
        return h[SCAN_ROWS - 1:SCAN_ROWS, :]

    hcar[0:1, :] = lax.fori_loop(0, tc // SCAN_ROWS, scan_chunk, hcar[0:1, :])

    u = cv_ref[...]
    v = u[:, 0:D_CONV] * jax.nn.sigmoid(u[:, D_CONV:2 * D_CONV])
    extc[CNV_TAIL:CNV_TAIL + tc, :] = v
    acc = _causal_taps(extc[...], cw_ref, cb_ref[...], CONV_K, CNV_TAIL, tc)
    extc[0:CNV_TAIL, :] = v[tc - CNV_TAIL:tc, :]
    mu = jnp.mean(acc, axis=-1, keepdims=True)
    var = jnp.mean(jnp.square(acc - mu), axis=-1, keepdims=True)
    y = (acc - mu) * lax.rsqrt(var + EPS) * lng_ref[...] + lnb_ref[...]
    ycnv_ref[...] = jax.nn.silu(y)


def _block_diag(w):
    n, rows, cols = w.shape
    eye = jnp.eye(n, dtype=bool)
    return jnp.where(eye[:, None, :, None], w[:, :, None, :], 0.0).reshape(n * rows, n * cols)


def _mixer(xl, gl, cv, lcw, lcb, wa, ba, wx, bx, lam, cw, cb, lng, lnb, batch, seq):
    seqrow = lambda w: pl.BlockSpec((None, TC, w), lambda b, t: (b, t, 0))
    full = lambda a: pl.BlockSpec(a.shape, lambda b, t: (0,) * a.ndim)
    small = [lcw, lcb[None, :], _block_diag(wa).astype(BF16), ba[None, :], _block_diag(wx).astype(BF16), bx[None, :],
             lam[None, :], jnp.zeros((32, D_CONV), F32).at[:CONV_K].set(cw), cb[None, :], lng[None, :], lnb[None, :]]
    return pl.pallas_call(
        _mixer_kernel,
        grid=(batch, seq // TC),
        in_specs=[seqrow(D_LRU), seqrow(D_LRU), seqrow(2 * D_CONV)] + [full(a) for a in small],
        out_specs=[seqrow(D_LRU), seqrow(D_CONV)],
        out_shape=[jax.ShapeDtypeStruct((batch, seq, D_LRU), F32), jax.ShapeDtypeStruct((batch, seq, D_CONV), F32)],
        scratch_shapes=[pltpu.VMEM((TC + LRU_TAIL, D_LRU), F32), pltpu.VMEM((TC + CNV_TAIL, D_CONV), F32),
                        pltpu.VMEM((TC, D_LRU), F32), pltpu.VMEM((TC, D_LRU), F32), pltpu.VMEM((8, D_LRU), F32)],
        compiler_params=_cparams(2),
        name="mixer_stream",
    )(xl.reshape(batch, seq, D_LRU), gl.reshape(batch, seq, D_LRU), cv.reshape(batch, seq, 2 * D_CONV), *small)


def _cmp_kernel(kc_ref, vc_ref, posk_ref, posv_ref, wk1_ref, wk2_ref, wv1_ref, wv2_ref, ko_ref, vo_ref, nat_ref):
    nh = ko_ref.shape[0]
    quarter = nh // 4

    def one(t_ref, pos_ref, w1_ref, w2_ref, o_ref):
        first = jnp.zeros((nh, w1_ref.shape[2]), F32)
        second = jnp.zeros((nh, w1_ref.shape[2]), F32)
        for l in range(CMP_STRIDE):
            x = t_ref[pl.ds(l, nh, stride=CMP_STRIDE), :]
            first = first + _dot((x + pos_ref[l:l + 1, :]).astype(BF16), w1_ref[l])
            second = second + _dot((x + pos_ref[CMP_STRIDE + l:CMP_STRIDE + l + 1, :]).astype(BF16),
                                   w1_ref[CMP_STRIDE + l])
        hid = jax.nn.gelu(first + pltpu.roll(second, nh - 1, 0))
        nat_ref[...] = _dot(hid.astype(BF16), w2_ref[...])
        for r in range(4):
            o_ref[r * quarter:(r + 1) * quarter, :] = nat_ref[pl.ds(r, quarter, stride=4), :].astype(o_ref.dtype)

    one(kc_ref, posk_ref, wk1_ref, wk2_ref, ko_ref)
    one(vc_ref, posv_ref, wv1_ref, wv2_ref, vo_ref)


def _cmp_weights(pos, w1, w2):
    hid = w1.shape[1]
    eye = jnp.eye(N_KV, dtype=bool)
    w1 = w1.astype(BF16).reshape(CMP_LEN, HEAD_DIM, hid)
    w1x = jnp.where(eye[None, :, None, :, None], w1[:, None, :, None, :], 0.0)
    w1x = w1x.reshape(CMP_LEN, N_KV * HEAD_DIM, N_KV * hid)
    w2x = _block_diag(jnp.broadcast_to(w2.astype(BF16), (N_KV,) + w2.shape))
    return jnp.tile(pos, (1, N_KV)), w1x, w2x


def _compress(kc, vc, pos_k, wk1, wk2, pos_v, wv1, wv2, batch, seq):
    nh = seq // CMP_STRIDE
    wide = N_KV * HEAD_DIM
    pk, wk1x, wk2x = _cmp_weights(pos_k, wk1, wk2)
    pv, wv1x, wv2x = _cmp_weights(pos_v, wv1, wv2)
    full = lambda a: pl.BlockSpec(a.shape, lambda b: (0,) * a.ndim)
    seqs = pl.BlockSpec((None, seq, wide), lambda b: (b, 0, 0))
    outs = pl.BlockSpec((None, nh, wide), lambda b: (b, 0, 0))
    return pl.pallas_call(
        _cmp_kernel,
        grid=(batch,),
        in_specs=[seqs, seqs, full(pk), full(pv), full(wk1x), full(wk2x), full(wv1x), full(wv2x)],
        out_specs=[outs, outs],
        out_shape=[jax.ShapeDtypeStruct((batch, nh, wide), BF16)] * 2,
        scratch_shapes=[pltpu.VMEM((nh, wide), F32)],
        compiler_params=_cparams(1),
        name="nsa_compress",
    )(kc.reshape(batch, seq, wide), vc.reshape(batch, seq, wide), pk, pv, wk1x, wk2x, wv1x, wv2x)


def _attn_kernel(q_ref, kcmp_ref, vcmp_ref, ksa_ref, vsa_ref, kw_ref, vwa_ref, gate_ref, o_ref,
                 qa_ref, part_ref, acc_ref, m_ref, accw_ref, mw_ref, sa_ref, sb_ref):
    rows = GQA * Q_BLOCK
    reps = KEY_TILE // LANES
    qb = pl.program_id(1)
    s0 = qb * Q_BLOCK
    kd = s0 // KEY_TILE
    off_d = pl.multiple_of(kd * KEY_TILE, KEY_TILE)
    off_p = pl.multiple_of(jnp.maximum(kd - 1, 0) * KEY_TILE, KEY_TILE)
    d0 = s0 - off_d

    lane = lax.broadcasted_iota(jnp.int32, (Q_BLOCK, LANES), 1)
    sub = lax.broadcasted_iota(jnp.int32, (Q_BLOCK, LANES), 0)
    r_q = lax.broadcasted_iota(jnp.int32, (Q_BLOCK, KEY_TILE), 0)
    c_k = lax.broadcasted_iota(jnp.int32, (Q_BLOCK, KEY_TILE), 1)
    rel = c_k - r_q
    gate = gate_ref[...]

    def tile_rows(plane):
        return _rep(plane, GQA, 0)

    def flash_step(acc, m, kv, s, v_tile):
        m_old = m[kv]
        m_new = jnp.maximum(m_old, jnp.broadcast_to(jnp.max(s, axis=-1, keepdims=True), m_old.shape))
        p = jnp.exp2(s - _rep(m_new, reps, 1))
        acc[kv] = acc[kv] * _rep(jnp.exp2(m_old - m_new), 2, 1) + _dot(p.astype(BF16), v_tile)
        m[kv] = m_new

    def flash_result(acc, kv):
        a = acc[kv]
        return a[:, 0:LANES] / a[:, LANES:2 * LANES]

    q128 = [jnp.concatenate([q_ref[:, (kv * GQA + g) * LANES:(kv * GQA + g + 1) * LANES] for g in range(GQA)], axis=0)
            for kv in range(N_KV)]

    kcmp = kcmp_ref[...]
    vcmp = vcmp_ref[...]
    cmp_end = SLC_LEN * (c_k & (N_SLC - 1)) + CMP_STRIDE * (c_k >> 7) + (CMP_LEN - 1)
    cmp_bias = tile_rows(jnp.where(cmp_end - r_q <= s0, 0.0, NEG))
    has_cmp = tile_rows(jnp.where(s0 + sub >= CMP_LEN - 1, 1.0, 0.0))
    o_cmp, imp_t = [], []
    for kv in range(N_KV):
        s_c = _dot_nt(q128[kv], kcmp) + cmp_bias
        mb = jnp.broadcast_to(jnp.max(s_c, axis=-1, keepdims=True), (rows, LANES))
        e = jnp.exp2(s_c - _rep(mb, reps, 1))
        lb = jnp.broadcast_to(jnp.sum(e, axis=-1, keepdims=True), (rows, LANES))
        inv = has_cmp / lb
        o_cmp.append(_dot(e.astype(BF16), vcmp) * inv)
        e4 = (e[:, 0:N_SLC] + e[:, N_SLC:2 * N_SLC] + e[:, 2 * N_SLC:3 * N_SLC] + e[:, 3 * N_SLC:4 * N_SLC]) * inv
        imp = e4[0:Q_BLOCK] + e4[Q_BLOCK:2 * Q_BLOCK] + e4[2 * Q_BLOCK:3 * Q_BLOCK] + e4[3 * Q_BLOCK:4 * Q_BLOCK]
        imp_t.append(imp.T)

    accw_ref[...] = jnp.zeros_like(accw_ref)
    mw_ref[...] = jnp.full_like(mw_ref, NEG)
    prev_bias = tile_rows(jnp.where(rel > jnp.where(kd > 0, d0, KEY_TILE), 0.0, NEG))
    diag_bias = tile_rows(jnp.where(rel <= d0, 0.0, NEG))
    kw_p, vw_p = kw_ref[pl.ds(off_p, KEY_TILE), :], vwa_ref[pl.ds(off_p, KEY_TILE), :]
    kw_d, vw_d = kw_ref[pl.ds(off_d, KEY_TILE), :], vwa_ref[pl.ds(off_d, KEY_TILE), :]
    for kv in range(N_KV):
        flash_step(accw_ref, mw_ref, kv, _dot_nt(q128[kv], kw_p) + prev_bias, vw_p)
    for kv in range(N_KV):
        flash_step(accw_ref, mw_ref, kv, _dot_nt(q128[kv], kw_d) + diag_bias, vw_d)
    for kv in range(N_KV):
        o_win = flash_result(accw_ref, kv)
        for g in range(GQA):
            col = 3 * (kv * GQA + g)
            rs = slice(g * Q_BLOCK, (g + 1) * Q_BLOCK)
            part_ref[kv, rs, :] = gate[:, col:col + 1] * o_cmp[kv][rs] + gate[:, col + 2:col + 3] * o_win[rs]

    blk = lax.broadcasted_iota(jnp.int32, (N_SLC, Q_BLOCK), 0)
    t_q = s0 + lax.broadcasted_iota(jnp.int32, (N_SLC, Q_BLOCK), 1)
    forced = (blk == 0) | (blk == (t_q >> 6))
    val = [jnp.where(forced, -jnp.inf, jnp.where(blk * SLC_LEN <= t_q, imp_t[kv], -1.0)) for kv in range(N_KV)]
    blk_f = blk.astype(F32)
    for _ in range(SLC_TOPN - 2):
        for kv in range(N_KV):
            mx = jnp.max(val[kv], axis=0, keepdims=True)
            first = jnp.min(jnp.where(val[kv] == mx, blk_f, float(N_SLC)), axis=0, keepdims=True)
            val[kv] = jnp.where(blk_f == first, -jnp.inf, val[kv])
    for kv in range(N_KV):
        bias = jnp.where(val[kv].T == -jnp.inf, 0.0, NEG).astype(BF16)
        qa_ref[kv] = jnp.concatenate([q128[kv], jnp.concatenate([bias] * GQA, axis=0)], axis=1)

    acc_ref[...] = jnp.zeros_like(acc_ref)
    m_ref[...] = jnp.full_like(m_ref, NEG)

    def scores_into(dst, kt):
        k_t = ksa_ref[pl.ds(pl.multiple_of(kt * KEY_TILE, KEY_TILE), KEY_TILE), :]
        for kv in range(N_KV):
            dst[kv] = _dot_nt(qa_ref[kv], k_t)

    def consume(src, kt, bias):
        v_t = vsa_ref[pl.ds(pl.multiple_of(kt * KEY_TILE, KEY_TILE), KEY_TILE), :]
        for kv in range(N_KV):
            flash_step(acc_ref, m_ref, kv, src[kv] if bias is None else src[kv] + bias, v_t)

    scores_into(sa_ref, 0)

    def slc_pairs(first, count):
        for i in range(count):
            a = first + 2 * i
            scores_into(sb_ref, a + 1)
            consume(sa_ref, a, None)
            scores_into(sa_ref, a + 2)
            consume(sb_ref, a + 1, None)

    def slc_quad(j, carry):
        slc_pairs(4 * j, 2)
        return carry

    def slc_pair(j, carry):
        slc_pairs(4 * n_quads + 2 * j, 1)
        return carry

    n_quads = kd // 4
    lax.fori_loop(0, n_quads, slc_quad, 0)
    lax.fori_loop(0, (kd // 2) & 1, slc_pair, 0)
    causal_bias = tile_rows(jnp.where(rel <= d0, 0.0, NEG))
    kd_odd = (kd & 1) == 1

    @pl.when(kd_odd)
    def _():
        scores_into(sb_ref, kd)
        consume(sa_ref, kd - 1, None)
        consume(sb_ref, kd, causal_bias)

    @pl.when(jnp.logical_not(kd_odd))
    def _():
        consume(sa_ref, kd, causal_bias)

    for kv in range(N_KV):
        o_slc = flash_result(acc_ref, kv)
        pieces = []
        for g in range(GQA):
            col = 3 * (kv * GQA + g)
            rs = slice(g * Q_BLOCK, (g + 1) * Q_BLOCK)
            pieces.append(part_ref[kv, rs, :] + gate[:, col + 1:col + 2] * o_slc[rs])
        for mm in range(2):
            even, odd = pieces[2 * mm], pieces[2 * mm + 1]
            if kv == 0:
                odd = pltpu.roll(odd, 64, 1)
            else:
                even = pltpu.roll(even, 64, 1)
            slab = kv * 2 + mm
            o_ref[:, slab * LANES:(slab + 1) * LANES] = jnp.where(lane < 64, even, odd)


def _attention(q, kcmp, vcmp, ksa, vsa, kw, vwa, gate, batch, seq):
    nh = kcmp.shape[1]
    assert nh == KEY_TILE
    rows = GQA * Q_BLOCK
    res = lambda w: pl.BlockSpec((None, seq, w), lambda b, i: (b, 0, 0))
    blk = lambda w: pl.BlockSpec((None, Q_BLOCK, w), lambda b, i: (b, i, 0))
    r3 = lambda a, w: a.reshape(batch, seq, w)
    return pl.pallas_call(
        _attn_kernel,
        grid=(batch, seq // Q_BLOCK),
        in_specs=[blk(N_HEADS * LANES),
                  pl.BlockSpec((None, nh, LANES), lambda b, i: (b, 0, 0)),
                  pl.BlockSpec((None, nh, LANES), lambda b, i: (b, 0, 0)),
                  res(2 * LANES), res(2 * LANES), res(LANES), res(2 * LANES), blk(LANES)],
        out_specs=blk(D_ATT),
        out_shape=jax.ShapeDtypeStruct((batch, seq, D_ATT), F32),
        scratch_shapes=[pltpu.VMEM((N_KV, rows, 2 * LANES), BF16), pltpu.VMEM((N_KV, rows, LANES), F32),
                        pltpu.VMEM((N_KV, rows, 2 * LANES), F32), pltpu.VMEM((N_KV, rows, LANES), F32),
                        pltpu.VMEM((N_KV, rows, 2 * LANES), F32), pltpu.VMEM((N_KV, rows, LANES), F32),
                        pltpu.VMEM((N_KV, rows, KEY_TILE), F32), pltpu.VMEM((N_KV, rows, KEY_TILE), F32)],
        compiler_params=_cparams(2),
        name="nsa_attention",
    )(r3(q, N_HEADS * LANES), kcmp, vcmp, r3(ksa, 2 * LANES), r3(vsa, 2 * LANES), r3(kw, LANES), r3(vwa, 2 * LANES),
      r3(gate, LANES))


def _outproj_kernel(x_ref, yl_ref, ya_ref, yc_ref, gn_ref, w_ref, mod_ref, n2_ref, rw_ref, rb_ref,
                    xn_ref, h2_ref, comb_ref, pos_ref, seg_ref):
    tm = x_ref.shape[0]
    a, b = D_LRU, D_LRU + D_ATT
    yl = _rms(yl_ref[...], gn_ref[:, 0:a]).astype(BF16)
    ya = _rms(ya_ref[...], gn_ref[:, a:b]).astype(BF16)
    yc = _rms(yc_ref[...], gn_ref[:, b:D_MODEL]).astype(BF16)
    y = _dot(yl, w_ref[0:a, :]) + _dot(ya, w_ref[a:b, :]) + _dot(yc, w_ref[b:D_MODEL, :])
    xn = x_ref[...] + mod_ref[2:3, :] * y
    xn_ref[...] = xn
    h2 = _rms(xn, n2_ref[...]) * (1.0 + mod_ref[4:5, :]) + mod_ref[3:4, :]

    h_hi = h2.astype(BF16)
    h_mid = (h2 - h_hi.astype(F32)).astype(BF16)
    logit = _dot(h_hi, rw_ref[0]) + (_dot(h_hi, rw_ref[1]) + _dot(h_mid, rw_ref[0])) + rb_ref[...]
    lane = lax.broadcasted_iota(jnp.int32, (tm, LANES), 1)
    ninf = -jnp.inf
    is_g = lane < MOE_GROUPS
    is_e = (lane >= MOE_GROUPS) & (lane < MOE_GROUPS + MOE_GROUPS * MOE_EXPERTS)
    lg_max = jnp.max(jnp.where(is_g, logit, ninf), axis=-1, keepdims=True)
    g_star = jnp.min(jnp.where(is_g & (logit == lg_max), lane, LANES), axis=-1, keepdims=True)
    pg_star = 1.0 / jnp.sum(jnp.where(is_g, jnp.exp(logit - lg_max), 0.0), axis=-1, keepdims=True)
    in_grp = is_e & (((lane - MOE_GROUPS) >> 2) == g_star)
    v1 = jnp.max(jnp.where(in_grp, logit, ninf), axis=-1, keepdims=True)
    i1 = jnp.min(jnp.where(in_grp & (logit == v1), lane, LANES), axis=-1, keepdims=True)
    rest = in_grp & (lane != i1)
    v2 = jnp.max(jnp.where(rest, logit, ninf), axis=-1, keepdims=True)
    i2 = jnp.min(jnp.where(rest & (logit == v2), lane, LANES), axis=-1, keepdims=True)
    d = jnp.exp(v2 - v1)
    pe1 = 1.0 / (1.0 + d)
    pe2 = d / (1.0 + d)
    comb = jnp.where(lane == i1, pe1, jnp.where(lane == i2, pe2, 0.0)) * pg_star

    onehot = jnp.where(lane == g_star, 1.0, 0.0)
    r_i = lax.broadcasted_iota(jnp.int32, (tm, tm), 0)
    c_i = lax.broadcasted_iota(jnp.int32, (tm, tm), 1)
    earlier = jnp.where(c_i < r_i, 1.0, 0.0).astype(BF16)
    rank = jnp.sum(onehot * _dot(earlier, onehot.astype(BF16)), axis=-1, keepdims=True)
    n_g = jnp.sum(onehot, axis=0, keepdims=True)
    n_pad = (((n_g.astype(jnp.int32) + (MOE_ALIGN - 1)) >> MOE_ALIGN_LOG2) << MOE_ALIGN_LOG2).astype(F32)
    lane1 = lax.broadcasted_iota(jnp.int32, (1, LANES), 1)
    start = jnp.zeros((1, LANES), F32)
    below = jnp.zeros((1, 1), F32)
    for g in range(1, MOE_GROUPS):
        below = below + jnp.sum(jnp.where(lane1 == g - 1, n_pad, 0.0), axis=-1, keepdims=True)
        start = start + jnp.where(lane1 == g, below, 0.0)
    pos = jnp.sum(onehot * start, axis=-1, keepdims=True) + rank
    pos_rep = jnp.broadcast_to(pos, (tm, LANES))
    slot = lax.broadcasted_iota(jnp.int32, (tm, MOE_SLOTS), 1).astype(F32)
    take_t = jnp.where(_rep(pos_rep, MOE_SLOTS // LANES, 1) == slot, 1.0, 0.0)
    take = take_t.T
    take = take.astype(BF16)
    h2_ref[...] = _dot(take, h_hi).astype(BF16)
    c_hi = comb.astype(BF16)
    c_rest = comb - c_hi.astype(F32)
    c_mid = c_rest.astype(BF16)
    c_lo = (c_rest - c_mid.astype(F32)).astype(BF16)
    comb_ref[...] = (_dot(take, c_hi) + _dot(take, c_mid)) + _dot(take, c_lo)
    pos_ref[...] = pos_rep
    seg_ref[...] = jnp.zeros(seg_ref.shape, jnp.int32)
    seg_ref[0:1, :] = n_pad.astype(jnp.int32)
    seg_ref[1:2, :] = start.astype(jnp.int32)


def _outproj(x2, yl, ya, yc, gn, w_out, mod_l, n2g, rw, rb, seq):
    t, d = x2.shape
    tps = seq // TM
    row = lambda w: pl.BlockSpec((TM, w), lambda i: (i, 0))
    full = lambda a: pl.BlockSpec(a.shape, lambda i: (0,) * a.ndim)
    return pl.pallas_call(
        _outproj_kernel,
        grid=(t // TM,),
        in_specs=[row(d), row(D_LRU), row(D_ATT), row(D_CONV), full(gn), full(w_out),
                  pl.BlockSpec((None, 6, d), lambda i: (i // tps, 0, 0)), full(n2g), full(rw), full(rb)],
        out_specs=[row(d), pl.BlockSpec((MOE_SLOTS, d), lambda i: (i, 0)), pl.BlockSpec((MOE_SLOTS, LANES), lambda i: (i, 0)),
                   row(LANES), pl.BlockSpec((None, 8, LANES), lambda i: (i, 0, 0))],
        out_shape=[jax.ShapeDtypeStruct((t, d), F32), jax.ShapeDtypeStruct((t // TM * MOE_SLOTS, d), BF16),
                   jax.ShapeDtypeStruct((t // TM * MOE_SLOTS, LANES), F32), jax.ShapeDtypeStruct((t, LANES), F32),
                   jax.ShapeDtypeStruct((t // TM, 8, LANES), jnp.int32)],
        compiler_params=_cparams(1),
        name="out_proj_router",
    )(x2, yl.reshape(t, D_LRU), ya.reshape(t, D_ATT), yc.reshape(t, D_CONV), gn, w_out, mod_l, n2g, rw, rb)


def _moe_plan(seg, n_chunks, n_gtiles):
    rows = seg[:, 0, :MOE_GROUPS]
    start = seg[:, 1, :MOE_GROUPS]
    ci = jnp.arange(n_chunks, dtype=jnp.int32)
    gi = jnp.arange(MOE_GROUPS, dtype=jnp.int32)
    cum = jnp.sum(jnp.where((ci[None, :] <= ci[:, None])[:, :, None], rows[None, :, :], 0), axis=1)
    before = cum - rows
    total = jnp.sum(rows, axis=0)
    tiles_g = (total + (TM - 1)) // TM
    ends = jnp.sum(jnp.where(gi[None, :] <= gi[:, None], tiles_g[None, :], 0), axis=1)
    base = (ends - tiles_g) * TM
    group_at = lambda tile: jnp.minimum(jnp.sum((tile[:, None] >= ends[None, :]).astype(jnp.int32), axis=1),
                                        MOE_GROUPS - 1)
    tiles = jnp.concatenate([group_at(jnp.arange(n_gtiles, dtype=jnp.int32)), ends[-1:]]).astype(jnp.int32)

    row = jnp.arange(n_gtiles * (TM // MOE_ALIGN), dtype=jnp.int32) * MOE_ALIGN
    is_g = group_at(row // TM)[:, None] == gi[None, :]
    of_group = lambda v: jnp.sum(jnp.where(is_g, v[None, :], 0), axis=1)
    p = row - of_group(base)
    cum_g = jnp.sum(jnp.where(is_g[:, None, :], cum[None, :, :], 0), axis=2)
    chunk = jnp.minimum(jnp.sum((p[:, None] >= cum_g).astype(jnp.int32), axis=1), n_chunks - 1)
    is_cg = (chunk[:, None] == ci[None, :])[:, :, None] & is_g[:, None, :]
    of_segment = lambda a: jnp.sum(jnp.where(is_cg, a[None, :, :], 0), axis=(1, 2))
    src = chunk * MOE_SLOTS + of_segment(start - before) + p
    zero_row = TM + MOE_GROUPS * MOE_ALIGN
    gather = jnp.where(p < of_group(total), src, zero_row) // MOE_ALIGN

    slot = jnp.arange(MOE_SLOTS // MOE_ALIGN, dtype=jnp.int32) * MOE_ALIGN
    inside = (slot[None, :, None] >= start[:, None, :]) & (slot[None, :, None] < (start + rows)[:, None, :])
    dst = jnp.sum(jnp.where(inside, (base[None, :] + before - start)[:, None, :], 0), axis=-1) + slot[None, :]
    scatter = jnp.where(jnp.any(inside, axis=-1), dst // MOE_ALIGN, -1)
    return gather.astype(jnp.int32), scatter.reshape(-1).astype(jnp.int32), tiles


def _experts_kernel(gather_ref, tiles_ref, h2s_ref, combs_ref, wg_ref, wu_ref, wd_ref, y_ref,
                    hbuf, cbuf, acc_ref, sem):
    k = pl.program_id(0)
    e = pl.program_id(1)
    n_used = tiles_ref[pl.num_programs(0)]
    per_tile = TM // MOE_ALIGN

    def tile_copies(tile, go):
        slot = tile % 2
        for j in range(per_tile):
            src = pl.multiple_of(gather_ref[tile * per_tile + j] * MOE_ALIGN, MOE_ALIGN)
            dst = pl.ds(j * MOE_ALIGN, MOE_ALIGN)
            go(pltpu.make_async_copy(h2s_ref.at[pl.ds(src, MOE_ALIGN)], hbuf.at[slot, dst], sem.at[0, slot]))
            go(pltpu.make_async_copy(combs_ref.at[pl.ds(src, MOE_ALIGN)], cbuf.at[slot, dst], sem.at[1, slot]))

    @pl.when(e == 0)
    def _():
        acc_ref[...] = jnp.zeros_like(acc_ref)

        @pl.when((k == 0) & (n_used > 0))
        def _():
            tile_copies(k, lambda cp: cp.start())

        @pl.when(k + 1 < n_used)
        def _():
            tile_copies(k + 1, lambda cp: cp.start())

        @pl.when(k < n_used)
        def _():
            tile_copies(k, lambda cp: cp.wait())

    @pl.when(k < n_used)
    def _():
        slot = k % 2
        h2 = hbuf[slot]
        hid = jax.nn.silu(_dot(h2, wg_ref[...].astype(BF16))) * _dot(h2, wu_ref[...].astype(BF16))
        lane = lax.broadcasted_iota(jnp.int32, (TM, LANES), 1)
        col = MOE_GROUPS + tiles_ref[k] * MOE_EXPERTS + e
        c = jnp.sum(jnp.where(lane == col, cbuf[slot], 0.0), axis=-1, keepdims=True)
        acc_ref[...] += _dot((hid * c).astype(BF16), wd_ref[...].astype(BF16))

    @pl.when(e == pl.num_programs(1) - 1)
    def _():
        y_ref[...] = acc_ref[...]


def _experts(gather, tiles, h2s, combs, wg, wu, wd, n_gtiles, first_expert):
    d = h2s.shape[1]

    def expert_block(k, e, ga, tl):
        e_eff = jnp.where(k < tl[n_gtiles], e, MOE_EXPERTS - 1)
        return (first_expert + tl[k] * MOE_EXPERTS + e_eff, 0, 0)

    weights = lambda s: pl.BlockSpec((None,) + s, expert_block)
    grid_spec = pltpu.PrefetchScalarGridSpec(
        num_scalar_prefetch=2,
        grid=(n_gtiles, MOE_EXPERTS),
        in_specs=[pl.BlockSpec(memory_space=pl.ANY), pl.BlockSpec(memory_space=pl.ANY),
                  weights((d, MOE_HID)), weights((d, MOE_HID)), weights((MOE_HID, d))],
        out_specs=pl.BlockSpec((TM, d), lambda k, e, ga, tl: (k, 0)),
        scratch_shapes=[pltpu.VMEM((2, TM, d), BF16), pltpu.VMEM((2, TM, LANES), F32), pltpu.VMEM((TM, d), F32),
                        pltpu.SemaphoreType.DMA((2, 2))],
    )
    return pl.pallas_call(
        _experts_kernel,
        grid_spec=grid_spec,
        out_shape=jax.ShapeDtypeStruct((n_gtiles * TM, d), F32),
        compiler_params=_cparams(2),
        name="moe_experts",
    )(gather, tiles, h2s, combs, wg, wu, wd)


def _finalize_kernel(scatter_ref, yg_ref, pos_ref, xn_ref, mod_ref, fg_ref, o_ref, ys_ref, sem, *, final):
    c = pl.program_id(0)
    per_chunk = MOE_SLOTS // MOE_ALIGN

    def chunk_copies(chunk, go):
        buf = chunk % 2
        for j in range(per_chunk):
            src = scatter_ref[chunk * per_chunk + j]

            @pl.when(src >= 0)
            def _(j=j, src=src):
                rows = pl.ds(pl.multiple_of(src * MOE_ALIGN, MOE_ALIGN), MOE_ALIGN)
                go(pltpu.make_async_copy(yg_ref.at[rows], ys_ref.at[buf, pl.ds(j * MOE_ALIGN, MOE_ALIGN)],
                                         sem.at[buf]))

    @pl.when(c == 0)
    def _():
        ys_ref[...] = jnp.zeros_like(ys_ref)
        chunk_copies(c, lambda cp: cp.start())

    @pl.when(c + 1 < pl.num_programs(0))
    def _():
        chunk_copies(c + 1, lambda cp: cp.start())

    chunk_copies(c, lambda cp: cp.wait())

    tm = xn_ref.shape[0]
    y = ys_ref[c % 2]
    hi = y.astype(BF16)
    rest = y - hi.astype(F32)
    mid = rest.astype(BF16)
    lo = (rest - mid.astype(F32)).astype(BF16)
    slot = lax.broadcasted_iota(jnp.int32, (tm, MOE_SLOTS), 1).astype(F32)
    take_t = jnp.where(_rep(pos_ref[...], MOE_SLOTS // LANES, 1) == slot, 1.0, 0.0).astype(BF16)
    y_tok = (_dot(take_t, hi) + _dot(take_t, mid)) + _dot(take_t, lo)
    xo = xn_ref[...] + mod_ref[5:6, :] * y_tok
    if final:
        xo = _rms(xo, fg_ref[...])
    o_ref[...] = xo


def _finalize(scatter, yg, pos, xn, mod_l, fg, seq, final):
    t, d = xn.shape
    tps = seq // TM
    grid_spec = pltpu.PrefetchScalarGridSpec(
        num_scalar_prefetch=1,
        grid=(t // TM,),
        in_specs=[pl.BlockSpec(memory_space=pl.ANY),
                  pl.BlockSpec((TM, LANES), lambda i, tb: (i, 0)),
                  pl.BlockSpec((TM, d), lambda i, tb: (i, 0)),
                  pl.BlockSpec((None, 6, d), lambda i, tb: (i // tps, 0, 0)),
                  pl.BlockSpec((1, d), lambda i, tb: (0, 0))],
        out_specs=pl.BlockSpec((TM, d), lambda i, tb: (i, 0)),
        scratch_shapes=[pltpu.VMEM((2, MOE_SLOTS, d), F32), pltpu.SemaphoreType.DMA((2,))],
    )
    return pl.pallas_call(
        functools.partial(_finalize_kernel, final=final),
        grid_spec=grid_spec,
        out_shape=jax.ShapeDtypeStruct((t, d), F32),
        compiler_params=_cparams(1),
        name="moe_finalize",
    )(scatter, yg, pos, xn, mod_l, fg)


def _permute_w_in(w):
    gate0 = D_LRU * 2 + D_ATT + 6 * LANES
    cv0 = gate0 + 3 * N_HEADS
    w = w.astype(BF16)
    pad = jnp.zeros((w.shape[0], IN_PAD - w.shape[1]), BF16)
    return jnp.concatenate([w[:, :gate0], w[:, cv0:], w[:, gate0:cv0], pad], axis=1)


def _router_weights(rg_w, rg_b, re_w, re_b):
    d = rg_w.shape[0]
    ne = MOE_GROUPS * MOE_EXPERTS
    w = jnp.concatenate([rg_w, jnp.transpose(re_w, (1, 0, 2)).reshape(d, ne),
                         jnp.zeros((d, LANES - MOE_GROUPS - ne), F32)], axis=1)
    b = jnp.concatenate([rg_b, re_b.reshape(ne), jnp.zeros((LANES - MOE_GROUPS - ne,), F32)])[None, :]
    w_hi = w.astype(BF16)
    w_mid = (w - w_hi.astype(F32)).astype(BF16)
    return jnp.stack([w_hi, w_mid]), b


def kernel(x, c, positions, ada_w, ada_b, norm1_g, norm2_g, w_in, lru_conv_w, lru_conv_b, lru_wa, lru_ba, lru_wx, lru_bx, lru_lambda, cmp_k_w1, cmp_k_w2, cmp_v_w1, cmp_v_w2, cmp_pos_k, cmp_pos_v, cnv_dw_w, cnv_dw_b, cnv_ln_g, cnv_ln_b, out_norm_g, w_out, moe_rg_w, moe_rg_b, moe_re_w, moe_re_b, moe_w_gate, moe_w_up, moe_w_down, final_norm_g):
    batch, seq, d = x.shape
    depth = ada_w.shape[0]
    assert d == D_MODEL and seq == N_SLC * SLC_LEN and seq % KEY_TILE == 0
    t = batch * seq
    ne = MOE_GROUPS * MOE_EXPERTS
    n_chunks = t // TM
    n_gtiles = n_chunks + -(-n_chunks * MOE_GROUPS * (MOE_ALIGN - 1) // TM) + MOE_GROUPS

    cos, sin = _rope_tables(positions)
    mod = _modulation(c, ada_w, ada_b)
    x2 = x.reshape(t, d)
    fg = final_norm_g[None, :]
    for l in range(depth):
        (xl, gl, q, kc, vc, ksa, vsa, kw, vwa, gate, cv) = _inproj(
            x2, mod[l], norm1_g[l][None, :], _permute_w_in(w_in[l]), cos, sin, seq)
        y_lru, y_cnv = _mixer(xl, gl, cv, lru_conv_w[l], lru_conv_b[l], lru_wa[l], lru_ba[l], lru_wx[l], lru_bx[l],
                              lru_lambda[l], cnv_dw_w[l], cnv_dw_b[l], cnv_ln_g[l], cnv_ln_b[l], batch, seq)
        kcmp, vcmp = _compress(kc, vc, cmp_pos_k[l], cmp_k_w1[l], cmp_k_w2[l], cmp_pos_v[l], cmp_v_w1[l], cmp_v_w2[l],
                               batch, seq)
        y_att = _attention(q, kcmp, vcmp, ksa, vsa, kw, vwa, gate, batch, seq)
        rw, rb = _router_weights(moe_rg_w[l], moe_rg_b[l], moe_re_w[l], moe_re_b[l])
        xn, h2, comb, pos, seg = _outproj(x2, y_lru, y_att, y_cnv, out_norm_g[l][None, :], w_out[l].astype(BF16),
                                          mod[l], norm2_g[l][None, :], rw, rb, seq)
        gather, scatter, tiles = _moe_plan(seg, n_chunks, n_gtiles)
        yg = _experts(gather, tiles, h2, comb,
                      moe_w_gate.reshape(depth * ne, d, MOE_HID), moe_w_up.reshape(depth * ne, d, MOE_HID),
                      moe_w_down.reshape(depth * ne, MOE_HID, d), n_gtiles, l * ne)
        x2 = _finalize(scatter, yg, pos, xn, mod[l], fg, seq, final=(l == depth - 1))
    return x2.reshape(batch, seq, d)
```

```python
import functools

import jax
import jax.numpy as jnp
from jax import lax
from jax.experimental import pallas as pl
from jax.experimental.pallas import tpu as pltpu

F32 = jnp.float32
BF16 = jnp.bfloat16

D_MODEL = 1024
D_LRU = 256
D_ATT = 512
D_CONV = 256
LRU_BLOCKS = 4
LRU_CONV_W = 4
LRU_C = 8.0
HEAD_DIM = 64
N_HEADS = 8
N_KV = 2
GQA = 4
ROPE_THETA = 10000.0
CMP_LEN = 32
CMP_STRIDE = 16
SLC_LEN = 64
SLC_TOPN = 16
WINDOW = 512
CONV_K = 31
MOE_GROUPS = 4
MOE_EXPERTS = 4
MOE_HID = 512
EPS = 1e-6
NEG = -1e30
FORCE = 1e9
LOG2_E = 1.4426950408889634

LANES = 128
Q_BLOCK = 128
KEY_TILE = 512
N_SLC = 128
TM = 512
MOE_ALIGN_LOG2 = 4
MOE_ALIGN = 1 << MOE_ALIGN_LOG2
MOE_SLOTS = 640
TC = 512
SCAN_ROWS = 64
LRU_TAIL = 8
CNV_TAIL = 32
IN_PAD = 2432
VMEM_LIMIT = 56 * 1024 * 1024


def _cparams(n_axes, vmem=VMEM_LIMIT):
    return pltpu.CompilerParams(dimension_semantics=("arbitrary",) * n_axes, vmem_limit_bytes=vmem)


def _dot(a, b):
    return jnp.dot(a, b, preferred_element_type=F32)


def _dot_nt(a, b):
    return lax.dot_general(a, b, (((1,), (1,)), ((), ())), preferred_element_type=F32)


def _rep(v, n, axis):
    return jnp.concatenate([v] * n, axis=axis)


def _rms(v, g):
    return v * lax.rsqrt(jnp.mean(v * v, axis=-1, keepdims=True) + EPS) * g


def _rope_kernel(pos_ref, inv_ref, sign_ref, cos_ref, sin_ref):
    ang = pos_ref[...].astype(F32) * inv_ref[...]
    cos_ref[...] = jnp.cos(ang)
    sin_ref[...] = jnp.sin(ang) * sign_ref[...]


def _rope_tables(positions):
    t = positions.size
    inv = ROPE_THETA ** (-jnp.arange(0, HEAD_DIM, 2, dtype=F32) / HEAD_DIM)
    inv128 = jnp.tile(inv, 4)[None, :]
    sign128 = jnp.tile(jnp.concatenate([-jnp.ones((32,), F32), jnp.ones((32,), F32)]), 2)[None, :]
    tr = 1024
    return pl.pallas_call(
        _rope_kernel,
        grid=(t // tr,),
        in_specs=[pl.BlockSpec((tr, 1), lambda i: (i, 0)),
                  pl.BlockSpec((1, LANES), lambda i: (0, 0)),
                  pl.BlockSpec((1, LANES), lambda i: (0, 0))],
        out_specs=[pl.BlockSpec((tr, LANES), lambda i: (i, 0))] * 2,
        out_shape=[jax.ShapeDtypeStruct((t, LANES), F32)] * 2,
        compiler_params=_cparams(1),
        name="rope_tables",
    )(positions.reshape(t, 1), inv128, sign128)


def _mod_kernel(c_ref, w_ref, b_ref, o_ref):
    sc = jax.nn.silu(c_ref[...])
    o_ref[...] = _dot(sc.astype(BF16), w_ref[...].astype(BF16)) + b_ref[...]


def _modulation(c, ada_w, ada_b):
    nl, d, n6 = ada_w.shape
    b = c.shape[0]
    rows = 16
    cp = jnp.zeros((rows, d), F32).at[:b].set(c)
    tn = 1536
    out = pl.pallas_call(
        _mod_kernel,
        grid=(nl, n6 // tn),
        in_specs=[pl.BlockSpec((rows, d), lambda l, j: (0, 0)),
                  pl.BlockSpec((None, d, tn), lambda l, j: (l, 0, j)),
                  pl.BlockSpec((None, 1, tn), lambda l, j: (l, 0, j))],
        out_specs=pl.BlockSpec((None, rows, tn), lambda l, j: (l, 0, j)),
        out_shape=jax.ShapeDtypeStruct((nl, rows, n6), F32),
        compiler_params=_cparams(2),
        name="adaln_mod",
    )(cp, ada_w, ada_b.reshape(nl, 1, n6))
    return out[:, :b].reshape(nl, b, 6, d)


def _inproj_kernel(x_ref, mod_ref, g_ref, w_ref, cos_ref, sin_ref,
                   xl_ref, gl_ref, q_ref, kc_ref, vc_ref, ksa_ref, vsa_ref, kw_ref, vwa_ref, gate_ref, cv_ref,
                   *, tiles_per_seq):
    tm = x_ref.shape[0]
    x = x_ref[...]
    h = _rms(x, g_ref[...]) * (1.0 + mod_ref[1:2, :]) + mod_ref[0:1, :]
    p = _dot(h.astype(BF16), w_ref[...])

    cos = cos_ref[...]
    sin = sin_ref[...]
    lane = lax.broadcasted_iota(jnp.int32, (tm, LANES), 1)
    first_half = (lane & 63) < 32

    def rope(v):
        rot = jnp.where(first_half, pltpu.roll(v, 96, 1), pltpu.roll(v, 32, 1))
        return v * cos + rot * sin

    xl_ref[...] = p[:, 0:256]
    gl_ref[...] = p[:, 256:512]
    scale = HEAD_DIM ** -0.5 * LOG2_E
    low = lane < 64
    for m in range(4):
        slab = rope(p[:, 512 + 128 * m:640 + 128 * m]) * scale
        swapped = pltpu.roll(slab, 64, 1)
        for hh in range(2):
            head = 2 * m + hh
            kv = head // GQA
            src = slab if hh == kv else swapped
            keep = low if kv == 0 else jnp.logical_not(low)
            q_ref[:, head * LANES:(head + 1) * LANES] = jnp.where(keep, src, 0.0).astype(BF16)
    kc_ref[...] = rope(p[:, 1024:1152])
    vc_ref[...] = p[:, 1152:1280]
    row = lax.broadcasted_iota(jnp.int32, (tm, LANES), 0)
    s_base = (pl.program_id(0) % tiles_per_seq) * tm
    onehot = jnp.where(lane == ((s_base + row) >> 6), 1.0, 0.0).astype(BF16)
    ones = jnp.ones((tm, LANES), BF16)
    ksa_ref[:, 0:LANES] = rope(p[:, 1280:1408]).astype(BF16)
    ksa_ref[:, LANES:2 * LANES] = onehot
    vsa_ref[:, 0:LANES] = p[:, 1408:1536].astype(BF16)
    vsa_ref[:, LANES:2 * LANES] = ones
    kw_ref[...] = rope(p[:, 1536:1664]).astype(BF16)
    vwa_ref[:, 0:LANES] = p[:, 1664:1792].astype(BF16)
    vwa_ref[:, LANES:2 * LANES] = ones
    cv_ref[...] = p[:, 1792:2304]
    gate_ref[...] = jax.nn.sigmoid(p[:, 2304:2432])


def _inproj(x2, mod_l, g, w_p, cos, sin, seq):
    t, d = x2.shape
    tps = seq // TM
    row = lambda w: pl.BlockSpec((TM, w), lambda i: (i, 0))
    outs = [(256, F32), (256, F32), (N_HEADS * LANES, BF16), (LANES, F32), (LANES, F32), (2 * LANES, BF16),
            (2 * LANES, BF16), (LANES, BF16), (2 * LANES, BF16), (LANES, F32), (2 * D_CONV, F32)]
    return pl.pallas_call(
        functools.partial(_inproj_kernel, tiles_per_seq=tps),
        grid=(t // TM,),
        in_specs=[row(d),
                  pl.BlockSpec((None, 6, d), lambda i: (i // tps, 0, 0)),
                  pl.BlockSpec((1, d), lambda i: (0, 0)),
                  pl.BlockSpec((d, IN_PAD), lambda i: (0, 0)),
                  row(LANES), row(LANES)],
        out_specs=[row(w) for w, _ in outs],
        out_shape=[jax.ShapeDtypeStruct((t, w), dt) for w, dt in outs],
        compiler_params=_cparams(1),
        name="in_proj",
    )(x2, mod_l, g, w_p, cos, sin)


def _shift_rows(v, s, fill, row):
    return jnp.where(row < s, fill, pltpu.roll(v, s, 0))


def _causal_taps(ext, w_ref, bias, width, tail, tc):
    acc = bias
    for b in range(min(8, width)):
        shifted = pltpu.roll(ext, b, 0) if b else ext
        for k in range(width):
            back = width - 1 - k
            if back % 8 == b:
                start = tail - (back - b)
                acc = acc + w_ref[k:k + 1, :] * shifted[start:start + tc, :]
    return acc


def _mixer_kernel(xl_ref, gl_ref, cv_ref, lcw_ref, lcb_ref, wa_ref, ba_ref, wx_ref, bx_ref, lam_ref,
                  cw_ref, cb_ref, lng_ref, lnb_ref, ylru_ref, ycnv_ref,
                  extl, extc, abuf, ubuf, hcar):
    tc = xl_ref.shape[0]

    @pl.when(pl.program_id(1) == 0)
    def _():
        extl[0:LRU_TAIL, :] = jnp.zeros((LRU_TAIL, D_LRU), F32)
        extc[0:CNV_TAIL, :] = jnp.zeros((CNV_TAIL, D_CONV), F32)
        hcar[...] = jnp.zeros_like(hcar)

    xl = xl_ref[...]
    extl[LRU_TAIL:LRU_TAIL + tc, :] = xl
    xc = _causal_taps(extl[...], lcw_ref, lcb_ref[...], LRU_CONV_W, LRU_TAIL, tc)
    extl[0:LRU_TAIL, :] = xl[tc - LRU_TAIL:tc, :]
    xcb = xc.astype(BF16)
    r = jax.nn.sigmoid(_dot(xcb, wa_ref[...]) + ba_ref[...])
    gi = jax.nn.sigmoid(_dot(xcb, wx_ref[...]) + bx_ref[...])
    log_a = LRU_C * r * jax.nn.log_sigmoid(lam_ref[...])
    th = jnp.tanh(log_a)
    one_minus_a2 = -2.0 * th / (1.0 - th)
    abuf[...] = jnp.exp(log_a)
    ubuf[...] = jnp.sqrt(one_minus_a2) * (gi * xc)

    row = lax.broadcasted_iota(jnp.int32, (SCAN_ROWS, D_LRU), 0)

    def scan_chunk(c, h_prev):
        off = pl.multiple_of(c * SCAN_ROWS, SCAN_ROWS)
        a = abuf[pl.ds(off, SCAN_ROWS), :]
        b = ubuf[pl.ds(off, SCAN_ROWS), :]
        s = 1
        while s < SCAN_ROWS:
            b = b + a * _shift_rows(b, s, 0.0, row)
            a = a * _shift_rows(a, s, 1.0, row)
            s *= 2
        h = b + a * h_prev
        ylru_ref[pl.ds(off, SCAN_ROWS), :] = h * jax.nn.gelu(gl_ref[pl.ds(off, SCAN_ROWS), :])
        return h[SCAN_ROWS - 1:SCAN_ROWS, :]

    hcar[0:1, :] = lax.fori_loop(0, tc // SCAN_ROWS, scan_chunk, hcar[0:1, :])

    u = cv_ref[...]
    v = u[:, 0:D_CONV] * jax.nn.sigmoid(u[:, D_CONV:2 * D_CONV])
    extc[CNV_TAIL:CNV_TAIL + tc, :] = v
    acc = _causal_taps(extc[...], cw_ref, cb_ref[...], CONV_K, CNV_TAIL, tc)
    extc[0:CNV_TAIL, :] = v[tc - CNV_TAIL:tc, :]
    mu = jnp.mean(acc, axis=-1, keepdims=True)
    var = jnp.mean(jnp.square(acc - mu), axis=-1, keepdims=True)
    y = (acc - mu) * lax.rsqrt(var + EPS) * lng_ref[...] + lnb_ref[...]
    ycnv_ref[...] = jax.nn.silu(y)


def _block_diag(w):
    n, rows, cols = w.shape
    eye = jnp.eye(n, dtype=bool)
    return jnp.where(eye[:, None, :, None], w[:, :, None, :], 0.0).reshape(n * rows, n * cols)


def _mixer(xl, gl, cv, lcw, lcb, wa, ba, wx, bx, lam, cw, cb, lng, lnb, batch, seq):
    seqrow = lambda w: pl.BlockSpec((None, TC, w), lambda b, t: (b, t, 0))
    full = lambda a: pl.BlockSpec(a.shape, lambda b, t: (0,) * a.ndim)
    small = [lcw, lcb[None, :], _block_diag(wa).astype(BF16), ba[None, :], _block_diag(wx).astype(BF16), bx[None, :],
             lam[None, :], jnp.zeros((32, D_CONV), F32).at[:CONV_K].set(cw), cb[None, :], lng[None, :], lnb[None, :]]
    return pl.pallas_call(
        _mixer_kernel,
        grid=(batch, seq // TC),
        in_specs=[seqrow(D_LRU), seqrow(D_LRU), seqrow(2 * D_CONV)] + [full(a) for a in small],
        out_specs=[seqrow(D_LRU), seqrow(D_CONV)],
        out_shape=[jax.ShapeDtypeStruct((batch, seq, D_LRU), F32), jax.ShapeDtypeStruct((batch, seq, D_CONV), F32)],
        scratch_shapes=[pltpu.VMEM((TC + LRU_TAIL, D_LRU), F32), pltpu.VMEM((TC + CNV_TAIL, D_CONV), F32),
                        pltpu.VMEM((TC, D_LRU), F32), pltpu.VMEM((TC, D_LRU), F32), pltpu.VMEM((8, D_LRU), F32)],
        compiler_params=_cparams(2),
        name="mixer_stream",
    )(xl.reshape(batch, seq, D_LRU), gl.reshape(batch, seq, D_LRU), cv.reshape(batch, seq, 2 * D_CONV), *small)


def _cmp_kernel(kc_ref, vc_ref, posk_ref, posv_ref, wk1_ref, wk2_ref, wv1_ref, wv2_ref, ko_ref, vo_ref, nat_ref):
    nh = ko_ref.shape[0]
    quarter = nh // 4

    def one(t_ref, pos_ref, w1_ref, w2_ref, o_ref):
        first = jnp.zeros((nh, w1_ref.shape[2]), F32)
        second = jnp.zeros((nh, w1_ref.shape[2]), F32)
        for l in range(CMP_STRIDE):
            x = t_ref[pl.ds(l, nh, stride=CMP_STRIDE), :]
            first = first + _dot((x + pos_ref[l:l + 1, :]).astype(BF16), w1_ref[l])
            second = second + _dot((x + pos_ref[CMP_STRIDE + l:CMP_STRIDE + l + 1, :]).astype(BF16),
                                   w1_ref[CMP_STRIDE + l])
        hid = jax.nn.gelu(first + pltpu.roll(second, nh - 1, 0))
        nat_ref[...] = _dot(hid.astype(BF16), w2_ref[...])
        for r in range(4):
            o_ref[r * quarter:(r + 1) * quarter, :] = nat_ref[pl.ds(r, quarter, stride=4), :].astype(o_ref.dtype)

    one(kc_ref, posk_ref, wk1_ref, wk2_ref, ko_ref)
    one(vc_ref, posv_ref, wv1_ref, wv2_ref, vo_ref)


def _cmp_weights(pos, w1, w2):
    hid = w1.shape[1]
    eye = jnp.eye(N_KV, dtype=bool)
    w1 = w1.astype(BF16).reshape(CMP_LEN, HEAD_DIM, hid)
    w1x = jnp.where(eye[None, :, None, :, None], w1[:, None, :, None, :], 0.0)
    w1x = w1x.reshape(CMP_LEN, N_KV * HEAD_DIM, N_KV * hid)
    w2x = _block_diag(jnp.broadcast_to(w2.astype(BF16), (N_KV,) + w2.shape))
    return jnp.tile(pos, (1, N_KV)), w1x, w2x


def _compress(kc, vc, pos_k, wk1, wk2, pos_v, wv1, wv2, batch, seq):
    nh = seq // CMP_STRIDE
    wide = N_KV * HEAD_DIM
    pk, wk1x, wk2x = _cmp_weights(pos_k, wk1, wk2)
    pv, wv1x, wv2x = _cmp_weights(pos_v, wv1, wv2)
    full = lambda a: pl.BlockSpec(a.shape, lambda b: (0,) * a.ndim)
    seqs = pl.BlockSpec((None, seq, wide), lambda b: (b, 0, 0))
    outs = pl.BlockSpec((None, nh, wide), lambda b: (b, 0, 0))
    return pl.pallas_call(
        _cmp_kernel,
        grid=(batch,),
        in_specs=[seqs, seqs, full(pk), full(pv), full(wk1x), full(wk2x), full(wv1x), full(wv2x)],
        out_specs=[outs, outs],
        out_shape=[jax.ShapeDtypeStruct((batch, nh, wide), BF16)] * 2,
        scratch_shapes=[pltpu.VMEM((nh, wide), F32)],
        compiler_params=_cparams(1),
        name="nsa_compress",
    )(kc.reshape(batch, seq, wide), vc.reshape(batch, seq, wide), pk, pv, wk1x, wk2x, wv1x, wv2x)


def _attn_kernel(q_ref, kcmp_ref, vcmp_ref, ksa_ref, vsa_ref, kw_ref, vwa_ref, gate_ref, o_ref,
                 qa_ref, part_ref, acc_ref, m_ref, accw_ref, mw_ref, sa_ref, sb_ref):
    rows = GQA * Q_BLOCK
    reps = KEY_TILE // LANES
    qb = pl.program_id(1)
    s0 = qb * Q_BLOCK
    kd = s0 // KEY_TILE
    off_d = pl.multiple_of(kd * KEY_TILE, KEY_TILE)
    off_p = pl.multiple_of(jnp.maximum(kd - 1, 0) * KEY_TILE, KEY_TILE)
    d0 = s0 - off_d

    lane = lax.broadcasted_iota(jnp.int32, (Q_BLOCK, LANES), 1)
    sub = lax.broadcasted_iota(jnp.int32, (Q_BLOCK, LANES), 0)
    r_q = lax.broadcasted_iota(jnp.int32, (Q_BLOCK, KEY_TILE), 0)
    c_k = lax.broadcasted_iota(jnp.int32, (Q_BLOCK, KEY_TILE), 1)
    rel = c_k - r_q
    gate = gate_ref[...]

    def tile_rows(plane):
        return _rep(plane, GQA, 0)

    def flash_step(acc, m, kv, s, v_tile):
        m_old = m[kv]
        m_new = jnp.maximum(m_old, jnp.broadcast_to(jnp.max(s, axis=-1, keepdims=True), m_old.shape))
        p = jnp.exp2(s - _rep(m_new, reps, 1))
        acc[kv] = acc[kv] * _rep(jnp.exp2(m_old - m_new), 2, 1) + _dot(p.astype(BF16), v_tile)
        m[kv] = m_new

    def flash_result(acc, kv):
        a = acc[kv]
        return a[:, 0:LANES] / a[:, LANES:2 * LANES]

    q128 = [jnp.concatenate([q_ref[:, (kv * GQA + g) * LANES:(kv * GQA + g + 1) * LANES] for g in range(GQA)], axis=0)
            for kv in range(N_KV)]

    kcmp = kcmp_ref[...]
    vcmp = vcmp_ref[...]
    cmp_end = SLC_LEN * (c_k & (N_SLC - 1)) + CMP_STRIDE * (c_k >> 7) + (CMP_LEN - 1)
    cmp_bias = tile_rows(jnp.where(cmp_end - r_q <= s0, 0.0, NEG))
    has_cmp = tile_rows(jnp.where(s0 + sub >= CMP_LEN - 1, 1.0, 0.0))
    o_cmp, imp_t = [], []
    for kv in range(N_KV):
        s_c = _dot_nt(q128[kv], kcmp) + cmp_bias
        mb = jnp.broadcast_to(jnp.max(s_c, axis=-1, keepdims=True), (rows, LANES))
        e = jnp.exp2(s_c - _rep(mb, reps, 1))
        lb = jnp.broadcast_to(jnp.sum(e, axis=-1, keepdims=True), (rows, LANES))
        inv = has_cmp / lb
        o_cmp.append(_dot(e.astype(BF16), vcmp) * inv)
        e4 = (e[:, 0:N_SLC] + e[:, N_SLC:2 * N_SLC] + e[:, 2 * N_SLC:3 * N_SLC] + e[:, 3 * N_SLC:4 * N_SLC]) * inv
        imp = e4[0:Q_BLOCK] + e4[Q_BLOCK:2 * Q_BLOCK] + e4[2 * Q_BLOCK:3 * Q_BLOCK] + e4[3 * Q_BLOCK:4 * Q_BLOCK]
        imp_t.append(imp.T)

    accw_ref[...] = jnp.zeros_like(accw_ref)
    mw_ref[...] = jnp.full_like(mw_ref, NEG)
    prev_bias = tile_rows(jnp.where(rel > jnp.where(kd > 0, d0, KEY_TILE), 0.0, NEG))
    diag_bias = tile_rows(jnp.where(rel <= d0, 0.0, NEG))
    kw_p, vw_p = kw_ref[pl.ds(off_p, KEY_TILE), :], vwa_ref[pl.ds(off_p, KEY_TILE), :]
    kw_d, vw_d = kw_ref[pl.ds(off_d, KEY_TILE), :], vwa_ref[pl.ds(off_d, KEY_TILE), :]
    for kv in range(N_KV):
        flash_step(accw_ref, mw_ref, kv, _dot_nt(q128[kv], kw_p) + prev_bias, vw_p)
    for kv in range(N_KV):
        flash_step(accw_ref, mw_ref, kv, _dot_nt(q128[kv], kw_d) + diag_bias, vw_d)
    for kv in range(N_KV):
        o_win = flash_result(accw_ref, kv)
        for g in range(GQA):
            col = 3 * (kv * GQA + g)
            rs = slice(g * Q_BLOCK, (g + 1) * Q_BLOCK)
            part_ref[kv, rs, :] = gate[:, col:col + 1] * o_cmp[kv][rs] + gate[:, col + 2:col + 3] * o_win[rs]

    blk = lax.broadcasted_iota(jnp.int32, (N_SLC, Q_BLOCK), 0)
    t_q = s0 + lax.broadcasted_iota(jnp.int32, (N_SLC, Q_BLOCK), 1)
    forced = (blk == 0) | (blk == (t_q >> 6))
    val = [jnp.where(forced, -jnp.inf, jnp.where(blk * SLC_LEN <= t_q, imp_t[kv], -1.0)) for kv in range(N_KV)]
    blk_f = blk.astype(F32)
    for _ in range(SLC_TOPN - 2):
        for kv in range(N_KV):
            mx = jnp.max(val[kv], axis=0, keepdims=True)
            first = jnp.min(jnp.where(val[kv] == mx, blk_f, float(N_SLC)), axis=0, keepdims=True)
            val[kv] = jnp.where(blk_f == first, -jnp.inf, val[kv])
    for kv in range(N_KV):
        bias = jnp.where(val[kv].T == -jnp.inf, 0.0, NEG).astype(BF16)
        qa_ref[kv] = jnp.concatenate([q128[kv], jnp.concatenate([bias] * GQA, axis=0)], axis=1)

    acc_ref[...] = jnp.zeros_like(acc_ref)
    m_ref[...] = jnp.full_like(m_ref, NEG)

    def scores_into(dst, kt):
        k_t = ksa_ref[pl.ds(pl.multiple_of(kt * KEY_TILE, KEY_TILE), KEY_TILE), :]
        for kv in range(N_KV):
            dst[kv] = _dot_nt(qa_ref[kv], k_t)

    def consume(src, kt, bias):
        v_t = vsa_ref[pl.ds(pl.multiple_of(kt * KEY_TILE, KEY_TILE), KEY_TILE), :]
        for kv in range(N_KV):
            flash_step(acc_ref, m_ref, kv, src[kv] if bias is None else src[kv] + bias, v_t)

    scores_into(sa_ref, 0)

    def slc_pairs(first, count):
        for i in range(count):
            a = first + 2 * i
            scores_into(sb_ref, a + 1)
            consume(sa_ref, a, None)
            scores_into(sa_ref, a + 2)
            consume(sb_ref, a + 1, None)

    def slc_quad(j, carry):
        slc_pairs(4 * j, 2)
        return carry

    def slc_pair(j, carry):
        slc_pairs(4 * n_quads + 2 * j, 1)
        return carry

    n_quads = kd // 4
    lax.fori_loop(0, n_quads, slc_quad, 0)
    lax.fori_loop(0, (kd // 2) & 1, slc_pair, 0)
    causal_bias = tile_rows(jnp.where(rel <= d0, 0.0, NEG))
    kd_odd = (kd & 1) == 1

    @pl.when(kd_odd)
    def _():
        scores_into(sb_ref, kd)
        consume(sa_ref, kd - 1, None)
        consume(sb_ref, kd, causal_bias)

    @pl.when(jnp.logical_not(kd_odd))
    def _():
        consume(sa_ref, kd, causal_bias)

    for kv in range(N_KV):
        o_slc = flash_result(acc_ref, kv)
        pieces = []
        for g in range(GQA):
            col = 3 * (kv * GQA + g)
            rs = slice(g * Q_BLOCK, (g + 1) * Q_BLOCK)
            pieces.append(part_ref[kv, rs, :] + gate[:, col + 1:col + 2] * o_slc[rs])
        for mm in range(2):
            even, odd = pieces[2 * mm], pieces[2 * mm + 1]
            if kv == 0:
                odd = pltpu.roll(odd, 64, 1)
            else:
                even = pltpu.roll(even, 64, 1)
            slab = kv * 2 + mm
            o_ref[:, slab * LANES:(slab + 1) * LANES] = jnp.where(lane < 64, even, odd)


def _attention(q, kcmp, vcmp, ksa, vsa, kw, vwa, gate, batch, seq):
    nh = kcmp.shape[1]
    assert nh == KEY_TILE
    rows = GQA * Q_BLOCK
    res = lambda w: pl.BlockSpec((None, seq, w), lambda b, i: (b, 0, 0))
    blk = lambda w: pl.BlockSpec((None, Q_BLOCK, w), lambda b, i: (b, i, 0))
    r3 = lambda a, w: a.reshape(batch, seq, w)
    return pl.pallas_call(
        _attn_kernel,
        grid=(batch, seq // Q_BLOCK),
        in_specs=[blk(N_HEADS * LANES),
                  pl.BlockSpec((None, nh, LANES), lambda b, i: (b, 0, 0)),
                  pl.BlockSpec((None, nh, LANES), lambda b, i: (b, 0, 0)),
                  res(2 * LANES), res(2 * LANES), res(LANES), res(2 * LANES), blk(LANES)],
        out_specs=blk(D_ATT),
        out_shape=jax.ShapeDtypeStruct((batch, seq, D_ATT), F32),
        scratch_shapes=[pltpu.VMEM((N_KV, rows, 2 * LANES), BF16), pltpu.VMEM((N_KV, rows, LANES), F32),
                        pltpu.VMEM((N_KV, rows, 2 * LANES), F32), pltpu.VMEM((N_KV, rows, LANES), F32),
                        pltpu.VMEM((N_KV, rows, 2 * LANES), F32), pltpu.VMEM((N_KV, rows, LANES), F32),
                        pltpu.VMEM((N_KV, rows, KEY_TILE), F32), pltpu.VMEM((N_KV, rows, KEY_TILE), F32)],
        compiler_params=_cparams(2),
        name="nsa_attention",
    )(r3(q, N_HEADS * LANES), kcmp, vcmp, r3(ksa, 2 * LANES), r3(vsa, 2 * LANES), r3(kw, LANES), r3(vwa, 2 * LANES),
      r3(gate, LANES))


def _outproj_kernel(x_ref, yl_ref, ya_ref, yc_ref, gn_ref, w_ref, mod_ref, n2_ref, rw_ref, rb_ref,
                    xn_ref, h2_ref, comb_ref, pos_ref, seg_ref):
    tm = x_ref.shape[0]
    a, b = D_LRU, D_LRU + D_ATT
    yl = _rms(yl_ref[...], gn_ref[:, 0:a]).astype(BF16)
    ya = _rms(ya_ref[...], gn_ref[:, a:b]).astype(BF16)
    yc = _rms(yc_ref[...], gn_ref[:, b:D_MODEL]).astype(BF16)
    y = _dot(yl, w_ref[0:a, :]) + _dot(ya, w_ref[a:b, :]) + _dot(yc, w_ref[b:D_MODEL, :])
    xn = x_ref[...] + mod_ref[2:3, :] * y
    xn_ref[...] = xn
    h2 = _rms(xn, n2_ref[...]) * (1.0 + mod_ref[4:5, :]) + mod_ref[3:4, :]

    h_hi = h2.astype(BF16)
    h_mid = (h2 - h_hi.astype(F32)).astype(BF16)
    logit = _dot(h_hi, rw_ref[0]) + (_dot(h_hi, rw_ref[1]) + _dot(h_mid, rw_ref[0])) + rb_ref[...]
    lane = lax.broadcasted_iota(jnp.int32, (tm, LANES), 1)
    ninf = -jnp.inf
    is_g = lane < MOE_GROUPS
    is_e = (lane >= MOE_GROUPS) & (lane < MOE_GROUPS + MOE_GROUPS * MOE_EXPERTS)
    lg_max = jnp.max(jnp.where(is_g, logit, ninf), axis=-1, keepdims=True)
    g_star = jnp.min(jnp.where(is_g & (logit == lg_max), lane, LANES), axis=-1, keepdims=True)
    pg_star = 1.0 / jnp.sum(jnp.where(is_g, jnp.exp(logit - lg_max), 0.0), axis=-1, keepdims=True)
    in_grp = is_e & (((lane - MOE_GROUPS) >> 2) == g_star)
    v1 = jnp.max(jnp.where(in_grp, logit, ninf), axis=-1, keepdims=True)
    i1 = jnp.min(jnp.where(in_grp & (logit == v1), lane, LANES), axis=-1, keepdims=True)
    rest = in_grp & (lane != i1)
    v2 = jnp.max(jnp.where(rest, logit, ninf), axis=-1, keepdims=True)
    i2 = jnp.min(jnp.where(rest & (logit == v2), lane, LANES), axis=-1, keepdims=True)
    d = jnp.exp(v2 - v1)
    pe1 = 1.0 / (1.0 + d)
    pe2 = d / (1.0 + d)
    comb = jnp.where(lane == i1, pe1, jnp.where(lane == i2, pe2, 0.0)) * pg_star

    onehot = jnp.where(lane == g_star, 1.0, 0.0)
    r_i = lax.broadcasted_iota(jnp.int32, (tm, tm), 0)
    c_i = lax.broadcasted_iota(jnp.int32, (tm, tm), 1)
    earlier = jnp.where(c_i < r_i, 1.0, 0.0).astype(BF16)
    rank = jnp.sum(onehot * _dot(earlier, onehot.astype(BF16)), axis=-1, keepdims=True)
    n_g = jnp.sum(onehot, axis=0, keepdims=True)
    n_pad = (((n_g.astype(jnp.int32) + (MOE_ALIGN - 1)) >> MOE_ALIGN_LOG2) << MOE_ALIGN_LOG2).astype(F32)
    lane1 = lax.broadcasted_iota(jnp.int32, (1, LANES), 1)
    start = jnp.zeros((1, LANES), F32)
    below = jnp.zeros((1, 1), F32)
    for g in range(1, MOE_GROUPS):
        below = below + jnp.sum(jnp.where(lane1 == g - 1, n_pad, 0.0), axis=-1, keepdims=True)
        start = start + jnp.where(lane1 == g, below, 0.0)
    pos = jnp.sum(onehot * start, axis=-1, keepdims=True) + rank
    pos_rep = jnp.broadcast_to(pos, (tm, LANES))
    slot = lax.broadcasted_iota(jnp.int32, (tm, MOE_SLOTS), 1).astype(F32)
    take_t = jnp.where(_rep(pos_rep, MOE_SLOTS // LANES, 1) == slot, 1.0, 0.0)
    take = take_t.T
    take = take.astype(BF16)
    h2_ref[...] = _dot(take, h_hi).astype(BF16)
    c_hi = comb.astype(BF16)
    c_rest = comb - c_hi.astype(F32)
    c_mid = c_rest.astype(BF16)
    c_lo = (c_rest - c_mid.astype(F32)).astype(BF16)
    comb_ref[...] = (_dot(take, c_hi) + _dot(take, c_mid)) + _dot(take, c_lo)
    pos_ref[...] = pos_rep
    seg_ref[...] = jnp.zeros(seg_ref.shape, jnp.int32)
    seg_ref[0:1, :] = n_pad.astype(jnp.int32)
    seg_ref[1:2, :] = start.astype(jnp.int32)


def _outproj(x2, yl, ya, yc, gn, w_out, mod_l, n2g, rw, rb, seq):
    t, d = x2.shape
    tps = seq // TM
    row = lambda w: pl.BlockSpec((TM, w), lambda i: (i, 0))
    full = lambda a: pl.BlockSpec(a.shape, lambda i: (0,) * a.ndim)
    return pl.pallas_call(
        _outproj_kernel,
        grid=(t // TM,),
        in_specs=[row(d), row(D_LRU), row(D_ATT), row(D_CONV), full(gn), full(w_out),
                  pl.BlockSpec((None, 6, d), lambda i: (i // tps, 0, 0)), full(n2g), full(rw), full(rb)],
        out_specs=[row(d), pl.BlockSpec((MOE_SLOTS, d), lambda i: (i, 0)), pl.BlockSpec((MOE_SLOTS, LANES), lambda i: (i, 0)),
                   row(LANES), pl.BlockSpec((None, 8, LANES), lambda i: (i, 0, 0))],
        out_shape=[jax.ShapeDtypeStruct((t, d), F32), jax.ShapeDtypeStruct((t // TM * MOE_SLOTS, d), BF16),
                   jax.ShapeDtypeStruct((t // TM * MOE_SLOTS, LANES), F32), jax.ShapeDtypeStruct((t, LANES), F32),
                   jax.ShapeDtypeStruct((t // TM, 8, LANES), jnp.int32)],
        compiler_params=_cparams(1),
        name="out_proj_router",
    )(x2, yl.reshape(t, D_LRU), ya.reshape(t, D_ATT), yc.reshape(t, D_CONV), gn, w_out, mod_l, n2g, rw, rb)


def _moe_plan(seg, n_chunks, n_gtiles):
    rows = seg[:, 0, :MOE_GROUPS]
    start = seg[:, 1, :MOE_GROUPS]
    ci = jnp.arange(n_chunks, dtype=jnp.int32)
    gi = jnp.arange(MOE_GROUPS, dtype=jnp.int32)
    cum = jnp.sum(jnp.where((ci[None, :] <= ci[:, None])[:, :, None], rows[None, :, :], 0), axis=1)
    before = cum - rows
    total = jnp.sum(rows, axis=0)
    tiles_g = (total + (TM - 1)) // TM
    ends = jnp.sum(jnp.where(gi[None, :] <= gi[:, None], tiles_g[None, :], 0), axis=1)
    base = (ends - tiles_g) * TM
    group_at = lambda tile: jnp.minimum(jnp.sum((tile[:, None] >= ends[None, :]).astype(jnp.int32), axis=1),
                                        MOE_GROUPS - 1)
    tiles = jnp.concatenate([group_at(jnp.arange(n_gtiles, dtype=jnp.int32)), ends[-1:]]).astype(jnp.int32)

    row = jnp.arange(n_gtiles * (TM // MOE_ALIGN), dtype=jnp.int32) * MOE_ALIGN
    is_g = group_at(row // TM)[:, None] == gi[None, :]
    of_group = lambda v: jnp.sum(jnp.where(is_g, v[None, :], 0), axis=1)
    p = row - of_group(base)
    cum_g = jnp.sum(jnp.where(is_g[:, None, :], cum[None, :, :], 0), axis=2)
    chunk = jnp.minimum(jnp.sum((p[:, None] >= cum_g).astype(jnp.int32), axis=1), n_chunks - 1)
    is_cg = (chunk[:, None] == ci[None, :])[:, :, None] & is_g[:, None, :]
    of_segment = lambda a: jnp.sum(jnp.where(is_cg, a[None, :, :], 0), axis=(1, 2))
    src = chunk * MOE_SLOTS + of_segment(start - before) + p
    zero_row = TM + MOE_GROUPS * MOE_ALIGN
    gather = jnp.where(p < of_group(total), src, zero_row) // MOE_ALIGN

    slot = jnp.arange(MOE_SLOTS // MOE_ALIGN, dtype=jnp.int32) * MOE_ALIGN
    inside = (slot[None, :, None] >= start[:, None, :]) & (slot[None, :, None] < (start + rows)[:, None, :])
    dst = jnp.sum(jnp.where(inside, (base[None, :] + before - start)[:, None, :], 0), axis=-1) + slot[None, :]
    scatter = jnp.where(jnp.any(inside, axis=-1), dst // MOE_ALIGN, -1)
    return gather.astype(jnp.int32), scatter.reshape(-1).astype(jnp.int32), tiles


def _experts_kernel(gather_ref, tiles_ref, h2s_ref, combs_ref, wg_ref, wu_ref, wd_ref, y_ref,
                    hbuf, cbuf, acc_ref, sem):
    k = pl.program_id(0)
    e = pl.program_id(1)
    n_used = tiles_ref[pl.num_programs(0)]
    per_tile = TM // MOE_ALIGN

    def tile_copies(tile, go):
        slot = tile % 2
        for j in range(per_tile):
            src = pl.multiple_of(gather_ref[tile * per_tile + j] * MOE_ALIGN, MOE_ALIGN)
            dst = pl.ds(j * MOE_ALIGN, MOE_ALIGN)
            go(pltpu.make_async_copy(h2s_ref.at[pl.ds(src, MOE_ALIGN)], hbuf.at[slot, dst], sem.at[0, slot]))
            go(pltpu.make_async_copy(combs_ref.at[pl.ds(src, MOE_ALIGN)], cbuf.at[slot, dst], sem.at[1, slot]))

    @pl.when(e == 0)
    def _():
        acc_ref[...] = jnp.zeros_like(acc_ref)

        @pl.when((k == 0) & (n_used > 0))
        def _():
            tile_copies(k, lambda cp: cp.start())

        @pl.when(k + 1 < n_used)
        def _():
            tile_copies(k + 1, lambda cp: cp.start())

        @pl.when(k < n_used)
        def _():
            tile_copies(k, lambda cp: cp.wait())

    @pl.when(k < n_used)
    def _():
        slot = k % 2
        h2 = hbuf[slot]
        hid = jax.nn.silu(_dot(h2, wg_ref[...].astype(BF16))) * _dot(h2, wu_ref[...].astype(BF16))
        lane = lax.broadcasted_iota(jnp.int32, (TM, LANES), 1)
        col = MOE_GROUPS + tiles_ref[k] * MOE_EXPERTS + e
        c = jnp.sum(jnp.where(lane == col, cbuf[slot], 0.0), axis=-1, keepdims=True)
        acc_ref[...] += _dot((hid * c).astype(BF16), wd_ref[...].astype(BF16))

    @pl.when(e == pl.num_programs(1) - 1)
    def _():
        y_ref[...] = acc_ref[...]


def _experts(gather, tiles, h2s, combs, wg, wu, wd, n_gtiles, first_expert):
    d = h2s.shape[1]

    def expert_block(k, e, ga, tl):
        e_eff = jnp.where(k < tl[n_gtiles], e, MOE_EXPERTS - 1)
        return (first_expert + tl[k] * MOE_EXPERTS + e_eff, 0, 0)

    weights = lambda s: pl.BlockSpec((None,) + s, expert_block)
    grid_spec = pltpu.PrefetchScalarGridSpec(
        num_scalar_prefetch=2,
        grid=(n_gtiles, MOE_EXPERTS),
        in_specs=[pl.BlockSpec(memory_space=pl.ANY), pl.BlockSpec(memory_space=pl.ANY),
                  weights((d, MOE_HID)), weights((d, MOE_HID)), weights((MOE_HID, d))],
        out_specs=pl.BlockSpec((TM, d), lambda k, e, ga, tl: (k, 0)),
        scratch_shapes=[pltpu.VMEM((2, TM, d), BF16), pltpu.VMEM((2, TM, LANES), F32), pltpu.VMEM((TM, d), F32),
                        pltpu.SemaphoreType.DMA((2, 2))],
    )
    return pl.pallas_call(
        _experts_kernel,
        grid_spec=grid_spec,
        out_shape=jax.ShapeDtypeStruct((n_gtiles * TM, d), F32),
        compiler_params=_cparams(2),
        name="moe_experts",
    )(gather, tiles, h2s, combs, wg, wu, wd)


def _finalize_kernel(scatter_ref, yg_ref, pos_ref, xn_ref, mod_ref, fg_ref, o_ref, ys_ref, sem, *, final):
    c = pl.program_id(0)
    per_chunk = MOE_SLOTS // MOE_ALIGN

    def chunk_copies(chunk, go):
        buf = chunk % 2
        for j in range(per_chunk):
            src = scatter_ref[chunk * per_chunk + j]

            @pl.when(src >= 0)
            def _(j=j, src=src):
                rows = pl.ds(pl.multiple_of(src * MOE_ALIGN, MOE_ALIGN), MOE_ALIGN)
                go(pltpu.make_async_copy(yg_ref.at[rows], ys_ref.at[buf, pl.ds(j * MOE_ALIGN, MOE_ALIGN)],
                                         sem.at[buf]))

    @pl.when(c == 0)
    def _():
        ys_ref[...] = jnp.zeros_like(ys_ref)
        chunk_copies(c, lambda cp: cp.start())

    @pl.when(c + 1 < pl.num_programs(0))
    def _():
        chunk_copies(c + 1, lambda cp: cp.start())

    chunk_copies(c, lambda cp: cp.wait())

    tm = xn_ref.shape[0]
    y = ys_ref[c % 2]
    hi = y.astype(BF16)
    mid = (y - hi.astype(F32)).astype(BF16)
    slot = lax.broadcasted_iota(jnp.int32, (tm, MOE_SLOTS), 1).astype(F32)
    take_t = jnp.where(_rep(pos_ref[...], MOE_SLOTS // LANES, 1) == slot, 1.0, 0.0).astype(BF16)
    y_tok = _dot(take_t, hi) + _dot(take_t, mid)
    xo = xn_ref[...] + mod_ref[5:6, :] * y_tok
    if final:
        xo = _rms(xo, fg_ref[...])
    o_ref[...] = xo


def _finalize(scatter, yg, pos, xn, mod_l, fg, seq, final):
    t, d = xn.shape
    tps = seq // TM
    grid_spec = pltpu.PrefetchScalarGridSpec(
        num_scalar_prefetch=1,
        grid=(t // TM,),
        in_specs=[pl.BlockSpec(memory_space=pl.ANY),
                  pl.BlockSpec((TM, LANES), lambda i, tb: (i, 0)),
                  pl.BlockSpec((TM, d), lambda i, tb: (i, 0)),
                  pl.BlockSpec((None, 6, d), lambda i, tb: (i // tps, 0, 0)),
                  pl.BlockSpec((1, d), lambda i, tb: (0, 0))],
        out_specs=pl.BlockSpec((TM, d), lambda i, tb: (i, 0)),
        scratch_shapes=[pltpu.VMEM((2, MOE_SLOTS, d), F32), pltpu.SemaphoreType.DMA((2,))],
    )
    return pl.pallas_call(
        functools.partial(_finalize_kernel, final=final),
        grid_spec=grid_spec,
        out_shape=jax.ShapeDtypeStruct((t, d), F32),
        compiler_params=_cparams(1),
        name="moe_finalize",
    )(scatter, yg, pos, xn, mod_l, fg)


def _permute_w_in(w):
    gate0 = D_LRU * 2 + D_ATT + 6 * LANES
    cv0 = gate0 + 3 * N_HEADS
    w = w.astype(BF16)
    pad = jnp.zeros((w.shape[0], IN_PAD - w.shape[1]), BF16)
    return jnp.concatenate([w[:, :gate0], w[:, cv0:], w[:, gate0:cv0], pad], axis=1)


def _router_weights(rg_w, rg_b, re_w, re_b):
    d = rg_w.shape[0]
    ne = MOE_GROUPS * MOE_EXPERTS
    w = jnp.concatenate([rg_w, jnp.transpose(re_w, (1, 0, 2)).reshape(d, ne),
                         jnp.zeros((d, LANES - MOE_GROUPS - ne), F32)], axis=1)
    b = jnp.concatenate([rg_b, re_b.reshape(ne), jnp.zeros((LANES - MOE_GROUPS - ne,), F32)])[None, :]
    w_hi = w.astype(BF16)
    w_mid = (w - w_hi.astype(F32)).astype(BF16)
    return jnp.stack([w_hi, w_mid]), b


def kernel(x, c, positions, ada_w, ada_b, norm1_g, norm2_g, w_in, lru_conv_w, lru_conv_b, lru_wa, lru_ba, lru_wx, lru_bx, lru_lambda, cmp_k_w1, cmp_k_w2, cmp_v_w1, cmp_v_w2, cmp_pos_k, cmp_pos_v, cnv_dw_w, cnv_dw_b, cnv_ln_g, cnv_ln_b, out_norm_g, w_out, moe_rg_w, moe_rg_b, moe_re_w, moe_re_b, moe_w_gate, moe_w_up, moe_w_down, final_norm_g):
    batch, seq, d = x.shape
    depth = ada_w.shape[0]
    assert d == D_MODEL and seq == N_SLC * SLC_LEN and seq % KEY_TILE == 0
    t = batch * seq
    ne = MOE_GROUPS * MOE_EXPERTS
    n_chunks = t // TM
    n_gtiles = n_chunks + -(-n_chunks * MOE_GROUPS * (MOE_ALIGN - 1) // TM) + MOE_GROUPS

    cos, sin = _rope_tables(positions)
    mod = _modulation(c, ada_w, ada_b)
    x2 = x.reshape(t, d)
    fg = final_norm_g[None, :]
    for l in range(depth):
        (xl, gl, q, kc, vc, ksa, vsa, kw, vwa, gate, cv) = _inproj(
            x2, mod[l], norm1_g[l][None, :], _permute_w_in(w_in[l]), cos, sin, seq)
        y_lru, y_cnv = _mixer(xl, gl, cv, lru_conv_w[l], lru_conv_b[l], lru_wa[l], lru_ba[l], lru_wx[l], lru_bx[l],
                              lru_lambda[l], cnv_dw_w[l], cnv_dw_b[l], cnv_ln_g[l], cnv_ln_b[l], batch, seq)
        kcmp, vcmp = _compress(kc, vc, cmp_pos_k[l], cmp_k_w1[l], cmp_k_w2[l], cmp_pos_v[l], cmp_v_w1[l], cmp_v_w2[l],
                               batch, seq)
        y_att = _attention(q, kcmp, vcmp, ksa, vsa, kw, vwa, gate, batch, seq)
        rw, rb = _router_weights(moe_rg_w[l], moe_rg_b[l], moe_re_w[l], moe_re_b[l])
        xn, h2, comb, pos, seg = _outproj(x2, y_lru, y_att, y_cnv, out_norm_g[l][None, :], w_out[l].astype(BF16),
                                          mod[l], norm2_g[l][None, :], rw, rb, seq)
        gather, scatter, tiles = _moe_plan(seg, n_chunks, n_gtiles)
        yg = _experts(gather, tiles, h2, comb,
                      moe_w_gate.reshape(depth * ne, d, MOE_HID), moe_w_up.reshape(depth * ne, d, MOE_HID),
                      moe_w_down.reshape(depth * ne, MOE_HID, d), n_gtiles, l * ne)
        x2 = _finalize(scatter, yg, pos, xn, mod[l], fg, seq, final=(l == depth - 1))
    return x2.reshape(batch, seq, d)
```
